```python
import math
import jax, jax.numpy as jnp
from jax import lax
import numpy as np

D_MODEL = 1024
BATCH = 1
SEQ = 16384
DEPTH = 2
DEC_BATCH = 32
DEC_SEQ = 2048
PAST_LEN = 128

EXPAND = 2
D_MIX = EXPAND * D_MODEL
N_GROUPS = 4
W_GROUP = D_MIX // N_GROUPS
HEAD_DIM = 128
N_HEADS = W_GROUP // HEAD_DIM
CONV_A_WIDTH = 31
DN_CONV_WIDTH = 5
CHUNK = 64
ROPE_BASE = 10000.0
RET_DECAY_OFFSET = 5.0
NORM_EPS = 1e-6
IN_SPLIT_SIZES = (
    W_GROUP, W_GROUP, W_GROUP,
    W_GROUP, W_GROUP, W_GROUP, W_GROUP,
    W_GROUP, W_GROUP, W_GROUP, W_GROUP, W_GROUP,
    3 * W_GROUP, 2 * N_HEADS, 2 * N_HEADS, W_GROUP,
)
D_IN = sum(IN_SPLIT_SIZES)

kernel_name = "hymba_style_bidir_hybrid_encoder"

F32 = jnp.float32


def rms_norm(x, g):
    xf = x.astype(F32)
    y = xf * lax.rsqrt(jnp.mean(xf * xf, axis=-1, keepdims=True) + NORM_EPS)
    return (y * g.astype(F32)).astype(x.dtype)


def layer_norm(x, g, b):
    mu = jnp.mean(x, axis=-1, keepdims=True)
    xc = x - mu
    var = jnp.mean(xc * xc, axis=-1, keepdims=True)
    return xc * lax.rsqrt(var + NORM_EPS) * g.astype(F32) + b.astype(F32)


def head_rms_norm(o, g):
    o = o * lax.rsqrt(jnp.mean(o * o, axis=-1, keepdims=True) + NORM_EPS)
    b, h, t, d = o.shape
    return o.transpose(0, 2, 1, 3).reshape(b, t, h * d) * g.astype(F32)


def to_heads(t):
    b, s, w = t.shape
    return t.reshape(b, s, w // HEAD_DIM, HEAD_DIM).transpose(0, 2, 1, 3)


def rev(t):
    return jnp.flip(t, axis=2)


def l2_normalize(t):
    return t * lax.rsqrt(jnp.sum(t * t, axis=-1, keepdims=True) + NORM_EPS)


def depthwise_conv_centred(x, w):
    k, c = w.shape
    return lax.conv_general_dilated(
        x, w.astype(x.dtype)[:, None, :], window_strides=(1,), padding=[(k // 2, k // 2)],
        dimension_numbers=('NWC', 'WIO', 'NWC'), feature_group_count=c)


def rotary(x, pos):
    half = x.shape[-1] // 2
    inv = 1.0 / (ROPE_BASE ** (jnp.arange(half, dtype=F32) / half))
    ang = pos[:, None] * inv[None, :]
    cos, sin = jnp.cos(ang), jnp.sin(ang)
    x1, x2 = x[..., :half], x[..., half:]
    return jnp.concatenate([x1 * cos - x2 * sin, x1 * sin + x2 * cos], axis=-1)


def conformer_conv_branch(val, glu_gate, conv_w, conv_b, ln_g, ln_b):
    u = val.astype(F32) * jax.nn.sigmoid(glu_gate.astype(F32))
    u = depthwise_conv_centred(u, conv_w) + conv_b.astype(F32)
    u = layer_norm(u, ln_g, ln_b)
    return jax.nn.silu(u)


def retention_chunked(q, k, v, log_gamma):
    b, h, t, dk = q.shape
    dv = v.shape[-1]
    n, c = t // CHUNK, CHUNK
    idx = jnp.arange(c, dtype=F32)
    rel = idx[:, None] - idx[None, :]
    intra_decay = jnp.where(rel >= 0, jnp.exp(log_gamma[:, None, None] * jnp.maximum(rel, 0.0)), 0.0)
    q_decay = jnp.exp(log_gamma[:, None] * (idx + 1.0))
    k_decay = jnp.exp(log_gamma[:, None] * (c - 1.0 - idx))
    chunk_decay = jnp.exp(log_gamma * c)
    qc = q.reshape(b, h, n, c, dk)
    kc = k.reshape(b, h, n, c, dk)
    vc = v.reshape(b, h, n, c, dv)
    scores = jnp.einsum('bhnid,bhnjd->bhnij', qc, kc) * intra_decay[None, :, None]
    o_intra = jnp.einsum('bhnij,bhnje->bhnie', scores, vc)
    kv = jnp.einsum('bhnjd,bhnje->nbhde', kc * k_decay[None, :, None, :, None], vc)

    def step(s, kv_n):
        return s * chunk_decay[None, :, None, None] + kv_n, s

    _, s_prev = lax.scan(step, jnp.zeros((b, h, dk, dv), F32), kv)
    o_inter = jnp.einsum('bhnid,nbhde->bhnie', qc * q_decay[None, :, None, :, None], s_prev)
    return (o_intra + o_inter).reshape(b, h, t, dv)


def retention_branch(q, k, v, norm_g):
    t = q.shape[1]
    q, k, v = to_heads(q.astype(F32)), to_heads(k.astype(F32)), to_heads(v.astype(F32))
    pos = jnp.arange(t, dtype=F32)
    q = rotary(q, pos) * (HEAD_DIM ** -0.5)
    k = rotary(k, pos)
    log_gamma_fwd = jnp.log1p(-jnp.exp2(-RET_DECAY_OFFSET - jnp.arange(N_HEADS, dtype=F32)))
    log_gamma_bwd = log_gamma_fwd[::-1]
    o = retention_chunked(q, k, v, log_gamma_fwd) + rev(retention_chunked(rev(q), rev(k), rev(v), log_gamma_bwd))
    return head_rms_norm(o, norm_g)


def chunk_gla(q, k, v, log_f):
    b, h, t, dk = q.shape
    dv = v.shape[-1]
    n, c = t // CHUNK, CHUNK

    def chunks(a):
        return jnp.moveaxis(a.reshape(b, h, n, c, a.shape[-1]), 2, 0)

    cum = jnp.cumsum(log_f.reshape(b, h, n, c, dk), axis=3)
    tri = jnp.tril(jnp.ones((c, c), dtype=bool))

    def step(s, inp):
        q_n, k_n, v_n, b_n = inp
        b_last = b_n[:, :, -1, :]
        decay = jnp.exp(jnp.where(tri[:, :, None], b_n[:, :, :, None, :] - b_n[:, :, None, :, :], -jnp.inf))
        attn = jnp.einsum('bhid,bhjd,bhijd->bhij', q_n, k_n, decay)
        o_n = jnp.einsum('bhid,bhde->bhie', q_n * jnp.exp(b_n), s) + jnp.einsum('bhij,bhje->bhie', attn, v_n)
        s = s * jnp.exp(b_last)[..., None] + jnp.einsum('bhjd,bhje->bhde', k_n * jnp.exp(b_last[:, :, None, :] - b_n), v_n)
        return s, o_n

    _, o = lax.scan(step, jnp.zeros((b, h, dk, dv), F32), (chunks(q), chunks(k), chunks(v), jnp.moveaxis(cum, 2, 0)))
    return jnp.moveaxis(o, 0, 2).reshape(b, h, t, dv)


def hgrn2_branch(q, f_fwd, f_bwd, i, lb, norm_g):
    q = to_heads(jax.nn.silu(q.astype(F32))) * (HEAD_DIM ** -0.5)
    v = to_heads(i.astype(F32))

    def forget(pre, lb_dir):
        log_f = jnp.logaddexp(jnp.log(lb_dir), jnp.log1p(-lb_dir) + jax.nn.log_sigmoid(pre.astype(F32)))
        log_f = to_heads(log_f)
        return -jnp.expm1(log_f), log_f

    k_f, g_f = forget(f_fwd, lb[0])
    k_b, g_b = forget(f_bwd, lb[1])
    o = chunk_gla(q, k_f, v, g_f) + rev(chunk_gla(rev(q), rev(k_b), rev(v), rev(g_b)))
    return head_rms_norm(o, norm_g)


def chunk_gated_delta(q, k, v, g, beta):
    b, h, t, dk = q.shape
    dv = v.shape[-1]
    n, c = t // CHUNK, CHUNK
    qc = q.reshape(b, h, n, c, dk)
    kc = k.reshape(b, h, n, c, dk)
    vc = v.reshape(b, h, n, c, dv)
    bc = beta.reshape(b, h, n, c)[..., None]
    cum = jnp.cumsum(g.reshape(b, h, n, c), axis=-1)
    tri = jnp.tril(jnp.ones((c, c), dtype=bool))
    strict = jnp.tril(jnp.ones((c, c), dtype=bool), -1)
    decay = jnp.exp(jnp.where(tri, cum[..., :, None] - cum[..., None, :], -jnp.inf))
    k_beta = kc * bc
    v_beta = vc * bc
    a_mat = jnp.where(strict, jnp.einsum('bhnid,bhnjd->bhnij', k_beta, kc) * decay, 0.0)
    t_mat = a_mat + jnp.eye(c, dtype=F32)
    u = lax.linalg.triangular_solve(t_mat, v_beta, left_side=True, lower=True, unit_diagonal=True)
    w = lax.linalg.triangular_solve(t_mat, k_beta * jnp.exp(cum)[..., None], left_side=True, lower=True, unit_diagonal=True)
    qk = jnp.einsum('bhnid,bhnjd->bhnij', qc, kc) * decay
    q_dec = qc * jnp.exp(cum)[..., None]
    k_dec = kc * jnp.exp(cum[..., -1:] - cum)[..., None]
    last = jnp.exp(cum[..., -1])

    def step(s, inp):
        u_n, w_n, qk_n, qd_n, kd_n, last_n = inp
        v_new = u_n - jnp.einsum('bhid,bhde->bhie', w_n, s)
        o_n = jnp.einsum('bhid,bhde->bhie', qd_n, s) + jnp.einsum('bhij,bhje->bhie', qk_n, v_new)
        s = s * last_n[..., None, None] + jnp.einsum('bhjd,bhje->bhde', kd_n, v_new)
        return s, o_n

    xs = (jnp.moveaxis(u, 2, 0), jnp.moveaxis(w, 2, 0), jnp.moveaxis(qk, 2, 0),
          jnp.moveaxis(q_dec, 2, 0), jnp.moveaxis(k_dec, 2, 0), jnp.moveaxis(last, 2, 0))
    _, o = lax.scan(step, jnp.zeros((b, h, dk, dv), F32), xs)
    return jnp.moveaxis(o, 0, 2).reshape(b, h, t, dv)


def gated_deltanet_branch(qkv, a_pre, beta_pre, conv_w, a_log, dt_bias, norm_g):
    bsz, t, _ = qkv.shape
    qkv = jax.nn.silu(depthwise_conv_centred(qkv.astype(F32), conv_w))
    q, k, v = jnp.split(qkv, 3, axis=-1)
    q = l2_normalize(to_heads(q)) * (HEAD_DIM ** -0.5)
    k = l2_normalize(to_heads(k))
    v = to_heads(v)
    a_pre = a_pre.astype(F32).reshape(bsz, t, 2, N_HEADS)
    beta = jax.nn.sigmoid(beta_pre.astype(F32).reshape(bsz, t, 2, N_HEADS))
    g = -jnp.exp(a_log.astype(F32)) * jax.nn.softplus(a_pre + dt_bias.astype(F32))
    g = g.transpose(2, 0, 3, 1)
    beta = beta.transpose(2, 0, 3, 1)
    o = chunk_gated_delta(q, k, v, g[0], beta[0]) + rev(chunk_gated_delta(rev(q), rev(k), rev(v), rev(g[1]), rev(beta[1])))
    return head_rms_norm(o, norm_g)


def hybrid_layer(x, c, ada_w, ada_b, norm_g, w_in, conv_a_w, conv_a_b, ln_a_g, ln_a_b, ret_norm_g,
                 hgrn_lb, hgrn_norm_g, dn_conv_w, dn_a_log, dn_dt_bias, dn_norm_g, w_out):
    mod = (jax.nn.silu(c) @ ada_w + ada_b)[:, None, :]
    shift, scale, gate = jnp.split(mod, 3, axis=-1)
    h = rms_norm(x, norm_g) * (1.0 + scale) + shift
    proj = h @ w_in
    points = [int(p) for p in np.cumsum(IN_SPLIT_SIZES)[:-1]]
    (a_val, a_glu, a_z, b_q, b_k, b_v, b_z, c_q, c_ff, c_fb, c_i, c_z,
     d_qkv, d_a, d_beta, d_z) = jnp.split(proj, points, axis=-1)
    o_a = conformer_conv_branch(a_val, a_glu, conv_a_w, conv_a_b, ln_a_g, ln_a_b)
    o_b = retention_branch(b_q, b_k, b_v, ret_norm_g)
    o_c = hgrn2_branch(c_q, c_ff, c_fb, c_i, hgrn_lb, hgrn_norm_g)
    o_d = gated_deltanet_branch(d_qkv, d_a, d_beta, dn_conv_w, dn_a_log, dn_dt_bias, dn_norm_g)
    mixed = jnp.concatenate([
        o_a * jax.nn.silu(a_z.astype(F32)),
        o_b * jax.nn.silu(b_z.astype(F32)),
        o_c * jax.nn.silu(c_z.astype(F32)),
        o_d * jax.nn.silu(d_z.astype(F32)),
    ], axis=-1)
    out = mixed.astype(x.dtype) @ w_out
    return x + (gate * out).astype(x.dtype)


def setup_inputs(seed: int = 0) -> dict:
    key = jax.random.key(seed)
    ks = jax.random.split(key, 24)

    def nrm(k, shape, s):
        return jax.random.normal(k, shape, F32) * s

    x_prompt = nrm(ks[0], (BATCH, SEQ, D_MODEL), 1.0)
    x_sample = nrm(ks[1], (DEC_BATCH, DEC_SEQ, D_MODEL), 1.0)
    c_prompt = nrm(ks[2], (BATCH, D_MODEL), 1.0)
    c_sample = nrm(ks[3], (DEC_BATCH, D_MODEL), 1.0)
    ada_w = nrm(ks[4], (DEPTH, D_MODEL, 3 * D_MODEL), 0.5 * D_MODEL ** -0.5)
    ada_b = nrm(ks[5], (DEPTH, 3 * D_MODEL), 0.02)
    norm_g = 1.0 + nrm(ks[6], (DEPTH, D_MODEL), 0.02)
    w_in = nrm(ks[7], (DEPTH, D_MODEL, D_IN), D_MODEL ** -0.5)
    conv_a_w = nrm(ks[8], (DEPTH, CONV_A_WIDTH, W_GROUP), CONV_A_WIDTH ** -0.5)
    conv_a_b = nrm(ks[9], (DEPTH, W_GROUP), 0.02)
    ln_a_g = 1.0 + nrm(ks[10], (DEPTH, W_GROUP), 0.02)
    ln_a_b = nrm(ks[11], (DEPTH, W_GROUP), 0.02)
    ret_norm_g = 1.0 + nrm(ks[12], (DEPTH, W_GROUP), 0.02)
    hgrn_lb_logits = nrm(ks[13], (DEPTH, 2, W_GROUP), 0.5)
    hgrn_norm_g = 1.0 + nrm(ks[14], (DEPTH, W_GROUP), 0.02)
    dn_conv_w = nrm(ks[15], (DEPTH, DN_CONV_WIDTH, 3 * W_GROUP), DN_CONV_WIDTH ** -0.5)
    dn_a_log = jnp.log(jax.random.uniform(ks[16], (DEPTH, 2, N_HEADS), F32, 1.0, 16.0))
    dt = jnp.exp(jax.random.uniform(ks[17], (DEPTH, 2, N_HEADS), F32, math.log(1e-3), math.log(1e-1)))
    dn_dt_bias = dt + jnp.log(-jnp.expm1(-dt))
    dn_norm_g = 1.0 + nrm(ks[18], (DEPTH, W_GROUP), 0.02)
    w_out = nrm(ks[19], (DEPTH, D_MIX, D_MODEL), D_MIX ** -0.5)
    final_g = 1.0 + nrm(ks[20], (D_MODEL,), 0.02)
    return {
        "x_prompt": x_prompt, "x_sample": x_sample, "c_prompt": c_prompt, "c_sample": c_sample,
        "ada_w": ada_w, "ada_b": ada_b, "norm_g": norm_g, "w_in": w_in,
        "conv_a_w": conv_a_w, "conv_a_b": conv_a_b, "ln_a_g": ln_a_g, "ln_a_b": ln_a_b,
        "ret_norm_g": ret_norm_g, "hgrn_lb_logits": hgrn_lb_logits, "hgrn_norm_g": hgrn_norm_g,
        "dn_conv_w": dn_conv_w, "dn_a_log": dn_a_log, "dn_dt_bias": dn_dt_bias, "dn_norm_g": dn_norm_g,
        "w_out": w_out, "final_g": final_g,
    }


def reference(x_prompt, x_sample, c_prompt, c_sample, ada_w, ada_b, norm_g, w_in, conv_a_w, conv_a_b,
              ln_a_g, ln_a_b, ret_norm_g, hgrn_lb_logits, hgrn_norm_g, dn_conv_w, dn_a_log, dn_dt_bias,
              dn_norm_g, w_out, final_g):
    lb = jnp.cumsum(jax.nn.softmax(hgrn_lb_logits.astype(F32), axis=0), axis=0)
    lb = lb - lb[:1]

    def trunk(x, c):
        for l in range(DEPTH):
            x = hybrid_layer(x, c, ada_w[l], ada_b[l], norm_g[l], w_in[l], conv_a_w[l], conv_a_b[l],
                             ln_a_g[l], ln_a_b[l], ret_norm_g[l], lb[l], hgrn_norm_g[l], dn_conv_w[l],
                             dn_a_log[l], dn_dt_bias[l], dn_norm_g[l], w_out[l])
        return rms_norm(x, final_g)

    y_prompt = trunk(x_prompt, c_prompt)
    y_sample = trunk(x_sample, c_sample)
    return (y_prompt, y_sample)
```

```python
import functools
import math

import numpy as np
import jax
import jax.numpy as jnp
from jax import lax
from jax.experimental import pallas as pl
from jax.experimental.pallas import tpu as pltpu

F32 = jnp.float32
BF16 = jnp.bfloat16

D_MODEL = 1024
W_GROUP = 512
HEAD_DIM = 128
N_HEADS = 4
CONV_A_WIDTH = 31
DN_CONV_WIDTH = 5
CHUNK = 64
SUB = 16
ROPE_BASE = 10000.0
RET_DECAY_OFFSET = 5.0
NORM_EPS = 1e-6
QK_SCALE = HEAD_DIM ** -0.5
NEG_BIG = -1e30

N_COL_BLOCKS = 16
SMALL_W = 128
D_PROJ = N_COL_BLOCKS * W_GROUP + SMALL_W
PROJ_TN = 640
HALO_A = 16
HALO_D = 8
VMEM_LIMIT = 56 * 1024 * 1024

CB_A_VAL, CB_A_GLU, CB_A_Z = 0, 1, 2
CB_B_Q, CB_B_K, CB_B_V, CB_B_Z = 3, 4, 5, 6
CB_C_Q, CB_C_FF, CB_C_FB, CB_C_I, CB_C_Z = 7, 8, 9, 10, 11
CB_D_Q, CB_D_K, CB_D_V, CB_D_Z = 12, 13, 14, 15
CB_SMALL = N_COL_BLOCKS * W_GROUP // SMALL_W


def _dot(a, b):
    return jnp.dot(a.astype(BF16), b.astype(BF16), preferred_element_type=F32)


def _dot_nt(a, b):
    return lax.dot_general(a.astype(BF16), b.astype(BF16), (((1,), (1,)), ((), ())), preferred_element_type=F32)


def _dot_tn(a, b):
    return lax.dot_general(a.astype(BF16), b.astype(BF16), (((0,), (0,)), ((), ())), preferred_element_type=F32)


def _split2(x):
    hi = x.astype(BF16)
    lo = (x - hi.astype(F32)).astype(BF16)
    return hi, lo


def _split3(x):
    hi = x.astype(BF16)
    r = x - hi.astype(F32)
    mid = r.astype(BF16)
    lo = (r - mid.astype(F32)).astype(BF16)
    return hi, mid, lo


def _dot_exact_lhs(m, x):
    mb = m.astype(BF16)
    acc = None
    for part in _split3(x):
        t = jnp.dot(mb, part, preferred_element_type=F32)
        acc = t if acc is None else acc + t
    return acc


def _sigmoid(x):
    return 1.0 / (1.0 + jnp.exp(-x))


def _silu(x):
    return x * _sigmoid(x)


def _softplus(x):
    return jnp.maximum(x, 0.0) + jnp.log1p(jnp.exp(-jnp.abs(x)))


def _log_sigmoid(x):
    return jnp.minimum(x, 0.0) - jnp.log1p(jnp.exp(-jnp.abs(x)))


class _Layout:
    def __init__(self, t_prompt, n_prompt, t_sample, n_sample, tb):
        self.tp = t_prompt * n_prompt
        self.t_prompt = t_prompt
        self.ts = t_sample
        self.n = self.tp + t_sample * n_sample
        self.tb = tb
        assert t_prompt % tb == 0 and t_sample % tb == 0
        self.nb = self.n // tb

    def seq_start(self, blk):
        s = blk * self.tb
        return jnp.where(s < self.tp, s % self.t_prompt == 0, (s - self.tp) % self.ts == 0)

    def seq_end(self, blk):
        e = (blk + 1) * self.tb
        return jnp.where(e <= self.tp, e % self.t_prompt == 0, (e - self.tp) % self.ts == 0)

    def pos_block(self, blk):
        s = blk * self.tb
        return jnp.where(s < self.tp, (s % self.t_prompt) // self.tb, ((s - self.tp) % self.ts) // self.tb)

    def batch_index(self, blk, rows):
        s = blk * rows
        return jnp.where(s < self.tp, s // self.t_prompt, self.tp // self.t_prompt + (s - self.tp) // self.ts)


def _cparams(sem):
    return pltpu.CompilerParams(dimension_semantics=sem, vmem_limit_bytes=VMEM_LIMIT)


def _mod_kernel(c_ref, w_ref, b_ref, o_ref):
    c = c_ref[...]
    o_ref[0] = jnp.dot(_silu(c), w_ref[0], preferred_element_type=F32,
                       precision=lax.Precision.HIGHEST) + b_ref[0]


def _modulation(c_all, ada_w, ada_b):
    depth = ada_w.shape[0]
    nb = c_all.shape[0]
    return pl.pallas_call(
        _mod_kernel,
        grid=(depth,),
        in_specs=[pl.BlockSpec((nb, D_MODEL), lambda l: (0, 0)),
                  pl.BlockSpec((1, D_MODEL, 3 * D_MODEL), lambda l: (l, 0, 0)),
                  pl.BlockSpec((1, 1, 3 * D_MODEL), lambda l: (l, 0, 0))],
        out_specs=pl.BlockSpec((1, nb, 3 * D_MODEL), lambda l: (l, 0, 0)),
        out_shape=jax.ShapeDtypeStruct((depth, nb, 3 * D_MODEL), F32),
        compiler_params=_cparams(("arbitrary",)),
        name="adaln_modulation",
    )(c_all, ada_w, ada_b.reshape(depth, 1, 3 * D_MODEL))


def _in_proj_kernel(x_ref, mod_ref, g_ref, w_ref, o_ref, h_ref):
    @pl.when(pl.program_id(1) == 0)
    def _():
        x = x_ref[...]
        y = x * lax.rsqrt(jnp.mean(x * x, axis=-1, keepdims=True) + NORM_EPS) * g_ref[...]
        shift = mod_ref[0, :, 0:D_MODEL]
        scale = mod_ref[0, :, D_MODEL:2 * D_MODEL]
        h_ref[...] = (y * (1.0 + scale) + shift).astype(BF16)

    o_ref[...] = jnp.dot(h_ref[...], w_ref[...], preferred_element_type=F32)


def _in_proj(x, mod, norm_g, w_perm, lay, tm):
    n = x.shape[0]
    lay_m = _Layout(lay.t_prompt, lay.tp // lay.t_prompt, lay.ts, (lay.n - lay.tp) // lay.ts, tm)
    return pl.pallas_call(
        _in_proj_kernel,
        grid=(n // tm, D_PROJ // PROJ_TN),
        in_specs=[pl.BlockSpec((tm, D_MODEL), lambda i, j: (i, 0)),
                  pl.BlockSpec((1, 1, 3 * D_MODEL), lambda i, j: (lay_m.batch_index(i, tm), 0, 0)),
                  pl.BlockSpec((1, D_MODEL), lambda i, j: (0, 0)),
                  pl.BlockSpec((D_MODEL, PROJ_TN), lambda i, j: (0, j))],
        out_specs=pl.BlockSpec((tm, PROJ_TN), lambda i, j: (i, j)),
        out_shape=jax.ShapeDtypeStruct((n, D_PROJ), F32),
        scratch_shapes=[pltpu.VMEM((tm, D_MODEL), BF16)],
        compiler_params=_cparams(("arbitrary", "arbitrary")),
        name="in_proj",
    )(x, mod, norm_g.reshape(1, D_MODEL), w_perm)


def _out_proj_kernel(final, x_ref, ma_ref, mb_ref, mc_ref, md_ref, mod_ref, w_ref, fg_ref, o_ref):
    acc = jnp.dot(ma_ref[...], w_ref[0:W_GROUP, :], preferred_element_type=F32)
    acc += jnp.dot(mb_ref[...], w_ref[W_GROUP:2 * W_GROUP, :], preferred_element_type=F32)
    acc += jnp.dot(mc_ref[...], w_ref[2 * W_GROUP:3 * W_GROUP, :], preferred_element_type=F32)
    acc += jnp.dot(md_ref[...], w_ref[3 * W_GROUP:4 * W_GROUP, :], preferred_element_type=F32)
    gate = mod_ref[0, :, 2 * D_MODEL:3 * D_MODEL]
    y = x_ref[...] + gate * acc
    if final:
        y = y * lax.rsqrt(jnp.mean(y * y, axis=-1, keepdims=True) + NORM_EPS) * fg_ref[...]
    o_ref[...] = y


def _out_proj(x, mixed, mod, w_out_bf, final_g, lay, tm, final):
    n = x.shape[0]
    lay_m = _Layout(lay.t_prompt, lay.tp // lay.t_prompt, lay.ts, (lay.n - lay.tp) // lay.ts, tm)
    mspec = pl.BlockSpec((tm, W_GROUP), lambda i: (i, 0))
    return pl.pallas_call(
        functools.partial(_out_proj_kernel, final),
        grid=(n // tm,),
        in_specs=[pl.BlockSpec((tm, D_MODEL), lambda i: (i, 0)), mspec, mspec, mspec, mspec,
                  pl.BlockSpec((1, 1, 3 * D_MODEL), lambda i: (lay_m.batch_index(i, tm), 0, 0)),
                  pl.BlockSpec((4 * W_GROUP, D_MODEL), lambda i: (0, 0)),
                  pl.BlockSpec((1, D_MODEL), lambda i: (0, 0))],
        out_specs=pl.BlockSpec((tm, D_MODEL), lambda i: (i, 0)),
        out_shape=jax.ShapeDtypeStruct((n, D_MODEL), F32),
        compiler_params=_cparams(("arbitrary",)),
        name="out_proj",
    )(x, *mixed, mod, w_out_bf, final_g.reshape(1, D_MODEL))


A_ROWS = 32


def _mixer_a_kernel(lay, val_ref, glu_ref, z_ref, vp_ref, gp_ref, vn_ref, gn_ref,
                    cw_ref, cb_ref, lg_ref, lb_ref, o_ref, u_ref):
    tb = lay.tb
    blk = pl.program_id(0)
    keep_prev = jnp.where(lay.seq_start(blk), 0.0, 1.0)
    keep_next = jnp.where(lay.seq_end(blk), 0.0, 1.0)
    u_ref[0:HALO_A, :] = vp_ref[...] * _sigmoid(gp_ref[...]) * keep_prev
    u_ref[HALO_A:HALO_A + tb, :] = val_ref[...] * _sigmoid(glu_ref[...])
    u_ref[HALO_A + tb:2 * HALO_A + tb, :] = vn_ref[...] * _sigmoid(gn_ref[...]) * keep_next
    half = CONV_A_WIDTH // 2

    for t in range(tb // A_ROWS):
        r0 = t * A_ROWS
        acc = jnp.zeros((A_ROWS, W_GROUP), F32) + cb_ref[...]
        for k in range(CONV_A_WIDTH):
            off = r0 + HALO_A - half + k
            acc = acc + cw_ref[k:k + 1, :] * u_ref[off:off + A_ROWS, :]
        mu = jnp.mean(acc, axis=-1, keepdims=True)
        xc = acc - mu
        var = jnp.mean(xc * xc, axis=-1, keepdims=True)
        y = xc * lax.rsqrt(var + NORM_EPS) * lg_ref[...] + lb_ref[...]
        o_ref[r0:r0 + A_ROWS, :] = (_silu(y) * _silu(z_ref[r0:r0 + A_ROWS, :])).astype(BF16)


def _mixer_a(proj, conv_w, conv_b, ln_g, ln_b, lay):
    tb, nb = lay.tb, lay.nb
    hb = tb // HALO_A
    n_halo = lay.n // HALO_A

    def cur(cb):
        return pl.BlockSpec((tb, W_GROUP), lambda i: (i, cb))

    def prev(cb):
        return pl.BlockSpec((HALO_A, W_GROUP), lambda i: (jnp.maximum(i * hb - 1, 0), cb))

    def nxt(cb):
        return pl.BlockSpec((HALO_A, W_GROUP), lambda i: (jnp.minimum((i + 1) * hb, n_halo - 1), cb))

    vec = pl.BlockSpec((1, W_GROUP), lambda i: (0, 0))
    return pl.pallas_call(
        functools.partial(_mixer_a_kernel, lay),
        grid=(nb,),
        in_specs=[cur(CB_A_VAL), cur(CB_A_GLU), cur(CB_A_Z), prev(CB_A_VAL), prev(CB_A_GLU),
                  nxt(CB_A_VAL), nxt(CB_A_GLU),
                  pl.BlockSpec((CONV_A_WIDTH, W_GROUP), lambda i: (0, 0)), vec, vec, vec],
        out_specs=pl.BlockSpec((tb, W_GROUP), lambda i: (i, 0)),
        out_shape=jax.ShapeDtypeStruct((lay.n, W_GROUP), BF16),
        scratch_shapes=[pltpu.VMEM((tb + 2 * HALO_A, W_GROUP), F32)],
        compiler_params=_cparams(("arbitrary",)),
        name="mixer_a_conv",
    )(proj, proj, proj, proj, proj, proj, proj, conv_w, conv_b.reshape(1, W_GROUP),
      ln_g.reshape(1, W_GROUP), ln_b.reshape(1, W_GROUP))


def _blk_index(lay, rev):
    nb = lay.nb
    return (lambda j: nb - 1 - j) if rev else (lambda j: j)


def _reset_state(lay, rev, s_ref):
    blk = _blk_index(lay, rev)(pl.program_id(0))
    boundary = lay.seq_end(blk) if rev else lay.seq_start(blk)

    @pl.when(boundary)
    def _():
        s_ref[...] = jnp.zeros(s_ref.shape, s_ref.dtype)


def _finish(rev, o_ref, ob_ref, z_ref, gain_ref, out_ref):
    if rev:
        out_ref[...] = o_ref[...]
        return
    for h in range(N_HEADS):
        sl = slice(h * HEAD_DIM, (h + 1) * HEAD_DIM)
        o = o_ref[:, sl] + ob_ref[:, sl]
        y = o * lax.rsqrt(jnp.mean(o * o, axis=-1, keepdims=True) + NORM_EPS) * gain_ref[:, sl]
        out_ref[:, sl] = (y * _silu(z_ref[:, sl])).astype(BF16)


def _causal_masks(rev, c):
    row = lax.broadcasted_iota(jnp.int32, (c, c), 0)
    col = lax.broadcasted_iota(jnp.int32, (c, c), 1)
    incl = (row <= col) if rev else (row >= col)
    strict = (row < col) if rev else (row > col)
    return row, col, incl, strict


def _ret_log_gamma(rev):
    lg = np.log1p(-np.exp2(-RET_DECAY_OFFSET - np.arange(N_HEADS, dtype=np.float64)))
    return lg[::-1].copy() if rev else lg


def _ret_consts(rev):
    lg = _ret_log_gamma(rev).astype(np.float32)
    i = np.arange(CHUNK, dtype=np.float32)
    idx = (CHUNK - 1 - i) if rev else i
    rel = idx[:, None] - idx[None, :]
    dm = np.where(rel >= 0, np.exp(lg[:, None, None] * np.maximum(rel, 0.0)), 0.0).astype(np.float32)
    qd = np.exp(lg[:, None] * (idx + 1.0)).astype(np.float32)
    kd = np.exp(lg[:, None] * (CHUNK - 1.0 - idx)).astype(np.float32)
    qd = np.broadcast_to(qd[:, :, None], (N_HEADS, CHUNK, HEAD_DIM)).copy()
    kd = np.broadcast_to(kd[:, :, None], (N_HEADS, CHUNK, HEAD_DIM)).copy()
    return jnp.asarray(dm), jnp.asarray(qd), jnp.asarray(kd)


def _ret_kernel(lay, rev, *refs):
    if rev:
        q_ref, k_ref, v_ref, cos_ref, sin_ref, dm_ref, qd_ref, kd_ref, out_ref, s_ref, qt_ref, kt_ref, o_ref = refs
        z_ref = ob_ref = gain_ref = None
    else:
        (q_ref, k_ref, v_ref, cos_ref, sin_ref, dm_ref, qd_ref, kd_ref, z_ref, ob_ref, gain_ref,
         out_ref, s_ref, qt_ref, kt_ref, o_ref) = refs
    tb = lay.tb
    nc = tb // CHUNK
    chunk_decay = [float(np.exp(np.float32(v) * CHUNK)) for v in _ret_log_gamma(rev)]
    _reset_state(lay, rev, s_ref)

    cos = cos_ref[...]
    sin = sin_ref[...]
    for h in range(N_HEADS):
        sl = slice(h * HEAD_DIM, (h + 1) * HEAD_DIM)
        qh = q_ref[:, sl]
        kh = k_ref[:, sl]
        qt_ref[:, sl] = (qh * cos + pltpu.roll(qh, HEAD_DIM // 2, 1) * sin) * QK_SCALE
        kt_ref[:, sl] = kh * cos + pltpu.roll(kh, HEAD_DIM // 2, 1) * sin

    def chunk(ci, carry):
        c = (nc - 1 - ci) if rev else ci
        rows = pl.ds(pl.multiple_of(c * CHUNK, CHUNK), CHUNK)
        for h in range(N_HEADS):
            sl = slice(h * HEAD_DIM, (h + 1) * HEAD_DIM)
            qc = qt_ref[rows, sl]
            kc = kt_ref[rows, sl]
            vc = v_ref[rows, sl]
            state = s_ref[h]
            scores = _dot_nt(qc, kc) * dm_ref[h]
            o_ref[rows, sl] = _dot(scores, vc) + _dot(qc * qd_ref[h], state)
            s_ref[h] = chunk_decay[h] * state + _dot_tn(kc * kd_ref[h], vc)
        return carry

    lax.fori_loop(0, nc, chunk, 0)
    _finish(rev, o_ref, ob_ref, z_ref, gain_ref, out_ref)


def _mixer_b(proj, cos2, sin2, o_bwd, gain, lay, rev):
    tb, nb = lay.tb, lay.nb
    bi = _blk_index(lay, rev)

    def cur(cb):
        return pl.BlockSpec((tb, W_GROUP), lambda j: (bi(j), cb))

    rot = pl.BlockSpec((tb, HEAD_DIM), lambda j: (lay.pos_block(bi(j)), 0))
    dm, qd, kd = _ret_consts(rev)
    in_specs = [cur(CB_B_Q), cur(CB_B_K), cur(CB_B_V), rot, rot,
                pl.BlockSpec((N_HEADS, CHUNK, CHUNK), lambda j: (0, 0, 0)),
                pl.BlockSpec((N_HEADS, CHUNK, HEAD_DIM), lambda j: (0, 0, 0)),
                pl.BlockSpec((N_HEADS, CHUNK, HEAD_DIM), lambda j: (0, 0, 0))]
    args = [proj, proj, proj, cos2, sin2, dm, qd, kd]
    if not rev:
        in_specs += [cur(CB_B_Z), pl.BlockSpec((tb, W_GROUP), lambda j: (bi(j), 0)),
                     pl.BlockSpec((1, W_GROUP), lambda j: (0, 0))]
        args += [proj, o_bwd, gain.reshape(1, W_GROUP)]
    return pl.pallas_call(
        functools.partial(_ret_kernel, lay, rev),
        grid=(nb,),
        in_specs=in_specs,
        out_specs=pl.BlockSpec((tb, W_GROUP), lambda j: (bi(j), 0)),
        out_shape=jax.ShapeDtypeStruct((lay.n, W_GROUP), F32 if rev else BF16),
        scratch_shapes=[pltpu.VMEM((N_HEADS, HEAD_DIM, HEAD_DIM), F32),
                        pltpu.VMEM((tb, W_GROUP), F32), pltpu.VMEM((tb, W_GROUP), F32),
                        pltpu.VMEM((tb, W_GROUP), F32)],
        compiler_params=_cparams(("arbitrary",)),
        name="mixer_b_retention_bwd" if rev else "mixer_b_retention_fwd",
    )(*args)


PAD_C = SUB


def _hgrn_kernel(lay, rev, *refs):
    if rev:
        (q_ref, f_ref, v_ref, llb_ref, l1lb_ref, out_ref,
         st_ref, qs_ref, g_ref, kk_ref, o_ref, kp_ref, bp_ref, vp_ref) = refs
        z_ref = ob_ref = gain_ref = None
    else:
        (q_ref, f_ref, v_ref, llb_ref, l1lb_ref, z_ref, ob_ref, gain_ref, out_ref,
         st_ref, qs_ref, g_ref, kk_ref, o_ref, kp_ref, bp_ref, vp_ref) = refs
    tb = lay.tb
    nc = tb // CHUNK
    nsub = CHUNK // SUB
    _reset_state(lay, rev, st_ref)

    qs_ref[...] = _silu(q_ref[...]) * QK_SCALE
    a = llb_ref[...]
    b = l1lb_ref[...] + _log_sigmoid(f_ref[...])
    g = jnp.maximum(a, b) + jnp.log1p(jnp.exp(-jnp.abs(a - b)))
    g_ref[...] = g
    kk_ref[...] = 1.0 - jnp.exp(g)

    zpad = jnp.zeros((PAD_C, W_GROUP), F32)
    for pref in (kp_ref, bp_ref, vp_ref):
        pref[0:PAD_C, :] = zpad
        pref[PAD_C + CHUNK:2 * PAD_C + CHUNK, :] = zpad

    _, _, incl, _ = _causal_masks(rev, CHUNK)
    cum_mat = jnp.where(incl, 1.0, 0.0)
    ones_mat = jnp.ones((HEAD_DIM, HEAD_DIM), BF16)
    sub_pos = lax.broadcasted_iota(jnp.int32, (CHUNK, HEAD_DIM), 0) % SUB

    def chunk(ci, carry):
        c = (nc - 1 - ci) if rev else ci
        rows = pl.ds(pl.multiple_of(c * CHUNK, CHUNK), CHUNK)
        bcum = _dot_exact_lhs(cum_mat, g_ref[rows, :])
        kp_ref[PAD_C:PAD_C + CHUNK, :] = kk_ref[rows, :]
        bp_ref[PAD_C:PAD_C + CHUNK, :] = bcum
        vp_ref[PAD_C:PAD_C + CHUNK, :] = v_ref[rows, :]
        for h in range(N_HEADS):
            sl = slice(h * HEAD_DIM, (h + 1) * HEAD_DIM)
            qc = qs_ref[rows, sl]
            kc = kp_ref[PAD_C:PAD_C + CHUNK, sl]
            bc = bp_ref[PAD_C:PAD_C + CHUNK, sl]
            vc = vp_ref[PAD_C:PAD_C + CHUNK, sl]
            state_t = st_ref[h]
            o = _dot_nt(qc * jnp.exp(bc), state_t)
            for lag in range(SUB):
                off = PAD_C + (lag if rev else -lag)
                ksh = kp_ref[off:off + CHUNK, sl]
                bsh = bp_ref[off:off + CHUNK, sl]
                vsh = vp_ref[off:off + CHUNK, sl]
                valid = (sub_pos + lag < SUB) if rev else (sub_pos >= lag)
                term = qc * ksh * jnp.exp(jnp.where(valid, bc - bsh, NEG_BIG))
                w = jnp.dot(term.astype(BF16), ones_mat, preferred_element_type=F32)
                o = o + w * vsh
            parts = [jnp.zeros((SUB, HEAD_DIM), F32)]
            for blk in range(1, nsub):
                if rev:
                    i0, i1 = (blk - 1) * SUB, blk * SUB
                    j0, j1 = blk * SUB, CHUNK
                    bref = bc[j0:j0 + 1, :]
                else:
                    i0, i1 = blk * SUB, (blk + 1) * SUB
                    j0, j1 = 0, blk * SUB
                    bref = bc[i0 - 1:i0, :]
                qh = qc[i0:i1, :] * jnp.exp(bc[i0:i1, :] - bref)
                kh = kc[j0:j1, :] * jnp.exp(bref - bc[j0:j1, :])
                parts.append(_dot(_dot_nt(qh, kh), vc[j0:j1, :]))
            if rev:
                parts = parts[1:] + parts[:1]
            o_ref[rows, sl] = o + jnp.concatenate(parts, axis=0)
            b_last = bc[0:1, :] if rev else bc[CHUNK - 1:CHUNK, :]
            st_ref[h] = state_t * jnp.exp(b_last) + _dot_tn(vc, kc * jnp.exp(b_last - bc))
        return carry

    lax.fori_loop(0, nc, chunk, 0)
    _finish(rev, o_ref, ob_ref, z_ref, gain_ref, out_ref)


def _mixer_c(proj, log_lb, log1m_lb, o_bwd, gain, lay, rev):
    tb, nb = lay.tb, lay.nb
    bi = _blk_index(lay, rev)

    def cur(cb):
        return pl.BlockSpec((tb, W_GROUP), lambda j: (bi(j), cb))

    vec = pl.BlockSpec((1, W_GROUP), lambda j: (0, 0))
    d = 1 if rev else 0
    in_specs = [cur(CB_C_Q), cur(CB_C_FB if rev else CB_C_FF), cur(CB_C_I), vec, vec]
    args = [proj, proj, proj, log_lb[d].reshape(1, W_GROUP), log1m_lb[d].reshape(1, W_GROUP)]
    if not rev:
        in_specs += [cur(CB_C_Z), pl.BlockSpec((tb, W_GROUP), lambda j: (bi(j), 0)), vec]
        args += [proj, o_bwd, gain.reshape(1, W_GROUP)]
    big = pltpu.VMEM((tb, W_GROUP), F32)
    pad = pltpu.VMEM((CHUNK + 2 * PAD_C, W_GROUP), F32)
    return pl.pallas_call(
        functools.partial(_hgrn_kernel, lay, rev),
        grid=(nb,),
        in_specs=in_specs,
        out_specs=pl.BlockSpec((tb, W_GROUP), lambda j: (bi(j), 0)),
        out_shape=jax.ShapeDtypeStruct((lay.n, W_GROUP), F32 if rev else BF16),
        scratch_shapes=[pltpu.VMEM((N_HEADS, HEAD_DIM, HEAD_DIM), F32), big, big, big, big, pad, pad, pad],
        compiler_params=_cparams(("arbitrary",)),
        name="mixer_c_hgrn2_bwd" if rev else "mixer_c_hgrn2_fwd",
    )(*args)


def _unit_tri_inverse(a_mat, strict, row, col):
    c = a_mat.shape[0]
    eye = jnp.where(row == col, 1.0, 0.0)

    def same_block(size):
        return (row // size) == (col // size)

    base = 8
    n = jnp.where(same_block(base), -a_mat, 0.0)
    x = eye + n
    n2 = _dot(n, n)
    x = x + _dot(x, n2)
    n4 = _dot(n2, n2)
    x = x + _dot(x, n4)
    size = base
    while size < c:
        off = jnp.where(same_block(2 * size) & jnp.logical_not(same_block(size)), a_mat, 0.0)
        x = x - _dot(_dot(x, off), x)
        size *= 2
    return x


def _delta_kernel(lay, rev, *refs):
    if rev:
        (q_ref, k_ref, v_ref, qp_ref, kp_ref, vp_ref, qn_ref, kn_ref, vn_ref, sm_ref,
         cw_ref, alog_ref, dtb_ref, out_ref,
         s_ref, qc_ref, kc_ref, vc_ref, g_ref, beta_ref, o_ref, x_ref) = refs
        z_ref = ob_ref = gain_ref = None
    else:
        (q_ref, k_ref, v_ref, qp_ref, kp_ref, vp_ref, qn_ref, kn_ref, vn_ref, sm_ref,
         cw_ref, alog_ref, dtb_ref, z_ref, ob_ref, gain_ref, out_ref,
         s_ref, qc_ref, kc_ref, vc_ref, g_ref, beta_ref, o_ref, x_ref) = refs
    tb = lay.tb
    nc = tb // CHUNK
    blk = _blk_index(lay, rev)(pl.program_id(0))
    _reset_state(lay, rev, s_ref)
    keep_prev = jnp.where(lay.seq_start(blk), 0.0, 1.0)
    keep_next = jnp.where(lay.seq_end(blk), 0.0, 1.0)
    half = DN_CONV_WIDTH // 2

    for part, (c_ref, p_ref, n_ref, dst_ref) in enumerate(
            ((q_ref, qp_ref, qn_ref, qc_ref), (k_ref, kp_ref, kn_ref, kc_ref), (v_ref, vp_ref, vn_ref, vc_ref))):
        x_ref[0:HALO_D, :] = p_ref[...] * keep_prev
        x_ref[HALO_D:HALO_D + tb, :] = c_ref[...]
        x_ref[HALO_D + tb:2 * HALO_D + tb, :] = n_ref[...] * keep_next
        wsl = slice(part * W_GROUP, (part + 1) * W_GROUP)
        acc = None
        for k in range(DN_CONV_WIDTH):
            off = HALO_D - half + k
            t = cw_ref[k:k + 1, wsl] * x_ref[off:off + tb, :]
            acc = t if acc is None else acc + t
        y = _silu(acc)
        if part < 2:
            for h in range(N_HEADS):
                sl = slice(h * HEAD_DIM, (h + 1) * HEAD_DIM)
                yh = y[:, sl]
                yn = yh * lax.rsqrt(jnp.sum(yh * yh, axis=-1, keepdims=True) + NORM_EPS)
                dst_ref[:, sl] = yn * QK_SCALE if part == 0 else yn
        else:
            dst_ref[...] = y

    sm = sm_ref[...]
    g_ref[...] = -jnp.exp(alog_ref[...]) * _softplus(sm + dtb_ref[...])
    beta_ref[...] = _sigmoid(sm)

    row, col, incl, strict = _causal_masks(rev, CHUNK)
    cum_mat = jnp.where(incl, 1.0, 0.0)
    eye_rows = jnp.where(lax.broadcasted_iota(jnp.int32, (8, SMALL_W), 0)
                         == lax.broadcasted_iota(jnp.int32, (8, SMALL_W), 1), 1.0, 0.0).astype(BF16)
    lane0 = N_HEADS if rev else 0

    def chunk(ci, carry):
        c = (nc - 1 - ci) if rev else ci
        rows = pl.ds(pl.multiple_of(c * CHUNK, CHUNK), CHUNK)
        cum = _dot_exact_lhs(cum_mat, g_ref[rows, :])
        cum_t = None
        sel = pltpu.roll(cum, SMALL_W - lane0, 1) if lane0 else cum
        for part in _split3(sel):
            t = lax.dot_general(eye_rows, part, (((1,), (1,)), ((), ())), preferred_element_type=F32)
            cum_t = t if cum_t is None else cum_t + t
        beta_all = beta_ref[rows, :]
        for h in range(N_HEADS):
            sl = slice(h * HEAD_DIM, (h + 1) * HEAD_DIM)
            qc = qc_ref[rows, sl]
            kc = kc_ref[rows, sl]
            vc = vc_ref[rows, sl]
            ci_col = cum[:, lane0 + h:lane0 + h + 1]
            cj_row = cum_t[h:h + 1, :]
            beta = beta_all[:, 2 * N_HEADS + lane0 + h:2 * N_HEADS + lane0 + h + 1]
            decay = jnp.exp(jnp.where(incl, ci_col - cj_row, NEG_BIG))
            k_beta = kc * beta
            a_mat = jnp.where(strict, _dot_nt(k_beta, kc) * decay, 0.0)
            t_inv = _unit_tri_inverse(a_mat, strict, row, col)
            e_ci = jnp.exp(ci_col)
            u = _dot(t_inv, vc * beta)
            w = _dot(t_inv, k_beta * e_ci)
            qk = _dot_nt(qc, kc) * decay
            c_last = ci_col[0:1, :] if rev else ci_col[CHUNK - 1:CHUNK, :]
            state = s_ref[h]
            v_new = u - _dot(w, state)
            o_ref[rows, sl] = _dot(qc * e_ci, state) + _dot(qk, v_new)
            s_ref[h] = state * jnp.exp(c_last) + _dot_tn(kc * jnp.exp(c_last - ci_col), v_new)
        return carry

    lax.fori_loop(0, nc, chunk, 0)
    _finish(rev, o_ref, ob_ref, z_ref, gain_ref, out_ref)


def _mixer_d(proj, conv_w, a_log, dt_bias, o_bwd, gain, lay, rev):
    tb, nb = lay.tb, lay.nb
    bi = _blk_index(lay, rev)
    hb = tb // HALO_D
    n_halo = lay.n // HALO_D

    def cur(cb):
        return pl.BlockSpec((tb, W_GROUP), lambda j: (bi(j), cb))

    def prev(cb):
        return pl.BlockSpec((HALO_D, W_GROUP), lambda j: (jnp.maximum(bi(j) * hb - 1, 0), cb))

    def nxt(cb):
        return pl.BlockSpec((HALO_D, W_GROUP), lambda j: (jnp.minimum((bi(j) + 1) * hb, n_halo - 1), cb))

    small = pl.BlockSpec((1, SMALL_W), lambda j: (0, 0))
    in_specs = [cur(CB_D_Q), cur(CB_D_K), cur(CB_D_V), prev(CB_D_Q), prev(CB_D_K), prev(CB_D_V),
                nxt(CB_D_Q), nxt(CB_D_K), nxt(CB_D_V),
                pl.BlockSpec((tb, SMALL_W), lambda j: (bi(j), CB_SMALL)),
                pl.BlockSpec((DN_CONV_WIDTH, 3 * W_GROUP), lambda j: (0, 0)), small, small]
    pad_lanes = SMALL_W - 2 * N_HEADS
    alog_row = jnp.pad(a_log.reshape(1, 2 * N_HEADS), ((0, 0), (0, pad_lanes)))
    dtb_row = jnp.pad(dt_bias.reshape(1, 2 * N_HEADS), ((0, 0), (0, pad_lanes)))
    args = [proj] * 10 + [conv_w, alog_row, dtb_row]
    if not rev:
        in_specs += [cur(CB_D_Z), pl.BlockSpec((tb, W_GROUP), lambda j: (bi(j), 0)),
                     pl.BlockSpec((1, W_GROUP), lambda j: (0, 0))]
        args += [proj, o_bwd, gain.reshape(1, W_GROUP)]
    big = pltpu.VMEM((tb, W_GROUP), F32)
    sml = pltpu.VMEM((tb, SMALL_W), F32)
    return pl.pallas_call(
        functools.partial(_delta_kernel, lay, rev),
        grid=(nb,),
        in_specs=in_specs,
        out_specs=pl.BlockSpec((tb, W_GROUP), lambda j: (bi(j), 0)),
        out_shape=jax.ShapeDtypeStruct((lay.n, W_GROUP), F32 if rev else BF16),
        scratch_shapes=[pltpu.VMEM((N_HEADS, HEAD_DIM, HEAD_DIM), F32), big, big, big, sml, sml, big,
                        pltpu.VMEM((tb + 2 * HALO_D, W_GROUP), F32)],
        compiler_params=_cparams(("arbitrary",)),
        name="mixer_d_deltanet_bwd" if rev else "mixer_d_deltanet_fwd",
    )(*args)


def _regroup_w_in(w):
    d_qkv_end = 15 * W_GROUP
    small = w[:, d_qkv_end:d_qkv_end + 4 * N_HEADS]
    d_z = w[:, d_qkv_end + 4 * N_HEADS:]
    pad = jnp.zeros((w.shape[0], SMALL_W - 4 * N_HEADS), w.dtype)
    return jnp.concatenate([w[:, :d_qkv_end], d_z, small, pad], axis=1).astype(BF16)


def _rotary_tables(t_max):
    half = HEAD_DIM // 2
    inv = 1.0 / (ROPE_BASE ** (jnp.arange(half, dtype=F32) / half))
    ang = jnp.arange(t_max, dtype=F32)[:, None] * inv[None, :]
    cos, sin = jnp.cos(ang), jnp.sin(ang)
    return jnp.concatenate([cos, cos], axis=-1), jnp.concatenate([-sin, sin], axis=-1)


def _pick_tile(t_prompt, t_sample, want):
    tile = want
    while t_prompt % tile or t_sample % tile:
        tile //= 2
    return tile


def kernel(x_prompt, x_sample, c_prompt, c_sample, ada_w, ada_b, norm_g, w_in, conv_a_w, conv_a_b, ln_a_g, ln_a_b,
           ret_norm_g, hgrn_lb_logits, hgrn_norm_g, dn_conv_w, dn_a_log, dn_dt_bias, dn_norm_g, w_out, final_g):
    depth = w_in.shape[0]
    n_prompt, t_prompt, _ = x_prompt.shape
    n_sample, t_sample, _ = x_sample.shape
    tb = _pick_tile(t_prompt, t_sample, 256)
    tm_in = _pick_tile(t_prompt, t_sample, 1024)
    tm_out = _pick_tile(t_prompt, t_sample, 512)
    lay = _Layout(t_prompt, n_prompt, t_sample, n_sample, tb)

    x = jnp.concatenate([x_prompt.reshape(-1, D_MODEL), x_sample.reshape(-1, D_MODEL)], axis=0)
    c_all = jnp.concatenate([c_prompt, c_sample], axis=0)
    n_c = c_all.shape[0]
    c_all = jnp.pad(c_all, ((0, (-n_c) % 8), (0, 0)))
    mod = _modulation(c_all, ada_w, ada_b)

    lb = jnp.cumsum(jax.nn.softmax(hgrn_lb_logits.astype(F32), axis=0), axis=0)
    lb = lb - lb[:1]
    log_lb = jnp.log(lb)
    log1m_lb = jnp.log1p(-lb)
    cos2, sin2 = _rotary_tables(max(t_prompt, t_sample))

    for l in range(depth):
        mod_l = mod[l].reshape(mod.shape[1], 1, 3 * D_MODEL)
        proj = _in_proj(x, mod_l, norm_g[l], _regroup_w_in(w_in[l]), lay, tm_in)
        m_a = _mixer_a(proj, conv_a_w[l], conv_a_b[l], ln_a_g[l], ln_a_b[l], lay)
        ob = _mixer_b(proj, cos2, sin2, None, None, lay, True)
        m_b = _mixer_b(proj, cos2, sin2, ob, ret_norm_g[l], lay, False)
        oc = _mixer_c(proj, log_lb[l], log1m_lb[l], None, None, lay, True)
        m_c = _mixer_c(proj, log_lb[l], log1m_lb[l], oc, hgrn_norm_g[l], lay, False)
        od = _mixer_d(proj, dn_conv_w[l], dn_a_log[l], dn_dt_bias[l], None, None, lay, True)
        m_d = _mixer_d(proj, dn_conv_w[l], dn_a_log[l], dn_dt_bias[l], od, dn_norm_g[l], lay, False)
        x = _out_proj(x, (m_a, m_b, m_c, m_d), mod_l, w_out[l].astype(BF16), final_g, lay, tm_out,
                      final=(l == depth - 1))

    n_tp = n_prompt * t_prompt
    return (x[:n_tp].reshape(n_prompt, t_prompt, D_MODEL), x[n_tp:].reshape(n_sample, t_sample, D_MODEL))
```

```python
import functools
import math

import numpy as np
import jax
import jax.numpy as jnp
from jax import lax
from jax.experimental import pallas as pl
from jax.experimental.pallas import tpu as pltpu

F32 = jnp.float32
BF16 = jnp.bfloat16

D_MODEL = 1024
W_GROUP = 512
HEAD_DIM = 128
N_HEADS = 4
CONV_A_WIDTH = 31
DN_CONV_WIDTH = 5
CHUNK = 64
SUB = 16
ROPE_BASE = 10000.0
RET_DECAY_OFFSET = 5.0
NORM_EPS = 1e-6
QK_SCALE = HEAD_DIM ** -0.5
NEG_BIG = -1e30

N_COL_BLOCKS = 16
SMALL_W = 128
D_PROJ = N_COL_BLOCKS * W_GROUP + SMALL_W
PROJ_TN = 640
HALO_A = 16
HALO_D = 8
VMEM_LIMIT = 56 * 1024 * 1024

CB_A_VAL, CB_A_GLU, CB_A_Z = 0, 1, 2
CB_B_Q, CB_B_K, CB_B_V, CB_B_Z = 3, 4, 5, 6
CB_C_Q, CB_C_FF, CB_C_FB, CB_C_I, CB_C_Z = 7, 8, 9, 10, 11
CB_D_Q, CB_D_K, CB_D_V, CB_D_Z = 12, 13, 14, 15
CB_SMALL = N_COL_BLOCKS * W_GROUP // SMALL_W


def _dot(a, b):
    return jnp.dot(a.astype(BF16), b.astype(BF16), preferred_element_type=F32)


def _dot_nt(a, b):
    return lax.dot_general(a.astype(BF16), b.astype(BF16), (((1,), (1,)), ((), ())), preferred_element_type=F32)


def _dot_tn(a, b):
    return lax.dot_general(a.astype(BF16), b.astype(BF16), (((0,), (0,)), ((), ())), preferred_element_type=F32)


def _split2(x):
    hi = x.astype(BF16)
    lo = (x - hi.astype(F32)).astype(BF16)
    return hi, lo


def _split3(x):
    hi = x.astype(BF16)
    r = x - hi.astype(F32)
    mid = r.astype(BF16)
    lo = (r - mid.astype(F32)).astype(BF16)
    return hi, mid, lo


def _dot_exact_lhs(m, x):
    mb = m.astype(BF16)
    acc = None
    for part in _split3(x):
        t = jnp.dot(mb, part, preferred_element_type=F32)
        acc = t if acc is None else acc + t
    return acc


def _sigmoid(x):
    return 1.0 / (1.0 + jnp.exp(-x))


def _silu(x):
    return x * _sigmoid(x)


def _softplus(x):
    return jnp.maximum(x, 0.0) + jnp.log1p(jnp.exp(-jnp.abs(x)))


def _log_sigmoid(x):
    return jnp.minimum(x, 0.0) - jnp.log1p(jnp.exp(-jnp.abs(x)))


class _Layout:
    def __init__(self, t_prompt, n_prompt, t_sample, n_sample, tb):
        self.tp = t_prompt * n_prompt
        self.t_prompt = t_prompt
        self.ts = t_sample
        self.n = self.tp + t_sample * n_sample
        self.tb = tb
        assert t_prompt % tb == 0 and t_sample % tb == 0
        self.nb = self.n // tb

    def seq_start(self, blk):
        s = blk * self.tb
        return jnp.where(s < self.tp, s % self.t_prompt == 0, (s - self.tp) % self.ts == 0)

    def seq_end(self, blk):
        e = (blk + 1) * self.tb
        return jnp.where(e <= self.tp, e % self.t_prompt == 0, (e - self.tp) % self.ts == 0)

    def pos_block(self, blk):
        s = blk * self.tb
        return jnp.where(s < self.tp, (s % self.t_prompt) // self.tb, ((s - self.tp) % self.ts) // self.tb)

    def batch_index(self, blk, rows):
        s = blk * rows
        return jnp.where(s < self.tp, s // self.t_prompt, self.tp // self.t_prompt + (s - self.tp) // self.ts)


def _cparams(sem):
    return pltpu.CompilerParams(dimension_semantics=sem, vmem_limit_bytes=VMEM_LIMIT)


def _mod_kernel(c_ref, w_ref, b_ref, o_ref):
    c = c_ref[...]
    o_ref[0] = jnp.dot(_silu(c), w_ref[0], preferred_element_type=F32,
                       precision=lax.Precision.HIGHEST) + b_ref[0]


def _modulation(c_all, ada_w, ada_b):
    depth = ada_w.shape[0]
    nb = c_all.shape[0]
    return pl.pallas_call(
        _mod_kernel,
        grid=(depth,),
        in_specs=[pl.BlockSpec((nb, D_MODEL), lambda l: (0, 0)),
                  pl.BlockSpec((1, D_MODEL, 3 * D_MODEL), lambda l: (l, 0, 0)),
                  pl.BlockSpec((1, 1, 3 * D_MODEL), lambda l: (l, 0, 0))],
        out_specs=pl.BlockSpec((1, nb, 3 * D_MODEL), lambda l: (l, 0, 0)),
        out_shape=jax.ShapeDtypeStruct((depth, nb, 3 * D_MODEL), F32),
        compiler_params=_cparams(("arbitrary",)),
        name="adaln_modulation",
    )(c_all, ada_w, ada_b.reshape(depth, 1, 3 * D_MODEL))


def _in_proj_kernel(x_ref, mod_ref, g_ref, w_ref, o_ref, h_ref):
    @pl.when(pl.program_id(1) == 0)
    def _():
        x = x_ref[...]
        y = x * lax.rsqrt(jnp.mean(x * x, axis=-1, keepdims=True) + NORM_EPS) * g_ref[...]
        shift = mod_ref[0, :, 0:D_MODEL]
        scale = mod_ref[0, :, D_MODEL:2 * D_MODEL]
        h_ref[...] = (y * (1.0 + scale) + shift).astype(BF16)

    o_ref[...] = jnp.dot(h_ref[...], w_ref[...], preferred_element_type=F32)


def _in_proj(x, mod, norm_g, w_perm, lay, tm):
    n = x.shape[0]
    lay_m = _Layout(lay.t_prompt, lay.tp // lay.t_prompt, lay.ts, (lay.n - lay.tp) // lay.ts, tm)
    return pl.pallas_call(
        _in_proj_kernel,
        grid=(n // tm, D_PROJ // PROJ_TN),
        in_specs=[pl.BlockSpec((tm, D_MODEL), lambda i, j: (i, 0)),
                  pl.BlockSpec((1, 1, 3 * D_MODEL), lambda i, j: (lay_m.batch_index(i, tm), 0, 0)),
                  pl.BlockSpec((1, D_MODEL), lambda i, j: (0, 0)),
                  pl.BlockSpec((D_MODEL, PROJ_TN), lambda i, j: (0, j))],
        out_specs=pl.BlockSpec((tm, PROJ_TN), lambda i, j: (i, j)),
        out_shape=jax.ShapeDtypeStruct((n, D_PROJ), F32),
        scratch_shapes=[pltpu.VMEM((tm, D_MODEL), BF16)],
        compiler_params=_cparams(("arbitrary", "arbitrary")),
        name="in_proj",
    )(x, mod, norm_g.reshape(1, D_MODEL), w_perm)


def _out_proj_kernel(final, x_ref, ma_ref, mb_ref, mc_ref, md_ref, mod_ref, w_ref, fg_ref, o_ref):
    acc = jnp.dot(ma_ref[...], w_ref[0:W_GROUP, :], preferred_element_type=F32)
    acc += jnp.dot(mb_ref[...], w_ref[W_GROUP:2 * W_GROUP, :], preferred_element_type=F32)
    acc += jnp.dot(mc_ref[...], w_ref[2 * W_GROUP:3 * W_GROUP, :], preferred_element_type=F32)
    acc += jnp.dot(md_ref[...], w_ref[3 * W_GROUP:4 * W_GROUP, :], preferred_element_type=F32)
    gate = mod_ref[0, :, 2 * D_MODEL:3 * D_MODEL]
    y = x_ref[...] + gate * acc
    if final:
        y = y * lax.rsqrt(jnp.mean(y * y, axis=-1, keepdims=True) + NORM_EPS) * fg_ref[...]
    o_ref[...] = y


def _out_proj(x, mixed, mod, w_out_bf, final_g, lay, tm, final):
    n = x.shape[0]
    lay_m = _Layout(lay.t_prompt, lay.tp // lay.t_prompt, lay.ts, (lay.n - lay.tp) // lay.ts, tm)
    mspec = pl.BlockSpec((tm, W_GROUP), lambda i: (i, 0))
    return pl.pallas_call(
        functools.partial(_out_proj_kernel, final),
        grid=(n // tm,),
        in_specs=[pl.BlockSpec((tm, D_MODEL), lambda i: (i, 0)), mspec, mspec, mspec, mspec,
                  pl.BlockSpec((1, 1, 3 * D_MODEL), lambda i: (lay_m.batch_index(i, tm), 0, 0)),
                  pl.BlockSpec((4 * W_GROUP, D_MODEL), lambda i: (0, 0)),
                  pl.BlockSpec((1, D_MODEL), lambda i: (0, 0))],
        out_specs=pl.BlockSpec((tm, D_MODEL), lambda i: (i, 0)),
        out_shape=jax.ShapeDtypeStruct((n, D_MODEL), F32),
        compiler_params=_cparams(("arbitrary",)),
        name="out_proj",
    )(x, *mixed, mod, w_out_bf, final_g.reshape(1, D_MODEL))


A_ROWS = 32


def _mixer_a_kernel(lay, val_ref, glu_ref, z_ref, vp_ref, gp_ref, vn_ref, gn_ref,
                    cw_ref, cb_ref, lg_ref, lb_ref, o_ref, u_ref):
    tb = lay.tb
    blk = pl.program_id(0)
    keep_prev = jnp.where(lay.seq_start(blk), 0.0, 1.0)
    keep_next = jnp.where(lay.seq_end(blk), 0.0, 1.0)
    u_ref[0:HALO_A, :] = vp_ref[...] * _sigmoid(gp_ref[...]) * keep_prev
    u_ref[HALO_A:HALO_A + tb, :] = val_ref[...] * _sigmoid(glu_ref[...])
    u_ref[HALO_A + tb:2 * HALO_A + tb, :] = vn_ref[...] * _sigmoid(gn_ref[...]) * keep_next
    half = CONV_A_WIDTH // 2

    for t in range(tb // A_ROWS):
        r0 = t * A_ROWS
        acc = jnp.zeros((A_ROWS, W_GROUP), F32) + cb_ref[...]
        for k in range(CONV_A_WIDTH):
            off = r0 + HALO_A - half + k
            acc = acc + cw_ref[k:k + 1, :] * u_ref[off:off + A_ROWS, :]
        mu = jnp.mean(acc, axis=-1, keepdims=True)
        xc = acc - mu
        var = jnp.mean(xc * xc, axis=-1, keepdims=True)
        y = xc * lax.rsqrt(var + NORM_EPS) * lg_ref[...] + lb_ref[...]
        o_ref[r0:r0 + A_ROWS, :] = (_silu(y) * _silu(z_ref[r0:r0 + A_ROWS, :])).astype(BF16)


def _mixer_a(proj, conv_w, conv_b, ln_g, ln_b, lay):
    tb, nb = lay.tb, lay.nb
    hb = tb // HALO_A
    n_halo = lay.n // HALO_A

    def cur(cb):
        return pl.BlockSpec((tb, W_GROUP), lambda i: (i, cb))

    def prev(cb):
        return pl.BlockSpec((HALO_A, W_GROUP), lambda i: (jnp.maximum(i * hb - 1, 0), cb))

    def nxt(cb):
        return pl.BlockSpec((HALO_A, W_GROUP), lambda i: (jnp.minimum((i + 1) * hb, n_halo - 1), cb))

    vec = pl.BlockSpec((1, W_GROUP), lambda i: (0, 0))
    return pl.pallas_call(
        functools.partial(_mixer_a_kernel, lay),
        grid=(nb,),
        in_specs=[cur(CB_A_VAL), cur(CB_A_GLU), cur(CB_A_Z), prev(CB_A_VAL), prev(CB_A_GLU),
                  nxt(CB_A_VAL), nxt(CB_A_GLU),
                  pl.BlockSpec((CONV_A_WIDTH, W_GROUP), lambda i: (0, 0)), vec, vec, vec],
        out_specs=pl.BlockSpec((tb, W_GROUP), lambda i: (i, 0)),
        out_shape=jax.ShapeDtypeStruct((lay.n, W_GROUP), BF16),
        scratch_shapes=[pltpu.VMEM((tb + 2 * HALO_A, W_GROUP), F32)],
        compiler_params=_cparams(("arbitrary",)),
        name="mixer_a_conv",
    )(proj, proj, proj, proj, proj, proj, proj, conv_w, conv_b.reshape(1, W_GROUP),
      ln_g.reshape(1, W_GROUP), ln_b.reshape(1, W_GROUP))


def _blk_index(lay, rev):
    nb = lay.nb
    return (lambda j: nb - 1 - j) if rev else (lambda j: j)


def _reset_state(lay, rev, s_ref):
    blk = _blk_index(lay, rev)(pl.program_id(0))
    boundary = lay.seq_end(blk) if rev else lay.seq_start(blk)

    @pl.when(boundary)
    def _():
        s_ref[...] = jnp.zeros(s_ref.shape, s_ref.dtype)


def _finish(rev, o_ref, ob_ref, z_ref, gain_ref, out_ref):
    if rev:
        out_ref[...] = o_ref[...]
        return
    for h in range(N_HEADS):
        sl = slice(h * HEAD_DIM, (h + 1) * HEAD_DIM)
        o = o_ref[:, sl] + ob_ref[:, sl]
        y = o * lax.rsqrt(jnp.mean(o * o, axis=-1, keepdims=True) + NORM_EPS) * gain_ref[:, sl]
        out_ref[:, sl] = (y * _silu(z_ref[:, sl])).astype(BF16)


def _causal_masks(rev, c):
    row = lax.broadcasted_iota(jnp.int32, (c, c), 0)
    col = lax.broadcasted_iota(jnp.int32, (c, c), 1)
    incl = (row <= col) if rev else (row >= col)
    strict = (row < col) if rev else (row > col)
    return row, col, incl, strict


def _ret_log_gamma(rev):
    lg = np.log1p(-np.exp2(-RET_DECAY_OFFSET - np.arange(N_HEADS, dtype=np.float64)))
    return lg[::-1].copy() if rev else lg


def _ret_consts(rev):
    lg = _ret_log_gamma(rev).astype(np.float32)
    i = np.arange(CHUNK, dtype=np.float32)
    idx = (CHUNK - 1 - i) if rev else i
    rel = idx[:, None] - idx[None, :]
    dm = np.where(rel >= 0, np.exp(lg[:, None, None] * np.maximum(rel, 0.0)), 0.0).astype(np.float32)
    qd = np.exp(lg[:, None] * (idx + 1.0)).astype(np.float32)
    kd = np.exp(lg[:, None] * (CHUNK - 1.0 - idx)).astype(np.float32)
    qd = np.broadcast_to(qd[:, :, None], (N_HEADS, CHUNK, HEAD_DIM)).copy()
    kd = np.broadcast_to(kd[:, :, None], (N_HEADS, CHUNK, HEAD_DIM)).copy()
    return jnp.asarray(dm), jnp.asarray(qd), jnp.asarray(kd)


def _ret_kernel(lay, rev, *refs):
    if rev:
        q_ref, k_ref, v_ref, cos_ref, sin_ref, dm_ref, qd_ref, kd_ref, out_ref, s_ref, qt_ref, kt_ref, o_ref = refs
        z_ref = ob_ref = gain_ref = None
    else:
        (q_ref, k_ref, v_ref, cos_ref, sin_ref, dm_ref, qd_ref, kd_ref, z_ref, ob_ref, gain_ref,
         out_ref, s_ref, qt_ref, kt_ref, o_ref) = refs
    tb = lay.tb
    nc = tb // CHUNK
    chunk_decay = [float(np.exp(np.float32(v) * CHUNK)) for v in _ret_log_gamma(rev)]
    _reset_state(lay, rev, s_ref)

    cos = cos_ref[...]
    sin = sin_ref[...]
    for h in range(N_HEADS):
        sl = slice(h * HEAD_DIM, (h + 1) * HEAD_DIM)
        qh = q_ref[:, sl]
        kh = k_ref[:, sl]
        qt_ref[:, sl] = (qh * cos + pltpu.roll(qh, HEAD_DIM // 2, 1) * sin) * QK_SCALE
        kt_ref[:, sl] = kh * cos + pltpu.roll(kh, HEAD_DIM // 2, 1) * sin

    def chunk(ci, carry):
        c = (nc - 1 - ci) if rev else ci
        rows = pl.ds(pl.multiple_of(c * CHUNK, CHUNK), CHUNK)
        for h in range(N_HEADS):
            sl = slice(h * HEAD_DIM, (h + 1) * HEAD_DIM)
            qc = qt_ref[rows, sl]
            kc = kt_ref[rows, sl]
            vc = v_ref[rows, sl]
            state = s_ref[h]
            scores = _dot_nt(qc, kc) * dm_ref[h]
            o_ref[rows, sl] = _dot(scores, vc) + _dot(qc * qd_ref[h], state)
            s_ref[h] = chunk_decay[h] * state + _dot_tn(kc * kd_ref[h], vc)
        return carry

    lax.fori_loop(0, nc, chunk, 0)
    _finish(rev, o_ref, ob_ref, z_ref, gain_ref, out_ref)


def _mixer_b(proj, cos2, sin2, o_bwd, gain, lay, rev):
    tb, nb = lay.tb, lay.nb
    bi = _blk_index(lay, rev)

    def cur(cb):
        return pl.BlockSpec((tb, W_GROUP), lambda j: (bi(j), cb))

    rot = pl.BlockSpec((tb, HEAD_DIM), lambda j: (lay.pos_block(bi(j)), 0))
    dm, qd, kd = _ret_consts(rev)
    in_specs = [cur(CB_B_Q), cur(CB_B_K), cur(CB_B_V), rot, rot,
                pl.BlockSpec((N_HEADS, CHUNK, CHUNK), lambda j: (0, 0, 0)),
                pl.BlockSpec((N_HEADS, CHUNK, HEAD_DIM), lambda j: (0, 0, 0)),
                pl.BlockSpec((N_HEADS, CHUNK, HEAD_DIM), lambda j: (0, 0, 0))]
    args = [proj, proj, proj, cos2, sin2, dm, qd, kd]
    if not rev:
        in_specs += [cur(CB_B_Z), pl.BlockSpec((tb, W_GROUP), lambda j: (bi(j), 0)),
                     pl.BlockSpec((1, W_GROUP), lambda j: (0, 0))]
        args += [proj, o_bwd, gain.reshape(1, W_GROUP)]
    return pl.pallas_call(
        functools.partial(_ret_kernel, lay, rev),
        grid=(nb,),
        in_specs=in_specs,
        out_specs=pl.BlockSpec((tb, W_GROUP), lambda j: (bi(j), 0)),
        out_shape=jax.ShapeDtypeStruct((lay.n, W_GROUP), F32 if rev else BF16),
        scratch_shapes=[pltpu.VMEM((N_HEADS, HEAD_DIM, HEAD_DIM), F32),
                        pltpu.VMEM((tb, W_GROUP), F32), pltpu.VMEM((tb, W_GROUP), F32),
                        pltpu.VMEM((tb, W_GROUP), F32)],
        compiler_params=_cparams(("arbitrary",)),
        name="mixer_b_retention_bwd" if rev else "mixer_b_retention_fwd",
    )(*args)


PAD_C = SUB


def _hgrn_kernel(lay, rev, *refs):
    if rev:
        (q_ref, f_ref, v_ref, llb_ref, l1lb_ref, out_ref,
         st_ref, qs_ref, g_ref, kk_ref, o_ref, kp_ref, bp_ref, vp_ref) = refs
        z_ref = ob_ref = gain_ref = None
    else:
        (q_ref, f_ref, v_ref, llb_ref, l1lb_ref, z_ref, ob_ref, gain_ref, out_ref,
         st_ref, qs_ref, g_ref, kk_ref, o_ref, kp_ref, bp_ref, vp_ref) = refs
    tb = lay.tb
    nc = tb // CHUNK
    nsub = CHUNK // SUB
    _reset_state(lay, rev, st_ref)

    qs_ref[...] = _silu(q_ref[...]) * QK_SCALE
    a = llb_ref[...]
    b = l1lb_ref[...] + _log_sigmoid(f_ref[...])
    g = jnp.maximum(a, b) + jnp.log1p(jnp.exp(-jnp.abs(a - b)))
    g_ref[...] = g
    kk_ref[...] = 1.0 - jnp.exp(g)

    zpad = jnp.zeros((PAD_C, W_GROUP), F32)
    for pref in (kp_ref, bp_ref, vp_ref):
        pref[0:PAD_C, :] = zpad
        pref[PAD_C + CHUNK:2 * PAD_C + CHUNK, :] = zpad

    _, _, incl, _ = _causal_masks(rev, CHUNK)
    cum_mat = jnp.where(incl, 1.0, 0.0)
    ones_mat = jnp.ones((HEAD_DIM, HEAD_DIM), BF16)
    sub_pos = lax.broadcasted_iota(jnp.int32, (CHUNK, HEAD_DIM), 0) % SUB

    def chunk(ci, carry):
        c = (nc - 1 - ci) if rev else ci
        rows = pl.ds(pl.multiple_of(c * CHUNK, CHUNK), CHUNK)
        bcum = _dot_exact_lhs(cum_mat, g_ref[rows, :])
        kp_ref[PAD_C:PAD_C + CHUNK, :] = kk_ref[rows, :]
        bp_ref[PAD_C:PAD_C + CHUNK, :] = bcum
        vp_ref[PAD_C:PAD_C + CHUNK, :] = v_ref[rows, :]
        for h in range(N_HEADS):
            sl = slice(h * HEAD_DIM, (h + 1) * HEAD_DIM)
            qc = qs_ref[rows, sl]
            kc = kp_ref[PAD_C:PAD_C + CHUNK, sl]
            bc = bp_ref[PAD_C:PAD_C + CHUNK, sl]
            vc = vp_ref[PAD_C:PAD_C + CHUNK, sl]
            state_t = st_ref[h]
            o = _dot_nt(qc * jnp.exp(bc), state_t)
            for lag in range(SUB):
                off = PAD_C + (lag if rev else -lag)
                ksh = kp_ref[off:off + CHUNK, sl]
                bsh = bp_ref[off:off + CHUNK, sl]
                vsh = vp_ref[off:off + CHUNK, sl]
                valid = (sub_pos + lag < SUB) if rev else (sub_pos >= lag)
                term = qc * ksh * jnp.exp(jnp.where(valid, bc - bsh, NEG_BIG))
                w = jnp.dot(term.astype(BF16), ones_mat, preferred_element_type=F32)
                o = o + w * vsh
            parts = [jnp.zeros((SUB, HEAD_DIM), F32)]
            for blk in range(1, nsub):
                if rev:
                    i0, i1 = (blk - 1) * SUB, blk * SUB
                    j0, j1 = blk * SUB, CHUNK
                    bref = bc[j0:j0 + 1, :]
                else:
                    i0, i1 = blk * SUB, (blk + 1) * SUB
                    j0, j1 = 0, blk * SUB
                    bref = bc[i0 - 1:i0, :]
                qh = qc[i0:i1, :] * jnp.exp(bc[i0:i1, :] - bref)
                kh = kc[j0:j1, :] * jnp.exp(bref - bc[j0:j1, :])
                parts.append(_dot(_dot_nt(qh, kh), vc[j0:j1, :]))
            if rev:
                parts = parts[1:] + parts[:1]
            o_ref[rows, sl] = o + jnp.concatenate(parts, axis=0)
            b_last = bc[0:1, :] if rev else bc[CHUNK - 1:CHUNK, :]
            st_ref[h] = state_t * jnp.exp(b_last) + _dot_tn(vc, kc * jnp.exp(b_last - bc))
        return carry

    lax.fori_loop(0, nc, chunk, 0)
    _finish(rev, o_ref, ob_ref, z_ref, gain_ref, out_ref)


def _mixer_c(proj, log_lb, log1m_lb, o_bwd, gain, lay, rev):
    tb, nb = lay.tb, lay.nb
    bi = _blk_index(lay, rev)

    def cur(cb):
        return pl.BlockSpec((tb, W_GROUP), lambda j: (bi(j), cb))

    vec = pl.BlockSpec((1, W_GROUP), lambda j: (0, 0))
    d = 1 if rev else 0
    in_specs = [cur(CB_C_Q), cur(CB_C_FB if rev else CB_C_FF), cur(CB_C_I), vec, vec]
    args = [proj, proj, proj, log_lb[d].reshape(1, W_GROUP), log1m_lb[d].reshape(1, W_GROUP)]
    if not rev:
        in_specs += [cur(CB_C_Z), pl.BlockSpec((tb, W_GROUP), lambda j: (bi(j), 0)), vec]
        args += [proj, o_bwd, gain.reshape(1, W_GROUP)]
    big = pltpu.VMEM((tb, W_GROUP), F32)
    pad = pltpu.VMEM((CHUNK + 2 * PAD_C, W_GROUP), F32)
    return pl.pallas_call(
        functools.partial(_hgrn_kernel, lay, rev),
        grid=(nb,),
        in_specs=in_specs,
        out_specs=pl.BlockSpec((tb, W_GROUP), lambda j: (bi(j), 0)),
        out_shape=jax.ShapeDtypeStruct((lay.n, W_GROUP), F32 if rev else BF16),
        scratch_shapes=[pltpu.VMEM((N_HEADS, HEAD_DIM, HEAD_DIM), F32), big, big, big, big, pad, pad, pad],
        compiler_params=_cparams(("arbitrary",)),
        name="mixer_c_hgrn2_bwd" if rev else "mixer_c_hgrn2_fwd",
    )(*args)


def _unit_tri_inverses(a_mats, row, col):
    c = a_mats[0].shape[0]
    eye = jnp.where(row == col, 1.0, 0.0)

    def same_block(size):
        return (row // size) == (col // size)

    base = 8
    m_base = same_block(base)
    ns = [jnp.where(m_base, -a, 0.0).astype(BF16) for a in a_mats]
    n2s = [jnp.dot(n, n, preferred_element_type=F32) for n in ns]
    xs = [eye + n.astype(F32) for n in ns]
    xs = [x + _dot(x, n2) for x, n2 in zip(xs, n2s)]
    n4s = [_dot(n2, n2) for n2 in n2s]
    xs = [x + _dot(x, n4) for x, n4 in zip(xs, n4s)]
    size = base
    while size < c:
        m_off = same_block(2 * size) & jnp.logical_not(same_block(size))
        xbs = [x.astype(BF16) for x in xs]
        ts = [jnp.dot(xb, jnp.where(m_off, a, 0.0).astype(BF16), preferred_element_type=F32)
              for xb, a in zip(xbs, a_mats)]
        xs = [x - jnp.dot(t.astype(BF16), xb, preferred_element_type=F32) for x, t, xb in zip(xs, ts, xbs)]
        size *= 2
    return xs


def _delta_kernel(lay, rev, *refs):
    if rev:
        (q_ref, k_ref, v_ref, qp_ref, kp_ref, vp_ref, qn_ref, kn_ref, vn_ref, sm_ref,
         cw_ref, alog_ref, dtb_ref, out_ref,
         s_ref, qc_ref, kc_ref, vc_ref, g_ref, beta_ref, o_ref, x_ref,
         u_ref, wq_ref, kd_ref, qk_ref, rhs_ref, last_ref) = refs
        z_ref = ob_ref = gain_ref = None
    else:
        (q_ref, k_ref, v_ref, qp_ref, kp_ref, vp_ref, qn_ref, kn_ref, vn_ref, sm_ref,
         cw_ref, alog_ref, dtb_ref, z_ref, ob_ref, gain_ref, out_ref,
         s_ref, qc_ref, kc_ref, vc_ref, g_ref, beta_ref, o_ref, x_ref,
         u_ref, wq_ref, kd_ref, qk_ref, rhs_ref, last_ref) = refs
    tb = lay.tb
    nc = tb // CHUNK
    blk = _blk_index(lay, rev)(pl.program_id(0))
    _reset_state(lay, rev, s_ref)
    keep_prev = jnp.where(lay.seq_start(blk), 0.0, 1.0)
    keep_next = jnp.where(lay.seq_end(blk), 0.0, 1.0)
    half = DN_CONV_WIDTH // 2

    for part, (c_ref, p_ref, n_ref, dst_ref) in enumerate(
            ((q_ref, qp_ref, qn_ref, qc_ref), (k_ref, kp_ref, kn_ref, kc_ref), (v_ref, vp_ref, vn_ref, vc_ref))):
        x_ref[0:HALO_D, :] = p_ref[...] * keep_prev
        x_ref[HALO_D:HALO_D + tb, :] = c_ref[...]
        x_ref[HALO_D + tb:2 * HALO_D + tb, :] = n_ref[...] * keep_next
        wsl = slice(part * W_GROUP, (part + 1) * W_GROUP)
        acc = None
        for k in range(DN_CONV_WIDTH):
            off = HALO_D - half + k
            t = cw_ref[k:k + 1, wsl] * x_ref[off:off + tb, :]
            acc = t if acc is None else acc + t
        y = _silu(acc)
        if part < 2:
            for h in range(N_HEADS):
                sl = slice(h * HEAD_DIM, (h + 1) * HEAD_DIM)
                yh = y[:, sl]
                yn = yh * lax.rsqrt(jnp.sum(yh * yh, axis=-1, keepdims=True) + NORM_EPS)
                dst_ref[:, sl] = yn * QK_SCALE if part == 0 else yn
        else:
            dst_ref[...] = y

    sm = sm_ref[...]
    g_ref[...] = -jnp.exp(alog_ref[...]) * _softplus(sm + dtb_ref[...])
    beta_ref[...] = _sigmoid(sm)

    row, col, incl, strict = _causal_masks(rev, CHUNK)
    cum_mat = jnp.where(incl, 1.0, 0.0)
    eye_rows = jnp.where(lax.broadcasted_iota(jnp.int32, (8, SMALL_W), 0)
                         == lax.broadcasted_iota(jnp.int32, (8, SMALL_W), 1), 1.0, 0.0).astype(BF16)
    lane0 = N_HEADS if rev else 0

    a_mats = []
    for c in range(nc):
        rows = slice(c * CHUNK, (c + 1) * CHUNK)
        cum = _dot_exact_lhs(cum_mat, g_ref[rows, :])
        cum_t = None
        sel = pltpu.roll(cum, SMALL_W - lane0, 1) if lane0 else cum
        for part in _split3(sel):
            t = lax.dot_general(eye_rows, part, (((1,), (1,)), ((), ())), preferred_element_type=F32)
            cum_t = t if cum_t is None else cum_t + t
        beta_all = beta_ref[rows, :]
        for h in range(N_HEADS):
            sl = slice(h * HEAD_DIM, (h + 1) * HEAD_DIM)
            qc = qc_ref[rows, sl]
            kc = kc_ref[rows, sl]
            vc = vc_ref[rows, sl]
            ci_b = jnp.broadcast_to(cum[:, lane0 + h:lane0 + h + 1], (CHUNK, HEAD_DIM))
            cj_row = cum_t[h:h + 1, :]
            b_lane = 2 * N_HEADS + lane0 + h
            beta_b = jnp.broadcast_to(beta_all[:, b_lane:b_lane + 1], (CHUNK, HEAD_DIM))
            decay = jnp.exp(jnp.where(incl, ci_b[:, 0:CHUNK] - cj_row, NEG_BIG))
            k_beta = kc * beta_b
            raw = _dot_nt(jnp.concatenate([k_beta, qc], axis=0), kc)
            a_mats.append(jnp.where(strict, raw[0:CHUNK] * decay, 0.0))
            qk_ref[c * N_HEADS + h] = (raw[CHUNK:2 * CHUNK] * decay).astype(BF16)
            e_ci = jnp.exp(ci_b)
            rhs_ref[c * N_HEADS + h] = jnp.concatenate([vc * beta_b, k_beta * e_ci], axis=1).astype(BF16)
            wq_ref[c, CHUNK:2 * CHUNK, sl] = (qc * e_ci).astype(BF16)
            c_last = ci_b[0:1, :] if rev else ci_b[CHUNK - 1:CHUNK, :]
            kd_ref[rows, sl] = (kc * jnp.exp(c_last - ci_b)).astype(BF16)
            last_ref[c * N_HEADS + h] = jnp.broadcast_to(jnp.exp(c_last), (8, HEAD_DIM))

    t_invs = _unit_tri_inverses(a_mats, row, col)
    for idx, t_inv in enumerate(t_invs):
        c, h = divmod(idx, N_HEADS)
        sl = slice(h * HEAD_DIM, (h + 1) * HEAD_DIM)
        uw = jnp.dot(t_inv.astype(BF16), rhs_ref[idx], preferred_element_type=F32)
        u_ref[c * CHUNK:(c + 1) * CHUNK, sl] = uw[:, 0:HEAD_DIM]
        wq_ref[c, 0:CHUNK, sl] = uw[:, HEAD_DIM:2 * HEAD_DIM].astype(BF16)

    states = [s_ref[h] for h in range(N_HEADS)]
    for ci in range(nc):
        c = (nc - 1 - ci) if rev else ci
        rows = slice(c * CHUNK, (c + 1) * CHUNK)
        for h in range(N_HEADS):
            sl = slice(h * HEAD_DIM, (h + 1) * HEAD_DIM)
            wq_s = jnp.dot(wq_ref[c, :, sl], states[h].astype(BF16), preferred_element_type=F32)
            v_new = (u_ref[rows, sl] - wq_s[0:CHUNK]).astype(BF16)
            o_ref[rows, sl] = wq_s[CHUNK:2 * CHUNK] + jnp.dot(qk_ref[c * N_HEADS + h], v_new,
                                                              preferred_element_type=F32)
            upd = lax.dot_general(kd_ref[rows, sl], v_new, (((0,), (0,)), ((), ())), preferred_element_type=F32)
            states[h] = states[h] * last_ref[c * N_HEADS + h][0:1, :] + upd
    for h in range(N_HEADS):
        s_ref[h] = states[h]
    _finish(rev, o_ref, ob_ref, z_ref, gain_ref, out_ref)


def _mixer_d(proj, conv_w, a_log, dt_bias, o_bwd, gain, lay, rev):
    tb, nb = lay.tb, lay.nb
    bi = _blk_index(lay, rev)
    hb = tb // HALO_D
    n_halo = lay.n // HALO_D

    def cur(cb):
        return pl.BlockSpec((tb, W_GROUP), lambda j: (bi(j), cb))

    def prev(cb):
        return pl.BlockSpec((HALO_D, W_GROUP), lambda j: (jnp.maximum(bi(j) * hb - 1, 0), cb))

    def nxt(cb):
        return pl.BlockSpec((HALO_D, W_GROUP), lambda j: (jnp.minimum((bi(j) + 1) * hb, n_halo - 1), cb))

    small = pl.BlockSpec((1, SMALL_W), lambda j: (0, 0))
    in_specs = [cur(CB_D_Q), cur(CB_D_K), cur(CB_D_V), prev(CB_D_Q), prev(CB_D_K), prev(CB_D_V),
                nxt(CB_D_Q), nxt(CB_D_K), nxt(CB_D_V),
                pl.BlockSpec((tb, SMALL_W), lambda j: (bi(j), CB_SMALL)),
                pl.BlockSpec((DN_CONV_WIDTH, 3 * W_GROUP), lambda j: (0, 0)), small, small]
    pad_lanes = SMALL_W - 2 * N_HEADS
    alog_row = jnp.pad(a_log.reshape(1, 2 * N_HEADS), ((0, 0), (0, pad_lanes)))
    dtb_row = jnp.pad(dt_bias.reshape(1, 2 * N_HEADS), ((0, 0), (0, pad_lanes)))
    args = [proj] * 10 + [conv_w, alog_row, dtb_row]
    if not rev:
        in_specs += [cur(CB_D_Z), pl.BlockSpec((tb, W_GROUP), lambda j: (bi(j), 0)),
                     pl.BlockSpec((1, W_GROUP), lambda j: (0, 0))]
        args += [proj, o_bwd, gain.reshape(1, W_GROUP)]
    big = pltpu.VMEM((tb, W_GROUP), F32)
    sml = pltpu.VMEM((tb, SMALL_W), F32)
    nc = tb // CHUNK
    return pl.pallas_call(
        functools.partial(_delta_kernel, lay, rev),
        grid=(nb,),
        in_specs=in_specs,
        out_specs=pl.BlockSpec((tb, W_GROUP), lambda j: (bi(j), 0)),
        out_shape=jax.ShapeDtypeStruct((lay.n, W_GROUP), F32 if rev else BF16),
        scratch_shapes=[pltpu.VMEM((N_HEADS, HEAD_DIM, HEAD_DIM), F32), big, big, big, sml, sml, big,
                        pltpu.VMEM((tb + 2 * HALO_D, W_GROUP), F32),
                        big,
                        pltpu.VMEM((nc, 2 * CHUNK, W_GROUP), BF16),
                        pltpu.VMEM((tb, W_GROUP), BF16),
                        pltpu.VMEM((nc * N_HEADS, CHUNK, CHUNK), BF16),
                        pltpu.VMEM((nc * N_HEADS, CHUNK, 2 * HEAD_DIM), BF16),
                        pltpu.VMEM((nc * N_HEADS, 8, HEAD_DIM), F32)],
        compiler_params=_cparams(("arbitrary",)),
        name="mixer_d_deltanet_bwd" if rev else "mixer_d_deltanet_fwd",
    )(*args)


def _regroup_w_in(w):
    d_qkv_end = 15 * W_GROUP
    small = w[:, d_qkv_end:d_qkv_end + 4 * N_HEADS]
    d_z = w[:, d_qkv_end + 4 * N_HEADS:]
    pad = jnp.zeros((w.shape[0], SMALL_W - 4 * N_HEADS), w.dtype)
    return jnp.concatenate([w[:, :d_qkv_end], d_z, small, pad], axis=1).astype(BF16)


def _rotary_tables(t_max):
    half = HEAD_DIM // 2
    inv = 1.0 / (ROPE_BASE ** (jnp.arange(half, dtype=F32) / half))
    ang = jnp.arange(t_max, dtype=F32)[:, None] * inv[None, :]
    cos, sin = jnp.cos(ang), jnp.sin(ang)
    return jnp.concatenate([cos, cos], axis=-1), jnp.concatenate([-sin, sin], axis=-1)


def _pick_tile(t_prompt, t_sample, want):
    tile = want
    while t_prompt % tile or t_sample % tile:
        tile //= 2
    return tile


def kernel(x_prompt, x_sample, c_prompt, c_sample, ada_w, ada_b, norm_g, w_in, conv_a_w, conv_a_b, ln_a_g, ln_a_b,
           ret_norm_g, hgrn_lb_logits, hgrn_norm_g, dn_conv_w, dn_a_log, dn_dt_bias, dn_norm_g, w_out, final_g):
    depth = w_in.shape[0]
    n_prompt, t_prompt, _ = x_prompt.shape
    n_sample, t_sample, _ = x_sample.shape
    tb = _pick_tile(t_prompt, t_sample, 256)
    tm_in = _pick_tile(t_prompt, t_sample, 1024)
    tm_out = _pick_tile(t_prompt, t_sample, 512)
    lay = _Layout(t_prompt, n_prompt, t_sample, n_sample, tb)

    x = jnp.concatenate([x_prompt.reshape(-1, D_MODEL), x_sample.reshape(-1, D_MODEL)], axis=0)
    c_all = jnp.concatenate([c_prompt, c_sample], axis=0)
    n_c = c_all.shape[0]
    c_all = jnp.pad(c_all, ((0, (-n_c) % 8), (0, 0)))
    mod = _modulation(c_all, ada_w, ada_b)

    lb = jnp.cumsum(jax.nn.softmax(hgrn_lb_logits.astype(F32), axis=0), axis=0)
    lb = lb - lb[:1]
    log_lb = jnp.log(lb)
    log1m_lb = jnp.log1p(-lb)
    cos2, sin2 = _rotary_tables(max(t_prompt, t_sample))

    for l in range(depth):
        mod_l = mod[l].reshape(mod.shape[1], 1, 3 * D_MODEL)
        proj = _in_proj(x, mod_l, norm_g[l], _regroup_w_in(w_in[l]), lay, tm_in)
        m_a = _mixer_a(proj, conv_a_w[l], conv_a_b[l], ln_a_g[l], ln_a_b[l], lay)
        ob = _mixer_b(proj, cos2, sin2, None, None, lay, True)
        m_b = _mixer_b(proj, cos2, sin2, ob, ret_norm_g[l], lay, False)
        oc = _mixer_c(proj, log_lb[l], log1m_lb[l], None, None, lay, True)
        m_c = _mixer_c(proj, log_lb[l], log1m_lb[l], oc, hgrn_norm_g[l], lay, False)
        od = _mixer_d(proj, dn_conv_w[l], dn_a_log[l], dn_dt_bias[l], None, None, lay, True)
        m_d = _mixer_d(proj, dn_conv_w[l], dn_a_log[l], dn_dt_bias[l], od, dn_norm_g[l], lay, False)
        x = _out_proj(x, (m_a, m_b, m_c, m_d), mod_l, w_out[l].astype(BF16), final_g, lay, tm_out,
                      final=(l == depth - 1))

    n_tp = n_prompt * t_prompt
    return (x[:n_tp].reshape(n_prompt, t_prompt, D_MODEL), x[n_tp:].reshape(n_sample, t_sample, D_MODEL))
```

```python
import functools
import math

import numpy as np
import jax
import jax.numpy as jnp
from jax import lax
from jax.experimental import pallas as pl
from jax.experimental.pallas import tpu as pltpu

F32 = jnp.float32
BF16 = jnp.bfloat16

D_MODEL = 1024
W_GROUP = 512
HEAD_DIM = 128
SUBLANES = 8
N_HEADS = 4
CONV_A_WIDTH = 31
DN_CONV_WIDTH = 5
CHUNK = 64
ROPE_BASE = 10000.0
RET_DECAY_OFFSET = 5.0
NORM_EPS = 1e-6
QK_SCALE = HEAD_DIM ** -0.5
NEG_BIG = -1e30

N_COL_BLOCKS = 16
SMALL_W = 128
D_PROJ = N_COL_BLOCKS * W_GROUP + SMALL_W
PROJ_TN = 640
HALO_A = 16
HALO_D = 8
VMEM_LIMIT = 56 * 1024 * 1024

CB_A_VAL, CB_A_GLU, CB_A_Z = 0, 1, 2
CB_B_Q, CB_B_K, CB_B_V, CB_B_Z = 3, 4, 5, 6
CB_C_Q, CB_C_FF, CB_C_FB, CB_C_I, CB_C_Z = 7, 8, 9, 10, 11
CB_D_Q, CB_D_K, CB_D_V, CB_D_Z = 12, 13, 14, 15
CB_SMALL = N_COL_BLOCKS * W_GROUP // SMALL_W


def _dot(a, b):
    return jnp.dot(a.astype(BF16), b.astype(BF16), preferred_element_type=F32)


def _dot_nt(a, b):
    return lax.dot_general(a.astype(BF16), b.astype(BF16), (((1,), (1,)), ((), ())), preferred_element_type=F32)


def _dot_tn(a, b):
    return lax.dot_general(a.astype(BF16), b.astype(BF16), (((0,), (0,)), ((), ())), preferred_element_type=F32)


def _split2(x):
    hi = x.astype(BF16)
    lo = (x - hi.astype(F32)).astype(BF16)
    return hi, lo


def _split3(x):
    hi = x.astype(BF16)
    r = x - hi.astype(F32)
    mid = r.astype(BF16)
    lo = (r - mid.astype(F32)).astype(BF16)
    return hi, mid, lo


def _dot_exact_lhs(m, x):
    mb = m.astype(BF16)
    acc = None
    for part in _split3(x):
        t = jnp.dot(mb, part, preferred_element_type=F32)
        acc = t if acc is None else acc + t
    return acc


def _sigmoid(x):
    return 1.0 / (1.0 + jnp.exp(-x))


def _silu(x):
    return x * _sigmoid(x)


def _softplus(x):
    return jnp.maximum(x, 0.0) + jnp.log(1.0 + jnp.exp(-jnp.abs(x)))


def _log_sigmoid(x):
    return jnp.minimum(x, 0.0) - jnp.log(1.0 + jnp.exp(-jnp.abs(x)))


class _Layout:
    def __init__(self, t_prompt, n_prompt, t_sample, n_sample, tb):
        self.tp = t_prompt * n_prompt
        self.t_prompt = t_prompt
        self.ts = t_sample
        self.n = self.tp + t_sample * n_sample
        self.tb = tb
        assert t_prompt % tb == 0 and t_sample % tb == 0
        self.nb = self.n // tb

    def seq_start(self, blk):
        s = blk * self.tb
        return jnp.where(s < self.tp, s % self.t_prompt == 0, (s - self.tp) % self.ts == 0)

    def seq_end(self, blk):
        e = (blk + 1) * self.tb
        return jnp.where(e <= self.tp, e % self.t_prompt == 0, (e - self.tp) % self.ts == 0)

    def pos_block(self, blk):
        s = blk * self.tb
        return jnp.where(s < self.tp, (s % self.t_prompt) // self.tb, ((s - self.tp) % self.ts) // self.tb)

    def batch_index(self, blk, rows):
        s = blk * rows
        return jnp.where(s < self.tp, s // self.t_prompt, self.tp // self.t_prompt + (s - self.tp) // self.ts)


def _cparams(sem):
    return pltpu.CompilerParams(dimension_semantics=sem, vmem_limit_bytes=VMEM_LIMIT)


def _mod_kernel(c_ref, w_ref, b_ref, o_ref):
    c = c_ref[...]
    o_ref[0] = jnp.dot(_silu(c), w_ref[0], preferred_element_type=F32,
                       precision=lax.Precision.HIGHEST) + b_ref[0]


def _modulation(c_all, ada_w, ada_b):
    depth = ada_w.shape[0]
    nb = c_all.shape[0]
    return pl.pallas_call(
        _mod_kernel,
        grid=(depth,),
        in_specs=[pl.BlockSpec((nb, D_MODEL), lambda l: (0, 0)),
                  pl.BlockSpec((1, D_MODEL, 3 * D_MODEL), lambda l: (l, 0, 0)),
                  pl.BlockSpec((1, 1, 3 * D_MODEL), lambda l: (l, 0, 0))],
        out_specs=pl.BlockSpec((1, nb, 3 * D_MODEL), lambda l: (l, 0, 0)),
        out_shape=jax.ShapeDtypeStruct((depth, nb, 3 * D_MODEL), F32),
        compiler_params=_cparams(("arbitrary",)),
        name="adaln_modulation",
    )(c_all, ada_w, ada_b.reshape(depth, 1, 3 * D_MODEL))


def _in_proj_kernel(x_ref, mod_ref, g_ref, w_ref, o_ref, h_ref):
    @pl.when(pl.program_id(1) == 0)
    def _():
        x = x_ref[...]
        y = x * lax.rsqrt(jnp.mean(x * x, axis=-1, keepdims=True) + NORM_EPS) * g_ref[...]
        shift = mod_ref[0, :, 0:D_MODEL]
        scale = mod_ref[0, :, D_MODEL:2 * D_MODEL]
        h_ref[...] = (y * (1.0 + scale) + shift).astype(BF16)

    o_ref[...] = jnp.dot(h_ref[...], w_ref[...], preferred_element_type=F32)


def _in_proj(x, mod, norm_g, w_perm, lay, tm):
    n = x.shape[0]
    lay_m = _Layout(lay.t_prompt, lay.tp // lay.t_prompt, lay.ts, (lay.n - lay.tp) // lay.ts, tm)
    return pl.pallas_call(
        _in_proj_kernel,
        grid=(n // tm, D_PROJ // PROJ_TN),
        in_specs=[pl.BlockSpec((tm, D_MODEL), lambda i, j: (i, 0)),
                  pl.BlockSpec((1, 1, 3 * D_MODEL), lambda i, j: (lay_m.batch_index(i, tm), 0, 0)),
                  pl.BlockSpec((1, D_MODEL), lambda i, j: (0, 0)),
                  pl.BlockSpec((D_MODEL, PROJ_TN), lambda i, j: (0, j))],
        out_specs=pl.BlockSpec((tm, PROJ_TN), lambda i, j: (i, j)),
        out_shape=jax.ShapeDtypeStruct((n, D_PROJ), F32),
        scratch_shapes=[pltpu.VMEM((tm, D_MODEL), BF16)],
        compiler_params=_cparams(("arbitrary", "arbitrary")),
        name="in_proj",
    )(x, mod, norm_g.reshape(1, D_MODEL), w_perm)


def _out_proj_kernel(final, x_ref, ma_ref, mb_ref, mc_ref, md_ref, mod_ref, w_ref, fg_ref, o_ref):
    acc = jnp.dot(ma_ref[...], w_ref[0:W_GROUP, :], preferred_element_type=F32)
    acc += jnp.dot(mb_ref[...], w_ref[W_GROUP:2 * W_GROUP, :], preferred_element_type=F32)
    acc += jnp.dot(mc_ref[...], w_ref[2 * W_GROUP:3 * W_GROUP, :], preferred_element_type=F32)
    acc += jnp.dot(md_ref[...], w_ref[3 * W_GROUP:4 * W_GROUP, :], preferred_element_type=F32)
    gate = mod_ref[0, :, 2 * D_MODEL:3 * D_MODEL]
    y = x_ref[...] + gate * acc
    if final:
        y = y * lax.rsqrt(jnp.mean(y * y, axis=-1, keepdims=True) + NORM_EPS) * fg_ref[...]
    o_ref[...] = y


def _out_proj(x, mixed, mod, w_out_bf, final_g, lay, tm, final):
    n = x.shape[0]
    lay_m = _Layout(lay.t_prompt, lay.tp // lay.t_prompt, lay.ts, (lay.n - lay.tp) // lay.ts, tm)
    mspec = pl.BlockSpec((tm, W_GROUP), lambda i: (i, 0))
    return pl.pallas_call(
        functools.partial(_out_proj_kernel, final),
        grid=(n // tm,),
        in_specs=[pl.BlockSpec((tm, D_MODEL), lambda i: (i, 0)), mspec, mspec, mspec, mspec,
                  pl.BlockSpec((1, 1, 3 * D_MODEL), lambda i: (lay_m.batch_index(i, tm), 0, 0)),
                  pl.BlockSpec((4 * W_GROUP, D_MODEL), lambda i: (0, 0)),
                  pl.BlockSpec((1, D_MODEL), lambda i: (0, 0))],
        out_specs=pl.BlockSpec((tm, D_MODEL), lambda i: (i, 0)),
        out_shape=jax.ShapeDtypeStruct((n, D_MODEL), F32),
        compiler_params=_cparams(("arbitrary",)),
        name="out_proj",
    )(x, *mixed, mod, w_out_bf, final_g.reshape(1, D_MODEL))


A_ROWS = 32


def _mixer_a_kernel(lay, val_ref, glu_ref, z_ref, vp_ref, gp_ref, vn_ref, gn_ref,
                    cw_ref, cb_ref, lg_ref, lb_ref, o_ref, u_ref, ush_ref, acc_ref):
    tb = lay.tb
    blk = pl.program_id(0)
    keep_prev = jnp.where(lay.seq_start(blk), 0.0, 1.0)
    keep_next = jnp.where(lay.seq_end(blk), 0.0, 1.0)
    u_ref[0:HALO_A, :] = vp_ref[...] * _sigmoid(gp_ref[...]) * keep_prev
    u_ref[HALO_A:HALO_A + tb, :] = val_ref[...] * _sigmoid(glu_ref[...])
    u_ref[HALO_A + tb:2 * HALO_A + tb, :] = vn_ref[...] * _sigmoid(gn_ref[...]) * keep_next
    half = CONV_A_WIDTH // 2
    n_sh = tb + 2 * HALO_A - SUBLANES
    for r in range(1, SUBLANES):
        ush_ref[r - 1] = u_ref[r:r + n_sh, :]

    def tile(t, carry):
        r0 = pl.multiple_of(t * A_ROWS, A_ROWS)
        acc = jnp.zeros((A_ROWS, W_GROUP), F32) + cb_ref[...]
        for k in range(CONV_A_WIDTH):
            a, r = divmod(HALO_A - half + k, SUBLANES)
            win = pl.ds(r0 + a * SUBLANES, A_ROWS)
            src = u_ref[win, :] if r == 0 else ush_ref[r - 1, win, :]
            acc = acc + cw_ref[k:k + 1, :] * src
        acc_ref[pl.ds(r0, A_ROWS), :] = acc
        return carry

    lax.fori_loop(0, tb // A_ROWS, tile, 0)
    acc = acc_ref[...]
    mu = jnp.mean(acc, axis=-1, keepdims=True)
    xc = acc - mu
    var = jnp.mean(xc * xc, axis=-1, keepdims=True)
    y = xc * lax.rsqrt(var + NORM_EPS) * lg_ref[...] + lb_ref[...]
    o_ref[...] = (_silu(y) * _silu(z_ref[...])).astype(BF16)


def _mixer_a(proj, conv_w, conv_b, ln_g, ln_b, lay):
    tb, nb = lay.tb, lay.nb
    hb = tb // HALO_A
    n_halo = lay.n // HALO_A

    def cur(cb):
        return pl.BlockSpec((tb, W_GROUP), lambda i: (i, cb))

    def prev(cb):
        return pl.BlockSpec((HALO_A, W_GROUP), lambda i: (jnp.maximum(i * hb - 1, 0), cb))

    def nxt(cb):
        return pl.BlockSpec((HALO_A, W_GROUP), lambda i: (jnp.minimum((i + 1) * hb, n_halo - 1), cb))

    vec = pl.BlockSpec((1, W_GROUP), lambda i: (0, 0))
    return pl.pallas_call(
        functools.partial(_mixer_a_kernel, lay),
        grid=(nb,),
        in_specs=[cur(CB_A_VAL), cur(CB_A_GLU), cur(CB_A_Z), prev(CB_A_VAL), prev(CB_A_GLU),
                  nxt(CB_A_VAL), nxt(CB_A_GLU),
                  pl.BlockSpec((CONV_A_WIDTH, W_GROUP), lambda i: (0, 0)), vec, vec, vec],
        out_specs=pl.BlockSpec((tb, W_GROUP), lambda i: (i, 0)),
        out_shape=jax.ShapeDtypeStruct((lay.n, W_GROUP), BF16),
        scratch_shapes=[pltpu.VMEM((tb + 2 * HALO_A, W_GROUP), F32),
                        pltpu.VMEM((SUBLANES - 1, tb + 2 * HALO_A - SUBLANES, W_GROUP), F32),
                        pltpu.VMEM((tb, W_GROUP), F32)],
        compiler_params=_cparams(("arbitrary",)),
        name="mixer_a_conv",
    )(proj, proj, proj, proj, proj, proj, proj, conv_w, conv_b.reshape(1, W_GROUP),
      ln_g.reshape(1, W_GROUP), ln_b.reshape(1, W_GROUP))


def _blk_index(lay, rev):
    nb = lay.nb
    return (lambda j: nb - 1 - j) if rev else (lambda j: j)


def _reset_state(lay, rev, s_ref):
    blk = _blk_index(lay, rev)(pl.program_id(0))
    boundary = lay.seq_end(blk) if rev else lay.seq_start(blk)

    @pl.when(boundary)
    def _():
        s_ref[...] = jnp.zeros(s_ref.shape, s_ref.dtype)


def _finish(rev, o_ref, ob_ref, z_ref, gain_ref, out_ref):
    if rev:
        out_ref[...] = o_ref[...]
        return
    for h in range(N_HEADS):
        sl = slice(h * HEAD_DIM, (h + 1) * HEAD_DIM)
        o = o_ref[:, sl] + ob_ref[:, sl]
        y = o * lax.rsqrt(jnp.mean(o * o, axis=-1, keepdims=True) + NORM_EPS) * gain_ref[:, sl]
        out_ref[:, sl] = (y * _silu(z_ref[:, sl])).astype(BF16)


def _causal_masks(rev, c):
    row = lax.broadcasted_iota(jnp.int32, (c, c), 0)
    col = lax.broadcasted_iota(jnp.int32, (c, c), 1)
    incl = (row <= col) if rev else (row >= col)
    strict = (row < col) if rev else (row > col)
    return row, col, incl, strict


def _ret_log_gamma(rev):
    lg = np.log1p(-np.exp2(-RET_DECAY_OFFSET - np.arange(N_HEADS, dtype=np.float64)))
    return lg[::-1].copy() if rev else lg


def _ret_consts(rev):
    lg = _ret_log_gamma(rev).astype(np.float32)
    i = np.arange(CHUNK, dtype=np.float32)
    idx = (CHUNK - 1 - i) if rev else i
    rel = idx[:, None] - idx[None, :]
    dm = np.where(rel >= 0, np.exp(lg[:, None, None] * np.maximum(rel, 0.0)), 0.0).astype(np.float32)
    qd = np.exp(lg[:, None] * (idx + 1.0)).astype(np.float32)
    kd = np.exp(lg[:, None] * (CHUNK - 1.0 - idx)).astype(np.float32)
    qd = np.broadcast_to(qd[:, :, None], (N_HEADS, CHUNK, HEAD_DIM)).copy()
    kd = np.broadcast_to(kd[:, :, None], (N_HEADS, CHUNK, HEAD_DIM)).copy()
    return jnp.asarray(dm), jnp.asarray(qd), jnp.asarray(kd)


def _ret_kernel(lay, rev, *refs):
    if rev:
        q_ref, k_ref, v_ref, cos_ref, sin_ref, dm_ref, qd_ref, kd_ref, out_ref, s_ref, qt_ref, kt_ref, o_ref = refs
        z_ref = ob_ref = gain_ref = None
    else:
        (q_ref, k_ref, v_ref, cos_ref, sin_ref, dm_ref, qd_ref, kd_ref, z_ref, ob_ref, gain_ref,
         out_ref, s_ref, qt_ref, kt_ref, o_ref) = refs
    tb = lay.tb
    nc = tb // CHUNK
    chunk_decay = [float(np.exp(np.float32(v) * CHUNK)) for v in _ret_log_gamma(rev)]
    _reset_state(lay, rev, s_ref)

    cos = cos_ref[...]
    sin = sin_ref[...]
    for h in range(N_HEADS):
        sl = slice(h * HEAD_DIM, (h + 1) * HEAD_DIM)
        qh = q_ref[:, sl]
        kh = k_ref[:, sl]
        qt_ref[:, sl] = (qh * cos + pltpu.roll(qh, HEAD_DIM // 2, 1) * sin) * QK_SCALE
        kt_ref[:, sl] = kh * cos + pltpu.roll(kh, HEAD_DIM // 2, 1) * sin

    states = [s_ref[h] for h in range(N_HEADS)]
    for ci in range(nc):
        c = (nc - 1 - ci) if rev else ci
        rows = slice(c * CHUNK, (c + 1) * CHUNK)
        heads = [slice(h * HEAD_DIM, (h + 1) * HEAD_DIM) for h in range(N_HEADS)]
        qcs = [qt_ref[rows, sl] for sl in heads]
        kcs = [kt_ref[rows, sl] for sl in heads]
        vcs = [v_ref[rows, sl].astype(BF16) for sl in heads]
        scores = [_dot_nt(q, k) for q, k in zip(qcs, kcs)]
        inters = [_dot(q * qd_ref[h], states[h]) for h, q in enumerate(qcs)]
        upds = [_dot_tn(k * kd_ref[h], v) for h, (k, v) in enumerate(zip(kcs, vcs))]
        intras = [jnp.dot((s * dm_ref[h]).astype(BF16), vcs[h], preferred_element_type=F32)
                  for h, s in enumerate(scores)]
        for h, sl in enumerate(heads):
            o_ref[rows, sl] = intras[h] + inters[h]
            states[h] = chunk_decay[h] * states[h] + upds[h]
    for h in range(N_HEADS):
        s_ref[h] = states[h]
    _finish(rev, o_ref, ob_ref, z_ref, gain_ref, out_ref)


def _mixer_b(proj, cos2, sin2, o_bwd, gain, lay, rev):
    tb, nb = lay.tb, lay.nb
    bi = _blk_index(lay, rev)

    def cur(cb):
        return pl.BlockSpec((tb, W_GROUP), lambda j: (bi(j), cb))

    rot = pl.BlockSpec((tb, HEAD_DIM), lambda j: (lay.pos_block(bi(j)), 0))
    dm, qd, kd = _ret_consts(rev)
    in_specs = [cur(CB_B_Q), cur(CB_B_K), cur(CB_B_V), rot, rot,
                pl.BlockSpec((N_HEADS, CHUNK, CHUNK), lambda j: (0, 0, 0)),
                pl.BlockSpec((N_HEADS, CHUNK, HEAD_DIM), lambda j: (0, 0, 0)),
                pl.BlockSpec((N_HEADS, CHUNK, HEAD_DIM), lambda j: (0, 0, 0))]
    args = [proj, proj, proj, cos2, sin2, dm, qd, kd]
    if not rev:
        in_specs += [cur(CB_B_Z), pl.BlockSpec((tb, W_GROUP), lambda j: (bi(j), 0)),
                     pl.BlockSpec((1, W_GROUP), lambda j: (0, 0))]
        args += [proj, o_bwd, gain.reshape(1, W_GROUP)]
    return pl.pallas_call(
        functools.partial(_ret_kernel, lay, rev),
        grid=(nb,),
        in_specs=in_specs,
        out_specs=pl.BlockSpec((tb, W_GROUP), lambda j: (bi(j), 0)),
        out_shape=jax.ShapeDtypeStruct((lay.n, W_GROUP), F32 if rev else BF16),
        scratch_shapes=[pltpu.VMEM((N_HEADS, HEAD_DIM, HEAD_DIM), F32),
                        pltpu.VMEM((tb, W_GROUP), F32), pltpu.VMEM((tb, W_GROUP), F32),
                        pltpu.VMEM((tb, W_GROUP), F32)],
        compiler_params=_cparams(("arbitrary",)),
        name="mixer_b_retention_bwd" if rev else "mixer_b_retention_fwd",
    )(*args)


HGRN_LEVELS = (32, 16, 8, 4, 2, 1)
LOG2_E = math.log2(math.e)


def _hgrn_consts(rev):
    c = CHUNK
    i = np.arange(c)[:, None]
    t = np.arange(c)[None, :]
    if rev:
        cum = t >= i
        rest = t < i
    else:
        cum = t <= i
        rest = t > i
    blocks = [cum, rest]
    masks = [np.eye(c, dtype=bool)]
    col = np.arange(c)[None, :]
    for s in HGRN_LEVELS:
        start = (i // (2 * s)) * (2 * s)
        later = (i - start) >= s
        if rev:
            ref = start + s
            p = np.where(later, (t >= ref) & (t < i), (t >= i) & (t < ref))
            q_half, k_half = 0, 1
        else:
            ref = start + s - 1
            p = np.where(later, (t > ref) & (t <= i), (t > i) & (t <= ref))
            q_half, k_half = 1, 0
        blocks.append(p)
        same = (i // (2 * s)) == (col // (2 * s))
        masks.append(same & ((i // s) % 2 == q_half) & ((col // s) % 2 == k_half))
    pst = np.concatenate(blocks, axis=0).astype(np.float32)
    pst = np.concatenate([pst, pst], axis=1)
    msk = np.stack(masks, axis=0).astype(np.float32)
    return jnp.asarray(pst, dtype=BF16), jnp.asarray(msk)


def _hgrn_kernel(lay, rev, *refs):
    if rev:
        (q_ref, f_ref, v_ref, llb_ref, l1lb_ref, pst_ref, msk_ref, out_ref,
         st_ref, qs_ref, g_ref, kk_ref, o_ref, e_all_ref) = refs
        z_ref = ob_ref = gain_ref = None
    else:
        (q_ref, f_ref, v_ref, llb_ref, l1lb_ref, pst_ref, msk_ref, z_ref, ob_ref, gain_ref, out_ref,
         st_ref, qs_ref, g_ref, kk_ref, o_ref, e_all_ref) = refs
    tb = lay.tb
    nc = tb // CHUNK
    n_lvl = len(HGRN_LEVELS)
    _reset_state(lay, rev, st_ref)

    qs_ref[...] = (_silu(q_ref[...]) * QK_SCALE).astype(BF16)
    a = llb_ref[...]
    b = l1lb_ref[...] + _log_sigmoid(f_ref[...])
    g2 = (jnp.maximum(a, b) + jnp.log(1.0 + jnp.exp(-jnp.abs(a - b)))) * LOG2_E
    g_ref[...] = g2
    kk_ref[...] = 1.0 - jnp.exp2(g2)

    pst = pst_ref[...]
    last_row = 0 if rev else CHUNK - 1
    heads = [slice(h * HEAD_DIM, (h + 1) * HEAD_DIM) for h in range(N_HEADS)]
    for ci in range(nc):
        c = (nc - 1 - ci) if rev else ci
        g_parts = jnp.concatenate(_split2(g_ref[c * CHUNK:(c + 1) * CHUNK, :]), axis=0)
        e_all_ref[ci] = jnp.exp2(jnp.dot(pst, g_parts, preferred_element_type=F32))

    states = [st_ref[h] for h in range(N_HEADS)]
    for ci in range(nc):
        c = (nc - 1 - ci) if rev else ci
        rows = slice(c * CHUNK, (c + 1) * CHUNK)
        e_ref = e_all_ref.at[ci]
        qcs = [qs_ref[rows, sl] for sl in heads]
        kcs = [kk_ref[rows, sl] for sl in heads]
        vcs = [v_ref[rows, sl].astype(BF16) for sl in heads]
        e_cums = [e_ref[0:CHUNK, sl] for sl in heads]
        scores = [[lax.dot_general(q, k.astype(BF16), (((1,), (1,)), ((), ())), preferred_element_type=F32)
                   for q, k in zip(qcs, kcs)]]
        for lvl in range(n_lvl):
            e_ls = [e_ref[(2 + lvl) * CHUNK:(3 + lvl) * CHUNK, sl] for sl in heads]
            scores.append([_dot_nt(q * e, k * e) for q, k, e in zip(qcs, kcs, e_ls)])
        inters = [_dot_nt(q * e, s) for q, e, s in zip(qcs, e_cums, states)]
        upds = [_dot_tn(v, k * e_ref[CHUNK:2 * CHUNK, sl]) for v, k, sl in zip(vcs, kcs, heads)]
        attns = []
        for h in range(N_HEADS):
            attn = scores[0][h] * msk_ref[0]
            for lvl in range(n_lvl):
                attn = attn + scores[1 + lvl][h] * msk_ref[1 + lvl]
            attns.append(attn.astype(BF16))
        intras = [jnp.dot(a, v, preferred_element_type=F32) for a, v in zip(attns, vcs)]
        for h, sl in enumerate(heads):
            o_ref[rows, sl] = inters[h] + intras[h]
            states[h] = states[h] * e_cums[h][last_row:last_row + 1, :] + upds[h]
    for h in range(N_HEADS):
        st_ref[h] = states[h]
    _finish(rev, o_ref, ob_ref, z_ref, gain_ref, out_ref)


def _mixer_c(proj, log_lb, log1m_lb, o_bwd, gain, lay, rev):
    tb, nb = lay.tb, lay.nb
    bi = _blk_index(lay, rev)

    def cur(cb):
        return pl.BlockSpec((tb, W_GROUP), lambda j: (bi(j), cb))

    vec = pl.BlockSpec((1, W_GROUP), lambda j: (0, 0))
    d = 1 if rev else 0
    pst, msk = _hgrn_consts(rev)
    n_stack = pst.shape[0]
    in_specs = [cur(CB_C_Q), cur(CB_C_FB if rev else CB_C_FF), cur(CB_C_I), vec, vec,
                pl.BlockSpec((n_stack, 2 * CHUNK), lambda j: (0, 0)),
                pl.BlockSpec((msk.shape[0], CHUNK, CHUNK), lambda j: (0, 0, 0))]
    args = [proj, proj, proj, log_lb[d].reshape(1, W_GROUP), log1m_lb[d].reshape(1, W_GROUP), pst, msk]
    if not rev:
        in_specs += [cur(CB_C_Z), pl.BlockSpec((tb, W_GROUP), lambda j: (bi(j), 0)), vec]
        args += [proj, o_bwd, gain.reshape(1, W_GROUP)]
    big = pltpu.VMEM((tb, W_GROUP), F32)
    return pl.pallas_call(
        functools.partial(_hgrn_kernel, lay, rev),
        grid=(nb,),
        in_specs=in_specs,
        out_specs=pl.BlockSpec((tb, W_GROUP), lambda j: (bi(j), 0)),
        out_shape=jax.ShapeDtypeStruct((lay.n, W_GROUP), F32 if rev else BF16),
        scratch_shapes=[pltpu.VMEM((N_HEADS, HEAD_DIM, HEAD_DIM), F32),
                        pltpu.VMEM((tb, W_GROUP), BF16),
                        big, big, big,
                        pltpu.VMEM((tb // CHUNK, n_stack, W_GROUP), F32)],
        compiler_params=_cparams(("arbitrary",)),
        name="mixer_c_hgrn2_bwd" if rev else "mixer_c_hgrn2_fwd",
    )(*args)


def _unit_tri_inverses(a_mats, row, col):
    c = a_mats[0].shape[0]
    eye = jnp.where(row == col, 1.0, 0.0)

    def same_block(size):
        return (row // size) == (col // size)

    base = 8
    m_base = same_block(base)
    ns = [jnp.where(m_base, -a, 0.0).astype(BF16) for a in a_mats]
    n2s = [jnp.dot(n, n, preferred_element_type=F32) for n in ns]
    xs = [eye + n.astype(F32) for n in ns]
    xs = [x + _dot(x, n2) for x, n2 in zip(xs, n2s)]
    n4s = [_dot(n2, n2) for n2 in n2s]
    xs = [x + _dot(x, n4) for x, n4 in zip(xs, n4s)]
    size = base
    while size < c:
        m_off = same_block(2 * size) & jnp.logical_not(same_block(size))
        xbs = [x.astype(BF16) for x in xs]
        ts = [jnp.dot(xb, jnp.where(m_off, a, 0.0).astype(BF16), preferred_element_type=F32)
              for xb, a in zip(xbs, a_mats)]
        xs = [x - jnp.dot(t.astype(BF16), xb, preferred_element_type=F32) for x, t, xb in zip(xs, ts, xbs)]
        size *= 2
    return xs


def _delta_pre_kernel(lay, q_ref, k_ref, v_ref, qp_ref, kp_ref, vp_ref, qn_ref, kn_ref, vn_ref, sm_ref,
                      cw_ref, alog_ref, dtb_ref, qo_ref, ko_ref, vo_ref, g_ref, beta_ref, x_ref):
    tb = lay.tb
    blk = pl.program_id(0)
    keep_prev = jnp.where(lay.seq_start(blk), 0.0, 1.0)
    keep_next = jnp.where(lay.seq_end(blk), 0.0, 1.0)
    half = DN_CONV_WIDTH // 2
    for part, (c_ref, p_ref, n_ref, dst_ref) in enumerate(
            ((q_ref, qp_ref, qn_ref, qo_ref), (k_ref, kp_ref, kn_ref, ko_ref), (v_ref, vp_ref, vn_ref, vo_ref))):
        x_ref[0:HALO_D, :] = p_ref[...] * keep_prev
        x_ref[HALO_D:HALO_D + tb, :] = c_ref[...]
        x_ref[HALO_D + tb:2 * HALO_D + tb, :] = n_ref[...] * keep_next
        wsl = slice(part * W_GROUP, (part + 1) * W_GROUP)
        acc = None
        for k in range(DN_CONV_WIDTH):
            off = HALO_D - half + k
            t = cw_ref[k:k + 1, wsl] * x_ref[off:off + tb, :]
            acc = t if acc is None else acc + t
        y = _silu(acc)
        if part < 2:
            for h in range(N_HEADS):
                sl = slice(h * HEAD_DIM, (h + 1) * HEAD_DIM)
                yh = y[:, sl]
                yn = yh * lax.rsqrt(jnp.sum(yh * yh, axis=-1, keepdims=True) + NORM_EPS)
                dst_ref[:, sl] = (yn * QK_SCALE if part == 0 else yn).astype(BF16)
        else:
            dst_ref[...] = y.astype(BF16)

    sm = sm_ref[...]
    g_ref[...] = -jnp.exp(alog_ref[...]) * _softplus(sm + dtb_ref[...])
    beta_ref[...] = _sigmoid(sm)


def _delta_kernel(lay, rev, *refs):
    if rev:
        (qc_ref, kc_ref, vc_ref, g_ref, beta_ref, out_ref,
         s_ref, o_ref, qd_ref, kd_ref, qk_ref, rhs_ref, last_ref, gq_ref, bm_ref) = refs
        z_ref = ob_ref = gain_ref = None
    else:
        (qc_ref, kc_ref, vc_ref, g_ref, beta_ref, z_ref, ob_ref, gain_ref, out_ref,
         s_ref, o_ref, qd_ref, kd_ref, qk_ref, rhs_ref, last_ref, gq_ref, bm_ref) = refs
    tb = lay.tb
    nc = tb // CHUNK
    _reset_state(lay, rev, s_ref)

    row, col, incl, strict = _causal_masks(rev, CHUNK)
    cum_mat = jnp.where(incl, 1.0, 0.0)
    eye_rows = jnp.where(lax.broadcasted_iota(jnp.int32, (8, SMALL_W), 0)
                         == lax.broadcasted_iota(jnp.int32, (8, SMALL_W), 1), 1.0, 0.0).astype(BF16)
    lane0 = N_HEADS if rev else 0

    cums = [_dot_exact_lhs(cum_mat, g_ref[c * CHUNK:(c + 1) * CHUNK, :]) for c in range(nc)]
    cum_ts = []
    for cum in cums:
        sel = pltpu.roll(cum, SMALL_W - lane0, 1) if lane0 else cum
        cum_t = None
        for part in _split3(sel):
            t = lax.dot_general(eye_rows, part, (((1,), (1,)), ((), ())), preferred_element_type=F32)
            cum_t = t if cum_t is None else cum_t + t
        cum_ts.append(cum_t)
    raws = []
    for c in range(nc):
        rows = slice(c * CHUNK, (c + 1) * CHUNK)
        beta_all = beta_ref[rows, :]
        for h in range(N_HEADS):
            sl = slice(h * HEAD_DIM, (h + 1) * HEAD_DIM)
            qc = qc_ref[rows, sl]
            kc = kc_ref[rows, sl]
            vc = vc_ref[rows, sl]
            ci_b = jnp.broadcast_to(cums[c][:, lane0 + h:lane0 + h + 1], (CHUNK, HEAD_DIM))
            b_lane = 2 * N_HEADS + lane0 + h
            beta_b = jnp.broadcast_to(beta_all[:, b_lane:b_lane + 1], (CHUNK, HEAD_DIM))
            k_beta = kc * beta_b
            raws.append(_dot_nt(jnp.concatenate([k_beta.astype(BF16), qc], axis=0), kc))
            e_ci = jnp.exp(ci_b)
            rhs_ref[c * N_HEADS + h] = jnp.concatenate([vc * beta_b, k_beta * e_ci], axis=1).astype(BF16)
            qd_ref[rows, sl] = qc * e_ci
            c_last = ci_b[0:1, :] if rev else ci_b[CHUNK - 1:CHUNK, :]
            kd_ref[rows, sl] = (kc * jnp.exp(c_last - ci_b)).astype(BF16)
            last_ref[c * N_HEADS + h] = jnp.broadcast_to(jnp.exp(c_last), (8, HEAD_DIM))
    a_mats = []
    for idx, raw in enumerate(raws):
        c, h = divmod(idx, N_HEADS)
        ci_col = cums[c][:, lane0 + h:lane0 + h + 1]
        cj_row = cum_ts[c][h:h + 1, :]
        decay = jnp.exp(jnp.where(incl, ci_col - cj_row, NEG_BIG))
        a_mats.append(jnp.where(strict, raw[0:CHUNK] * decay, 0.0))
        qk_ref[idx] = (raw[CHUNK:2 * CHUNK] * decay).astype(BF16)

    t_invs = _unit_tri_inverses(a_mats, row, col)
    where = [(slice((idx // N_HEADS) * CHUNK, (idx // N_HEADS + 1) * CHUNK),
              slice((idx % N_HEADS) * HEAD_DIM, (idx % N_HEADS + 1) * HEAD_DIM)) for idx in range(len(t_invs))]
    uws = [jnp.dot(t_inv.astype(BF16), rhs_ref[idx], preferred_element_type=F32).astype(BF16)
           for idx, t_inv in enumerate(t_invs)]
    kuws = [lax.dot_general(kd_ref[rows, sl], uw, (((0,), (0,)), ((), ())), preferred_element_type=F32)
            for (rows, sl), uw in zip(where, uws)]
    quws = [jnp.dot(qk_ref[idx], uw, preferred_element_type=F32) for idx, uw in enumerate(uws)]
    for idx, ((rows, sl), kuw, quw) in enumerate(zip(where, kuws, quws)):
        bm_ref[idx] = kuw[:, 0:HEAD_DIM]
        gq_ref[idx, 0:HEAD_DIM, :] = kuw[:, HEAD_DIM:2 * HEAD_DIM].astype(BF16)
        gq_ref[idx, HEAD_DIM:HEAD_DIM + CHUNK, :] = (qd_ref[rows, sl] - quw[:, HEAD_DIM:2 * HEAD_DIM]).astype(BF16)
        o_ref[rows, sl] = quw[:, 0:HEAD_DIM]

    states = [s_ref[h] for h in range(N_HEADS)]
    for ci in range(nc):
        c = (nc - 1 - ci) if rev else ci
        rows = slice(c * CHUNK, (c + 1) * CHUNK)
        for h in range(N_HEADS):
            idx = c * N_HEADS + h
            sl = slice(h * HEAD_DIM, (h + 1) * HEAD_DIM)
            gs = jnp.dot(gq_ref[idx], states[h].astype(BF16), preferred_element_type=F32)
            o_ref[rows, sl] += gs[HEAD_DIM:HEAD_DIM + CHUNK]
            states[h] = states[h] * last_ref[idx][0:1, :] - gs[0:HEAD_DIM] + bm_ref[idx]
    for h in range(N_HEADS):
        s_ref[h] = states[h]
    _finish(rev, o_ref, ob_ref, z_ref, gain_ref, out_ref)


def _mixer_d_pre(proj, conv_w, a_log, dt_bias, lay):
    tb, nb = lay.tb, lay.nb
    hb = tb // HALO_D
    n_halo = lay.n // HALO_D

    def cur(cb):
        return pl.BlockSpec((tb, W_GROUP), lambda i: (i, cb))

    def prev(cb):
        return pl.BlockSpec((HALO_D, W_GROUP), lambda i: (jnp.maximum(i * hb - 1, 0), cb))

    def nxt(cb):
        return pl.BlockSpec((HALO_D, W_GROUP), lambda i: (jnp.minimum((i + 1) * hb, n_halo - 1), cb))

    small = pl.BlockSpec((1, SMALL_W), lambda i: (0, 0))
    pad_lanes = SMALL_W - 2 * N_HEADS
    alog_row = jnp.pad(a_log.reshape(1, 2 * N_HEADS), ((0, 0), (0, pad_lanes)))
    dtb_row = jnp.pad(dt_bias.reshape(1, 2 * N_HEADS), ((0, 0), (0, pad_lanes)))
    wide = pl.BlockSpec((tb, W_GROUP), lambda i: (i, 0))
    narrow = pl.BlockSpec((tb, SMALL_W), lambda i: (i, 0))
    return pl.pallas_call(
        functools.partial(_delta_pre_kernel, lay),
        grid=(nb,),
        in_specs=[cur(CB_D_Q), cur(CB_D_K), cur(CB_D_V), prev(CB_D_Q), prev(CB_D_K), prev(CB_D_V),
                  nxt(CB_D_Q), nxt(CB_D_K), nxt(CB_D_V),
                  pl.BlockSpec((tb, SMALL_W), lambda i: (i, CB_SMALL)),
                  pl.BlockSpec((DN_CONV_WIDTH, 3 * W_GROUP), lambda i: (0, 0)), small, small],
        out_specs=[wide, wide, wide, narrow, narrow],
        out_shape=[jax.ShapeDtypeStruct((lay.n, W_GROUP), BF16)] * 3
        + [jax.ShapeDtypeStruct((lay.n, SMALL_W), F32)] * 2,
        scratch_shapes=[pltpu.VMEM((tb + 2 * HALO_D, W_GROUP), F32)],
        compiler_params=_cparams(("arbitrary",)),
        name="mixer_d_deltanet_pre",
    )(*([proj] * 10), conv_w, alog_row, dtb_row)


def _mixer_d(pre, proj, o_bwd, gain, lay, rev):
    tb, nb = lay.tb, lay.nb
    bi = _blk_index(lay, rev)
    wide = pl.BlockSpec((tb, W_GROUP), lambda j: (bi(j), 0))
    narrow = pl.BlockSpec((tb, SMALL_W), lambda j: (bi(j), 0))
    in_specs = [wide, wide, wide, narrow, narrow]
    args = list(pre)
    if not rev:
        in_specs += [pl.BlockSpec((tb, W_GROUP), lambda j: (bi(j), CB_D_Z)), wide,
                     pl.BlockSpec((1, W_GROUP), lambda j: (0, 0))]
        args += [proj, o_bwd, gain.reshape(1, W_GROUP)]
    big = pltpu.VMEM((tb, W_GROUP), F32)
    nc = tb // CHUNK
    return pl.pallas_call(
        functools.partial(_delta_kernel, lay, rev),
        grid=(nb,),
        in_specs=in_specs,
        out_specs=wide,
        out_shape=jax.ShapeDtypeStruct((lay.n, W_GROUP), F32 if rev else BF16),
        scratch_shapes=[pltpu.VMEM((N_HEADS, HEAD_DIM, HEAD_DIM), F32),
                        big,
                        big,
                        pltpu.VMEM((tb, W_GROUP), BF16),
                        pltpu.VMEM((nc * N_HEADS, CHUNK, CHUNK), BF16),
                        pltpu.VMEM((nc * N_HEADS, CHUNK, 2 * HEAD_DIM), BF16),
                        pltpu.VMEM((nc * N_HEADS, 8, HEAD_DIM), F32),
                        pltpu.VMEM((nc * N_HEADS, HEAD_DIM + CHUNK, HEAD_DIM), BF16),
                        pltpu.VMEM((nc * N_HEADS, HEAD_DIM, HEAD_DIM), F32)],
        compiler_params=_cparams(("arbitrary",)),
        name="mixer_d_deltanet_bwd" if rev else "mixer_d_deltanet_fwd",
    )(*args)


def _regroup_w_in(w):
    d_qkv_end = 15 * W_GROUP
    small = w[:, d_qkv_end:d_qkv_end + 4 * N_HEADS]
    d_z = w[:, d_qkv_end + 4 * N_HEADS:]
    pad = jnp.zeros((w.shape[0], SMALL_W - 4 * N_HEADS), w.dtype)
    return jnp.concatenate([w[:, :d_qkv_end], d_z, small, pad], axis=1).astype(BF16)


def _rotary_tables(t_max):
    half = HEAD_DIM // 2
    inv = 1.0 / (ROPE_BASE ** (jnp.arange(half, dtype=F32) / half))
    ang = jnp.arange(t_max, dtype=F32)[:, None] * inv[None, :]
    cos, sin = jnp.cos(ang), jnp.sin(ang)
    return jnp.concatenate([cos, cos], axis=-1), jnp.concatenate([-sin, sin], axis=-1)


def _pick_tile(t_prompt, t_sample, want):
    tile = want
    while t_prompt % tile or t_sample % tile:
        tile //= 2
    return tile


def kernel(x_prompt, x_sample, c_prompt, c_sample, ada_w, ada_b, norm_g, w_in, conv_a_w, conv_a_b, ln_a_g, ln_a_b,
           ret_norm_g, hgrn_lb_logits, hgrn_norm_g, dn_conv_w, dn_a_log, dn_dt_bias, dn_norm_g, w_out, final_g):
    depth = w_in.shape[0]
    n_prompt, t_prompt, _ = x_prompt.shape
    n_sample, t_sample, _ = x_sample.shape
    tb = _pick_tile(t_prompt, t_sample, 256)
    tm_in = _pick_tile(t_prompt, t_sample, 1024)
    tm_out = _pick_tile(t_prompt, t_sample, 512)
    lay = _Layout(t_prompt, n_prompt, t_sample, n_sample, tb)

    x = jnp.concatenate([x_prompt.reshape(-1, D_MODEL), x_sample.reshape(-1, D_MODEL)], axis=0)
    c_all = jnp.concatenate([c_prompt, c_sample], axis=0)
    n_c = c_all.shape[0]
    c_all = jnp.pad(c_all, ((0, (-n_c) % 8), (0, 0)))
    mod = _modulation(c_all, ada_w, ada_b)

    lb = jnp.cumsum(jax.nn.softmax(hgrn_lb_logits.astype(F32), axis=0), axis=0)
    lb = lb - lb[:1]
    log_lb = jnp.log(lb)
    log1m_lb = jnp.log1p(-lb)
    cos2, sin2 = _rotary_tables(max(t_prompt, t_sample))

    for l in range(depth):
        mod_l = mod[l].reshape(mod.shape[1], 1, 3 * D_MODEL)
        proj = _in_proj(x, mod_l, norm_g[l], _regroup_w_in(w_in[l]), lay, tm_in)
        m_a = _mixer_a(proj, conv_a_w[l], conv_a_b[l], ln_a_g[l], ln_a_b[l], lay)
        ob = _mixer_b(proj, cos2, sin2, None, None, lay, True)
        m_b = _mixer_b(proj, cos2, sin2, ob, ret_norm_g[l], lay, False)
        oc = _mixer_c(proj, log_lb[l], log1m_lb[l], None, None, lay, True)
        m_c = _mixer_c(proj, log_lb[l], log1m_lb[l], oc, hgrn_norm_g[l], lay, False)
        d_pre = _mixer_d_pre(proj, dn_conv_w[l], dn_a_log[l], dn_dt_bias[l], lay)
        od = _mixer_d(d_pre, proj, None, None, lay, True)
        m_d = _mixer_d(d_pre, proj, od, dn_norm_g[l], lay, False)
        x = _out_proj(x, (m_a, m_b, m_c, m_d), mod_l, w_out[l].astype(BF16), final_g, lay, tm_out,
                      final=(l == depth - 1))

    n_tp = n_prompt * t_prompt
    return (x[:n_tp].reshape(n_prompt, t_prompt, D_MODEL), x[n_tp:].reshape(n_sample, t_sample, D_MODEL))
```

```python
import functools
import math

import numpy as np
import jax
import jax.numpy as jnp
from jax import lax
from jax.experimental import pallas as pl
from jax.experimental.pallas import tpu as pltpu

F32 = jnp.float32
BF16 = jnp.bfloat16

D_MODEL = 1024
W_GROUP = 512
HEAD_DIM = 128
SUBLANES = 8
N_HEADS = 4
CONV_A_WIDTH = 31
DN_CONV_WIDTH = 5
CHUNK = 64
ROPE_BASE = 10000.0
RET_DECAY_OFFSET = 5.0
NORM_EPS = 1e-6
QK_SCALE = HEAD_DIM ** -0.5
NEG_BIG = -1e30

N_COL_BLOCKS = 16
SMALL_W = 128
D_PROJ = N_COL_BLOCKS * W_GROUP
PROJ_TN = 2048
HALO_A = 16
HALO_D = 16
VMEM_LIMIT = 56 * 1024 * 1024

CB_A_VAL, CB_A_GLU, CB_A_Z = 0, 1, 2
CB_B_Q, CB_B_K, CB_B_V, CB_B_Z = 3, 4, 5, 6
CB_C_Q, CB_C_FF, CB_C_FB, CB_C_I, CB_C_Z = 7, 8, 9, 10, 11
CB_D_Q, CB_D_K, CB_D_V, CB_D_Z = 12, 13, 14, 15


def _dot(a, b):
    return jnp.dot(a.astype(BF16), b.astype(BF16), preferred_element_type=F32)


def _dot_nt(a, b):
    return lax.dot_general(a.astype(BF16), b.astype(BF16), (((1,), (1,)), ((), ())), preferred_element_type=F32)


def _dot_tn(a, b):
    return lax.dot_general(a.astype(BF16), b.astype(BF16), (((0,), (0,)), ((), ())), preferred_element_type=F32)


def _split2(x):
    hi = x.astype(BF16)
    lo = (x - hi.astype(F32)).astype(BF16)
    return hi, lo


def _split3(x):
    hi = x.astype(BF16)
    r = x - hi.astype(F32)
    mid = r.astype(BF16)
    lo = (r - mid.astype(F32)).astype(BF16)
    return hi, mid, lo


def _dot_exact_lhs(m, x):
    mb = m.astype(BF16)
    acc = None
    for part in _split3(x):
        t = jnp.dot(mb, part, preferred_element_type=F32)
        acc = t if acc is None else acc + t
    return acc


def _sigmoid(x):
    return 1.0 / (1.0 + jnp.exp(-x))


def _silu(x):
    return x * _sigmoid(x)


def _softplus(x):
    return jnp.maximum(x, 0.0) + jnp.log(1.0 + jnp.exp(-jnp.abs(x)))


def _log_sigmoid(x):
    return jnp.minimum(x, 0.0) - jnp.log(1.0 + jnp.exp(-jnp.abs(x)))


class _Layout:
    def __init__(self, t_prompt, n_prompt, t_sample, n_sample, tb):
        self.tp = t_prompt * n_prompt
        self.t_prompt = t_prompt
        self.ts = t_sample
        self.n = self.tp + t_sample * n_sample
        self.tb = tb
        assert t_prompt % tb == 0 and t_sample % tb == 0
        self.nb = self.n // tb

    def seq_start(self, blk):
        s = blk * self.tb
        return jnp.where(s < self.tp, s % self.t_prompt == 0, (s - self.tp) % self.ts == 0)

    def seq_end(self, blk):
        e = (blk + 1) * self.tb
        return jnp.where(e <= self.tp, e % self.t_prompt == 0, (e - self.tp) % self.ts == 0)

    def pos_block(self, blk):
        s = blk * self.tb
        return jnp.where(s < self.tp, (s % self.t_prompt) // self.tb, ((s - self.tp) % self.ts) // self.tb)

    def batch_index(self, blk, rows):
        s = blk * rows
        return jnp.where(s < self.tp, s // self.t_prompt, self.tp // self.t_prompt + (s - self.tp) // self.ts)


def _cparams(sem):
    return pltpu.CompilerParams(dimension_semantics=sem, vmem_limit_bytes=VMEM_LIMIT)


def _mod_kernel(c_ref, w_ref, b_ref, o_ref):
    c = c_ref[...]
    o_ref[0] = jnp.dot(_silu(c), w_ref[0], preferred_element_type=F32,
                       precision=lax.Precision.HIGHEST) + b_ref[0]


def _modulation(c_all, ada_w, ada_b):
    depth = ada_w.shape[0]
    nb = c_all.shape[0]
    return pl.pallas_call(
        _mod_kernel,
        grid=(depth,),
        in_specs=[pl.BlockSpec((nb, D_MODEL), lambda l: (0, 0)),
                  pl.BlockSpec((1, D_MODEL, 3 * D_MODEL), lambda l: (l, 0, 0)),
                  pl.BlockSpec((1, 1, 3 * D_MODEL), lambda l: (l, 0, 0))],
        out_specs=pl.BlockSpec((1, nb, 3 * D_MODEL), lambda l: (l, 0, 0)),
        out_shape=jax.ShapeDtypeStruct((depth, nb, 3 * D_MODEL), F32),
        compiler_params=_cparams(("arbitrary",)),
        name="adaln_modulation",
    )(c_all, ada_w, ada_b.reshape(depth, 1, 3 * D_MODEL))


def _load_x(x_refs, i, npb):
    if len(x_refs) == 1:
        return x_refs[0][...]
    return jnp.where(i < npb, x_refs[0][...], x_refs[1][...])


def _x_specs(n_x, tm, npb, row_block):
    def spec(fn):
        return pl.BlockSpec((tm, D_MODEL), lambda *ids: (fn(row_block(*ids)), 0))

    if n_x == 1:
        return [spec(lambda i: i)]
    return [spec(lambda i: jnp.minimum(i, npb - 1)), spec(lambda i: jnp.maximum(i - npb, 0))]


def _in_proj_kernel(n_x, npb, *refs):
    x_refs = refs[:n_x]
    mod_ref, g_ref, w_ref, ws_ref, o_ref, os_ref, h_ref = refs[n_x:]

    @pl.when(pl.program_id(1) == 0)
    def _():
        x = _load_x(x_refs, pl.program_id(0), npb)
        y = x * lax.rsqrt(jnp.mean(x * x, axis=-1, keepdims=True) + NORM_EPS) * g_ref[...]
        shift = mod_ref[0, :, 0:D_MODEL]
        scale = mod_ref[0, :, D_MODEL:2 * D_MODEL]
        h = (y * (1.0 + scale) + shift).astype(BF16)
        h_ref[...] = h
        os_ref[...] = jnp.dot(h, ws_ref[...], preferred_element_type=F32)

    o_ref[...] = jnp.dot(h_ref[...], w_ref[...], preferred_element_type=F32).astype(BF16)


def _in_proj(xs, mod, norm_g, w_main, w_small, lay, tm):
    n = lay.n
    npb = lay.tp // tm
    lay_m = _Layout(lay.t_prompt, lay.tp // lay.t_prompt, lay.ts, (lay.n - lay.tp) // lay.ts, tm)
    return pl.pallas_call(
        functools.partial(_in_proj_kernel, len(xs), npb),
        grid=(n // tm, D_PROJ // PROJ_TN),
        in_specs=_x_specs(len(xs), tm, npb, lambda i, j: i) + [
                  pl.BlockSpec((1, 1, 3 * D_MODEL), lambda i, j: (lay_m.batch_index(i, tm), 0, 0)),
                  pl.BlockSpec((1, D_MODEL), lambda i, j: (0, 0)),
                  pl.BlockSpec((D_MODEL, PROJ_TN), lambda i, j: (0, j)),
                  pl.BlockSpec((D_MODEL, SMALL_W), lambda i, j: (0, 0))],
        out_specs=[pl.BlockSpec((tm, PROJ_TN), lambda i, j: (i, j)),
                   pl.BlockSpec((tm, SMALL_W), lambda i, j: (i, 0))],
        out_shape=[jax.ShapeDtypeStruct((n, D_PROJ), BF16), jax.ShapeDtypeStruct((n, SMALL_W), F32)],
        scratch_shapes=[pltpu.VMEM((tm, D_MODEL), BF16)],
        compiler_params=_cparams(("arbitrary", "arbitrary")),
        name="in_proj",
    )(*xs, mod, norm_g.reshape(1, D_MODEL), w_main, w_small)


def _out_proj_kernel(final, n_x, npb, *refs):
    x_refs = refs[:n_x]
    ma_ref, mb_ref, mc_ref, md_ref, mod_ref, w_ref, fg_ref = refs[n_x:n_x + 7]
    o_refs = refs[n_x + 7:]
    i = pl.program_id(0)
    acc = jnp.dot(ma_ref[...], w_ref[0:W_GROUP, :], preferred_element_type=F32)
    acc += jnp.dot(mb_ref[...], w_ref[W_GROUP:2 * W_GROUP, :], preferred_element_type=F32)
    acc += jnp.dot(mc_ref[...], w_ref[2 * W_GROUP:3 * W_GROUP, :], preferred_element_type=F32)
    acc += jnp.dot(md_ref[...], w_ref[3 * W_GROUP:4 * W_GROUP, :], preferred_element_type=F32)
    gate = mod_ref[0, :, 2 * D_MODEL:3 * D_MODEL]
    y = _load_x(x_refs, i, npb) + gate * acc
    if not final:
        o_refs[0][...] = y
        return
    y = y * lax.rsqrt(jnp.mean(y * y, axis=-1, keepdims=True) + NORM_EPS) * fg_ref[...]

    @pl.when(i < npb)
    def _():
        o_refs[0][...] = y

    @pl.when(i >= npb)
    def _():
        o_refs[1][...] = y


def _out_proj(xs, mixed, mod, w_out_bf, final_g, lay, tm, final):
    n = lay.n
    npb = lay.tp // tm
    lay_m = _Layout(lay.t_prompt, lay.tp // lay.t_prompt, lay.ts, (lay.n - lay.tp) // lay.ts, tm)
    mspec = pl.BlockSpec((tm, W_GROUP), lambda i: (i, 0))
    if final:
        out_specs = [pl.BlockSpec((tm, D_MODEL), lambda i: (jnp.minimum(i, npb - 1), 0)),
                     pl.BlockSpec((tm, D_MODEL), lambda i: (jnp.maximum(i - npb, 0), 0))]
        out_shape = [jax.ShapeDtypeStruct((lay.tp, D_MODEL), F32), jax.ShapeDtypeStruct((n - lay.tp, D_MODEL), F32)]
    else:
        out_specs = [pl.BlockSpec((tm, D_MODEL), lambda i: (i, 0))]
        out_shape = [jax.ShapeDtypeStruct((n, D_MODEL), F32)]
    out = pl.pallas_call(
        functools.partial(_out_proj_kernel, final, len(xs), npb),
        grid=(n // tm,),
        in_specs=_x_specs(len(xs), tm, npb, lambda i: i) + [mspec, mspec, mspec, mspec,
                  pl.BlockSpec((1, 1, 3 * D_MODEL), lambda i: (lay_m.batch_index(i, tm), 0, 0)),
                  pl.BlockSpec((4 * W_GROUP, D_MODEL), lambda i: (0, 0)),
                  pl.BlockSpec((1, D_MODEL), lambda i: (0, 0))],
        out_specs=out_specs,
        out_shape=out_shape,
        compiler_params=_cparams(("arbitrary",)),
        name="out_proj",
    )(*xs, *mixed, mod, w_out_bf, final_g.reshape(1, D_MODEL))
    return tuple(out)


A_ROWS = 32


def _mixer_a_kernel(lay, val_ref, glu_ref, z_ref, vp_ref, gp_ref, vn_ref, gn_ref,
                    cw_ref, cb_ref, lg_ref, lb_ref, o_ref, u_ref, ush_ref, acc_ref):
    tb = lay.tb
    blk = pl.program_id(0)
    keep_prev = jnp.where(lay.seq_start(blk), 0.0, 1.0)
    keep_next = jnp.where(lay.seq_end(blk), 0.0, 1.0)
    u_ref[0:HALO_A, :] = vp_ref[...].astype(F32) * _sigmoid(gp_ref[...].astype(F32)) * keep_prev
    u_ref[HALO_A:HALO_A + tb, :] = val_ref[...].astype(F32) * _sigmoid(glu_ref[...].astype(F32))
    u_ref[HALO_A + tb:2 * HALO_A + tb, :] = vn_ref[...].astype(F32) * _sigmoid(gn_ref[...].astype(F32)) * keep_next
    half = CONV_A_WIDTH // 2
    n_sh = tb + 2 * HALO_A - SUBLANES
    for r in range(1, SUBLANES):
        ush_ref[r - 1] = u_ref[r:r + n_sh, :]

    def tile(t, carry):
        r0 = pl.multiple_of(t * A_ROWS, A_ROWS)
        acc = jnp.zeros((A_ROWS, W_GROUP), F32) + cb_ref[...]
        for k in range(CONV_A_WIDTH):
            a, r = divmod(HALO_A - half + k, SUBLANES)
            win = pl.ds(r0 + a * SUBLANES, A_ROWS)
            src = u_ref[win, :] if r == 0 else ush_ref[r - 1, win, :]
            acc = acc + cw_ref[k:k + 1, :] * src
        acc_ref[pl.ds(r0, A_ROWS), :] = acc
        return carry

    lax.fori_loop(0, tb // A_ROWS, tile, 0)
    acc = acc_ref[...]
    mu = jnp.mean(acc, axis=-1, keepdims=True)
    xc = acc - mu
    var = jnp.mean(xc * xc, axis=-1, keepdims=True)
    y = xc * lax.rsqrt(var + NORM_EPS) * lg_ref[...] + lb_ref[...]
    o_ref[...] = (_silu(y) * _silu(z_ref[...].astype(F32))).astype(BF16)


def _mixer_a(proj, conv_w, conv_b, ln_g, ln_b, lay):
    tb, nb = lay.tb, lay.nb
    hb = tb // HALO_A
    n_halo = lay.n // HALO_A

    def cur(cb):
        return pl.BlockSpec((tb, W_GROUP), lambda i: (i, cb))

    def prev(cb):
        return pl.BlockSpec((HALO_A, W_GROUP), lambda i: (jnp.maximum(i * hb - 1, 0), cb))

    def nxt(cb):
        return pl.BlockSpec((HALO_A, W_GROUP), lambda i: (jnp.minimum((i + 1) * hb, n_halo - 1), cb))

    vec = pl.BlockSpec((1, W_GROUP), lambda i: (0, 0))
    return pl.pallas_call(
        functools.partial(_mixer_a_kernel, lay),
        grid=(nb,),
        in_specs=[cur(CB_A_VAL), cur(CB_A_GLU), cur(CB_A_Z), prev(CB_A_VAL), prev(CB_A_GLU),
                  nxt(CB_A_VAL), nxt(CB_A_GLU),
                  pl.BlockSpec((CONV_A_WIDTH, W_GROUP), lambda i: (0, 0)), vec, vec, vec],
        out_specs=pl.BlockSpec((tb, W_GROUP), lambda i: (i, 0)),
        out_shape=jax.ShapeDtypeStruct((lay.n, W_GROUP), BF16),
        scratch_shapes=[pltpu.VMEM((tb + 2 * HALO_A, W_GROUP), F32),
                        pltpu.VMEM((SUBLANES - 1, tb + 2 * HALO_A - SUBLANES, W_GROUP), F32),
                        pltpu.VMEM((tb, W_GROUP), F32)],
        compiler_params=_cparams(("arbitrary",)),
        name="mixer_a_conv",
    )(proj, proj, proj, proj, proj, proj, proj, conv_w, conv_b.reshape(1, W_GROUP),
      ln_g.reshape(1, W_GROUP), ln_b.reshape(1, W_GROUP))


def _blk_index(lay, rev):
    nb = lay.nb
    return (lambda j: nb - 1 - j) if rev else (lambda j: j)


def _reset_state(lay, rev, s_ref):
    blk = _blk_index(lay, rev)(pl.program_id(0))
    boundary = lay.seq_end(blk) if rev else lay.seq_start(blk)

    @pl.when(boundary)
    def _():
        s_ref[...] = jnp.zeros(s_ref.shape, s_ref.dtype)


def _finish(rev, o_ref, ob_ref, z_ref, gain_ref, out_ref):
    if rev:
        out_ref[...] = o_ref[...]
        return
    for h in range(N_HEADS):
        sl = slice(h * HEAD_DIM, (h + 1) * HEAD_DIM)
        o = o_ref[:, sl] + ob_ref[:, sl]
        y = o * lax.rsqrt(jnp.mean(o * o, axis=-1, keepdims=True) + NORM_EPS) * gain_ref[:, sl]
        out_ref[:, sl] = (y * _silu(z_ref[:, sl].astype(F32))).astype(BF16)


def _causal_masks(rev, c):
    row = lax.broadcasted_iota(jnp.int32, (c, c), 0)
    col = lax.broadcasted_iota(jnp.int32, (c, c), 1)
    incl = (row <= col) if rev else (row >= col)
    strict = (row < col) if rev else (row > col)
    return row, col, incl, strict


def _ret_log_gamma(rev):
    lg = np.log1p(-np.exp2(-RET_DECAY_OFFSET - np.arange(N_HEADS, dtype=np.float64)))
    return lg[::-1].copy() if rev else lg


def _ret_consts(rev):
    lg = _ret_log_gamma(rev).astype(np.float32)
    i = np.arange(CHUNK, dtype=np.float32)
    idx = (CHUNK - 1 - i) if rev else i
    rel = idx[:, None] - idx[None, :]
    dm = np.where(rel >= 0, np.exp(lg[:, None, None] * np.maximum(rel, 0.0)), 0.0).astype(np.float32)
    qd = np.exp(lg[:, None] * (idx + 1.0)).astype(np.float32)
    kd = np.exp(lg[:, None] * (CHUNK - 1.0 - idx)).astype(np.float32)
    qd = np.broadcast_to(qd[:, :, None], (N_HEADS, CHUNK, HEAD_DIM)).copy()
    kd = np.broadcast_to(kd[:, :, None], (N_HEADS, CHUNK, HEAD_DIM)).copy()
    return jnp.asarray(dm), jnp.asarray(qd), jnp.asarray(kd)


def _ret_kernel(lay, rev, *refs):
    if rev:
        q_ref, k_ref, v_ref, cos_ref, sin_ref, dm_ref, qd_ref, kd_ref, out_ref, s_ref, qt_ref, kt_ref, o_ref = refs
        z_ref = ob_ref = gain_ref = None
    else:
        (q_ref, k_ref, v_ref, cos_ref, sin_ref, dm_ref, qd_ref, kd_ref, z_ref, ob_ref, gain_ref,
         out_ref, s_ref, qt_ref, kt_ref, o_ref) = refs
    tb = lay.tb
    nc = tb // CHUNK
    chunk_decay = [float(np.exp(np.float32(v) * CHUNK)) for v in _ret_log_gamma(rev)]
    _reset_state(lay, rev, s_ref)

    cos = cos_ref[...]
    sin = sin_ref[...]
    for h in range(N_HEADS):
        sl = slice(h * HEAD_DIM, (h + 1) * HEAD_DIM)
        qh = q_ref[:, sl].astype(F32)
        kh = k_ref[:, sl].astype(F32)
        qt_ref[:, sl] = (qh * cos + pltpu.roll(qh, HEAD_DIM // 2, 1) * sin) * QK_SCALE
        kt_ref[:, sl] = kh * cos + pltpu.roll(kh, HEAD_DIM // 2, 1) * sin

    states = [s_ref[h] for h in range(N_HEADS)]
    for ci in range(nc):
        c = (nc - 1 - ci) if rev else ci
        rows = slice(c * CHUNK, (c + 1) * CHUNK)
        heads = [slice(h * HEAD_DIM, (h + 1) * HEAD_DIM) for h in range(N_HEADS)]
        qcs = [qt_ref[rows, sl] for sl in heads]
        kcs = [kt_ref[rows, sl] for sl in heads]
        vcs = [v_ref[rows, sl].astype(BF16) for sl in heads]
        scores = [_dot_nt(q, k) for q, k in zip(qcs, kcs)]
        inters = [_dot(q * qd_ref[h], states[h]) for h, q in enumerate(qcs)]
        upds = [_dot_tn(k * kd_ref[h], v) for h, (k, v) in enumerate(zip(kcs, vcs))]
        intras = [jnp.dot((s * dm_ref[h]).astype(BF16), vcs[h], preferred_element_type=F32)
                  for h, s in enumerate(scores)]
        for h, sl in enumerate(heads):
            o_ref[rows, sl] = intras[h] + inters[h]
            states[h] = chunk_decay[h] * states[h] + upds[h]
    for h in range(N_HEADS):
        s_ref[h] = states[h]
    _finish(rev, o_ref, ob_ref, z_ref, gain_ref, out_ref)


def _mixer_b(proj, cos2, sin2, o_bwd, gain, lay, rev):
    tb, nb = lay.tb, lay.nb
    bi = _blk_index(lay, rev)

    def cur(cb):
        return pl.BlockSpec((tb, W_GROUP), lambda j: (bi(j), cb))

    rot = pl.BlockSpec((tb, HEAD_DIM), lambda j: (lay.pos_block(bi(j)), 0))
    dm, qd, kd = _ret_consts(rev)
    in_specs = [cur(CB_B_Q), cur(CB_B_K), cur(CB_B_V), rot, rot,
                pl.BlockSpec((N_HEADS, CHUNK, CHUNK), lambda j: (0, 0, 0)),
                pl.BlockSpec((N_HEADS, CHUNK, HEAD_DIM), lambda j: (0, 0, 0)),
                pl.BlockSpec((N_HEADS, CHUNK, HEAD_DIM), lambda j: (0, 0, 0))]
    args = [proj, proj, proj, cos2, sin2, dm, qd, kd]
    if not rev:
        in_specs += [cur(CB_B_Z), pl.BlockSpec((tb, W_GROUP), lambda j: (bi(j), 0)),
                     pl.BlockSpec((1, W_GROUP), lambda j: (0, 0))]
        args += [proj, o_bwd, gain.reshape(1, W_GROUP)]
    return pl.pallas_call(
        functools.partial(_ret_kernel, lay, rev),
        grid=(nb,),
        in_specs=in_specs,
        out_specs=pl.BlockSpec((tb, W_GROUP), lambda j: (bi(j), 0)),
        out_shape=jax.ShapeDtypeStruct((lay.n, W_GROUP), F32 if rev else BF16),
        scratch_shapes=[pltpu.VMEM((N_HEADS, HEAD_DIM, HEAD_DIM), F32),
                        pltpu.VMEM((tb, W_GROUP), F32), pltpu.VMEM((tb, W_GROUP), F32),
                        pltpu.VMEM((tb, W_GROUP), F32)],
        compiler_params=_cparams(("arbitrary",)),
        name="mixer_b_retention_bwd" if rev else "mixer_b_retention_fwd",
    )(*args)


HGRN_LEVELS = (32, 16, 8, 4, 2, 1)
LOG2_E = math.log2(math.e)


def _hgrn_consts(rev):
    c = CHUNK
    i = np.arange(c)[:, None]
    t = np.arange(c)[None, :]
    if rev:
        cum = t >= i
        rest = t < i
    else:
        cum = t <= i
        rest = t > i
    blocks = [cum, rest]
    masks = [np.eye(c, dtype=bool)]
    col = np.arange(c)[None, :]
    for s in HGRN_LEVELS:
        start = (i // (2 * s)) * (2 * s)
        later = (i - start) >= s
        if rev:
            ref = start + s
            p = np.where(later, (t >= ref) & (t < i), (t >= i) & (t < ref))
            q_half, k_half = 0, 1
        else:
            ref = start + s - 1
            p = np.where(later, (t > ref) & (t <= i), (t > i) & (t <= ref))
            q_half, k_half = 1, 0
        blocks.append(p)
        same = (i // (2 * s)) == (col // (2 * s))
        masks.append(same & ((i // s) % 2 == q_half) & ((col // s) % 2 == k_half))
    pst = np.concatenate(blocks, axis=0).astype(np.float32)
    pst = np.concatenate([pst, pst], axis=1)
    msk = np.stack(masks, axis=0).astype(np.float32)
    return jnp.asarray(pst, dtype=BF16), jnp.asarray(msk)


def _hgrn_kernel(lay, rev, *refs):
    if rev:
        (q_ref, f_ref, v_ref, llb_ref, l1lb_ref, pst_ref, msk_ref, out_ref,
         st_ref, qs_ref, g_ref, kk_ref, o_ref, e_all_ref) = refs
        z_ref = ob_ref = gain_ref = None
    else:
        (q_ref, f_ref, v_ref, llb_ref, l1lb_ref, pst_ref, msk_ref, z_ref, ob_ref, gain_ref, out_ref,
         st_ref, qs_ref, g_ref, kk_ref, o_ref, e_all_ref) = refs
    tb = lay.tb
    nc = tb // CHUNK
    n_lvl = len(HGRN_LEVELS)
    _reset_state(lay, rev, st_ref)

    qs_ref[...] = (_silu(q_ref[...].astype(F32)) * QK_SCALE).astype(BF16)
    a = llb_ref[...]
    b = l1lb_ref[...] + _log_sigmoid(f_ref[...].astype(F32))
    g2 = (jnp.maximum(a, b) + jnp.log(1.0 + jnp.exp(-jnp.abs(a - b)))) * LOG2_E
    g_ref[...] = g2
    kk_ref[...] = 1.0 - jnp.exp2(g2)

    pst = pst_ref[...]
    last_row = 0 if rev else CHUNK - 1
    heads = [slice(h * HEAD_DIM, (h + 1) * HEAD_DIM) for h in range(N_HEADS)]
    for ci in range(nc):
        c = (nc - 1 - ci) if rev else ci
        g_parts = jnp.concatenate(_split2(g_ref[c * CHUNK:(c + 1) * CHUNK, :]), axis=0)
        e_all_ref[ci] = jnp.exp2(jnp.dot(pst, g_parts, preferred_element_type=F32))

    states = [st_ref[h] for h in range(N_HEADS)]
    for ci in range(nc):
        c = (nc - 1 - ci) if rev else ci
        rows = slice(c * CHUNK, (c + 1) * CHUNK)
        e_ref = e_all_ref.at[ci]
        qcs = [qs_ref[rows, sl] for sl in heads]
        kcs = [kk_ref[rows, sl] for sl in heads]
        vcs = [v_ref[rows, sl].astype(BF16) for sl in heads]
        e_cums = [e_ref[0:CHUNK, sl] for sl in heads]
        scores = [[lax.dot_general(q, k.astype(BF16), (((1,), (1,)), ((), ())), preferred_element_type=F32)
                   for q, k in zip(qcs, kcs)]]
        for lvl in range(n_lvl):
            e_ls = [e_ref[(2 + lvl) * CHUNK:(3 + lvl) * CHUNK, sl] for sl in heads]
            scores.append([_dot_nt(q * e, k * e) for q, k, e in zip(qcs, kcs, e_ls)])
        inters = [_dot_nt(q * e, s) for q, e, s in zip(qcs, e_cums, states)]
        upds = [_dot_tn(v, k * e_ref[CHUNK:2 * CHUNK, sl]) for v, k, sl in zip(vcs, kcs, heads)]
        attns = []
        for h in range(N_HEADS):
            attn = scores[0][h] * msk_ref[0]
            for lvl in range(n_lvl):
                attn = attn + scores[1 + lvl][h] * msk_ref[1 + lvl]
            attns.append(attn.astype(BF16))
        intras = [jnp.dot(a, v, preferred_element_type=F32) for a, v in zip(attns, vcs)]
        for h, sl in enumerate(heads):
            o_ref[rows, sl] = inters[h] + intras[h]
            states[h] = states[h] * e_cums[h][last_row:last_row + 1, :] + upds[h]
    for h in range(N_HEADS):
        st_ref[h] = states[h]
    _finish(rev, o_ref, ob_ref, z_ref, gain_ref, out_ref)


def _mixer_c(proj, log_lb, log1m_lb, o_bwd, gain, lay, rev):
    tb, nb = lay.tb, lay.nb
    bi = _blk_index(lay, rev)

    def cur(cb):
        return pl.BlockSpec((tb, W_GROUP), lambda j: (bi(j), cb))

    vec = pl.BlockSpec((1, W_GROUP), lambda j: (0, 0))
    d = 1 if rev else 0
    pst, msk = _hgrn_consts(rev)
    n_stack = pst.shape[0]
    in_specs = [cur(CB_C_Q), cur(CB_C_FB if rev else CB_C_FF), cur(CB_C_I), vec, vec,
                pl.BlockSpec((n_stack, 2 * CHUNK), lambda j: (0, 0)),
                pl.BlockSpec((msk.shape[0], CHUNK, CHUNK), lambda j: (0, 0, 0))]
    args = [proj, proj, proj, log_lb[d].reshape(1, W_GROUP), log1m_lb[d].reshape(1, W_GROUP), pst, msk]
    if not rev:
        in_specs += [cur(CB_C_Z), pl.BlockSpec((tb, W_GROUP), lambda j: (bi(j), 0)), vec]
        args += [proj, o_bwd, gain.reshape(1, W_GROUP)]
    big = pltpu.VMEM((tb, W_GROUP), F32)
    return pl.pallas_call(
        functools.partial(_hgrn_kernel, lay, rev),
        grid=(nb,),
        in_specs=in_specs,
        out_specs=pl.BlockSpec((tb, W_GROUP), lambda j: (bi(j), 0)),
        out_shape=jax.ShapeDtypeStruct((lay.n, W_GROUP), F32 if rev else BF16),
        scratch_shapes=[pltpu.VMEM((N_HEADS, HEAD_DIM, HEAD_DIM), F32),
                        pltpu.VMEM((tb, W_GROUP), BF16),
                        big, big, big,
                        pltpu.VMEM((tb // CHUNK, n_stack, W_GROUP), F32)],
        compiler_params=_cparams(("arbitrary",)),
        name="mixer_c_hgrn2_bwd" if rev else "mixer_c_hgrn2_fwd",
    )(*args)


def _unit_tri_inverses(a_mats, row, col):
    c = a_mats[0].shape[0]
    eye = jnp.where(row == col, 1.0, 0.0)

    def same_block(size):
        return (row // size) == (col // size)

    base = 8
    m_base = same_block(base)
    ns = [jnp.where(m_base, -a, 0.0).astype(BF16) for a in a_mats]
    n2s = [jnp.dot(n, n, preferred_element_type=F32) for n in ns]
    xs = [eye + n.astype(F32) for n in ns]
    xs = [x + _dot(x, n2) for x, n2 in zip(xs, n2s)]
    n4s = [_dot(n2, n2) for n2 in n2s]
    xs = [x + _dot(x, n4) for x, n4 in zip(xs, n4s)]
    size = base
    while size < c:
        m_off = same_block(2 * size) & jnp.logical_not(same_block(size))
        xbs = [x.astype(BF16) for x in xs]
        ts = [jnp.dot(xb, jnp.where(m_off, a, 0.0).astype(BF16), preferred_element_type=F32)
              for xb, a in zip(xbs, a_mats)]
        xs = [x - jnp.dot(t.astype(BF16), xb, preferred_element_type=F32) for x, t, xb in zip(xs, ts, xbs)]
        size *= 2
    return xs


def _delta_pre_kernel(lay, q_ref, k_ref, v_ref, qp_ref, kp_ref, vp_ref, qn_ref, kn_ref, vn_ref, sm_ref,
                      cw_ref, alog_ref, dtb_ref, qo_ref, ko_ref, vo_ref, g_ref, beta_ref, x_ref):
    tb = lay.tb
    blk = pl.program_id(0)
    keep_prev = jnp.where(lay.seq_start(blk), 0.0, 1.0)
    keep_next = jnp.where(lay.seq_end(blk), 0.0, 1.0)
    half = DN_CONV_WIDTH // 2
    for part, (c_ref, p_ref, n_ref, dst_ref) in enumerate(
            ((q_ref, qp_ref, qn_ref, qo_ref), (k_ref, kp_ref, kn_ref, ko_ref), (v_ref, vp_ref, vn_ref, vo_ref))):
        x_ref[0:HALO_D, :] = p_ref[...].astype(F32) * keep_prev
        x_ref[HALO_D:HALO_D + tb, :] = c_ref[...].astype(F32)
        x_ref[HALO_D + tb:2 * HALO_D + tb, :] = n_ref[...].astype(F32) * keep_next
        wsl = slice(part * W_GROUP, (part + 1) * W_GROUP)
        acc = None
        for k in range(DN_CONV_WIDTH):
            off = HALO_D - half + k
            t = cw_ref[k:k + 1, wsl] * x_ref[off:off + tb, :]
            acc = t if acc is None else acc + t
        y = _silu(acc)
        if part < 2:
            for h in range(N_HEADS):
                sl = slice(h * HEAD_DIM, (h + 1) * HEAD_DIM)
                yh = y[:, sl]
                yn = yh * lax.rsqrt(jnp.sum(yh * yh, axis=-1, keepdims=True) + NORM_EPS)
                dst_ref[:, sl] = (yn * QK_SCALE if part == 0 else yn).astype(BF16)
        else:
            dst_ref[...] = y.astype(BF16)

    sm = sm_ref[...]
    g_ref[...] = -jnp.exp(alog_ref[...]) * _softplus(sm + dtb_ref[...])
    beta_ref[...] = _sigmoid(sm)


def _delta_kernel(lay, rev, *refs):
    if rev:
        (qc_ref, kc_ref, vc_ref, g_ref, beta_ref, out_ref,
         s_ref, o_ref, qd_ref, kd_ref, qk_ref, rhs_ref, last_ref, gq_ref, bm_ref) = refs
        z_ref = ob_ref = gain_ref = None
    else:
        (qc_ref, kc_ref, vc_ref, g_ref, beta_ref, z_ref, ob_ref, gain_ref, out_ref,
         s_ref, o_ref, qd_ref, kd_ref, qk_ref, rhs_ref, last_ref, gq_ref, bm_ref) = refs
    tb = lay.tb
    nc = tb // CHUNK
    _reset_state(lay, rev, s_ref)

    row, col, incl, strict = _causal_masks(rev, CHUNK)
    cum_mat = jnp.where(incl, 1.0, 0.0)
    eye_rows = jnp.where(lax.broadcasted_iota(jnp.int32, (8, SMALL_W), 0)
                         == lax.broadcasted_iota(jnp.int32, (8, SMALL_W), 1), 1.0, 0.0).astype(BF16)
    lane0 = N_HEADS if rev else 0

    cums = [_dot_exact_lhs(cum_mat, g_ref[c * CHUNK:(c + 1) * CHUNK, :]) for c in range(nc)]
    cum_ts = []
    for cum in cums:
        sel = pltpu.roll(cum, SMALL_W - lane0, 1) if lane0 else cum
        cum_t = None
        for part in _split3(sel):
            t = lax.dot_general(eye_rows, part, (((1,), (1,)), ((), ())), preferred_element_type=F32)
            cum_t = t if cum_t is None else cum_t + t
        cum_ts.append(cum_t)
    raws = []
    for c in range(nc):
        rows = slice(c * CHUNK, (c + 1) * CHUNK)
        beta_all = beta_ref[rows, :]
        for h in range(N_HEADS):
            sl = slice(h * HEAD_DIM, (h + 1) * HEAD_DIM)
            qc = qc_ref[rows, sl]
            kc = kc_ref[rows, sl]
            vc = vc_ref[rows, sl]
            ci_b = jnp.broadcast_to(cums[c][:, lane0 + h:lane0 + h + 1], (CHUNK, HEAD_DIM))
            b_lane = 2 * N_HEADS + lane0 + h
            beta_b = jnp.broadcast_to(beta_all[:, b_lane:b_lane + 1], (CHUNK, HEAD_DIM))
            k_beta = kc * beta_b
            raws.append(_dot_nt(jnp.concatenate([k_beta.astype(BF16), qc], axis=0), kc))
            e_ci = jnp.exp(ci_b)
            rhs_ref[c * N_HEADS + h] = jnp.concatenate([vc * beta_b, k_beta * e_ci], axis=1).astype(BF16)
            qd_ref[rows, sl] = qc * e_ci
            c_last = ci_b[0:1, :] if rev else ci_b[CHUNK - 1:CHUNK, :]
            kd_ref[rows, sl] = (kc * jnp.exp(c_last - ci_b)).astype(BF16)
            last_ref[c * N_HEADS + h] = jnp.broadcast_to(jnp.exp(c_last), (8, HEAD_DIM))
    a_mats = []
    for idx, raw in enumerate(raws):
        c, h = divmod(idx, N_HEADS)
        ci_col = cums[c][:, lane0 + h:lane0 + h + 1]
        cj_row = cum_ts[c][h:h + 1, :]
        decay = jnp.exp(jnp.where(incl, ci_col - cj_row, NEG_BIG))
        a_mats.append(jnp.where(strict, raw[0:CHUNK] * decay, 0.0))
        qk_ref[idx] = (raw[CHUNK:2 * CHUNK] * decay).astype(BF16)

    t_invs = _unit_tri_inverses(a_mats, row, col)
    where = [(slice((idx // N_HEADS) * CHUNK, (idx // N_HEADS + 1) * CHUNK),
              slice((idx % N_HEADS) * HEAD_DIM, (idx % N_HEADS + 1) * HEAD_DIM)) for idx in range(len(t_invs))]
    uws = [jnp.dot(t_inv.astype(BF16), rhs_ref[idx], preferred_element_type=F32).astype(BF16)
           for idx, t_inv in enumerate(t_invs)]
    kuws = [lax.dot_general(kd_ref[rows, sl], uw, (((0,), (0,)), ((), ())), preferred_element_type=F32)
            for (rows, sl), uw in zip(where, uws)]
    quws = [jnp.dot(qk_ref[idx], uw, preferred_element_type=F32) for idx, uw in enumerate(uws)]
    for idx, ((rows, sl), kuw, quw) in enumerate(zip(where, kuws, quws)):
        bm_ref[idx] = kuw[:, 0:HEAD_DIM]
        gq_ref[idx, 0:HEAD_DIM, :] = kuw[:, HEAD_DIM:2 * HEAD_DIM].astype(BF16)
        gq_ref[idx, HEAD_DIM:HEAD_DIM + CHUNK, :] = (qd_ref[rows, sl] - quw[:, HEAD_DIM:2 * HEAD_DIM]).astype(BF16)
        o_ref[rows, sl] = quw[:, 0:HEAD_DIM]

    states = [s_ref[h] for h in range(N_HEADS)]
    for ci in range(nc):
        c = (nc - 1 - ci) if rev else ci
        rows = slice(c * CHUNK, (c + 1) * CHUNK)
        for h in range(N_HEADS):
            idx = c * N_HEADS + h
            sl = slice(h * HEAD_DIM, (h + 1) * HEAD_DIM)
            gs = jnp.dot(gq_ref[idx], states[h].astype(BF16), preferred_element_type=F32)
            o_ref[rows, sl] += gs[HEAD_DIM:HEAD_DIM + CHUNK]
            states[h] = states[h] * last_ref[idx][0:1, :] - gs[0:HEAD_DIM] + bm_ref[idx]
    for h in range(N_HEADS):
        s_ref[h] = states[h]
    _finish(rev, o_ref, ob_ref, z_ref, gain_ref, out_ref)


def _mixer_d_pre(proj, proj_small, conv_w, a_log, dt_bias, lay):
    tb, nb = lay.tb, lay.nb
    hb = tb // HALO_D
    n_halo = lay.n // HALO_D

    def cur(cb):
        return pl.BlockSpec((tb, W_GROUP), lambda i: (i, cb))

    def prev(cb):
        return pl.BlockSpec((HALO_D, W_GROUP), lambda i: (jnp.maximum(i * hb - 1, 0), cb))

    def nxt(cb):
        return pl.BlockSpec((HALO_D, W_GROUP), lambda i: (jnp.minimum((i + 1) * hb, n_halo - 1), cb))

    small = pl.BlockSpec((1, SMALL_W), lambda i: (0, 0))
    pad_lanes = SMALL_W - 2 * N_HEADS
    alog_row = jnp.pad(a_log.reshape(1, 2 * N_HEADS), ((0, 0), (0, pad_lanes)))
    dtb_row = jnp.pad(dt_bias.reshape(1, 2 * N_HEADS), ((0, 0), (0, pad_lanes)))
    wide = pl.BlockSpec((tb, W_GROUP), lambda i: (i, 0))
    narrow = pl.BlockSpec((tb, SMALL_W), lambda i: (i, 0))
    return pl.pallas_call(
        functools.partial(_delta_pre_kernel, lay),
        grid=(nb,),
        in_specs=[cur(CB_D_Q), cur(CB_D_K), cur(CB_D_V), prev(CB_D_Q), prev(CB_D_K), prev(CB_D_V),
                  nxt(CB_D_Q), nxt(CB_D_K), nxt(CB_D_V),
                  pl.BlockSpec((tb, SMALL_W), lambda i: (i, 0)),
                  pl.BlockSpec((DN_CONV_WIDTH, 3 * W_GROUP), lambda i: (0, 0)), small, small],
        out_specs=[wide, wide, wide, narrow, narrow],
        out_shape=[jax.ShapeDtypeStruct((lay.n, W_GROUP), BF16)] * 3
        + [jax.ShapeDtypeStruct((lay.n, SMALL_W), F32)] * 2,
        scratch_shapes=[pltpu.VMEM((tb + 2 * HALO_D, W_GROUP), F32)],
        compiler_params=_cparams(("arbitrary",)),
        name="mixer_d_deltanet_pre",
    )(*([proj] * 9), proj_small, conv_w, alog_row, dtb_row)


def _mixer_d(pre, proj, o_bwd, gain, lay, rev):
    tb, nb = lay.tb, lay.nb
    bi = _blk_index(lay, rev)
    wide = pl.BlockSpec((tb, W_GROUP), lambda j: (bi(j), 0))
    narrow = pl.BlockSpec((tb, SMALL_W), lambda j: (bi(j), 0))
    in_specs = [wide, wide, wide, narrow, narrow]
    args = list(pre)
    if not rev:
        in_specs += [pl.BlockSpec((tb, W_GROUP), lambda j: (bi(j), CB_D_Z)), wide,
                     pl.BlockSpec((1, W_GROUP), lambda j: (0, 0))]
        args += [proj, o_bwd, gain.reshape(1, W_GROUP)]
    big = pltpu.VMEM((tb, W_GROUP), F32)
    nc = tb // CHUNK
    return pl.pallas_call(
        functools.partial(_delta_kernel, lay, rev),
        grid=(nb,),
        in_specs=in_specs,
        out_specs=wide,
        out_shape=jax.ShapeDtypeStruct((lay.n, W_GROUP), F32 if rev else BF16),
        scratch_shapes=[pltpu.VMEM((N_HEADS, HEAD_DIM, HEAD_DIM), F32),
                        big,
                        big,
                        pltpu.VMEM((tb, W_GROUP), BF16),
                        pltpu.VMEM((nc * N_HEADS, CHUNK, CHUNK), BF16),
                        pltpu.VMEM((nc * N_HEADS, CHUNK, 2 * HEAD_DIM), BF16),
                        pltpu.VMEM((nc * N_HEADS, 8, HEAD_DIM), F32),
                        pltpu.VMEM((nc * N_HEADS, HEAD_DIM + CHUNK, HEAD_DIM), BF16),
                        pltpu.VMEM((nc * N_HEADS, HEAD_DIM, HEAD_DIM), F32)],
        compiler_params=_cparams(("arbitrary",)),
        name="mixer_d_deltanet_bwd" if rev else "mixer_d_deltanet_fwd",
    )(*args)


def _regroup_w_in(w):
    d_qkv_end = 15 * W_GROUP
    small = w[:, d_qkv_end:d_qkv_end + 4 * N_HEADS]
    d_z = w[:, d_qkv_end + 4 * N_HEADS:]
    pad = jnp.zeros((w.shape[0], SMALL_W - 4 * N_HEADS), w.dtype)
    main = jnp.concatenate([w[:, :d_qkv_end], d_z], axis=1).astype(BF16)
    return main, jnp.concatenate([small, pad], axis=1).astype(BF16)


def _rotary_tables(t_max):
    half = HEAD_DIM // 2
    inv = 1.0 / (ROPE_BASE ** (jnp.arange(half, dtype=F32) / half))
    ang = jnp.arange(t_max, dtype=F32)[:, None] * inv[None, :]
    cos, sin = jnp.cos(ang), jnp.sin(ang)
    return jnp.concatenate([cos, cos], axis=-1), jnp.concatenate([-sin, sin], axis=-1)


def _pick_tile(t_prompt, t_sample, want):
    tile = want
    while t_prompt % tile or t_sample % tile:
        tile //= 2
    return tile


def kernel(x_prompt, x_sample, c_prompt, c_sample, ada_w, ada_b, norm_g, w_in, conv_a_w, conv_a_b, ln_a_g, ln_a_b,
           ret_norm_g, hgrn_lb_logits, hgrn_norm_g, dn_conv_w, dn_a_log, dn_dt_bias, dn_norm_g, w_out, final_g):
    depth = w_in.shape[0]
    n_prompt, t_prompt, _ = x_prompt.shape
    n_sample, t_sample, _ = x_sample.shape
    tb = _pick_tile(t_prompt, t_sample, 256)
    tm_in = _pick_tile(t_prompt, t_sample, 1024)
    tm_out = _pick_tile(t_prompt, t_sample, 512)
    lay = _Layout(t_prompt, n_prompt, t_sample, n_sample, tb)

    xs = (x_prompt.reshape(-1, D_MODEL), x_sample.reshape(-1, D_MODEL))
    c_all = jnp.concatenate([c_prompt, c_sample], axis=0)
    n_c = c_all.shape[0]
    c_all = jnp.pad(c_all, ((0, (-n_c) % 8), (0, 0)))
    mod = _modulation(c_all, ada_w, ada_b)

    lb = jnp.cumsum(jax.nn.softmax(hgrn_lb_logits.astype(F32), axis=0), axis=0)
    lb = lb - lb[:1]
    log_lb = jnp.log(lb)
    log1m_lb = jnp.log1p(-lb)
    cos2, sin2 = _rotary_tables(max(t_prompt, t_sample))

    for l in range(depth):
        mod_l = mod[l].reshape(mod.shape[1], 1, 3 * D_MODEL)
        proj, proj_small = _in_proj(xs, mod_l, norm_g[l], *_regroup_w_in(w_in[l]), lay, tm_in)
        m_a = _mixer_a(proj, conv_a_w[l], conv_a_b[l], ln_a_g[l], ln_a_b[l], lay)
        ob = _mixer_b(proj, cos2, sin2, None, None, lay, True)
        m_b = _mixer_b(proj, cos2, sin2, ob, ret_norm_g[l], lay, False)
        oc = _mixer_c(proj, log_lb[l], log1m_lb[l], None, None, lay, True)
        m_c = _mixer_c(proj, log_lb[l], log1m_lb[l], oc, hgrn_norm_g[l], lay, False)
        d_pre = _mixer_d_pre(proj, proj_small, dn_conv_w[l], dn_a_log[l], dn_dt_bias[l], lay)
        od = _mixer_d(d_pre, proj, None, None, lay, True)
        m_d = _mixer_d(d_pre, proj, od, dn_norm_g[l], lay, False)
        xs = _out_proj(xs, (m_a, m_b, m_c, m_d), mod_l, w_out[l].astype(BF16), final_g, lay, tm_out,
                       final=(l == depth - 1))

    return (xs[0].reshape(n_prompt, t_prompt, D_MODEL), xs[1].reshape(n_sample, t_sample, D_MODEL))
```

```python
import functools
import math

import numpy as np
import jax
import jax.numpy as jnp
from jax import lax
from jax.experimental import pallas as pl
from jax.experimental.pallas import tpu as pltpu

F32 = jnp.float32
BF16 = jnp.bfloat16

D_MODEL = 1024
W_GROUP = 512
HEAD_DIM = 128
SUBLANES = 8
N_HEADS = 4
CONV_A_WIDTH = 31
DN_CONV_WIDTH = 5
CHUNK = 64
ROPE_BASE = 10000.0
RET_DECAY_OFFSET = 5.0
NORM_EPS = 1e-6
QK_SCALE = HEAD_DIM ** -0.5
NEG_BIG = -1e30

N_COL_BLOCKS = 16
SMALL_W = 128
SUB_BLOCK = 256
D_PROJ = N_COL_BLOCKS * W_GROUP
PROJ_TN = 2048
HALO_A = 16
HALO_D = 16
VMEM_LIMIT = 56 * 1024 * 1024

CB_A_VAL, CB_A_GLU, CB_A_Z = 0, 1, 2
CB_B_Q, CB_B_K, CB_B_V, CB_B_Z = 3, 4, 5, 6
CB_C_Q, CB_C_FF, CB_C_FB, CB_C_I, CB_C_Z = 7, 8, 9, 10, 11
CB_D_Q, CB_D_K, CB_D_V, CB_D_Z = 12, 13, 14, 15


def _dot(a, b):
    return jnp.dot(a.astype(BF16), b.astype(BF16), preferred_element_type=F32)


def _dot_nt(a, b):
    return lax.dot_general(a.astype(BF16), b.astype(BF16), (((1,), (1,)), ((), ())), preferred_element_type=F32)


def _dot_tn(a, b):
    return lax.dot_general(a.astype(BF16), b.astype(BF16), (((0,), (0,)), ((), ())), preferred_element_type=F32)


def _split2(x):
    hi = x.astype(BF16)
    lo = (x - hi.astype(F32)).astype(BF16)
    return hi, lo


def _split3(x):
    hi = x.astype(BF16)
    r = x - hi.astype(F32)
    mid = r.astype(BF16)
    lo = (r - mid.astype(F32)).astype(BF16)
    return hi, mid, lo


def _dot_exact_lhs(m, x):
    mb = m.astype(BF16)
    acc = None
    for part in _split3(x):
        t = jnp.dot(mb, part, preferred_element_type=F32)
        acc = t if acc is None else acc + t
    return acc


def _sigmoid(x):
    return 1.0 / (1.0 + jnp.exp(-x))


def _silu(x):
    return x * _sigmoid(x)


def _softplus(x):
    return jnp.maximum(x, 0.0) + jnp.log(1.0 + jnp.exp(-jnp.abs(x)))


def _log_sigmoid(x):
    return jnp.minimum(x, 0.0) - jnp.log(1.0 + jnp.exp(-jnp.abs(x)))


class _Layout:
    def __init__(self, t_prompt, n_prompt, t_sample, n_sample, tb):
        self.tp = t_prompt * n_prompt
        self.t_prompt = t_prompt
        self.ts = t_sample
        self.n = self.tp + t_sample * n_sample
        self.tb = tb
        assert t_prompt % tb == 0 and t_sample % tb == 0
        self.nb = self.n // tb

    def seq_start(self, blk):
        s = blk * self.tb
        return jnp.where(s < self.tp, s % self.t_prompt == 0, (s - self.tp) % self.ts == 0)

    def seq_end(self, blk):
        e = (blk + 1) * self.tb
        return jnp.where(e <= self.tp, e % self.t_prompt == 0, (e - self.tp) % self.ts == 0)

    def pos_block(self, blk):
        s = blk * self.tb
        return jnp.where(s < self.tp, (s % self.t_prompt) // self.tb, ((s - self.tp) % self.ts) // self.tb)

    def batch_index(self, blk, rows):
        s = blk * rows
        return jnp.where(s < self.tp, s // self.t_prompt, self.tp // self.t_prompt + (s - self.tp) // self.ts)


def _cparams(sem):
    return pltpu.CompilerParams(dimension_semantics=sem, vmem_limit_bytes=VMEM_LIMIT)


def _mod_kernel(c_ref, w_ref, b_ref, o_ref):
    c = c_ref[...]
    o_ref[0] = jnp.dot(_silu(c), w_ref[0], preferred_element_type=F32,
                       precision=lax.Precision.HIGHEST) + b_ref[0]


def _modulation(c_all, ada_w, ada_b):
    depth = ada_w.shape[0]
    nb = c_all.shape[0]
    return pl.pallas_call(
        _mod_kernel,
        grid=(depth,),
        in_specs=[pl.BlockSpec((nb, D_MODEL), lambda l: (0, 0)),
                  pl.BlockSpec((1, D_MODEL, 3 * D_MODEL), lambda l: (l, 0, 0)),
                  pl.BlockSpec((1, 1, 3 * D_MODEL), lambda l: (l, 0, 0))],
        out_specs=pl.BlockSpec((1, nb, 3 * D_MODEL), lambda l: (l, 0, 0)),
        out_shape=jax.ShapeDtypeStruct((depth, nb, 3 * D_MODEL), F32),
        compiler_params=_cparams(("arbitrary",)),
        name="adaln_modulation",
    )(c_all, ada_w, ada_b.reshape(depth, 1, 3 * D_MODEL))


def _load_x(x_refs, i, npb):
    if len(x_refs) == 1:
        return x_refs[0][...]
    return jnp.where(i < npb, x_refs[0][...], x_refs[1][...])


def _x_specs(n_x, tm, npb, row_block):
    def spec(fn):
        return pl.BlockSpec((tm, D_MODEL), lambda *ids: (fn(row_block(*ids)), 0))

    if n_x == 1:
        return [spec(lambda i: i)]
    return [spec(lambda i: jnp.minimum(i, npb - 1)), spec(lambda i: jnp.maximum(i - npb, 0))]


def _in_proj_kernel(n_x, npb, *refs):
    x_refs = refs[:n_x]
    mod_ref, g_ref, w_ref, ws_ref, o_ref, os_ref, h_ref = refs[n_x:]

    @pl.when(pl.program_id(1) == 0)
    def _():
        x = _load_x(x_refs, pl.program_id(0), npb)
        y = x * lax.rsqrt(jnp.mean(x * x, axis=-1, keepdims=True) + NORM_EPS) * g_ref[...]
        shift = mod_ref[0, :, 0:D_MODEL]
        scale = mod_ref[0, :, D_MODEL:2 * D_MODEL]
        h = (y * (1.0 + scale) + shift).astype(BF16)
        h_ref[...] = h
        os_ref[...] = jnp.dot(h, ws_ref[...], preferred_element_type=F32)

    o_ref[...] = jnp.dot(h_ref[...], w_ref[...], preferred_element_type=F32).astype(BF16)


def _in_proj(xs, mod, norm_g, w_main, w_small, lay, tm):
    n = lay.n
    npb = lay.tp // tm
    lay_m = _Layout(lay.t_prompt, lay.tp // lay.t_prompt, lay.ts, (lay.n - lay.tp) // lay.ts, tm)
    return pl.pallas_call(
        functools.partial(_in_proj_kernel, len(xs), npb),
        grid=(n // tm, D_PROJ // PROJ_TN),
        in_specs=_x_specs(len(xs), tm, npb, lambda i, j: i) + [
                  pl.BlockSpec((1, 1, 3 * D_MODEL), lambda i, j: (lay_m.batch_index(i, tm), 0, 0)),
                  pl.BlockSpec((1, D_MODEL), lambda i, j: (0, 0)),
                  pl.BlockSpec((D_MODEL, PROJ_TN), lambda i, j: (0, j)),
                  pl.BlockSpec((D_MODEL, SMALL_W), lambda i, j: (0, 0))],
        out_specs=[pl.BlockSpec((tm, PROJ_TN), lambda i, j: (i, j)),
                   pl.BlockSpec((tm, SMALL_W), lambda i, j: (i, 0))],
        out_shape=[jax.ShapeDtypeStruct((n, D_PROJ), BF16), jax.ShapeDtypeStruct((n, SMALL_W), F32)],
        scratch_shapes=[pltpu.VMEM((tm, D_MODEL), BF16)],
        compiler_params=_cparams(("arbitrary", "arbitrary")),
        name="in_proj",
    )(*xs, mod, norm_g.reshape(1, D_MODEL), w_main, w_small)


def _out_proj_kernel(final, n_x, npb, *refs):
    x_refs = refs[:n_x]
    ma_ref, mb_ref, mc_ref, md_ref, mod_ref, w_ref, fg_ref = refs[n_x:n_x + 7]
    o_refs = refs[n_x + 7:]
    i = pl.program_id(0)
    acc = jnp.dot(ma_ref[...], w_ref[0:W_GROUP, :], preferred_element_type=F32)
    acc += jnp.dot(mb_ref[...], w_ref[W_GROUP:2 * W_GROUP, :], preferred_element_type=F32)
    acc += jnp.dot(mc_ref[...], w_ref[2 * W_GROUP:3 * W_GROUP, :], preferred_element_type=F32)
    acc += jnp.dot(md_ref[...], w_ref[3 * W_GROUP:4 * W_GROUP, :], preferred_element_type=F32)
    gate = mod_ref[0, :, 2 * D_MODEL:3 * D_MODEL]
    y = _load_x(x_refs, i, npb) + gate * acc
    if not final:
        o_refs[0][...] = y
        return
    y = y * lax.rsqrt(jnp.mean(y * y, axis=-1, keepdims=True) + NORM_EPS) * fg_ref[...]

    @pl.when(i < npb)
    def _():
        o_refs[0][...] = y

    @pl.when(i >= npb)
    def _():
        o_refs[1][...] = y


def _out_proj(xs, mixed, mod, w_out_bf, final_g, lay, tm, final):
    n = lay.n
    npb = lay.tp // tm
    lay_m = _Layout(lay.t_prompt, lay.tp // lay.t_prompt, lay.ts, (lay.n - lay.tp) // lay.ts, tm)
    mspec = pl.BlockSpec((tm, W_GROUP), lambda i: (i, 0))
    if final:
        out_specs = [pl.BlockSpec((tm, D_MODEL), lambda i: (jnp.minimum(i, npb - 1), 0)),
                     pl.BlockSpec((tm, D_MODEL), lambda i: (jnp.maximum(i - npb, 0), 0))]
        out_shape = [jax.ShapeDtypeStruct((lay.tp, D_MODEL), F32), jax.ShapeDtypeStruct((n - lay.tp, D_MODEL), F32)]
    else:
        out_specs = [pl.BlockSpec((tm, D_MODEL), lambda i: (i, 0))]
        out_shape = [jax.ShapeDtypeStruct((n, D_MODEL), F32)]
    out = pl.pallas_call(
        functools.partial(_out_proj_kernel, final, len(xs), npb),
        grid=(n // tm,),
        in_specs=_x_specs(len(xs), tm, npb, lambda i: i) + [mspec, mspec, mspec, mspec,
                  pl.BlockSpec((1, 1, 3 * D_MODEL), lambda i: (lay_m.batch_index(i, tm), 0, 0)),
                  pl.BlockSpec((4 * W_GROUP, D_MODEL), lambda i: (0, 0)),
                  pl.BlockSpec((1, D_MODEL), lambda i: (0, 0))],
        out_specs=out_specs,
        out_shape=out_shape,
        compiler_params=_cparams(("arbitrary",)),
        name="out_proj",
    )(*xs, *mixed, mod, w_out_bf, final_g.reshape(1, D_MODEL))
    return tuple(out)


A_ROWS = 32


def _mixer_a_kernel(lay, val_ref, glu_ref, z_ref, vp_ref, gp_ref, vn_ref, gn_ref,
                    cw_ref, cb_ref, lg_ref, lb_ref, o_ref, u_ref, ush_ref, acc_ref):
    tb = lay.tb
    blk = pl.program_id(0)
    keep_prev = jnp.where(lay.seq_start(blk), 0.0, 1.0)
    keep_next = jnp.where(lay.seq_end(blk), 0.0, 1.0)
    u_ref[0:HALO_A, :] = vp_ref[...].astype(F32) * _sigmoid(gp_ref[...].astype(F32)) * keep_prev
    u_ref[HALO_A:HALO_A + tb, :] = val_ref[...].astype(F32) * _sigmoid(glu_ref[...].astype(F32))
    u_ref[HALO_A + tb:2 * HALO_A + tb, :] = vn_ref[...].astype(F32) * _sigmoid(gn_ref[...].astype(F32)) * keep_next
    half = CONV_A_WIDTH // 2
    n_sh = tb + 2 * HALO_A - SUBLANES
    for r in range(1, SUBLANES):
        ush_ref[r - 1] = u_ref[r:r + n_sh, :]

    def tile(t, carry):
        r0 = pl.multiple_of(t * A_ROWS, A_ROWS)
        acc = jnp.zeros((A_ROWS, W_GROUP), F32) + cb_ref[...]
        for k in range(CONV_A_WIDTH):
            a, r = divmod(HALO_A - half + k, SUBLANES)
            win = pl.ds(r0 + a * SUBLANES, A_ROWS)
            src = u_ref[win, :] if r == 0 else ush_ref[r - 1, win, :]
            acc = acc + cw_ref[k:k + 1, :] * src
        acc_ref[pl.ds(r0, A_ROWS), :] = acc
        return carry

    lax.fori_loop(0, tb // A_ROWS, tile, 0)
    acc = acc_ref[...]
    mu = jnp.mean(acc, axis=-1, keepdims=True)
    xc = acc - mu
    var = jnp.mean(xc * xc, axis=-1, keepdims=True)
    y = xc * lax.rsqrt(var + NORM_EPS) * lg_ref[...] + lb_ref[...]
    o_ref[...] = (_silu(y) * _silu(z_ref[...].astype(F32))).astype(BF16)


def _mixer_a(proj, conv_w, conv_b, ln_g, ln_b, lay):
    tb, nb = lay.tb, lay.nb
    hb = tb // HALO_A
    n_halo = lay.n // HALO_A

    def cur(cb):
        return pl.BlockSpec((tb, W_GROUP), lambda i: (i, cb))

    def prev(cb):
        return pl.BlockSpec((HALO_A, W_GROUP), lambda i: (jnp.maximum(i * hb - 1, 0), cb))

    def nxt(cb):
        return pl.BlockSpec((HALO_A, W_GROUP), lambda i: (jnp.minimum((i + 1) * hb, n_halo - 1), cb))

    vec = pl.BlockSpec((1, W_GROUP), lambda i: (0, 0))
    return pl.pallas_call(
        functools.partial(_mixer_a_kernel, lay),
        grid=(nb,),
        in_specs=[cur(CB_A_VAL), cur(CB_A_GLU), cur(CB_A_Z), prev(CB_A_VAL), prev(CB_A_GLU),
                  nxt(CB_A_VAL), nxt(CB_A_GLU),
                  pl.BlockSpec((CONV_A_WIDTH, W_GROUP), lambda i: (0, 0)), vec, vec, vec],
        out_specs=pl.BlockSpec((tb, W_GROUP), lambda i: (i, 0)),
        out_shape=jax.ShapeDtypeStruct((lay.n, W_GROUP), BF16),
        scratch_shapes=[pltpu.VMEM((tb + 2 * HALO_A, W_GROUP), F32),
                        pltpu.VMEM((SUBLANES - 1, tb + 2 * HALO_A - SUBLANES, W_GROUP), F32),
                        pltpu.VMEM((tb, W_GROUP), F32)],
        compiler_params=_cparams(("arbitrary",)),
        name="mixer_a_conv",
    )(proj, proj, proj, proj, proj, proj, proj, conv_w, conv_b.reshape(1, W_GROUP),
      ln_g.reshape(1, W_GROUP), ln_b.reshape(1, W_GROUP))


def _blk_index(lay, rev):
    nb = lay.nb
    return (lambda j: nb - 1 - j) if rev else (lambda j: j)


def _reset_state(lay, rev, s_ref):
    blk = _blk_index(lay, rev)(pl.program_id(0))
    boundary = lay.seq_end(blk) if rev else lay.seq_start(blk)

    @pl.when(boundary)
    def _():
        s_ref[...] = jnp.zeros(s_ref.shape, s_ref.dtype)


def _sub_blocks(lay, rev, body):
    nsb = lay.tb // SUB_BLOCK

    def trip(i, carry):
        piece = (nsb - 1 - i) if rev else i
        body(pl.multiple_of(piece * SUB_BLOCK, SUB_BLOCK))
        return carry

    lax.fori_loop(0, nsb, trip, 0)


def _at(base, off, n):
    return pl.ds(pl.multiple_of(base + off, CHUNK), n)


def _finish(rev, base, o_ref, ob_ref, z_ref, gain_ref, out_ref):
    rows = _at(base, 0, SUB_BLOCK)
    if rev:
        out_ref[rows, :] = o_ref[...]
        return
    for h in range(N_HEADS):
        sl = slice(h * HEAD_DIM, (h + 1) * HEAD_DIM)
        o = o_ref[:, sl] + ob_ref[rows, sl]
        y = o * lax.rsqrt(jnp.mean(o * o, axis=-1, keepdims=True) + NORM_EPS) * gain_ref[:, sl]
        out_ref[rows, sl] = (y * _silu(z_ref[rows, sl].astype(F32))).astype(BF16)


def _causal_masks(rev, c):
    row = lax.broadcasted_iota(jnp.int32, (c, c), 0)
    col = lax.broadcasted_iota(jnp.int32, (c, c), 1)
    incl = (row <= col) if rev else (row >= col)
    strict = (row < col) if rev else (row > col)
    return row, col, incl, strict


def _ret_log_gamma(rev):
    lg = np.log1p(-np.exp2(-RET_DECAY_OFFSET - np.arange(N_HEADS, dtype=np.float64)))
    return lg[::-1].copy() if rev else lg


def _ret_consts(rev):
    lg = _ret_log_gamma(rev).astype(np.float32)
    i = np.arange(CHUNK, dtype=np.float32)
    idx = (CHUNK - 1 - i) if rev else i
    rel = idx[:, None] - idx[None, :]
    dm = np.where(rel >= 0, np.exp(lg[:, None, None] * np.maximum(rel, 0.0)), 0.0).astype(np.float32)
    qd = np.exp(lg[:, None] * (idx + 1.0)).astype(np.float32)
    kd = np.exp(lg[:, None] * (CHUNK - 1.0 - idx)).astype(np.float32)
    qd = np.broadcast_to(qd[:, :, None], (N_HEADS, CHUNK, HEAD_DIM)).copy()
    kd = np.broadcast_to(kd[:, :, None], (N_HEADS, CHUNK, HEAD_DIM)).copy()
    return jnp.asarray(dm), jnp.asarray(qd), jnp.asarray(kd)


def _ret_kernel(lay, rev, *refs):
    if rev:
        q_ref, k_ref, v_ref, cos_ref, sin_ref, dm_ref, qd_ref, kd_ref, out_ref, s_ref, qt_ref, kt_ref, o_ref = refs
        z_ref = ob_ref = gain_ref = None
    else:
        (q_ref, k_ref, v_ref, cos_ref, sin_ref, dm_ref, qd_ref, kd_ref, z_ref, ob_ref, gain_ref,
         out_ref, s_ref, qt_ref, kt_ref, o_ref) = refs
    nc = SUB_BLOCK // CHUNK
    chunk_decay = [float(np.exp(np.float32(v) * CHUNK)) for v in _ret_log_gamma(rev)]
    heads = [slice(h * HEAD_DIM, (h + 1) * HEAD_DIM) for h in range(N_HEADS)]
    _reset_state(lay, rev, s_ref)

    def piece(base):
        blk = _at(base, 0, SUB_BLOCK)
        cos = cos_ref[blk, :]
        sin = sin_ref[blk, :]
        for sl in heads:
            qh = q_ref[blk, sl].astype(F32)
            kh = k_ref[blk, sl].astype(F32)
            qt_ref[:, sl] = (qh * cos + pltpu.roll(qh, HEAD_DIM // 2, 1) * sin) * QK_SCALE
            kt_ref[:, sl] = kh * cos + pltpu.roll(kh, HEAD_DIM // 2, 1) * sin

        states = [s_ref[h] for h in range(N_HEADS)]
        for ci in range(nc):
            c = (nc - 1 - ci) if rev else ci
            rows = slice(c * CHUNK, (c + 1) * CHUNK)
            qcs = [qt_ref[rows, sl] for sl in heads]
            kcs = [kt_ref[rows, sl] for sl in heads]
            vcs = [v_ref[_at(base, c * CHUNK, CHUNK), sl].astype(BF16) for sl in heads]
            scores = [_dot_nt(q, k) for q, k in zip(qcs, kcs)]
            inters = [_dot(q * qd_ref[h], states[h]) for h, q in enumerate(qcs)]
            upds = [_dot_tn(k * kd_ref[h], v) for h, (k, v) in enumerate(zip(kcs, vcs))]
            intras = [jnp.dot((s * dm_ref[h]).astype(BF16), vcs[h], preferred_element_type=F32)
                      for h, s in enumerate(scores)]
            for h, sl in enumerate(heads):
                o_ref[rows, sl] = intras[h] + inters[h]
                states[h] = chunk_decay[h] * states[h] + upds[h]
        for h in range(N_HEADS):
            s_ref[h] = states[h]
        _finish(rev, base, o_ref, ob_ref, z_ref, gain_ref, out_ref)

    _sub_blocks(lay, rev, piece)


def _mixer_b(proj, cos2, sin2, o_bwd, gain, lay, rev):
    tb, nb = lay.tb, lay.nb
    bi = _blk_index(lay, rev)

    def cur(cb):
        return pl.BlockSpec((tb, W_GROUP), lambda j: (bi(j), cb))

    rot = pl.BlockSpec((tb, HEAD_DIM), lambda j: (lay.pos_block(bi(j)), 0))
    dm, qd, kd = _ret_consts(rev)
    in_specs = [cur(CB_B_Q), cur(CB_B_K), cur(CB_B_V), rot, rot,
                pl.BlockSpec((N_HEADS, CHUNK, CHUNK), lambda j: (0, 0, 0)),
                pl.BlockSpec((N_HEADS, CHUNK, HEAD_DIM), lambda j: (0, 0, 0)),
                pl.BlockSpec((N_HEADS, CHUNK, HEAD_DIM), lambda j: (0, 0, 0))]
    args = [proj, proj, proj, cos2, sin2, dm, qd, kd]
    if not rev:
        in_specs += [cur(CB_B_Z), pl.BlockSpec((tb, W_GROUP), lambda j: (bi(j), 0)),
                     pl.BlockSpec((1, W_GROUP), lambda j: (0, 0))]
        args += [proj, o_bwd, gain.reshape(1, W_GROUP)]
    return pl.pallas_call(
        functools.partial(_ret_kernel, lay, rev),
        grid=(nb,),
        in_specs=in_specs,
        out_specs=pl.BlockSpec((tb, W_GROUP), lambda j: (bi(j), 0)),
        out_shape=jax.ShapeDtypeStruct((lay.n, W_GROUP), F32 if rev else BF16),
        scratch_shapes=[pltpu.VMEM((N_HEADS, HEAD_DIM, HEAD_DIM), F32),
                        pltpu.VMEM((SUB_BLOCK, W_GROUP), F32), pltpu.VMEM((SUB_BLOCK, W_GROUP), F32),
                        pltpu.VMEM((SUB_BLOCK, W_GROUP), F32)],
        compiler_params=_cparams(("arbitrary",)),
        name="mixer_b_retention_bwd" if rev else "mixer_b_retention_fwd",
    )(*args)


HGRN_LEVELS = (32, 16, 8, 4, 2, 1)
LOG2_E = math.log2(math.e)


def _hgrn_consts(rev):
    c = CHUNK
    i = np.arange(c)[:, None]
    t = np.arange(c)[None, :]
    if rev:
        cum = t >= i
        rest = t < i
    else:
        cum = t <= i
        rest = t > i
    blocks = [cum, rest]
    masks = [np.eye(c, dtype=bool)]
    col = np.arange(c)[None, :]
    for s in HGRN_LEVELS:
        start = (i // (2 * s)) * (2 * s)
        later = (i - start) >= s
        if rev:
            ref = start + s
            p = np.where(later, (t >= ref) & (t < i), (t >= i) & (t < ref))
            q_half, k_half = 0, 1
        else:
            ref = start + s - 1
            p = np.where(later, (t > ref) & (t <= i), (t > i) & (t <= ref))
            q_half, k_half = 1, 0
        blocks.append(p)
        same = (i // (2 * s)) == (col // (2 * s))
        masks.append(same & ((i // s) % 2 == q_half) & ((col // s) % 2 == k_half))
    pst = np.concatenate(blocks, axis=0).astype(np.float32)
    pst = np.concatenate([pst, pst], axis=1)
    msk = np.stack(masks, axis=0).astype(np.float32)
    return jnp.asarray(pst, dtype=BF16), jnp.asarray(msk)


def _hgrn_kernel(lay, rev, *refs):
    if rev:
        (q_ref, f_ref, v_ref, llb_ref, l1lb_ref, pst_ref, msk_ref, out_ref,
         st_ref, qs_ref, g_ref, kk_ref, o_ref, e_all_ref) = refs
        z_ref = ob_ref = gain_ref = None
    else:
        (q_ref, f_ref, v_ref, llb_ref, l1lb_ref, pst_ref, msk_ref, z_ref, ob_ref, gain_ref, out_ref,
         st_ref, qs_ref, g_ref, kk_ref, o_ref, e_all_ref) = refs
    nc = SUB_BLOCK // CHUNK
    n_lvl = len(HGRN_LEVELS)
    last_row = 0 if rev else CHUNK - 1
    heads = [slice(h * HEAD_DIM, (h + 1) * HEAD_DIM) for h in range(N_HEADS)]
    _reset_state(lay, rev, st_ref)

    def piece(base):
        blk = _at(base, 0, SUB_BLOCK)
        qs_ref[...] = (_silu(q_ref[blk, :].astype(F32)) * QK_SCALE).astype(BF16)
        a = llb_ref[...]
        b = l1lb_ref[...] + _log_sigmoid(f_ref[blk, :].astype(F32))
        g2 = (jnp.maximum(a, b) + jnp.log(1.0 + jnp.exp(-jnp.abs(a - b)))) * LOG2_E
        g_ref[...] = g2
        kk_ref[...] = 1.0 - jnp.exp2(g2)

        pst = pst_ref[...]
        for ci in range(nc):
            c = (nc - 1 - ci) if rev else ci
            g_parts = jnp.concatenate(_split2(g_ref[c * CHUNK:(c + 1) * CHUNK, :]), axis=0)
            e_all_ref[ci] = jnp.exp2(jnp.dot(pst, g_parts, preferred_element_type=F32))

        states = [st_ref[h] for h in range(N_HEADS)]
        for ci in range(nc):
            c = (nc - 1 - ci) if rev else ci
            rows = slice(c * CHUNK, (c + 1) * CHUNK)
            e_ref = e_all_ref.at[ci]
            qcs = [qs_ref[rows, sl] for sl in heads]
            kcs = [kk_ref[rows, sl] for sl in heads]
            vcs = [v_ref[_at(base, c * CHUNK, CHUNK), sl].astype(BF16) for sl in heads]
            e_cums = [e_ref[0:CHUNK, sl] for sl in heads]
            scores = [[lax.dot_general(q, k.astype(BF16), (((1,), (1,)), ((), ())), preferred_element_type=F32)
                       for q, k in zip(qcs, kcs)]]
            for lvl in range(n_lvl):
                e_ls = [e_ref[(2 + lvl) * CHUNK:(3 + lvl) * CHUNK, sl] for sl in heads]
                scores.append([_dot_nt(q * e, k * e) for q, k, e in zip(qcs, kcs, e_ls)])
            inters = [_dot_nt(q * e, s) for q, e, s in zip(qcs, e_cums, states)]
            e_rests = [e_ref[CHUNK:2 * CHUNK, sl] for sl in heads]
            upds = [_dot_tn(v, k * e) for v, k, e in zip(vcs, kcs, e_rests)]
            attns = []
            for h in range(N_HEADS):
                attn = scores[0][h] * msk_ref[0]
                for lvl in range(n_lvl):
                    attn = attn + scores[1 + lvl][h] * msk_ref[1 + lvl]
                attns.append(attn.astype(BF16))
            intras = [jnp.dot(a, v, preferred_element_type=F32) for a, v in zip(attns, vcs)]
            for h, sl in enumerate(heads):
                o_ref[rows, sl] = inters[h] + intras[h]
                states[h] = states[h] * e_cums[h][last_row:last_row + 1, :] + upds[h]
        for h in range(N_HEADS):
            st_ref[h] = states[h]
        _finish(rev, base, o_ref, ob_ref, z_ref, gain_ref, out_ref)

    _sub_blocks(lay, rev, piece)


def _mixer_c(proj, log_lb, log1m_lb, o_bwd, gain, lay, rev):
    tb, nb = lay.tb, lay.nb
    bi = _blk_index(lay, rev)

    def cur(cb):
        return pl.BlockSpec((tb, W_GROUP), lambda j: (bi(j), cb))

    vec = pl.BlockSpec((1, W_GROUP), lambda j: (0, 0))
    d = 1 if rev else 0
    pst, msk = _hgrn_consts(rev)
    n_stack = pst.shape[0]
    in_specs = [cur(CB_C_Q), cur(CB_C_FB if rev else CB_C_FF), cur(CB_C_I), vec, vec,
                pl.BlockSpec((n_stack, 2 * CHUNK), lambda j: (0, 0)),
                pl.BlockSpec((msk.shape[0], CHUNK, CHUNK), lambda j: (0, 0, 0))]
    args = [proj, proj, proj, log_lb[d].reshape(1, W_GROUP), log1m_lb[d].reshape(1, W_GROUP), pst, msk]
    if not rev:
        in_specs += [cur(CB_C_Z), pl.BlockSpec((tb, W_GROUP), lambda j: (bi(j), 0)), vec]
        args += [proj, o_bwd, gain.reshape(1, W_GROUP)]
    big = pltpu.VMEM((SUB_BLOCK, W_GROUP), F32)
    return pl.pallas_call(
        functools.partial(_hgrn_kernel, lay, rev),
        grid=(nb,),
        in_specs=in_specs,
        out_specs=pl.BlockSpec((tb, W_GROUP), lambda j: (bi(j), 0)),
        out_shape=jax.ShapeDtypeStruct((lay.n, W_GROUP), F32 if rev else BF16),
        scratch_shapes=[pltpu.VMEM((N_HEADS, HEAD_DIM, HEAD_DIM), F32),
                        pltpu.VMEM((SUB_BLOCK, W_GROUP), BF16),
                        big, big, big,
                        pltpu.VMEM((SUB_BLOCK // CHUNK, n_stack, W_GROUP), F32)],
        compiler_params=_cparams(("arbitrary",)),
        name="mixer_c_hgrn2_bwd" if rev else "mixer_c_hgrn2_fwd",
    )(*args)


def _unit_tri_inverses(a_mats, row, col):
    c = a_mats[0].shape[0]
    eye = jnp.where(row == col, 1.0, 0.0)

    def same_block(size):
        return (row // size) == (col // size)

    base = 8
    m_base = same_block(base)
    ns = [jnp.where(m_base, -a, 0.0).astype(BF16) for a in a_mats]
    n2s = [jnp.dot(n, n, preferred_element_type=F32) for n in ns]
    xs = [eye + n.astype(F32) for n in ns]
    xs = [x + _dot(x, n2) for x, n2 in zip(xs, n2s)]
    n4s = [_dot(n2, n2) for n2 in n2s]
    xs = [x + _dot(x, n4) for x, n4 in zip(xs, n4s)]
    size = base
    while size < c:
        m_off = same_block(2 * size) & jnp.logical_not(same_block(size))
        xbs = [x.astype(BF16) for x in xs]
        ts = [jnp.dot(xb, jnp.where(m_off, a, 0.0).astype(BF16), preferred_element_type=F32)
              for xb, a in zip(xbs, a_mats)]
        xs = [x - jnp.dot(t.astype(BF16), xb, preferred_element_type=F32) for x, t, xb in zip(xs, ts, xbs)]
        size *= 2
    return xs


def _delta_pre_kernel(lay, q_ref, k_ref, v_ref, qp_ref, kp_ref, vp_ref, qn_ref, kn_ref, vn_ref, sm_ref,
                      cw_ref, alog_ref, dtb_ref, qo_ref, ko_ref, vo_ref, g_ref, beta_ref, x_ref):
    tb = lay.tb
    blk = pl.program_id(0)
    keep_prev = jnp.where(lay.seq_start(blk), 0.0, 1.0)
    keep_next = jnp.where(lay.seq_end(blk), 0.0, 1.0)
    half = DN_CONV_WIDTH // 2
    for part, (c_ref, p_ref, n_ref, dst_ref) in enumerate(
            ((q_ref, qp_ref, qn_ref, qo_ref), (k_ref, kp_ref, kn_ref, ko_ref), (v_ref, vp_ref, vn_ref, vo_ref))):
        x_ref[0:HALO_D, :] = p_ref[...].astype(F32) * keep_prev
        x_ref[HALO_D:HALO_D + tb, :] = c_ref[...].astype(F32)
        x_ref[HALO_D + tb:2 * HALO_D + tb, :] = n_ref[...].astype(F32) * keep_next
        wsl = slice(part * W_GROUP, (part + 1) * W_GROUP)
        acc = None
        for k in range(DN_CONV_WIDTH):
            off = HALO_D - half + k
            t = cw_ref[k:k + 1, wsl] * x_ref[off:off + tb, :]
            acc = t if acc is None else acc + t
        y = _silu(acc)
        if part < 2:
            for h in range(N_HEADS):
                sl = slice(h * HEAD_DIM, (h + 1) * HEAD_DIM)
                yh = y[:, sl]
                yn = yh * lax.rsqrt(jnp.sum(yh * yh, axis=-1, keepdims=True) + NORM_EPS)
                dst_ref[:, sl] = (yn * QK_SCALE if part == 0 else yn).astype(BF16)
        else:
            dst_ref[...] = y.astype(BF16)

    sm = sm_ref[...]
    g_ref[...] = -jnp.exp(alog_ref[...]) * _softplus(sm + dtb_ref[...])
    beta_ref[...] = _sigmoid(sm)


def _delta_kernel(lay, rev, *refs):
    if rev:
        (qc_ref, kc_ref, vc_ref, g_ref, beta_ref, out_ref,
         s_ref, o_ref, qd_ref, kd_ref, qk_ref, rhs_ref, last_ref, gq_ref, bm_ref) = refs
        z_ref = ob_ref = gain_ref = None
    else:
        (qc_ref, kc_ref, vc_ref, g_ref, beta_ref, z_ref, ob_ref, gain_ref, out_ref,
         s_ref, o_ref, qd_ref, kd_ref, qk_ref, rhs_ref, last_ref, gq_ref, bm_ref) = refs
    _reset_state(lay, rev, s_ref)
    named = (qc_ref, kc_ref, vc_ref, g_ref, beta_ref, z_ref, ob_ref, gain_ref, out_ref,
             s_ref, o_ref, qd_ref, kd_ref, qk_ref, rhs_ref, last_ref, gq_ref, bm_ref)
    _sub_blocks(lay, rev, functools.partial(_delta_piece, rev, named))


def _delta_piece(rev, named, base):
    (qc_ref, kc_ref, vc_ref, g_ref, beta_ref, z_ref, ob_ref, gain_ref, out_ref,
     s_ref, o_ref, qd_ref, kd_ref, qk_ref, rhs_ref, last_ref, gq_ref, bm_ref) = named
    nc = SUB_BLOCK // CHUNK
    row, col, incl, strict = _causal_masks(rev, CHUNK)
    cum_mat = jnp.where(incl, 1.0, 0.0)
    eye_rows = jnp.where(lax.broadcasted_iota(jnp.int32, (8, SMALL_W), 0)
                         == lax.broadcasted_iota(jnp.int32, (8, SMALL_W), 1), 1.0, 0.0).astype(BF16)
    lane0 = N_HEADS if rev else 0

    cums = [_dot_exact_lhs(cum_mat, g_ref[_at(base, c * CHUNK, CHUNK), :]) for c in range(nc)]
    cum_ts = []
    for cum in cums:
        sel = pltpu.roll(cum, SMALL_W - lane0, 1) if lane0 else cum
        cum_t = None
        for part in _split3(sel):
            t = lax.dot_general(eye_rows, part, (((1,), (1,)), ((), ())), preferred_element_type=F32)
            cum_t = t if cum_t is None else cum_t + t
        cum_ts.append(cum_t)
    raws = []
    for c in range(nc):
        rows = slice(c * CHUNK, (c + 1) * CHUNK)
        src = _at(base, c * CHUNK, CHUNK)
        beta_all = beta_ref[src, :]
        for h in range(N_HEADS):
            sl = slice(h * HEAD_DIM, (h + 1) * HEAD_DIM)
            qc = qc_ref[src, sl]
            kc = kc_ref[src, sl]
            vc = vc_ref[src, sl]
            ci_b = jnp.broadcast_to(cums[c][:, lane0 + h:lane0 + h + 1], (CHUNK, HEAD_DIM))
            b_lane = 2 * N_HEADS + lane0 + h
            beta_b = jnp.broadcast_to(beta_all[:, b_lane:b_lane + 1], (CHUNK, HEAD_DIM))
            k_beta = kc * beta_b
            raws.append(_dot_nt(jnp.concatenate([k_beta.astype(BF16), qc], axis=0), kc))
            e_ci = jnp.exp(ci_b)
            rhs_ref[c * N_HEADS + h] = jnp.concatenate([vc * beta_b, k_beta * e_ci], axis=1).astype(BF16)
            qd_ref[rows, sl] = qc * e_ci
            c_last = ci_b[0:1, :] if rev else ci_b[CHUNK - 1:CHUNK, :]
            kd_ref[rows, sl] = (kc * jnp.exp(c_last - ci_b)).astype(BF16)
            last_ref[c * N_HEADS + h] = jnp.broadcast_to(jnp.exp(c_last), (8, HEAD_DIM))
    a_mats = []
    for idx, raw in enumerate(raws):
        c, h = divmod(idx, N_HEADS)
        ci_col = cums[c][:, lane0 + h:lane0 + h + 1]
        cj_row = cum_ts[c][h:h + 1, :]
        decay = jnp.exp(jnp.where(incl, ci_col - cj_row, NEG_BIG))
        a_mats.append(jnp.where(strict, raw[0:CHUNK] * decay, 0.0))
        qk_ref[idx] = (raw[CHUNK:2 * CHUNK] * decay).astype(BF16)

    t_invs = _unit_tri_inverses(a_mats, row, col)
    where = [(slice((idx // N_HEADS) * CHUNK, (idx // N_HEADS + 1) * CHUNK),
              slice((idx % N_HEADS) * HEAD_DIM, (idx % N_HEADS + 1) * HEAD_DIM)) for idx in range(len(t_invs))]
    uws = [jnp.dot(t_inv.astype(BF16), rhs_ref[idx], preferred_element_type=F32).astype(BF16)
           for idx, t_inv in enumerate(t_invs)]
    kuws = [lax.dot_general(kd_ref[rows, sl], uw, (((0,), (0,)), ((), ())), preferred_element_type=F32)
            for (rows, sl), uw in zip(where, uws)]
    quws = [jnp.dot(qk_ref[idx], uw, preferred_element_type=F32) for idx, uw in enumerate(uws)]
    for idx, ((rows, sl), kuw, quw) in enumerate(zip(where, kuws, quws)):
        bm_ref[idx] = kuw[:, 0:HEAD_DIM]
        gq_ref[idx, 0:HEAD_DIM, :] = kuw[:, HEAD_DIM:2 * HEAD_DIM].astype(BF16)
        gq_ref[idx, HEAD_DIM:HEAD_DIM + CHUNK, :] = (qd_ref[rows, sl] - quw[:, HEAD_DIM:2 * HEAD_DIM]).astype(BF16)
        o_ref[rows, sl] = quw[:, 0:HEAD_DIM]

    states = [s_ref[h] for h in range(N_HEADS)]
    for ci in range(nc):
        c = (nc - 1 - ci) if rev else ci
        rows = slice(c * CHUNK, (c + 1) * CHUNK)
        for h in range(N_HEADS):
            idx = c * N_HEADS + h
            sl = slice(h * HEAD_DIM, (h + 1) * HEAD_DIM)
            gs = jnp.dot(gq_ref[idx], states[h].astype(BF16), preferred_element_type=F32)
            o_ref[rows, sl] += gs[HEAD_DIM:HEAD_DIM + CHUNK]
            states[h] = states[h] * last_ref[idx][0:1, :] - gs[0:HEAD_DIM] + bm_ref[idx]
    for h in range(N_HEADS):
        s_ref[h] = states[h]
    _finish(rev, base, o_ref, ob_ref, z_ref, gain_ref, out_ref)


def _mixer_d_pre(proj, proj_small, conv_w, a_log, dt_bias, lay):
    tb, nb = lay.tb, lay.nb
    hb = tb // HALO_D
    n_halo = lay.n // HALO_D

    def cur(cb):
        return pl.BlockSpec((tb, W_GROUP), lambda i: (i, cb))

    def prev(cb):
        return pl.BlockSpec((HALO_D, W_GROUP), lambda i: (jnp.maximum(i * hb - 1, 0), cb))

    def nxt(cb):
        return pl.BlockSpec((HALO_D, W_GROUP), lambda i: (jnp.minimum((i + 1) * hb, n_halo - 1), cb))

    small = pl.BlockSpec((1, SMALL_W), lambda i: (0, 0))
    pad_lanes = SMALL_W - 2 * N_HEADS
    alog_row = jnp.pad(a_log.reshape(1, 2 * N_HEADS), ((0, 0), (0, pad_lanes)))
    dtb_row = jnp.pad(dt_bias.reshape(1, 2 * N_HEADS), ((0, 0), (0, pad_lanes)))
    wide = pl.BlockSpec((tb, W_GROUP), lambda i: (i, 0))
    narrow = pl.BlockSpec((tb, SMALL_W), lambda i: (i, 0))
    return pl.pallas_call(
        functools.partial(_delta_pre_kernel, lay),
        grid=(nb,),
        in_specs=[cur(CB_D_Q), cur(CB_D_K), cur(CB_D_V), prev(CB_D_Q), prev(CB_D_K), prev(CB_D_V),
                  nxt(CB_D_Q), nxt(CB_D_K), nxt(CB_D_V),
                  pl.BlockSpec((tb, SMALL_W), lambda i: (i, 0)),
                  pl.BlockSpec((DN_CONV_WIDTH, 3 * W_GROUP), lambda i: (0, 0)), small, small],
        out_specs=[wide, wide, wide, narrow, narrow],
        out_shape=[jax.ShapeDtypeStruct((lay.n, W_GROUP), BF16)] * 3
        + [jax.ShapeDtypeStruct((lay.n, SMALL_W), F32)] * 2,
        scratch_shapes=[pltpu.VMEM((tb + 2 * HALO_D, W_GROUP), F32)],
        compiler_params=_cparams(("arbitrary",)),
        name="mixer_d_deltanet_pre",
    )(*([proj] * 9), proj_small, conv_w, alog_row, dtb_row)


def _mixer_d(pre, proj, o_bwd, gain, lay, rev):
    tb, nb = lay.tb, lay.nb
    bi = _blk_index(lay, rev)
    wide = pl.BlockSpec((tb, W_GROUP), lambda j: (bi(j), 0))
    narrow = pl.BlockSpec((tb, SMALL_W), lambda j: (bi(j), 0))
    in_specs = [wide, wide, wide, narrow, narrow]
    args = list(pre)
    if not rev:
        in_specs += [pl.BlockSpec((tb, W_GROUP), lambda j: (bi(j), CB_D_Z)), wide,
                     pl.BlockSpec((1, W_GROUP), lambda j: (0, 0))]
        args += [proj, o_bwd, gain.reshape(1, W_GROUP)]
    big = pltpu.VMEM((SUB_BLOCK, W_GROUP), F32)
    nc = SUB_BLOCK // CHUNK
    return pl.pallas_call(
        functools.partial(_delta_kernel, lay, rev),
        grid=(nb,),
        in_specs=in_specs,
        out_specs=wide,
        out_shape=jax.ShapeDtypeStruct((lay.n, W_GROUP), F32 if rev else BF16),
        scratch_shapes=[pltpu.VMEM((N_HEADS, HEAD_DIM, HEAD_DIM), F32),
                        big,
                        big,
                        pltpu.VMEM((SUB_BLOCK, W_GROUP), BF16),
                        pltpu.VMEM((nc * N_HEADS, CHUNK, CHUNK), BF16),
                        pltpu.VMEM((nc * N_HEADS, CHUNK, 2 * HEAD_DIM), BF16),
                        pltpu.VMEM((nc * N_HEADS, 8, HEAD_DIM), F32),
                        pltpu.VMEM((nc * N_HEADS, HEAD_DIM + CHUNK, HEAD_DIM), BF16),
                        pltpu.VMEM((nc * N_HEADS, HEAD_DIM, HEAD_DIM), F32)],
        compiler_params=_cparams(("arbitrary",)),
        name="mixer_d_deltanet_bwd" if rev else "mixer_d_deltanet_fwd",
    )(*args)


def _regroup_w_in(w):
    d_qkv_end = 15 * W_GROUP
    small = w[:, d_qkv_end:d_qkv_end + 4 * N_HEADS]
    d_z = w[:, d_qkv_end + 4 * N_HEADS:]
    pad = jnp.zeros((w.shape[0], SMALL_W - 4 * N_HEADS), w.dtype)
    main = jnp.concatenate([w[:, :d_qkv_end], d_z], axis=1).astype(BF16)
    return main, jnp.concatenate([small, pad], axis=1).astype(BF16)


def _rotary_tables(t_max):
    half = HEAD_DIM // 2
    inv = 1.0 / (ROPE_BASE ** (jnp.arange(half, dtype=F32) / half))
    ang = jnp.arange(t_max, dtype=F32)[:, None] * inv[None, :]
    cos, sin = jnp.cos(ang), jnp.sin(ang)
    return jnp.concatenate([cos, cos], axis=-1), jnp.concatenate([-sin, sin], axis=-1)


def _pick_tile(t_prompt, t_sample, want):
    tile = want
    while t_prompt % tile or t_sample % tile:
        tile //= 2
    return tile


def kernel(x_prompt, x_sample, c_prompt, c_sample, ada_w, ada_b, norm_g, w_in, conv_a_w, conv_a_b, ln_a_g, ln_a_b,
           ret_norm_g, hgrn_lb_logits, hgrn_norm_g, dn_conv_w, dn_a_log, dn_dt_bias, dn_norm_g, w_out, final_g):
    depth = w_in.shape[0]
    n_prompt, t_prompt, _ = x_prompt.shape
    n_sample, t_sample, _ = x_sample.shape
    tm_in = _pick_tile(t_prompt, t_sample, 1024)
    tm_out = _pick_tile(t_prompt, t_sample, 512)
    lay = _Layout(t_prompt, n_prompt, t_sample, n_sample, _pick_tile(t_prompt, t_sample, 1024))
    lay_a = _Layout(t_prompt, n_prompt, t_sample, n_sample, _pick_tile(t_prompt, t_sample, 512))
    lay_dp = _Layout(t_prompt, n_prompt, t_sample, n_sample, _pick_tile(t_prompt, t_sample, SUB_BLOCK))

    xs = (x_prompt.reshape(-1, D_MODEL), x_sample.reshape(-1, D_MODEL))
    c_all = jnp.concatenate([c_prompt, c_sample], axis=0)
    n_c = c_all.shape[0]
    c_all = jnp.pad(c_all, ((0, (-n_c) % 8), (0, 0)))
    mod = _modulation(c_all, ada_w, ada_b)

    lb = jnp.cumsum(jax.nn.softmax(hgrn_lb_logits.astype(F32), axis=0), axis=0)
    lb = lb - lb[:1]
    log_lb = jnp.log(lb)
    log1m_lb = jnp.log1p(-lb)
    cos2, sin2 = _rotary_tables(max(t_prompt, t_sample))

    for l in range(depth):
        mod_l = mod[l].reshape(mod.shape[1], 1, 3 * D_MODEL)
        proj, proj_small = _in_proj(xs, mod_l, norm_g[l], *_regroup_w_in(w_in[l]), lay, tm_in)
        m_a = _mixer_a(proj, conv_a_w[l], conv_a_b[l], ln_a_g[l], ln_a_b[l], lay_a)
        ob = _mixer_b(proj, cos2, sin2, None, None, lay, True)
        m_b = _mixer_b(proj, cos2, sin2, ob, ret_norm_g[l], lay, False)
        oc = _mixer_c(proj, log_lb[l], log1m_lb[l], None, None, lay, True)
        m_c = _mixer_c(proj, log_lb[l], log1m_lb[l], oc, hgrn_norm_g[l], lay, False)
        d_pre = _mixer_d_pre(proj, proj_small, dn_conv_w[l], dn_a_log[l], dn_dt_bias[l], lay_dp)
        od = _mixer_d(d_pre, proj, None, None, lay, True)
        m_d = _mixer_d(d_pre, proj, od, dn_norm_g[l], lay, False)
        xs = _out_proj(xs, (m_a, m_b, m_c, m_d), mod_l, w_out[l].astype(BF16), final_g, lay, tm_out,
                       final=(l == depth - 1))

    return (xs[0].reshape(n_prompt, t_prompt, D_MODEL), xs[1].reshape(n_sample, t_sample, D_MODEL))
```

```python
import functools
import math

import numpy as np
import jax
import jax.numpy as jnp
from jax import lax
from jax.experimental import pallas as pl
from jax.experimental.pallas import tpu as pltpu

F32 = jnp.float32
BF16 = jnp.bfloat16

D_MODEL = 1024
W_GROUP = 512
HEAD_DIM = 128
SUBLANES = 8
N_HEADS = 4
CONV_A_WIDTH = 31
DN_CONV_WIDTH = 5
CHUNK = 64
ROPE_BASE = 10000.0
RET_DECAY_OFFSET = 5.0
NORM_EPS = 1e-6
QK_SCALE = HEAD_DIM ** -0.5
NEG_BIG = -1e30

N_COL_BLOCKS = 16
SMALL_W = 128
SUB_BLOCK = 256
D_PROJ = N_COL_BLOCKS * W_GROUP
PROJ_TN = 2048
HALO_A = 16
HALO_D = 16
VMEM_LIMIT = 56 * 1024 * 1024

CB_A_VAL, CB_A_GLU, CB_A_Z = 0, 1, 2
CB_B_Q, CB_B_K, CB_B_V, CB_B_Z = 3, 4, 5, 6
CB_C_Q, CB_C_FF, CB_C_FB, CB_C_I, CB_C_Z = 7, 8, 9, 10, 11
CB_D_Q, CB_D_K, CB_D_V, CB_D_Z = 12, 13, 14, 15


def _dot(a, b):
    return jnp.dot(a.astype(BF16), b.astype(BF16), preferred_element_type=F32)


def _dot_nt(a, b):
    return lax.dot_general(a.astype(BF16), b.astype(BF16), (((1,), (1,)), ((), ())), preferred_element_type=F32)


def _dot_tn(a, b):
    return lax.dot_general(a.astype(BF16), b.astype(BF16), (((0,), (0,)), ((), ())), preferred_element_type=F32)


def _split2(x):
    hi = x.astype(BF16)
    lo = (x - hi.astype(F32)).astype(BF16)
    return hi, lo


def _split3(x):
    hi = x.astype(BF16)
    r = x - hi.astype(F32)
    mid = r.astype(BF16)
    lo = (r - mid.astype(F32)).astype(BF16)
    return hi, mid, lo


def _dot_exact_lhs(m, x):
    mb = m.astype(BF16)
    acc = None
    for part in _split3(x):
        t = jnp.dot(mb, part, preferred_element_type=F32)
        acc = t if acc is None else acc + t
    return acc


def _sigmoid(x):
    return 1.0 / (1.0 + jnp.exp(-x))


def _silu(x):
    return x * _sigmoid(x)


def _softplus(x):
    return jnp.maximum(x, 0.0) + jnp.log(1.0 + jnp.exp(-jnp.abs(x)))


def _log_sigmoid(x):
    return jnp.minimum(x, 0.0) - jnp.log(1.0 + jnp.exp(-jnp.abs(x)))


class _Layout:
    def __init__(self, t_prompt, n_prompt, t_sample, n_sample, tb):
        self.tp = t_prompt * n_prompt
        self.t_prompt = t_prompt
        self.ts = t_sample
        self.n = self.tp + t_sample * n_sample
        self.tb = tb
        assert t_prompt % tb == 0 and t_sample % tb == 0
        self.nb = self.n // tb

    def seq_start(self, blk):
        s = blk * self.tb
        return jnp.where(s < self.tp, s % self.t_prompt == 0, (s - self.tp) % self.ts == 0)

    def seq_end(self, blk):
        e = (blk + 1) * self.tb
        return jnp.where(e <= self.tp, e % self.t_prompt == 0, (e - self.tp) % self.ts == 0)

    def pos_block(self, blk):
        s = blk * self.tb
        return jnp.where(s < self.tp, (s % self.t_prompt) // self.tb, ((s - self.tp) % self.ts) // self.tb)

    def batch_index(self, blk, rows):
        s = blk * rows
        return jnp.where(s < self.tp, s // self.t_prompt, self.tp // self.t_prompt + (s - self.tp) // self.ts)


def _cparams(sem):
    return pltpu.CompilerParams(dimension_semantics=sem, vmem_limit_bytes=VMEM_LIMIT)


def _mod_kernel(c_ref, w_ref, b_ref, o_ref):
    c = c_ref[...]
    o_ref[0] = jnp.dot(_silu(c), w_ref[0], preferred_element_type=F32,
                       precision=lax.Precision.HIGHEST) + b_ref[0]


def _modulation(c_all, ada_w, ada_b):
    depth = ada_w.shape[0]
    nb = c_all.shape[0]
    return pl.pallas_call(
        _mod_kernel,
        grid=(depth,),
        in_specs=[pl.BlockSpec((nb, D_MODEL), lambda l: (0, 0)),
                  pl.BlockSpec((1, D_MODEL, 3 * D_MODEL), lambda l: (l, 0, 0)),
                  pl.BlockSpec((1, 1, 3 * D_MODEL), lambda l: (l, 0, 0))],
        out_specs=pl.BlockSpec((1, nb, 3 * D_MODEL), lambda l: (l, 0, 0)),
        out_shape=jax.ShapeDtypeStruct((depth, nb, 3 * D_MODEL), F32),
        compiler_params=_cparams(("arbitrary",)),
        name="adaln_modulation",
    )(c_all, ada_w, ada_b.reshape(depth, 1, 3 * D_MODEL))


def _load_x(x_refs, i, npb):
    if len(x_refs) == 1:
        return x_refs[0][...]
    return jnp.where(i < npb, x_refs[0][...], x_refs[1][...])


def _x_specs(n_x, tm, npb, row_block):
    def spec(fn):
        return pl.BlockSpec((tm, D_MODEL), lambda *ids: (fn(row_block(*ids)), 0))

    if n_x == 1:
        return [spec(lambda i: i)]
    return [spec(lambda i: jnp.minimum(i, npb - 1)), spec(lambda i: jnp.maximum(i - npb, 0))]


def _in_proj_kernel(n_x, npb, *refs):
    x_refs = refs[:n_x]
    mod_ref, g_ref, w_ref, ws_ref, o_ref, os_ref, h_ref = refs[n_x:]

    @pl.when(pl.program_id(1) == 0)
    def _():
        x = _load_x(x_refs, pl.program_id(0), npb)
        y = x * lax.rsqrt(jnp.mean(x * x, axis=-1, keepdims=True) + NORM_EPS) * g_ref[...]
        shift = mod_ref[0, :, 0:D_MODEL]
        scale = mod_ref[0, :, D_MODEL:2 * D_MODEL]
        h = (y * (1.0 + scale) + shift).astype(BF16)
        h_ref[...] = h
        os_ref[...] = jnp.dot(h, ws_ref[...], preferred_element_type=F32)

    o_ref[...] = jnp.dot(h_ref[...], w_ref[...], preferred_element_type=F32).astype(BF16)


def _in_proj(xs, mod, norm_g, w_main, w_small, lay, tm):
    n = lay.n
    npb = lay.tp // tm
    lay_m = _Layout(lay.t_prompt, lay.tp // lay.t_prompt, lay.ts, (lay.n - lay.tp) // lay.ts, tm)
    return pl.pallas_call(
        functools.partial(_in_proj_kernel, len(xs), npb),
        grid=(n // tm, D_PROJ // PROJ_TN),
        in_specs=_x_specs(len(xs), tm, npb, lambda i, j: i) + [
                  pl.BlockSpec((1, 1, 3 * D_MODEL), lambda i, j: (lay_m.batch_index(i, tm), 0, 0)),
                  pl.BlockSpec((1, D_MODEL), lambda i, j: (0, 0)),
                  pl.BlockSpec((D_MODEL, PROJ_TN), lambda i, j: (0, j)),
                  pl.BlockSpec((D_MODEL, SMALL_W), lambda i, j: (0, 0))],
        out_specs=[pl.BlockSpec((tm, PROJ_TN), lambda i, j: (i, j)),
                   pl.BlockSpec((tm, SMALL_W), lambda i, j: (i, 0))],
        out_shape=[jax.ShapeDtypeStruct((n, D_PROJ), BF16), jax.ShapeDtypeStruct((n, SMALL_W), F32)],
        scratch_shapes=[pltpu.VMEM((tm, D_MODEL), BF16)],
        compiler_params=_cparams(("arbitrary", "arbitrary")),
        name="in_proj",
    )(*xs, mod, norm_g.reshape(1, D_MODEL), w_main, w_small)


def _out_proj_kernel(final, n_x, npb, *refs):
    x_refs = refs[:n_x]
    ma_ref, mb_ref, mc_ref, md_ref, mod_ref, w_ref, fg_ref = refs[n_x:n_x + 7]
    o_refs = refs[n_x + 7:]
    i = pl.program_id(0)
    acc = jnp.dot(ma_ref[...], w_ref[0:W_GROUP, :], preferred_element_type=F32)
    acc += jnp.dot(mb_ref[...], w_ref[W_GROUP:2 * W_GROUP, :], preferred_element_type=F32)
    acc += jnp.dot(mc_ref[...], w_ref[2 * W_GROUP:3 * W_GROUP, :], preferred_element_type=F32)
    acc += jnp.dot(md_ref[...], w_ref[3 * W_GROUP:4 * W_GROUP, :], preferred_element_type=F32)
    gate = mod_ref[0, :, 2 * D_MODEL:3 * D_MODEL]
    y = _load_x(x_refs, i, npb) + gate * acc
    if not final:
        o_refs[0][...] = y
        return
    y = y * lax.rsqrt(jnp.mean(y * y, axis=-1, keepdims=True) + NORM_EPS) * fg_ref[...]

    @pl.when(i < npb)
    def _():
        o_refs[0][...] = y

    @pl.when(i >= npb)
    def _():
        o_refs[1][...] = y


def _out_proj(xs, mixed, mod, w_out_bf, final_g, lay, tm, final):
    n = lay.n
    npb = lay.tp // tm
    lay_m = _Layout(lay.t_prompt, lay.tp // lay.t_prompt, lay.ts, (lay.n - lay.tp) // lay.ts, tm)
    mspec = pl.BlockSpec((tm, W_GROUP), lambda i: (i, 0))
    if final:
        out_specs = [pl.BlockSpec((tm, D_MODEL), lambda i: (jnp.minimum(i, npb - 1), 0)),
                     pl.BlockSpec((tm, D_MODEL), lambda i: (jnp.maximum(i - npb, 0), 0))]
        out_shape = [jax.ShapeDtypeStruct((lay.tp, D_MODEL), F32), jax.ShapeDtypeStruct((n - lay.tp, D_MODEL), F32)]
    else:
        out_specs = [pl.BlockSpec((tm, D_MODEL), lambda i: (i, 0))]
        out_shape = [jax.ShapeDtypeStruct((n, D_MODEL), F32)]
    out = pl.pallas_call(
        functools.partial(_out_proj_kernel, final, len(xs), npb),
        grid=(n // tm,),
        in_specs=_x_specs(len(xs), tm, npb, lambda i: i) + [mspec, mspec, mspec, mspec,
                  pl.BlockSpec((1, 1, 3 * D_MODEL), lambda i: (lay_m.batch_index(i, tm), 0, 0)),
                  pl.BlockSpec((4 * W_GROUP, D_MODEL), lambda i: (0, 0)),
                  pl.BlockSpec((1, D_MODEL), lambda i: (0, 0))],
        out_specs=out_specs,
        out_shape=out_shape,
        compiler_params=_cparams(("arbitrary",)),
        name="out_proj",
    )(*xs, *mixed, mod, w_out_bf, final_g.reshape(1, D_MODEL))
    return tuple(out)


A_ROWS = 32


def _mixer_a_kernel(lay, val_ref, glu_ref, z_ref, vp_ref, gp_ref, vn_ref, gn_ref,
                    cw_ref, cb_ref, lg_ref, lb_ref, o_ref, u_ref, ush_ref, acc_ref):
    tb = lay.tb
    blk = pl.program_id(0)
    keep_prev = jnp.where(lay.seq_start(blk), 0.0, 1.0)
    keep_next = jnp.where(lay.seq_end(blk), 0.0, 1.0)
    u_ref[0:HALO_A, :] = vp_ref[...].astype(F32) * _sigmoid(gp_ref[...].astype(F32)) * keep_prev
    u_ref[HALO_A:HALO_A + tb, :] = val_ref[...].astype(F32) * _sigmoid(glu_ref[...].astype(F32))
    u_ref[HALO_A + tb:2 * HALO_A + tb, :] = vn_ref[...].astype(F32) * _sigmoid(gn_ref[...].astype(F32)) * keep_next
    half = CONV_A_WIDTH // 2
    n_sh = tb + 2 * HALO_A - SUBLANES
    for r in range(1, SUBLANES):
        ush_ref[r - 1] = u_ref[r:r + n_sh, :]

    def tile(t, carry):
        r0 = pl.multiple_of(t * A_ROWS, A_ROWS)
        acc = jnp.zeros((A_ROWS, W_GROUP), F32) + cb_ref[...]
        for k in range(CONV_A_WIDTH):
            a, r = divmod(HALO_A - half + k, SUBLANES)
            win = pl.ds(r0 + a * SUBLANES, A_ROWS)
            src = u_ref[win, :] if r == 0 else ush_ref[r - 1, win, :]
            acc = acc + cw_ref[k:k + 1, :] * src
        acc_ref[pl.ds(r0, A_ROWS), :] = acc
        return carry

    lax.fori_loop(0, tb // A_ROWS, tile, 0)
    acc = acc_ref[...]
    mu = jnp.mean(acc, axis=-1, keepdims=True)
    xc = acc - mu
    var = jnp.mean(xc * xc, axis=-1, keepdims=True)
    y = xc * lax.rsqrt(var + NORM_EPS) * lg_ref[...] + lb_ref[...]
    o_ref[...] = (_silu(y) * _silu(z_ref[...].astype(F32))).astype(BF16)


def _mixer_a(proj, conv_w, conv_b, ln_g, ln_b, lay):
    tb, nb = lay.tb, lay.nb
    hb = tb // HALO_A
    n_halo = lay.n // HALO_A

    def cur(cb):
        return pl.BlockSpec((tb, W_GROUP), lambda i: (i, cb))

    def prev(cb):
        return pl.BlockSpec((HALO_A, W_GROUP), lambda i: (jnp.maximum(i * hb - 1, 0), cb))

    def nxt(cb):
        return pl.BlockSpec((HALO_A, W_GROUP), lambda i: (jnp.minimum((i + 1) * hb, n_halo - 1), cb))

    vec = pl.BlockSpec((1, W_GROUP), lambda i: (0, 0))
    return pl.pallas_call(
        functools.partial(_mixer_a_kernel, lay),
        grid=(nb,),
        in_specs=[cur(CB_A_VAL), cur(CB_A_GLU), cur(CB_A_Z), prev(CB_A_VAL), prev(CB_A_GLU),
                  nxt(CB_A_VAL), nxt(CB_A_GLU),
                  pl.BlockSpec((CONV_A_WIDTH, W_GROUP), lambda i: (0, 0)), vec, vec, vec],
        out_specs=pl.BlockSpec((tb, W_GROUP), lambda i: (i, 0)),
        out_shape=jax.ShapeDtypeStruct((lay.n, W_GROUP), BF16),
        scratch_shapes=[pltpu.VMEM((tb + 2 * HALO_A, W_GROUP), F32),
                        pltpu.VMEM((SUBLANES - 1, tb + 2 * HALO_A - SUBLANES, W_GROUP), F32),
                        pltpu.VMEM((tb, W_GROUP), F32)],
        compiler_params=_cparams(("arbitrary",)),
        name="mixer_a_conv",
    )(proj, proj, proj, proj, proj, proj, proj, conv_w, conv_b.reshape(1, W_GROUP),
      ln_g.reshape(1, W_GROUP), ln_b.reshape(1, W_GROUP))


def _blk_index(lay, rev):
    nb = lay.nb
    return (lambda j: nb - 1 - j) if rev else (lambda j: j)


def _reset_state(lay, rev, s_ref):
    blk = _blk_index(lay, rev)(pl.program_id(0))
    boundary = lay.seq_end(blk) if rev else lay.seq_start(blk)

    @pl.when(boundary)
    def _():
        s_ref[...] = jnp.zeros(s_ref.shape, s_ref.dtype)


def _sub_blocks(lay, rev, body):
    nsb = lay.tb // SUB_BLOCK

    def trip(i, carry):
        piece = (nsb - 1 - i) if rev else i
        body(pl.multiple_of(piece * SUB_BLOCK, SUB_BLOCK))
        return carry

    lax.fori_loop(0, nsb, trip, 0)


def _at(base, off, n):
    return pl.ds(pl.multiple_of(base + off, CHUNK), n)


def _finish(rev, base, o_ref, ob_ref, z_ref, gain_ref, out_ref):
    rows = _at(base, 0, SUB_BLOCK)
    if rev:
        out_ref[rows, :] = o_ref[...]
        return
    for h in range(N_HEADS):
        sl = slice(h * HEAD_DIM, (h + 1) * HEAD_DIM)
        o = o_ref[:, sl] + ob_ref[rows, sl]
        y = o * lax.rsqrt(jnp.mean(o * o, axis=-1, keepdims=True) + NORM_EPS) * gain_ref[:, sl]
        out_ref[rows, sl] = (y * _silu(z_ref[rows, sl].astype(F32))).astype(BF16)


def _causal_masks(rev, c):
    row = lax.broadcasted_iota(jnp.int32, (c, c), 0)
    col = lax.broadcasted_iota(jnp.int32, (c, c), 1)
    incl = (row <= col) if rev else (row >= col)
    strict = (row < col) if rev else (row > col)
    return row, col, incl, strict


def _ret_log_gamma(rev):
    lg = np.log1p(-np.exp2(-RET_DECAY_OFFSET - np.arange(N_HEADS, dtype=np.float64)))
    return lg[::-1].copy() if rev else lg


def _ret_consts(rev):
    lg = _ret_log_gamma(rev).astype(np.float32)
    i = np.arange(CHUNK, dtype=np.float32)
    idx = (CHUNK - 1 - i) if rev else i
    rel = idx[:, None] - idx[None, :]
    dm = np.where(rel >= 0, np.exp(lg[:, None, None] * np.maximum(rel, 0.0)), 0.0).astype(np.float32)
    qd = np.exp(lg[:, None] * (idx + 1.0)).astype(np.float32)
    kd = np.exp(lg[:, None] * (CHUNK - 1.0 - idx)).astype(np.float32)
    qd = np.broadcast_to(qd[:, :, None], (N_HEADS, CHUNK, HEAD_DIM)).copy()
    kd = np.broadcast_to(kd[:, :, None], (N_HEADS, CHUNK, HEAD_DIM)).copy()
    return jnp.asarray(dm), jnp.asarray(qd), jnp.asarray(kd)


def _ret_kernel(lay, rev, *refs):
    if rev:
        q_ref, k_ref, v_ref, cos_ref, sin_ref, dm_ref, qd_ref, kd_ref, out_ref, s_ref, qt_ref, kt_ref, o_ref = refs
        z_ref = ob_ref = gain_ref = None
    else:
        (q_ref, k_ref, v_ref, cos_ref, sin_ref, dm_ref, qd_ref, kd_ref, z_ref, ob_ref, gain_ref,
         out_ref, s_ref, qt_ref, kt_ref, o_ref) = refs
    nc = SUB_BLOCK // CHUNK
    chunk_decay = [float(np.exp(np.float32(v) * CHUNK)) for v in _ret_log_gamma(rev)]
    heads = [slice(h * HEAD_DIM, (h + 1) * HEAD_DIM) for h in range(N_HEADS)]
    _reset_state(lay, rev, s_ref)

    def piece(base):
        blk = _at(base, 0, SUB_BLOCK)
        cos = cos_ref[blk, :]
        sin = sin_ref[blk, :]
        for sl in heads:
            qh = q_ref[blk, sl].astype(F32)
            kh = k_ref[blk, sl].astype(F32)
            qt_ref[:, sl] = (qh * cos + pltpu.roll(qh, HEAD_DIM // 2, 1) * sin) * QK_SCALE
            kt_ref[:, sl] = kh * cos + pltpu.roll(kh, HEAD_DIM // 2, 1) * sin

        states = [s_ref[h] for h in range(N_HEADS)]
        for ci in range(nc):
            c = (nc - 1 - ci) if rev else ci
            rows = slice(c * CHUNK, (c + 1) * CHUNK)
            qcs = [qt_ref[rows, sl] for sl in heads]
            kcs = [kt_ref[rows, sl] for sl in heads]
            vcs = [v_ref[_at(base, c * CHUNK, CHUNK), sl].astype(BF16) for sl in heads]
            scores = [_dot_nt(q, k) for q, k in zip(qcs, kcs)]
            inters = [_dot(q * qd_ref[h], states[h]) for h, q in enumerate(qcs)]
            upds = [_dot_tn(k * kd_ref[h], v) for h, (k, v) in enumerate(zip(kcs, vcs))]
            intras = [jnp.dot((s * dm_ref[h]).astype(BF16), vcs[h], preferred_element_type=F32)
                      for h, s in enumerate(scores)]
            for h, sl in enumerate(heads):
                o_ref[rows, sl] = intras[h] + inters[h]
                states[h] = chunk_decay[h] * states[h] + upds[h]
        for h in range(N_HEADS):
            s_ref[h] = states[h]
        _finish(rev, base, o_ref, ob_ref, z_ref, gain_ref, out_ref)

    _sub_blocks(lay, rev, piece)


def _mixer_b(proj, cos2, sin2, o_bwd, gain, lay, rev):
    tb, nb = lay.tb, lay.nb
    bi = _blk_index(lay, rev)

    def cur(cb):
        return pl.BlockSpec((tb, W_GROUP), lambda j: (bi(j), cb))

    rot = pl.BlockSpec((tb, HEAD_DIM), lambda j: (lay.pos_block(bi(j)), 0))
    dm, qd, kd = _ret_consts(rev)
    in_specs = [cur(CB_B_Q), cur(CB_B_K), cur(CB_B_V), rot, rot,
                pl.BlockSpec((N_HEADS, CHUNK, CHUNK), lambda j: (0, 0, 0)),
                pl.BlockSpec((N_HEADS, CHUNK, HEAD_DIM), lambda j: (0, 0, 0)),
                pl.BlockSpec((N_HEADS, CHUNK, HEAD_DIM), lambda j: (0, 0, 0))]
    args = [proj, proj, proj, cos2, sin2, dm, qd, kd]
    if not rev:
        in_specs += [cur(CB_B_Z), pl.BlockSpec((tb, W_GROUP), lambda j: (bi(j), 0)),
                     pl.BlockSpec((1, W_GROUP), lambda j: (0, 0))]
        args += [proj, o_bwd, gain.reshape(1, W_GROUP)]
    return pl.pallas_call(
        functools.partial(_ret_kernel, lay, rev),
        grid=(nb,),
        in_specs=in_specs,
        out_specs=pl.BlockSpec((tb, W_GROUP), lambda j: (bi(j), 0)),
        out_shape=jax.ShapeDtypeStruct((lay.n, W_GROUP), F32 if rev else BF16),
        scratch_shapes=[pltpu.VMEM((N_HEADS, HEAD_DIM, HEAD_DIM), F32),
                        pltpu.VMEM((SUB_BLOCK, W_GROUP), F32), pltpu.VMEM((SUB_BLOCK, W_GROUP), F32),
                        pltpu.VMEM((SUB_BLOCK, W_GROUP), F32)],
        compiler_params=_cparams(("arbitrary",)),
        name="mixer_b_retention_bwd" if rev else "mixer_b_retention_fwd",
    )(*args)


HGRN_LEVELS = (32, 16, 8, 4, 2, 1)
LOG2_E = math.log2(math.e)


def _hgrn_consts(rev):
    c = CHUNK
    i = np.arange(c)[:, None]
    t = np.arange(c)[None, :]
    if rev:
        cum = t >= i
        rest = t < i
    else:
        cum = t <= i
        rest = t > i
    blocks = [cum, rest]
    masks = [np.eye(c, dtype=bool)]
    col = np.arange(c)[None, :]
    for s in HGRN_LEVELS:
        start = (i // (2 * s)) * (2 * s)
        later = (i - start) >= s
        if rev:
            ref = start + s
            p = np.where(later, (t >= ref) & (t < i), (t >= i) & (t < ref))
            q_half, k_half = 0, 1
        else:
            ref = start + s - 1
            p = np.where(later, (t > ref) & (t <= i), (t > i) & (t <= ref))
            q_half, k_half = 1, 0
        blocks.append(p)
        same = (i // (2 * s)) == (col // (2 * s))
        masks.append(same & ((i // s) % 2 == q_half) & ((col // s) % 2 == k_half))
    pst = np.concatenate(blocks, axis=0).astype(np.float32)
    pst = np.concatenate([pst, pst], axis=1)
    msk = np.stack(masks, axis=0).astype(np.float32)
    return jnp.asarray(pst, dtype=BF16), jnp.asarray(msk)


def _hgrn_kernel(lay, rev, *refs):
    if rev:
        (q_ref, f_ref, v_ref, llb_ref, l1lb_ref, pst_ref, msk_ref, out_ref,
         st_ref, qs_ref, g_ref, kk_ref, o_ref, e_all_ref, e_last_ref) = refs
        z_ref = ob_ref = gain_ref = None
    else:
        (q_ref, f_ref, v_ref, llb_ref, l1lb_ref, pst_ref, msk_ref, z_ref, ob_ref, gain_ref, out_ref,
         st_ref, qs_ref, g_ref, kk_ref, o_ref, e_all_ref, e_last_ref) = refs
    nc = SUB_BLOCK // CHUNK
    n_lvl = len(HGRN_LEVELS)
    last_row = 0 if rev else CHUNK - 1
    heads = [slice(h * HEAD_DIM, (h + 1) * HEAD_DIM) for h in range(N_HEADS)]
    _reset_state(lay, rev, st_ref)

    def piece(base):
        blk = _at(base, 0, SUB_BLOCK)
        qs_ref[...] = (_silu(q_ref[blk, :].astype(F32)) * QK_SCALE).astype(BF16)
        a = llb_ref[...]
        b = l1lb_ref[...] + _log_sigmoid(f_ref[blk, :].astype(F32))
        g2 = (jnp.maximum(a, b) + jnp.log(1.0 + jnp.exp(-jnp.abs(a - b)))) * LOG2_E
        g_ref[...] = g2
        kk_ref[...] = (1.0 - jnp.exp2(g2)).astype(BF16)

        pst = pst_ref[...]
        for ci in range(nc):
            c = (nc - 1 - ci) if rev else ci
            g_parts = jnp.concatenate(_split2(g_ref[c * CHUNK:(c + 1) * CHUNK, :]), axis=0)
            e_f32 = jnp.exp2(jnp.dot(pst, g_parts, preferred_element_type=F32))
            e_all_ref[ci] = e_f32.astype(BF16)
            e_last_ref[ci] = e_f32[last_row:last_row + 1, :]

        states = [st_ref[h] for h in range(N_HEADS)]
        for ci in range(nc):
            c = (nc - 1 - ci) if rev else ci
            rows = slice(c * CHUNK, (c + 1) * CHUNK)
            e_ref = e_all_ref.at[ci]
            qcs = [qs_ref[rows, sl] for sl in heads]
            kcs = [kk_ref[rows, sl] for sl in heads]
            vcs = [v_ref[_at(base, c * CHUNK, CHUNK), sl].astype(BF16) for sl in heads]
            e_cums = [e_ref[0:CHUNK, sl] for sl in heads]
            scores = [[_dot_nt(q, k) for q, k in zip(qcs, kcs)]]
            for lvl in range(n_lvl):
                e_ls = [e_ref[(2 + lvl) * CHUNK:(3 + lvl) * CHUNK, sl] for sl in heads]
                scores.append([_dot_nt(q * e, k * e) for q, k, e in zip(qcs, kcs, e_ls)])
            inters = [_dot_nt(q * e, s) for q, e, s in zip(qcs, e_cums, states)]
            e_rests = [e_ref[CHUNK:2 * CHUNK, sl] for sl in heads]
            upds = [_dot_tn(v, k * e) for v, k, e in zip(vcs, kcs, e_rests)]
            attns = []
            for h in range(N_HEADS):
                attn = scores[0][h] * msk_ref[0]
                for lvl in range(n_lvl):
                    attn = attn + scores[1 + lvl][h] * msk_ref[1 + lvl]
                attns.append(attn.astype(BF16))
            intras = [jnp.dot(a, v, preferred_element_type=F32) for a, v in zip(attns, vcs)]
            for h, sl in enumerate(heads):
                o_ref[rows, sl] = inters[h] + intras[h]
                states[h] = states[h] * e_last_ref[ci, :, sl] + upds[h]
        for h in range(N_HEADS):
            st_ref[h] = states[h]
        _finish(rev, base, o_ref, ob_ref, z_ref, gain_ref, out_ref)

    _sub_blocks(lay, rev, piece)


def _mixer_c(proj, log_lb, log1m_lb, o_bwd, gain, lay, rev):
    tb, nb = lay.tb, lay.nb
    bi = _blk_index(lay, rev)

    def cur(cb):
        return pl.BlockSpec((tb, W_GROUP), lambda j: (bi(j), cb))

    vec = pl.BlockSpec((1, W_GROUP), lambda j: (0, 0))
    d = 1 if rev else 0
    pst, msk = _hgrn_consts(rev)
    n_stack = pst.shape[0]
    in_specs = [cur(CB_C_Q), cur(CB_C_FB if rev else CB_C_FF), cur(CB_C_I), vec, vec,
                pl.BlockSpec((n_stack, 2 * CHUNK), lambda j: (0, 0)),
                pl.BlockSpec((msk.shape[0], CHUNK, CHUNK), lambda j: (0, 0, 0))]
    args = [proj, proj, proj, log_lb[d].reshape(1, W_GROUP), log1m_lb[d].reshape(1, W_GROUP), pst, msk]
    if not rev:
        in_specs += [cur(CB_C_Z), pl.BlockSpec((tb, W_GROUP), lambda j: (bi(j), 0)), vec]
        args += [proj, o_bwd, gain.reshape(1, W_GROUP)]
    big = pltpu.VMEM((SUB_BLOCK, W_GROUP), F32)
    return pl.pallas_call(
        functools.partial(_hgrn_kernel, lay, rev),
        grid=(nb,),
        in_specs=in_specs,
        out_specs=pl.BlockSpec((tb, W_GROUP), lambda j: (bi(j), 0)),
        out_shape=jax.ShapeDtypeStruct((lay.n, W_GROUP), F32 if rev else BF16),
        scratch_shapes=[pltpu.VMEM((N_HEADS, HEAD_DIM, HEAD_DIM), F32),
                        pltpu.VMEM((SUB_BLOCK, W_GROUP), BF16),
                        big, pltpu.VMEM((SUB_BLOCK, W_GROUP), BF16), big,
                        pltpu.VMEM((SUB_BLOCK // CHUNK, n_stack, W_GROUP), BF16),
                        pltpu.VMEM((SUB_BLOCK // CHUNK, 1, W_GROUP), F32)],
        compiler_params=_cparams(("arbitrary",)),
        name="mixer_c_hgrn2_bwd" if rev else "mixer_c_hgrn2_fwd",
    )(*args)


def _unit_tri_inverses(a_mats, row, col):
    c = a_mats[0].shape[0]
    eye = jnp.where(row == col, 1.0, 0.0)

    def same_block(size):
        return (row // size) == (col // size)

    base = 8
    m_base = same_block(base)
    ns = [jnp.where(m_base, -a, 0.0).astype(BF16) for a in a_mats]
    n2s = [jnp.dot(n, n, preferred_element_type=F32) for n in ns]
    xs = [eye + n.astype(F32) for n in ns]
    xs = [x + _dot(x, n2) for x, n2 in zip(xs, n2s)]
    n4s = [_dot(n2, n2) for n2 in n2s]
    xs = [x + _dot(x, n4) for x, n4 in zip(xs, n4s)]
    size = base
    while size < c:
        m_off = same_block(2 * size) & jnp.logical_not(same_block(size))
        xbs = [x.astype(BF16) for x in xs]
        ts = [jnp.dot(xb, jnp.where(m_off, a, 0.0).astype(BF16), preferred_element_type=F32)
              for xb, a in zip(xbs, a_mats)]
        xs = [x - jnp.dot(t.astype(BF16), xb, preferred_element_type=F32) for x, t, xb in zip(xs, ts, xbs)]
        size *= 2
    return xs


SHIFT_ROWS = 128


def _shift_mats():
    half = DN_CONV_WIDTH // 2
    t = np.arange(SHIFT_ROWS)[:, None]
    s = np.arange(SHIFT_ROWS + 2 * HALO_D)[None, :]
    mats = [(s == HALO_D + t + d) for d in range(-half, half + 1) if d != 0]
    return jnp.asarray(np.concatenate(mats, axis=0).astype(np.float32), dtype=BF16)


def _delta_pre_kernel(lay, q_ref, k_ref, v_ref, qp_ref, kp_ref, vp_ref, qn_ref, kn_ref, vn_ref, sm_ref,
                      cw_ref, alog_ref, dtb_ref, sh_ref, qo_ref, ko_ref, vo_ref, cum_ref, beta_ref, cumt_ref,
                      xq_ref, xk_ref, xv_ref):
    tb = lay.tb
    blk = pl.program_id(0)
    keep_prev = jnp.where(lay.seq_start(blk), 0.0, 1.0)
    keep_next = jnp.where(lay.seq_end(blk), 0.0, 1.0)
    half = DN_CONV_WIDTH // 2
    parts = ((q_ref, qp_ref, qn_ref, qo_ref, xq_ref), (k_ref, kp_ref, kn_ref, ko_ref, xk_ref),
             (v_ref, vp_ref, vn_ref, vo_ref, xv_ref))
    for c_ref, p_ref, n_ref, _, x_ref in parts:
        x_ref[0:HALO_D, :] = (p_ref[...].astype(F32) * keep_prev).astype(BF16)
        x_ref[HALO_D:HALO_D + tb, :] = c_ref[...]
        x_ref[HALO_D + tb:2 * HALO_D + tb, :] = (n_ref[...].astype(F32) * keep_next).astype(BF16)
    tiles = [(part, r0) for part in range(len(parts)) for r0 in range(0, tb, SHIFT_ROWS)]
    shifted = [jnp.dot(sh_ref[...], parts[part][4][r0:r0 + SHIFT_ROWS + 2 * HALO_D, :], preferred_element_type=F32)
               for part, r0 in tiles]
    for (part, r0), sh in zip(tiles, shifted):
        x_ref, dst_ref = parts[part][4], parts[part][3]
        wsl = slice(part * W_GROUP, (part + 1) * W_GROUP)
        acc = cw_ref[half:half + 1, wsl] * x_ref[HALO_D + r0:HALO_D + r0 + SHIFT_ROWS, :].astype(F32)
        for j, k in enumerate([k for k in range(DN_CONV_WIDTH) if k != half]):
            acc = acc + cw_ref[k:k + 1, wsl] * sh[j * SHIFT_ROWS:(j + 1) * SHIFT_ROWS, :]
        y = _silu(acc)
        rows = slice(r0, r0 + SHIFT_ROWS)
        if part < 2:
            for h in range(N_HEADS):
                sl = slice(h * HEAD_DIM, (h + 1) * HEAD_DIM)
                yh = y[:, sl]
                yn = yh * lax.rsqrt(jnp.sum(yh * yh, axis=-1, keepdims=True) + NORM_EPS)
                dst_ref[rows, sl] = (yn * QK_SCALE if part == 0 else yn).astype(BF16)
        else:
            dst_ref[rows, :] = y.astype(BF16)

    sm = sm_ref[...]
    g = -jnp.exp(alog_ref[...]) * _softplus(sm + dtb_ref[...])
    beta_ref[...] = _sigmoid(sm)

    row = lax.broadcasted_iota(jnp.int32, (CHUNK, CHUNK), 0)
    col = lax.broadcasted_iota(jnp.int32, (CHUNK, CHUNK), 1)
    lower = jnp.where(row >= col, 1.0, 0.0)
    upper = jnp.where(row <= col, 1.0, 0.0)
    fwd_lane = lax.broadcasted_iota(jnp.int32, (CHUNK, SMALL_W), 1) < N_HEADS
    eye_rows = jnp.where(lax.broadcasted_iota(jnp.int32, (SUBLANES, SMALL_W), 0)
                         == lax.broadcasted_iota(jnp.int32, (SUBLANES, SMALL_W), 1), 1.0, 0.0).astype(BF16)
    for c in range(tb // CHUNK):
        gc = g[c * CHUNK:(c + 1) * CHUNK, :]
        cum = jnp.where(fwd_lane, _dot_exact_lhs(lower, gc), _dot_exact_lhs(upper, gc))
        cum_ref[c * CHUNK:(c + 1) * CHUNK, :] = cum
        cum_t = None
        for piece in _split3(cum):
            t = lax.dot_general(eye_rows, piece, (((1,), (1,)), ((), ())), preferred_element_type=F32)
            cum_t = t if cum_t is None else cum_t + t
        cumt_ref[c] = cum_t


def _delta_kernel(lay, rev, *refs):
    if rev:
        (qc_ref, kc_ref, vc_ref, cum_ref, beta_ref, cumt_ref, out_ref,
         s_ref, o_ref, qd_ref, kd_ref, qk_ref, rhs_ref, last_ref, gq_ref, bm_ref) = refs
        z_ref = ob_ref = gain_ref = None
    else:
        (qc_ref, kc_ref, vc_ref, cum_ref, beta_ref, cumt_ref, z_ref, ob_ref, gain_ref, out_ref,
         s_ref, o_ref, qd_ref, kd_ref, qk_ref, rhs_ref, last_ref, gq_ref, bm_ref) = refs
    _reset_state(lay, rev, s_ref)
    named = (qc_ref, kc_ref, vc_ref, cum_ref, beta_ref, cumt_ref, z_ref, ob_ref, gain_ref, out_ref,
             s_ref, o_ref, qd_ref, kd_ref, qk_ref, rhs_ref, last_ref, gq_ref, bm_ref)
    _sub_blocks(lay, rev, functools.partial(_delta_piece, rev, named))


def _delta_piece(rev, named, base):
    (qc_ref, kc_ref, vc_ref, cum_ref, beta_ref, cumt_ref, z_ref, ob_ref, gain_ref, out_ref,
     s_ref, o_ref, qd_ref, kd_ref, qk_ref, rhs_ref, last_ref, gq_ref, bm_ref) = named
    nc = SUB_BLOCK // CHUNK
    row, col, incl, strict = _causal_masks(rev, CHUNK)
    lane0 = N_HEADS if rev else 0

    cums = [cum_ref[_at(base, c * CHUNK, CHUNK), :] for c in range(nc)]
    cum_ts = [cumt_ref[base // CHUNK + c] for c in range(nc)]
    raws = []
    for c in range(nc):
        rows = slice(c * CHUNK, (c + 1) * CHUNK)
        src = _at(base, c * CHUNK, CHUNK)
        beta_all = beta_ref[src, :]
        for h in range(N_HEADS):
            sl = slice(h * HEAD_DIM, (h + 1) * HEAD_DIM)
            qc = qc_ref[src, sl]
            kc = kc_ref[src, sl]
            vc = vc_ref[src, sl]
            ci_b = jnp.broadcast_to(cums[c][:, lane0 + h:lane0 + h + 1], (CHUNK, HEAD_DIM))
            b_lane = 2 * N_HEADS + lane0 + h
            beta_b = jnp.broadcast_to(beta_all[:, b_lane:b_lane + 1], (CHUNK, HEAD_DIM))
            k_beta = kc * beta_b
            raws.append(_dot_nt(jnp.concatenate([k_beta.astype(BF16), qc], axis=0), kc))
            e_ci = jnp.exp(ci_b)
            rhs_ref[c * N_HEADS + h] = jnp.concatenate([vc * beta_b, k_beta * e_ci], axis=1).astype(BF16)
            qd_ref[rows, sl] = qc * e_ci
            c_last = ci_b[0:1, :] if rev else ci_b[CHUNK - 1:CHUNK, :]
            kd_ref[rows, sl] = (kc * jnp.exp(c_last - ci_b)).astype(BF16)
            last_ref[c * N_HEADS + h] = jnp.broadcast_to(jnp.exp(c_last), (8, HEAD_DIM))
    a_mats = []
    for idx, raw in enumerate(raws):
        c, h = divmod(idx, N_HEADS)
        ci_col = cums[c][:, lane0 + h:lane0 + h + 1]
        cj_row = cum_ts[c][lane0 + h:lane0 + h + 1, :]
        decay = jnp.exp(jnp.where(incl, ci_col - cj_row, NEG_BIG))
        a_mats.append(jnp.where(strict, raw[0:CHUNK] * decay, 0.0))
        qk_ref[idx] = (raw[CHUNK:2 * CHUNK] * decay).astype(BF16)

    t_invs = _unit_tri_inverses(a_mats, row, col)
    where = [(slice((idx // N_HEADS) * CHUNK, (idx // N_HEADS + 1) * CHUNK),
              slice((idx % N_HEADS) * HEAD_DIM, (idx % N_HEADS + 1) * HEAD_DIM)) for idx in range(len(t_invs))]
    uws = [jnp.dot(t_inv.astype(BF16), rhs_ref[idx], preferred_element_type=F32).astype(BF16)
           for idx, t_inv in enumerate(t_invs)]
    kuws = [lax.dot_general(kd_ref[rows, sl], uw, (((0,), (0,)), ((), ())), preferred_element_type=F32)
            for (rows, sl), uw in zip(where, uws)]
    quws = [jnp.dot(qk_ref[idx], uw, preferred_element_type=F32) for idx, uw in enumerate(uws)]
    for idx, ((rows, sl), kuw, quw) in enumerate(zip(where, kuws, quws)):
        bm_ref[idx] = kuw[:, 0:HEAD_DIM]
        gq_ref[idx, 0:HEAD_DIM, :] = kuw[:, HEAD_DIM:2 * HEAD_DIM].astype(BF16)
        gq_ref[idx, HEAD_DIM:HEAD_DIM + CHUNK, :] = (qd_ref[rows, sl] - quw[:, HEAD_DIM:2 * HEAD_DIM]).astype(BF16)
        o_ref[rows, sl] = quw[:, 0:HEAD_DIM]

    states = [s_ref[h] for h in range(N_HEADS)]
    for ci in range(nc):
        c = (nc - 1 - ci) if rev else ci
        rows = slice(c * CHUNK, (c + 1) * CHUNK)
        for h in range(N_HEADS):
            idx = c * N_HEADS + h
            sl = slice(h * HEAD_DIM, (h + 1) * HEAD_DIM)
            gs = jnp.dot(gq_ref[idx], states[h].astype(BF16), preferred_element_type=F32)
            o_ref[rows, sl] += gs[HEAD_DIM:HEAD_DIM + CHUNK]
            states[h] = states[h] * last_ref[idx][0:1, :] - gs[0:HEAD_DIM] + bm_ref[idx]
    for h in range(N_HEADS):
        s_ref[h] = states[h]
    _finish(rev, base, o_ref, ob_ref, z_ref, gain_ref, out_ref)


def _mixer_d_pre(proj, proj_small, conv_w, a_log, dt_bias, lay):
    tb, nb = lay.tb, lay.nb
    hb = tb // HALO_D
    n_halo = lay.n // HALO_D

    def cur(cb):
        return pl.BlockSpec((tb, W_GROUP), lambda i: (i, cb))

    def prev(cb):
        return pl.BlockSpec((HALO_D, W_GROUP), lambda i: (jnp.maximum(i * hb - 1, 0), cb))

    def nxt(cb):
        return pl.BlockSpec((HALO_D, W_GROUP), lambda i: (jnp.minimum((i + 1) * hb, n_halo - 1), cb))

    small = pl.BlockSpec((1, SMALL_W), lambda i: (0, 0))
    pad_lanes = SMALL_W - 2 * N_HEADS
    alog_row = jnp.pad(a_log.reshape(1, 2 * N_HEADS), ((0, 0), (0, pad_lanes)))
    dtb_row = jnp.pad(dt_bias.reshape(1, 2 * N_HEADS), ((0, 0), (0, pad_lanes)))
    wide = pl.BlockSpec((tb, W_GROUP), lambda i: (i, 0))
    narrow = pl.BlockSpec((tb, SMALL_W), lambda i: (i, 0))
    shifts = _shift_mats()
    return pl.pallas_call(
        functools.partial(_delta_pre_kernel, lay),
        grid=(nb,),
        in_specs=[cur(CB_D_Q), cur(CB_D_K), cur(CB_D_V), prev(CB_D_Q), prev(CB_D_K), prev(CB_D_V),
                  nxt(CB_D_Q), nxt(CB_D_K), nxt(CB_D_V),
                  pl.BlockSpec((tb, SMALL_W), lambda i: (i, 0)),
                  pl.BlockSpec((DN_CONV_WIDTH, 3 * W_GROUP), lambda i: (0, 0)), small, small,
                  pl.BlockSpec(shifts.shape, lambda i: (0, 0))],
        out_specs=[wide, wide, wide, narrow, narrow,
                   pl.BlockSpec((tb // CHUNK, SUBLANES, CHUNK), lambda i: (i, 0, 0))],
        out_shape=[jax.ShapeDtypeStruct((lay.n, W_GROUP), BF16)] * 3
        + [jax.ShapeDtypeStruct((lay.n, SMALL_W), F32)] * 2
        + [jax.ShapeDtypeStruct((lay.n // CHUNK, SUBLANES, CHUNK), F32)],
        scratch_shapes=[pltpu.VMEM((tb + 2 * HALO_D, W_GROUP), BF16)] * 3,
        compiler_params=_cparams(("arbitrary",)),
        name="mixer_d_deltanet_pre",
    )(*([proj] * 9), proj_small, conv_w, alog_row, dtb_row, shifts)


def _mixer_d(pre, proj, o_bwd, gain, lay, rev):
    tb, nb = lay.tb, lay.nb
    bi = _blk_index(lay, rev)
    wide = pl.BlockSpec((tb, W_GROUP), lambda j: (bi(j), 0))
    narrow = pl.BlockSpec((tb, SMALL_W), lambda j: (bi(j), 0))
    in_specs = [wide, wide, wide, narrow, narrow,
                pl.BlockSpec((tb // CHUNK, SUBLANES, CHUNK), lambda j: (bi(j), 0, 0))]
    args = list(pre)
    if not rev:
        in_specs += [pl.BlockSpec((tb, W_GROUP), lambda j: (bi(j), CB_D_Z)), wide,
                     pl.BlockSpec((1, W_GROUP), lambda j: (0, 0))]
        args += [proj, o_bwd, gain.reshape(1, W_GROUP)]
    big = pltpu.VMEM((SUB_BLOCK, W_GROUP), F32)
    nc = SUB_BLOCK // CHUNK
    return pl.pallas_call(
        functools.partial(_delta_kernel, lay, rev),
        grid=(nb,),
        in_specs=in_specs,
        out_specs=wide,
        out_shape=jax.ShapeDtypeStruct((lay.n, W_GROUP), F32 if rev else BF16),
        scratch_shapes=[pltpu.VMEM((N_HEADS, HEAD_DIM, HEAD_DIM), F32),
                        big,
                        big,
                        pltpu.VMEM((SUB_BLOCK, W_GROUP), BF16),
                        pltpu.VMEM((nc * N_HEADS, CHUNK, CHUNK), BF16),
                        pltpu.VMEM((nc * N_HEADS, CHUNK, 2 * HEAD_DIM), BF16),
                        pltpu.VMEM((nc * N_HEADS, 8, HEAD_DIM), F32),
                        pltpu.VMEM((nc * N_HEADS, HEAD_DIM + CHUNK, HEAD_DIM), BF16),
                        pltpu.VMEM((nc * N_HEADS, HEAD_DIM, HEAD_DIM), F32)],
        compiler_params=_cparams(("arbitrary",)),
        name="mixer_d_deltanet_bwd" if rev else "mixer_d_deltanet_fwd",
    )(*args)


def _regroup_w_in(w):
    d_qkv_end = 15 * W_GROUP
    small = w[:, d_qkv_end:d_qkv_end + 4 * N_HEADS]
    d_z = w[:, d_qkv_end + 4 * N_HEADS:]
    pad = jnp.zeros((w.shape[0], SMALL_W - 4 * N_HEADS), w.dtype)
    main = jnp.concatenate([w[:, :d_qkv_end], d_z], axis=1).astype(BF16)
    return main, jnp.concatenate([small, pad], axis=1).astype(BF16)


def _rotary_tables(t_max):
    half = HEAD_DIM // 2
    inv = 1.0 / (ROPE_BASE ** (jnp.arange(half, dtype=F32) / half))
    ang = jnp.arange(t_max, dtype=F32)[:, None] * inv[None, :]
    cos, sin = jnp.cos(ang), jnp.sin(ang)
    return jnp.concatenate([cos, cos], axis=-1), jnp.concatenate([-sin, sin], axis=-1)


def _pick_tile(t_prompt, t_sample, want):
    tile = want
    while t_prompt % tile or t_sample % tile:
        tile //= 2
    return tile


def kernel(x_prompt, x_sample, c_prompt, c_sample, ada_w, ada_b, norm_g, w_in, conv_a_w, conv_a_b, ln_a_g, ln_a_b,
           ret_norm_g, hgrn_lb_logits, hgrn_norm_g, dn_conv_w, dn_a_log, dn_dt_bias, dn_norm_g, w_out, final_g):
    depth = w_in.shape[0]
    n_prompt, t_prompt, _ = x_prompt.shape
    n_sample, t_sample, _ = x_sample.shape
    tm_in = _pick_tile(t_prompt, t_sample, 1024)
    tm_out = _pick_tile(t_prompt, t_sample, 512)
    lay = _Layout(t_prompt, n_prompt, t_sample, n_sample, _pick_tile(t_prompt, t_sample, 1024))
    lay_a = _Layout(t_prompt, n_prompt, t_sample, n_sample, _pick_tile(t_prompt, t_sample, 512))
    lay_dp = _Layout(t_prompt, n_prompt, t_sample, n_sample, _pick_tile(t_prompt, t_sample, SUB_BLOCK))

    xs = (x_prompt.reshape(-1, D_MODEL), x_sample.reshape(-1, D_MODEL))
    c_all = jnp.concatenate([c_prompt, c_sample], axis=0)
    n_c = c_all.shape[0]
    c_all = jnp.pad(c_all, ((0, (-n_c) % 8), (0, 0)))
    mod = _modulation(c_all, ada_w, ada_b)

    lb = jnp.cumsum(jax.nn.softmax(hgrn_lb_logits.astype(F32), axis=0), axis=0)
    lb = lb - lb[:1]
    log_lb = jnp.log(lb)
    log1m_lb = jnp.log1p(-lb)
    cos2, sin2 = _rotary_tables(max(t_prompt, t_sample))

    for l in range(depth):
        mod_l = mod[l].reshape(mod.shape[1], 1, 3 * D_MODEL)
        proj, proj_small = _in_proj(xs, mod_l, norm_g[l], *_regroup_w_in(w_in[l]), lay, tm_in)
        m_a = _mixer_a(proj, conv_a_w[l], conv_a_b[l], ln_a_g[l], ln_a_b[l], lay_a)
        ob = _mixer_b(proj, cos2, sin2, None, None, lay, True)
        m_b = _mixer_b(proj, cos2, sin2, ob, ret_norm_g[l], lay, False)
        oc = _mixer_c(proj, log_lb[l], log1m_lb[l], None, None, lay, True)
        m_c = _mixer_c(proj, log_lb[l], log1m_lb[l], oc, hgrn_norm_g[l], lay, False)
        d_pre = _mixer_d_pre(proj, proj_small, dn_conv_w[l], dn_a_log[l], dn_dt_bias[l], lay_dp)
        od = _mixer_d(d_pre, proj, None, None, lay, True)
        m_d = _mixer_d(d_pre, proj, od, dn_norm_g[l], lay, False)
        xs = _out_proj(xs, (m_a, m_b, m_c, m_d), mod_l, w_out[l].astype(BF16), final_g, lay, tm_out,
                       final=(l == depth - 1))

    return (xs[0].reshape(n_prompt, t_prompt, D_MODEL), xs[1].reshape(n_sample, t_sample, D_MODEL))
```

```python
import functools
import math

import numpy as np
import jax
import jax.numpy as jnp
from jax import lax
from jax.experimental import pallas as pl
from jax.experimental.pallas import tpu as pltpu

F32 = jnp.float32
BF16 = jnp.bfloat16

D_MODEL = 1024
W_GROUP = 512
HEAD_DIM = 128
SUBLANES = 8
N_HEADS = 4
CONV_A_WIDTH = 31
DN_CONV_WIDTH = 5
CHUNK = 64
ROPE_BASE = 10000.0
RET_DECAY_OFFSET = 5.0
NORM_EPS = 1e-6
QK_SCALE = HEAD_DIM ** -0.5
NEG_BIG = -1e30

N_COL_BLOCKS = 16
SMALL_W = 128
SUB_BLOCK = 256
D_PROJ = N_COL_BLOCKS * W_GROUP
PROJ_TN = 2048
HALO_A = 16
HALO_D = 16
VMEM_LIMIT = 56 * 1024 * 1024

CB_A_VAL, CB_A_GLU, CB_A_Z = 0, 1, 2
CB_B_Q, CB_B_K, CB_B_V, CB_B_Z = 3, 4, 5, 6
CB_C_Q, CB_C_FF, CB_C_FB, CB_C_I, CB_C_Z = 7, 8, 9, 10, 11
CB_D_Q, CB_D_K, CB_D_V, CB_D_Z = 12, 13, 14, 15


def _dot(a, b):
    return jnp.dot(a.astype(BF16), b.astype(BF16), preferred_element_type=F32)


def _dot_nt(a, b):
    return lax.dot_general(a.astype(BF16), b.astype(BF16), (((1,), (1,)), ((), ())), preferred_element_type=F32)


def _dot_tn(a, b):
    return lax.dot_general(a.astype(BF16), b.astype(BF16), (((0,), (0,)), ((), ())), preferred_element_type=F32)


def _split2(x):
    hi = x.astype(BF16)
    lo = (x - hi.astype(F32)).astype(BF16)
    return hi, lo


def _split3(x):
    hi = x.astype(BF16)
    r = x - hi.astype(F32)
    mid = r.astype(BF16)
    lo = (r - mid.astype(F32)).astype(BF16)
    return hi, mid, lo


def _dot_exact_lhs(m, x):
    mb = m.astype(BF16)
    acc = None
    for part in _split3(x):
        t = jnp.dot(mb, part, preferred_element_type=F32)
        acc = t if acc is None else acc + t
    return acc


def _sigmoid(x):
    return 0.5 * jnp.tanh(0.5 * x) + 0.5


def _silu(x):
    return x * _sigmoid(x)


def _softplus(x):
    return jnp.maximum(x, 0.0) + jnp.log(1.0 + jnp.exp(-jnp.abs(x)))


def _log_sigmoid(x):
    return jnp.minimum(x, 0.0) - jnp.log(1.0 + jnp.exp(-jnp.abs(x)))


class _Layout:
    def __init__(self, t_prompt, n_prompt, t_sample, n_sample, tb):
        self.tp = t_prompt * n_prompt
        self.t_prompt = t_prompt
        self.ts = t_sample
        self.n = self.tp + t_sample * n_sample
        self.tb = tb
        assert t_prompt % tb == 0 and t_sample % tb == 0
        self.nb = self.n // tb

    def seq_start(self, blk):
        s = blk * self.tb
        return jnp.where(s < self.tp, s % self.t_prompt == 0, (s - self.tp) % self.ts == 0)

    def seq_end(self, blk):
        e = (blk + 1) * self.tb
        return jnp.where(e <= self.tp, e % self.t_prompt == 0, (e - self.tp) % self.ts == 0)

    def pos_block(self, blk):
        s = blk * self.tb
        return jnp.where(s < self.tp, (s % self.t_prompt) // self.tb, ((s - self.tp) % self.ts) // self.tb)

    def batch_index(self, blk, rows):
        s = blk * rows
        return jnp.where(s < self.tp, s // self.t_prompt, self.tp // self.t_prompt + (s - self.tp) // self.ts)


def _cparams(sem):
    return pltpu.CompilerParams(dimension_semantics=sem, vmem_limit_bytes=VMEM_LIMIT)


def _mod_kernel(c_ref, w_ref, b_ref, o_ref):
    c = c_ref[...]
    o_ref[0] = jnp.dot(_silu(c), w_ref[0], preferred_element_type=F32,
                       precision=lax.Precision.HIGHEST) + b_ref[0]


def _modulation(c_all, ada_w, ada_b):
    depth = ada_w.shape[0]
    nb = c_all.shape[0]
    return pl.pallas_call(
        _mod_kernel,
        grid=(depth,),
        in_specs=[pl.BlockSpec((nb, D_MODEL), lambda l: (0, 0)),
                  pl.BlockSpec((1, D_MODEL, 3 * D_MODEL), lambda l: (l, 0, 0)),
                  pl.BlockSpec((1, 1, 3 * D_MODEL), lambda l: (l, 0, 0))],
        out_specs=pl.BlockSpec((1, nb, 3 * D_MODEL), lambda l: (l, 0, 0)),
        out_shape=jax.ShapeDtypeStruct((depth, nb, 3 * D_MODEL), F32),
        compiler_params=_cparams(("arbitrary",)),
        name="adaln_modulation",
    )(c_all, ada_w, ada_b.reshape(depth, 1, 3 * D_MODEL))


def _load_x(x_refs, i, npb):
    if len(x_refs) == 1:
        return x_refs[0][...]
    return jnp.where(i < npb, x_refs[0][...], x_refs[1][...])


def _x_specs(n_x, tm, npb, row_block):
    def spec(fn):
        return pl.BlockSpec((tm, D_MODEL), lambda *ids: (fn(row_block(*ids)), 0))

    if n_x == 1:
        return [spec(lambda i: i)]
    return [spec(lambda i: jnp.minimum(i, npb - 1)), spec(lambda i: jnp.maximum(i - npb, 0))]


def _in_proj_kernel(n_x, npb, *refs):
    x_refs = refs[:n_x]
    mod_ref, g_ref, w_ref, ws_ref, o_ref, os_ref, h_ref = refs[n_x:]

    @pl.when(pl.program_id(1) == 0)
    def _():
        x = _load_x(x_refs, pl.program_id(0), npb)
        y = x * lax.rsqrt(jnp.mean(x * x, axis=-1, keepdims=True) + NORM_EPS) * g_ref[...]
        shift = mod_ref[0, :, 0:D_MODEL]
        scale = mod_ref[0, :, D_MODEL:2 * D_MODEL]
        h = (y * (1.0 + scale) + shift).astype(BF16)
        h_ref[...] = h
        os_ref[...] = jnp.dot(h, ws_ref[...], preferred_element_type=F32)

    o_ref[...] = jnp.dot(h_ref[...], w_ref[...], preferred_element_type=F32).astype(BF16)


def _in_proj(xs, mod, norm_g, w_main, w_small, lay, tm):
    n = lay.n
    npb = lay.tp // tm
    lay_m = _Layout(lay.t_prompt, lay.tp // lay.t_prompt, lay.ts, (lay.n - lay.tp) // lay.ts, tm)
    return pl.pallas_call(
        functools.partial(_in_proj_kernel, len(xs), npb),
        grid=(n // tm, D_PROJ // PROJ_TN),
        in_specs=_x_specs(len(xs), tm, npb, lambda i, j: i) + [
                  pl.BlockSpec((1, 1, 3 * D_MODEL), lambda i, j: (lay_m.batch_index(i, tm), 0, 0)),
                  pl.BlockSpec((1, D_MODEL), lambda i, j: (0, 0)),
                  pl.BlockSpec((D_MODEL, PROJ_TN), lambda i, j: (0, j)),
                  pl.BlockSpec((D_MODEL, SMALL_W), lambda i, j: (0, 0))],
        out_specs=[pl.BlockSpec((tm, PROJ_TN), lambda i, j: (i, j)),
                   pl.BlockSpec((tm, SMALL_W), lambda i, j: (i, 0))],
        out_shape=[jax.ShapeDtypeStruct((n, D_PROJ), BF16), jax.ShapeDtypeStruct((n, SMALL_W), F32)],
        scratch_shapes=[pltpu.VMEM((tm, D_MODEL), BF16)],
        compiler_params=_cparams(("arbitrary", "arbitrary")),
        name="in_proj",
    )(*xs, mod, norm_g.reshape(1, D_MODEL), w_main, w_small)


def _out_proj_kernel(final, n_x, npb, *refs):
    x_refs = refs[:n_x]
    ma_ref, mb_ref, mc_ref, md_ref, mod_ref, w_ref, fg_ref = refs[n_x:n_x + 7]
    o_refs = refs[n_x + 7:]
    i = pl.program_id(0)
    acc = jnp.dot(ma_ref[...], w_ref[0:W_GROUP, :], preferred_element_type=F32)
    acc += jnp.dot(mb_ref[...], w_ref[W_GROUP:2 * W_GROUP, :], preferred_element_type=F32)
    acc += jnp.dot(mc_ref[...], w_ref[2 * W_GROUP:3 * W_GROUP, :], preferred_element_type=F32)
    acc += jnp.dot(md_ref[...], w_ref[3 * W_GROUP:4 * W_GROUP, :], preferred_element_type=F32)
    gate = mod_ref[0, :, 2 * D_MODEL:3 * D_MODEL]
    y = _load_x(x_refs, i, npb) + gate * acc
    if not final:
        o_refs[0][...] = y
        return
    y = y * lax.rsqrt(jnp.mean(y * y, axis=-1, keepdims=True) + NORM_EPS) * fg_ref[...]

    @pl.when(i < npb)
    def _():
        o_refs[0][...] = y

    @pl.when(i >= npb)
    def _():
        o_refs[1][...] = y


def _out_proj(xs, mixed, mod, w_out_bf, final_g, lay, tm, final):
    n = lay.n
    npb = lay.tp // tm
    lay_m = _Layout(lay.t_prompt, lay.tp // lay.t_prompt, lay.ts, (lay.n - lay.tp) // lay.ts, tm)
    mspec = pl.BlockSpec((tm, W_GROUP), lambda i: (i, 0))
    if final:
        out_specs = [pl.BlockSpec((tm, D_MODEL), lambda i: (jnp.minimum(i, npb - 1), 0)),
                     pl.BlockSpec((tm, D_MODEL), lambda i: (jnp.maximum(i - npb, 0), 0))]
        out_shape = [jax.ShapeDtypeStruct((lay.tp, D_MODEL), F32), jax.ShapeDtypeStruct((n - lay.tp, D_MODEL), F32)]
    else:
        out_specs = [pl.BlockSpec((tm, D_MODEL), lambda i: (i, 0))]
        out_shape = [jax.ShapeDtypeStruct((n, D_MODEL), F32)]
    out = pl.pallas_call(
        functools.partial(_out_proj_kernel, final, len(xs), npb),
        grid=(n // tm,),
        in_specs=_x_specs(len(xs), tm, npb, lambda i: i) + [mspec, mspec, mspec, mspec,
                  pl.BlockSpec((1, 1, 3 * D_MODEL), lambda i: (lay_m.batch_index(i, tm), 0, 0)),
                  pl.BlockSpec((4 * W_GROUP, D_MODEL), lambda i: (0, 0)),
                  pl.BlockSpec((1, D_MODEL), lambda i: (0, 0))],
        out_specs=out_specs,
        out_shape=out_shape,
        compiler_params=_cparams(("arbitrary",)),
        name="out_proj",
    )(*xs, *mixed, mod, w_out_bf, final_g.reshape(1, D_MODEL))
    return tuple(out)


A_ROWS = 32


def _mixer_a_kernel(lay, val_ref, glu_ref, z_ref, vp_ref, gp_ref, vn_ref, gn_ref,
                    cw_ref, cb_ref, lg_ref, lb_ref, o_ref, u_ref, ush_ref, acc_ref):
    tb = lay.tb
    blk = pl.program_id(0)
    keep_prev = jnp.where(lay.seq_start(blk), 0.0, 1.0)
    keep_next = jnp.where(lay.seq_end(blk), 0.0, 1.0)
    u_ref[0:HALO_A, :] = vp_ref[...].astype(F32) * _sigmoid(gp_ref[...].astype(F32)) * keep_prev
    u_ref[HALO_A:HALO_A + tb, :] = val_ref[...].astype(F32) * _sigmoid(glu_ref[...].astype(F32))
    u_ref[HALO_A + tb:2 * HALO_A + tb, :] = vn_ref[...].astype(F32) * _sigmoid(gn_ref[...].astype(F32)) * keep_next
    half = CONV_A_WIDTH // 2
    n_sh = tb + 2 * HALO_A - SUBLANES
    for r in range(1, SUBLANES):
        ush_ref[r - 1] = u_ref[r:r + n_sh, :]

    def tile(t, carry):
        r0 = pl.multiple_of(t * A_ROWS, A_ROWS)
        acc = jnp.zeros((A_ROWS, W_GROUP), F32) + cb_ref[...]
        for k in range(CONV_A_WIDTH):
            a, r = divmod(HALO_A - half + k, SUBLANES)
            win = pl.ds(r0 + a * SUBLANES, A_ROWS)
            src = u_ref[win, :] if r == 0 else ush_ref[r - 1, win, :]
            acc = acc + cw_ref[k:k + 1, :] * src
        acc_ref[pl.ds(r0, A_ROWS), :] = acc
        return carry

    lax.fori_loop(0, tb // A_ROWS, tile, 0)
    acc = acc_ref[...]
    mu = jnp.mean(acc, axis=-1, keepdims=True)
    xc = acc - mu
    var = jnp.mean(xc * xc, axis=-1, keepdims=True)
    y = xc * lax.rsqrt(var + NORM_EPS) * lg_ref[...] + lb_ref[...]
    o_ref[...] = (_silu(y) * _silu(z_ref[...].astype(F32))).astype(BF16)


def _mixer_a(proj, conv_w, conv_b, ln_g, ln_b, lay):
    tb, nb = lay.tb, lay.nb
    hb = tb // HALO_A
    n_halo = lay.n // HALO_A

    def cur(cb):
        return pl.BlockSpec((tb, W_GROUP), lambda i: (i, cb))

    def prev(cb):
        return pl.BlockSpec((HALO_A, W_GROUP), lambda i: (jnp.maximum(i * hb - 1, 0), cb))

    def nxt(cb):
        return pl.BlockSpec((HALO_A, W_GROUP), lambda i: (jnp.minimum((i + 1) * hb, n_halo - 1), cb))

    vec = pl.BlockSpec((1, W_GROUP), lambda i: (0, 0))
    return pl.pallas_call(
        functools.partial(_mixer_a_kernel, lay),
        grid=(nb,),
        in_specs=[cur(CB_A_VAL), cur(CB_A_GLU), cur(CB_A_Z), prev(CB_A_VAL), prev(CB_A_GLU),
                  nxt(CB_A_VAL), nxt(CB_A_GLU),
                  pl.BlockSpec((CONV_A_WIDTH, W_GROUP), lambda i: (0, 0)), vec, vec, vec],
        out_specs=pl.BlockSpec((tb, W_GROUP), lambda i: (i, 0)),
        out_shape=jax.ShapeDtypeStruct((lay.n, W_GROUP), BF16),
        scratch_shapes=[pltpu.VMEM((tb + 2 * HALO_A, W_GROUP), F32),
                        pltpu.VMEM((SUBLANES - 1, tb + 2 * HALO_A - SUBLANES, W_GROUP), F32),
                        pltpu.VMEM((tb, W_GROUP), F32)],
        compiler_params=_cparams(("arbitrary",)),
        name="mixer_a_conv",
    )(proj, proj, proj, proj, proj, proj, proj, conv_w, conv_b.reshape(1, W_GROUP),
      ln_g.reshape(1, W_GROUP), ln_b.reshape(1, W_GROUP))


def _blk_index(lay, rev):
    nb = lay.nb
    return (lambda j: nb - 1 - j) if rev else (lambda j: j)


def _reset_state(lay, rev, s_ref):
    blk = _blk_index(lay, rev)(pl.program_id(0))
    boundary = lay.seq_end(blk) if rev else lay.seq_start(blk)

    @pl.when(boundary)
    def _():
        s_ref[...] = jnp.zeros(s_ref.shape, s_ref.dtype)


def _sub_blocks(lay, rev, body):
    nsb = lay.tb // SUB_BLOCK

    def trip(i, carry):
        piece = (nsb - 1 - i) if rev else i
        body(pl.multiple_of(piece * SUB_BLOCK, SUB_BLOCK))
        return carry

    lax.fori_loop(0, nsb, trip, 0)


def _at(base, off, n):
    return pl.ds(pl.multiple_of(base + off, CHUNK), n)


def _finish(rev, base, o_ref, ob_ref, z_ref, gain_ref, out_ref):
    rows = _at(base, 0, SUB_BLOCK)
    if rev:
        out_ref[rows, :] = o_ref[...]
        return
    for h in range(N_HEADS):
        sl = slice(h * HEAD_DIM, (h + 1) * HEAD_DIM)
        o = o_ref[:, sl] + ob_ref[rows, sl]
        y = o * lax.rsqrt(jnp.mean(o * o, axis=-1, keepdims=True) + NORM_EPS) * gain_ref[:, sl]
        out_ref[rows, sl] = (y * _silu(z_ref[rows, sl].astype(F32))).astype(BF16)


def _causal_masks(rev, c):
    row = lax.broadcasted_iota(jnp.int32, (c, c), 0)
    col = lax.broadcasted_iota(jnp.int32, (c, c), 1)
    incl = (row <= col) if rev else (row >= col)
    strict = (row < col) if rev else (row > col)
    return row, col, incl, strict


def _ret_log_gamma(rev):
    lg = np.log1p(-np.exp2(-RET_DECAY_OFFSET - np.arange(N_HEADS, dtype=np.float64)))
    return lg[::-1].copy() if rev else lg


RET_CHUNK = SUB_BLOCK


def _ret_consts(rev):
    lg = _ret_log_gamma(rev).astype(np.float32)
    i = np.arange(RET_CHUNK, dtype=np.float32)
    idx = (RET_CHUNK - 1 - i) if rev else i
    rel = idx[:, None] - idx[None, :]
    dm = np.where(rel >= 0, np.exp(lg[:, None, None] * np.maximum(rel, 0.0)), 0.0).astype(np.float32)
    qd = np.exp(lg[:, None] * (idx + 1.0)).astype(np.float32)
    kd = np.exp(lg[:, None] * (RET_CHUNK - 1.0 - idx)).astype(np.float32)
    qd = np.broadcast_to(qd[:, :, None], (N_HEADS, RET_CHUNK, HEAD_DIM)).copy()
    kd = np.broadcast_to(kd[:, :, None], (N_HEADS, RET_CHUNK, HEAD_DIM)).copy()
    return jnp.asarray(dm), jnp.asarray(qd), jnp.asarray(kd)


def _ret_kernel(lay, rev, *refs):
    if rev:
        q_ref, k_ref, v_ref, cos_ref, sin_ref, dm_ref, qd_ref, kd_ref, out_ref, s_ref, qt_ref, kt_ref, o_ref = refs
        z_ref = ob_ref = gain_ref = None
    else:
        (q_ref, k_ref, v_ref, cos_ref, sin_ref, dm_ref, qd_ref, kd_ref, z_ref, ob_ref, gain_ref,
         out_ref, s_ref, qt_ref, kt_ref, o_ref) = refs
    nc = SUB_BLOCK // RET_CHUNK
    chunk_decay = [float(np.exp(np.float32(v) * RET_CHUNK)) for v in _ret_log_gamma(rev)]
    heads = [slice(h * HEAD_DIM, (h + 1) * HEAD_DIM) for h in range(N_HEADS)]
    _reset_state(lay, rev, s_ref)

    def piece(base):
        blk = _at(base, 0, SUB_BLOCK)
        cos = cos_ref[blk, :]
        sin = sin_ref[blk, :]
        for sl in heads:
            qh = q_ref[blk, sl].astype(F32)
            kh = k_ref[blk, sl].astype(F32)
            qt_ref[:, sl] = (qh * cos + pltpu.roll(qh, HEAD_DIM // 2, 1) * sin) * QK_SCALE
            kt_ref[:, sl] = kh * cos + pltpu.roll(kh, HEAD_DIM // 2, 1) * sin

        states = [s_ref[h] for h in range(N_HEADS)]
        for ci in range(nc):
            c = (nc - 1 - ci) if rev else ci
            rows = slice(c * RET_CHUNK, (c + 1) * RET_CHUNK)
            qcs = [qt_ref[rows, sl] for sl in heads]
            kcs = [kt_ref[rows, sl] for sl in heads]
            vcs = [v_ref[_at(base, c * RET_CHUNK, RET_CHUNK), sl].astype(BF16) for sl in heads]
            scores = [_dot_nt(q, k) for q, k in zip(qcs, kcs)]
            inters = [_dot(q * qd_ref[h], states[h]) for h, q in enumerate(qcs)]
            upds = [_dot_tn(k * kd_ref[h], v) for h, (k, v) in enumerate(zip(kcs, vcs))]
            intras = [jnp.dot((s * dm_ref[h]).astype(BF16), vcs[h], preferred_element_type=F32)
                      for h, s in enumerate(scores)]
            for h, sl in enumerate(heads):
                o_ref[rows, sl] = intras[h] + inters[h]
                states[h] = chunk_decay[h] * states[h] + upds[h]
        for h in range(N_HEADS):
            s_ref[h] = states[h]
        _finish(rev, base, o_ref, ob_ref, z_ref, gain_ref, out_ref)

    _sub_blocks(lay, rev, piece)


def _mixer_b(proj, cos2, sin2, o_bwd, gain, lay, rev):
    tb, nb = lay.tb, lay.nb
    bi = _blk_index(lay, rev)

    def cur(cb):
        return pl.BlockSpec((tb, W_GROUP), lambda j: (bi(j), cb))

    rot = pl.BlockSpec((tb, HEAD_DIM), lambda j: (lay.pos_block(bi(j)), 0))
    dm, qd, kd = _ret_consts(rev)
    in_specs = [cur(CB_B_Q), cur(CB_B_K), cur(CB_B_V), rot, rot,
                pl.BlockSpec((N_HEADS, RET_CHUNK, RET_CHUNK), lambda j: (0, 0, 0)),
                pl.BlockSpec((N_HEADS, RET_CHUNK, HEAD_DIM), lambda j: (0, 0, 0)),
                pl.BlockSpec((N_HEADS, RET_CHUNK, HEAD_DIM), lambda j: (0, 0, 0))]
    args = [proj, proj, proj, cos2, sin2, dm, qd, kd]
    if not rev:
        in_specs += [cur(CB_B_Z), pl.BlockSpec((tb, W_GROUP), lambda j: (bi(j), 0)),
                     pl.BlockSpec((1, W_GROUP), lambda j: (0, 0))]
        args += [proj, o_bwd, gain.reshape(1, W_GROUP)]
    return pl.pallas_call(
        functools.partial(_ret_kernel, lay, rev),
        grid=(nb,),
        in_specs=in_specs,
        out_specs=pl.BlockSpec((tb, W_GROUP), lambda j: (bi(j), 0)),
        out_shape=jax.ShapeDtypeStruct((lay.n, W_GROUP), F32 if rev else BF16),
        scratch_shapes=[pltpu.VMEM((N_HEADS, HEAD_DIM, HEAD_DIM), F32),
                        pltpu.VMEM((SUB_BLOCK, W_GROUP), F32), pltpu.VMEM((SUB_BLOCK, W_GROUP), F32),
                        pltpu.VMEM((SUB_BLOCK, W_GROUP), F32)],
        compiler_params=_cparams(("arbitrary",)),
        name="mixer_b_retention_bwd" if rev else "mixer_b_retention_fwd",
    )(*args)


HGRN_LEVELS = (32, 16, 8, 4, 2, 1)
LOG2_E = math.log2(math.e)


def _hgrn_consts(rev):
    c = CHUNK
    i = np.arange(c)[:, None]
    t = np.arange(c)[None, :]
    if rev:
        cum = t >= i
        rest = t < i
    else:
        cum = t <= i
        rest = t > i
    blocks = [cum, rest]
    masks = [np.eye(c, dtype=bool)]
    col = np.arange(c)[None, :]
    for s in HGRN_LEVELS:
        start = (i // (2 * s)) * (2 * s)
        later = (i - start) >= s
        if rev:
            ref = start + s
            p = np.where(later, (t >= ref) & (t < i), (t >= i) & (t < ref))
            q_half, k_half = 0, 1
        else:
            ref = start + s - 1
            p = np.where(later, (t > ref) & (t <= i), (t > i) & (t <= ref))
            q_half, k_half = 1, 0
        blocks.append(p)
        same = (i // (2 * s)) == (col // (2 * s))
        masks.append(same & ((i // s) % 2 == q_half) & ((col // s) % 2 == k_half))
    pst = np.concatenate(blocks, axis=0).astype(np.float32)
    pst = np.concatenate([pst, pst], axis=1)
    msk = np.stack(masks, axis=0).astype(np.float32)
    return jnp.asarray(pst, dtype=BF16), jnp.asarray(msk)


def _hgrn_kernel(lay, rev, *refs):
    if rev:
        (q_ref, f_ref, v_ref, llb_ref, l1lb_ref, pst_ref, msk_ref, out_ref,
         st_ref, qs_ref, g_ref, kk_ref, o_ref, e_all_ref, e_last_ref) = refs
        z_ref = ob_ref = gain_ref = None
    else:
        (q_ref, f_ref, v_ref, llb_ref, l1lb_ref, pst_ref, msk_ref, z_ref, ob_ref, gain_ref, out_ref,
         st_ref, qs_ref, g_ref, kk_ref, o_ref, e_all_ref, e_last_ref) = refs
    nc = SUB_BLOCK // CHUNK
    n_lvl = len(HGRN_LEVELS)
    last_row = 0 if rev else CHUNK - 1
    heads = [slice(h * HEAD_DIM, (h + 1) * HEAD_DIM) for h in range(N_HEADS)]
    _reset_state(lay, rev, st_ref)

    def piece(base):
        blk = _at(base, 0, SUB_BLOCK)
        qs_ref[...] = (_silu(q_ref[blk, :].astype(F32)) * QK_SCALE).astype(BF16)
        a = llb_ref[...]
        b = l1lb_ref[...] + _log_sigmoid(f_ref[blk, :].astype(F32))
        g2 = (jnp.maximum(a, b) + jnp.log(1.0 + jnp.exp(-jnp.abs(a - b)))) * LOG2_E
        g_ref[...] = g2
        kk_ref[...] = (1.0 - jnp.exp2(g2)).astype(BF16)

        pst = pst_ref[...]
        for ci in range(nc):
            c = (nc - 1 - ci) if rev else ci
            g_parts = jnp.concatenate(_split2(g_ref[c * CHUNK:(c + 1) * CHUNK, :]), axis=0)
            e_f32 = jnp.exp2(jnp.dot(pst, g_parts, preferred_element_type=F32))
            e_all_ref[ci] = e_f32.astype(BF16)
            e_last_ref[ci] = e_f32[last_row:last_row + 1, :]

        states = [st_ref[h] for h in range(N_HEADS)]
        for ci in range(nc):
            c = (nc - 1 - ci) if rev else ci
            rows = slice(c * CHUNK, (c + 1) * CHUNK)
            e_ref = e_all_ref.at[ci]
            qcs = [qs_ref[rows, sl] for sl in heads]
            kcs = [kk_ref[rows, sl] for sl in heads]
            vcs = [v_ref[_at(base, c * CHUNK, CHUNK), sl].astype(BF16) for sl in heads]
            e_cums = [e_ref[0:CHUNK, sl] for sl in heads]
            scores = [[_dot_nt(q, k) for q, k in zip(qcs, kcs)]]
            for lvl in range(n_lvl):
                e_ls = [e_ref[(2 + lvl) * CHUNK:(3 + lvl) * CHUNK, sl] for sl in heads]
                scores.append([_dot_nt(q * e, k * e) for q, k, e in zip(qcs, kcs, e_ls)])
            inters = [_dot_nt(q * e, s) for q, e, s in zip(qcs, e_cums, states)]
            e_rests = [e_ref[CHUNK:2 * CHUNK, sl] for sl in heads]
            upds = [_dot_tn(v, k * e) for v, k, e in zip(vcs, kcs, e_rests)]
            attns = []
            for h in range(N_HEADS):
                attn = scores[0][h] * msk_ref[0]
                for lvl in range(n_lvl):
                    attn = attn + scores[1 + lvl][h] * msk_ref[1 + lvl]
                attns.append(attn.astype(BF16))
            intras = [jnp.dot(a, v, preferred_element_type=F32) for a, v in zip(attns, vcs)]
            for h, sl in enumerate(heads):
                o_ref[rows, sl] = inters[h] + intras[h]
                states[h] = states[h] * e_last_ref[ci, :, sl] + upds[h]
        for h in range(N_HEADS):
            st_ref[h] = states[h]
        _finish(rev, base, o_ref, ob_ref, z_ref, gain_ref, out_ref)

    _sub_blocks(lay, rev, piece)


def _mixer_c(proj, log_lb, log1m_lb, o_bwd, gain, lay, rev):
    tb, nb = lay.tb, lay.nb
    bi = _blk_index(lay, rev)

    def cur(cb):
        return pl.BlockSpec((tb, W_GROUP), lambda j: (bi(j), cb))

    vec = pl.BlockSpec((1, W_GROUP), lambda j: (0, 0))
    d = 1 if rev else 0
    pst, msk = _hgrn_consts(rev)
    n_stack = pst.shape[0]
    in_specs = [cur(CB_C_Q), cur(CB_C_FB if rev else CB_C_FF), cur(CB_C_I), vec, vec,
                pl.BlockSpec((n_stack, 2 * CHUNK), lambda j: (0, 0)),
                pl.BlockSpec((msk.shape[0], CHUNK, CHUNK), lambda j: (0, 0, 0))]
    args = [proj, proj, proj, log_lb[d].reshape(1, W_GROUP), log1m_lb[d].reshape(1, W_GROUP), pst, msk]
    if not rev:
        in_specs += [cur(CB_C_Z), pl.BlockSpec((tb, W_GROUP), lambda j: (bi(j), 0)), vec]
        args += [proj, o_bwd, gain.reshape(1, W_GROUP)]
    big = pltpu.VMEM((SUB_BLOCK, W_GROUP), F32)
    return pl.pallas_call(
        functools.partial(_hgrn_kernel, lay, rev),
        grid=(nb,),
        in_specs=in_specs,
        out_specs=pl.BlockSpec((tb, W_GROUP), lambda j: (bi(j), 0)),
        out_shape=jax.ShapeDtypeStruct((lay.n, W_GROUP), F32 if rev else BF16),
        scratch_shapes=[pltpu.VMEM((N_HEADS, HEAD_DIM, HEAD_DIM), F32),
                        pltpu.VMEM((SUB_BLOCK, W_GROUP), BF16),
                        big, pltpu.VMEM((SUB_BLOCK, W_GROUP), BF16), big,
                        pltpu.VMEM((SUB_BLOCK // CHUNK, n_stack, W_GROUP), BF16),
                        pltpu.VMEM((SUB_BLOCK // CHUNK, 1, W_GROUP), F32)],
        compiler_params=_cparams(("arbitrary",)),
        name="mixer_c_hgrn2_bwd" if rev else "mixer_c_hgrn2_fwd",
    )(*args)


def _unit_tri_inverses(a_mats, row, col):
    c = a_mats[0].shape[0]
    eye = jnp.where(row == col, 1.0, 0.0)

    def same_block(size):
        return (row // size) == (col // size)

    base = 8
    m_base = same_block(base)
    ns = [jnp.where(m_base, -a, 0.0).astype(BF16) for a in a_mats]
    n2s = [jnp.dot(n, n, preferred_element_type=F32) for n in ns]
    xs = [eye + n.astype(F32) for n in ns]
    xs = [x + _dot(x, n2) for x, n2 in zip(xs, n2s)]
    n4s = [_dot(n2, n2) for n2 in n2s]
    xs = [x + _dot(x, n4) for x, n4 in zip(xs, n4s)]
    size = base
    while size < c:
        m_off = same_block(2 * size) & jnp.logical_not(same_block(size))
        xbs = [x.astype(BF16) for x in xs]
        ts = [jnp.dot(xb, jnp.where(m_off, a, 0.0).astype(BF16), preferred_element_type=F32)
              for xb, a in zip(xbs, a_mats)]
        xs = [x - jnp.dot(t.astype(BF16), xb, preferred_element_type=F32) for x, t, xb in zip(xs, ts, xbs)]
        size *= 2
    return xs


SHIFT_ROWS = 128


def _shift_mats():
    half = DN_CONV_WIDTH // 2
    t = np.arange(SHIFT_ROWS)[:, None]
    s = np.arange(SHIFT_ROWS + 2 * HALO_D)[None, :]
    mats = [(s == HALO_D + t + d) for d in range(-half, half + 1) if d != 0]
    return jnp.asarray(np.concatenate(mats, axis=0).astype(np.float32), dtype=BF16)


def _delta_pre_kernel(lay, q_ref, k_ref, v_ref, qp_ref, kp_ref, vp_ref, qn_ref, kn_ref, vn_ref, sm_ref,
                      cw_ref, alog_ref, dtb_ref, sh_ref, qo_ref, ko_ref, vo_ref, cum_ref, beta_ref, cumt_ref,
                      xq_ref, xk_ref, xv_ref):
    tb = lay.tb
    blk = pl.program_id(0)
    keep_prev = jnp.where(lay.seq_start(blk), 0.0, 1.0)
    keep_next = jnp.where(lay.seq_end(blk), 0.0, 1.0)
    half = DN_CONV_WIDTH // 2
    parts = ((q_ref, qp_ref, qn_ref, qo_ref, xq_ref), (k_ref, kp_ref, kn_ref, ko_ref, xk_ref),
             (v_ref, vp_ref, vn_ref, vo_ref, xv_ref))
    for c_ref, p_ref, n_ref, _, x_ref in parts:
        x_ref[0:HALO_D, :] = (p_ref[...].astype(F32) * keep_prev).astype(BF16)
        x_ref[HALO_D:HALO_D + tb, :] = c_ref[...]
        x_ref[HALO_D + tb:2 * HALO_D + tb, :] = (n_ref[...].astype(F32) * keep_next).astype(BF16)
    tiles = [(part, r0) for part in range(len(parts)) for r0 in range(0, tb, SHIFT_ROWS)]
    shifted = [jnp.dot(sh_ref[...], parts[part][4][r0:r0 + SHIFT_ROWS + 2 * HALO_D, :], preferred_element_type=F32)
               for part, r0 in tiles]
    for (part, r0), sh in zip(tiles, shifted):
        x_ref, dst_ref = parts[part][4], parts[part][3]
        wsl = slice(part * W_GROUP, (part + 1) * W_GROUP)
        acc = cw_ref[half:half + 1, wsl] * x_ref[HALO_D + r0:HALO_D + r0 + SHIFT_ROWS, :].astype(F32)
        for j, k in enumerate([k for k in range(DN_CONV_WIDTH) if k != half]):
            acc = acc + cw_ref[k:k + 1, wsl] * sh[j * SHIFT_ROWS:(j + 1) * SHIFT_ROWS, :]
        y = _silu(acc)
        rows = slice(r0, r0 + SHIFT_ROWS)
        if part < 2:
            for h in range(N_HEADS):
                sl = slice(h * HEAD_DIM, (h + 1) * HEAD_DIM)
                yh = y[:, sl]
                yn = yh * lax.rsqrt(jnp.sum(yh * yh, axis=-1, keepdims=True) + NORM_EPS)
                dst_ref[rows, sl] = (yn * QK_SCALE if part == 0 else yn).astype(BF16)
        else:
            dst_ref[rows, :] = y.astype(BF16)

    sm = sm_ref[...]
    g = -jnp.exp(alog_ref[...]) * _softplus(sm + dtb_ref[...])
    beta_ref[...] = _sigmoid(sm)

    row = lax.broadcasted_iota(jnp.int32, (CHUNK, CHUNK), 0)
    col = lax.broadcasted_iota(jnp.int32, (CHUNK, CHUNK), 1)
    lower = jnp.where(row >= col, 1.0, 0.0)
    upper = jnp.where(row <= col, 1.0, 0.0)
    fwd_lane = lax.broadcasted_iota(jnp.int32, (CHUNK, SMALL_W), 1) < N_HEADS
    eye_rows = jnp.where(lax.broadcasted_iota(jnp.int32, (SUBLANES, SMALL_W), 0)
                         == lax.broadcasted_iota(jnp.int32, (SUBLANES, SMALL_W), 1), 1.0, 0.0).astype(BF16)
    for c in range(tb // CHUNK):
        gc = g[c * CHUNK:(c + 1) * CHUNK, :]
        cum = jnp.where(fwd_lane, _dot_exact_lhs(lower, gc), _dot_exact_lhs(upper, gc))
        cum_ref[c * CHUNK:(c + 1) * CHUNK, :] = cum
        cum_t = None
        for piece in _split3(cum):
            t = lax.dot_general(eye_rows, piece, (((1,), (1,)), ((), ())), preferred_element_type=F32)
            cum_t = t if cum_t is None else cum_t + t
        cumt_ref[c] = cum_t


def _delta_kernel(lay, rev, *refs):
    if rev:
        (qc_ref, kc_ref, vc_ref, cum_ref, beta_ref, cumt_ref, out_ref,
         s_ref, o_ref, qd_ref, kd_ref, qk_ref, rhs_ref, last_ref, gq_ref, bm_ref) = refs
        z_ref = ob_ref = gain_ref = None
    else:
        (qc_ref, kc_ref, vc_ref, cum_ref, beta_ref, cumt_ref, z_ref, ob_ref, gain_ref, out_ref,
         s_ref, o_ref, qd_ref, kd_ref, qk_ref, rhs_ref, last_ref, gq_ref, bm_ref) = refs
    _reset_state(lay, rev, s_ref)
    named = (qc_ref, kc_ref, vc_ref, cum_ref, beta_ref, cumt_ref, z_ref, ob_ref, gain_ref, out_ref,
             s_ref, o_ref, qd_ref, kd_ref, qk_ref, rhs_ref, last_ref, gq_ref, bm_ref)
    _sub_blocks(lay, rev, functools.partial(_delta_piece, rev, named))


def _delta_piece(rev, named, base):
    (qc_ref, kc_ref, vc_ref, cum_ref, beta_ref, cumt_ref, z_ref, ob_ref, gain_ref, out_ref,
     s_ref, o_ref, qd_ref, kd_ref, qk_ref, rhs_ref, last_ref, gq_ref, bm_ref) = named
    nc = SUB_BLOCK // CHUNK
    row, col, incl, strict = _causal_masks(rev, CHUNK)
    lane0 = N_HEADS if rev else 0

    cums = [cum_ref[_at(base, c * CHUNK, CHUNK), :] for c in range(nc)]
    cum_ts = [cumt_ref[base // CHUNK + c] for c in range(nc)]
    raws = []
    for c in range(nc):
        rows = slice(c * CHUNK, (c + 1) * CHUNK)
        src = _at(base, c * CHUNK, CHUNK)
        beta_all = beta_ref[src, :]
        for h in range(N_HEADS):
            sl = slice(h * HEAD_DIM, (h + 1) * HEAD_DIM)
            qc = qc_ref[src, sl]
            kc = kc_ref[src, sl]
            vc = vc_ref[src, sl]
            ci_b = jnp.broadcast_to(cums[c][:, lane0 + h:lane0 + h + 1], (CHUNK, HEAD_DIM))
            b_lane = 2 * N_HEADS + lane0 + h
            beta_b = jnp.broadcast_to(beta_all[:, b_lane:b_lane + 1], (CHUNK, HEAD_DIM))
            k_beta = kc * beta_b
            raws.append(_dot_nt(jnp.concatenate([k_beta.astype(BF16), qc], axis=0), kc))
            e_ci = jnp.exp(ci_b)
            rhs_ref[c * N_HEADS + h] = jnp.concatenate([vc * beta_b, k_beta * e_ci], axis=1).astype(BF16)
            qd_ref[rows, sl] = qc * e_ci
            c_last = ci_b[0:1, :] if rev else ci_b[CHUNK - 1:CHUNK, :]
            kd_ref[rows, sl] = (kc * jnp.exp(c_last - ci_b)).astype(BF16)
            last_ref[c * N_HEADS + h] = jnp.broadcast_to(jnp.exp(c_last), (8, HEAD_DIM))
    a_mats = []
    for idx, raw in enumerate(raws):
        c, h = divmod(idx, N_HEADS)
        ci_col = cums[c][:, lane0 + h:lane0 + h + 1]
        cj_row = cum_ts[c][lane0 + h:lane0 + h + 1, :]
        decay = jnp.exp(jnp.where(incl, ci_col - cj_row, NEG_BIG))
        a_mats.append(jnp.where(strict, raw[0:CHUNK] * decay, 0.0))
        qk_ref[idx] = (raw[CHUNK:2 * CHUNK] * decay).astype(BF16)

    t_invs = _unit_tri_inverses(a_mats, row, col)
    where = [(slice((idx // N_HEADS) * CHUNK, (idx // N_HEADS + 1) * CHUNK),
              slice((idx % N_HEADS) * HEAD_DIM, (idx % N_HEADS + 1) * HEAD_DIM)) for idx in range(len(t_invs))]
    uws = [jnp.dot(t_inv.astype(BF16), rhs_ref[idx], preferred_element_type=F32).astype(BF16)
           for idx, t_inv in enumerate(t_invs)]
    kuws = [lax.dot_general(kd_ref[rows, sl], uw, (((0,), (0,)), ((), ())), preferred_element_type=F32)
            for (rows, sl), uw in zip(where, uws)]
    quws = [jnp.dot(qk_ref[idx], uw, preferred_element_type=F32) for idx, uw in enumerate(uws)]
    for idx, ((rows, sl), kuw, quw) in enumerate(zip(where, kuws, quws)):
        bm_ref[idx] = kuw[:, 0:HEAD_DIM]
        gq_ref[idx, 0:HEAD_DIM, :] = kuw[:, HEAD_DIM:2 * HEAD_DIM].astype(BF16)
        gq_ref[idx, HEAD_DIM:HEAD_DIM + CHUNK, :] = (qd_ref[rows, sl] - quw[:, HEAD_DIM:2 * HEAD_DIM]).astype(BF16)
        o_ref[rows, sl] = quw[:, 0:HEAD_DIM]

    states = [s_ref[h] for h in range(N_HEADS)]
    for ci in range(nc):
        c = (nc - 1 - ci) if rev else ci
        rows = slice(c * CHUNK, (c + 1) * CHUNK)
        for h in range(N_HEADS):
            idx = c * N_HEADS + h
            sl = slice(h * HEAD_DIM, (h + 1) * HEAD_DIM)
            gs = jnp.dot(gq_ref[idx], states[h].astype(BF16), preferred_element_type=F32)
            o_ref[rows, sl] += gs[HEAD_DIM:HEAD_DIM + CHUNK]
            states[h] = states[h] * last_ref[idx][0:1, :] - gs[0:HEAD_DIM] + bm_ref[idx]
    for h in range(N_HEADS):
        s_ref[h] = states[h]
    _finish(rev, base, o_ref, ob_ref, z_ref, gain_ref, out_ref)


def _mixer_d_pre(proj, proj_small, conv_w, a_log, dt_bias, lay):
    tb, nb = lay.tb, lay.nb
    hb = tb // HALO_D
    n_halo = lay.n // HALO_D

    def cur(cb):
        return pl.BlockSpec((tb, W_GROUP), lambda i: (i, cb))

    def prev(cb):
        return pl.BlockSpec((HALO_D, W_GROUP), lambda i: (jnp.maximum(i * hb - 1, 0), cb))

    def nxt(cb):
        return pl.BlockSpec((HALO_D, W_GROUP), lambda i: (jnp.minimum((i + 1) * hb, n_halo - 1), cb))

    small = pl.BlockSpec((1, SMALL_W), lambda i: (0, 0))
    pad_lanes = SMALL_W - 2 * N_HEADS
    alog_row = jnp.pad(a_log.reshape(1, 2 * N_HEADS), ((0, 0), (0, pad_lanes)))
    dtb_row = jnp.pad(dt_bias.reshape(1, 2 * N_HEADS), ((0, 0), (0, pad_lanes)))
    wide = pl.BlockSpec((tb, W_GROUP), lambda i: (i, 0))
    narrow = pl.BlockSpec((tb, SMALL_W), lambda i: (i, 0))
    shifts = _shift_mats()
    return pl.pallas_call(
        functools.partial(_delta_pre_kernel, lay),
        grid=(nb,),
        in_specs=[cur(CB_D_Q), cur(CB_D_K), cur(CB_D_V), prev(CB_D_Q), prev(CB_D_K), prev(CB_D_V),
                  nxt(CB_D_Q), nxt(CB_D_K), nxt(CB_D_V),
                  pl.BlockSpec((tb, SMALL_W), lambda i: (i, 0)),
                  pl.BlockSpec((DN_CONV_WIDTH, 3 * W_GROUP), lambda i: (0, 0)), small, small,
                  pl.BlockSpec(shifts.shape, lambda i: (0, 0))],
        out_specs=[wide, wide, wide, narrow, narrow,
                   pl.BlockSpec((tb // CHUNK, SUBLANES, CHUNK), lambda i: (i, 0, 0))],
        out_shape=[jax.ShapeDtypeStruct((lay.n, W_GROUP), BF16)] * 3
        + [jax.ShapeDtypeStruct((lay.n, SMALL_W), F32)] * 2
        + [jax.ShapeDtypeStruct((lay.n // CHUNK, SUBLANES, CHUNK), F32)],
        scratch_shapes=[pltpu.VMEM((tb + 2 * HALO_D, W_GROUP), BF16)] * 3,
        compiler_params=_cparams(("arbitrary",)),
        name="mixer_d_deltanet_pre",
    )(*([proj] * 9), proj_small, conv_w, alog_row, dtb_row, shifts)


def _mixer_d(pre, proj, o_bwd, gain, lay, rev):
    tb, nb = lay.tb, lay.nb
    bi = _blk_index(lay, rev)
    wide = pl.BlockSpec((tb, W_GROUP), lambda j: (bi(j), 0))
    narrow = pl.BlockSpec((tb, SMALL_W), lambda j: (bi(j), 0))
    in_specs = [wide, wide, wide, narrow, narrow,
                pl.BlockSpec((tb // CHUNK, SUBLANES, CHUNK), lambda j: (bi(j), 0, 0))]
    args = list(pre)
    if not rev:
        in_specs += [pl.BlockSpec((tb, W_GROUP), lambda j: (bi(j), CB_D_Z)), wide,
                     pl.BlockSpec((1, W_GROUP), lambda j: (0, 0))]
        args += [proj, o_bwd, gain.reshape(1, W_GROUP)]
    big = pltpu.VMEM((SUB_BLOCK, W_GROUP), F32)
    nc = SUB_BLOCK // CHUNK
    return pl.pallas_call(
        functools.partial(_delta_kernel, lay, rev),
        grid=(nb,),
        in_specs=in_specs,
        out_specs=wide,
        out_shape=jax.ShapeDtypeStruct((lay.n, W_GROUP), F32 if rev else BF16),
        scratch_shapes=[pltpu.VMEM((N_HEADS, HEAD_DIM, HEAD_DIM), F32),
                        big,
                        big,
                        pltpu.VMEM((SUB_BLOCK, W_GROUP), BF16),
                        pltpu.VMEM((nc * N_HEADS, CHUNK, CHUNK), BF16),
                        pltpu.VMEM((nc * N_HEADS, CHUNK, 2 * HEAD_DIM), BF16),
                        pltpu.VMEM((nc * N_HEADS, 8, HEAD_DIM), F32),
                        pltpu.VMEM((nc * N_HEADS, HEAD_DIM + CHUNK, HEAD_DIM), BF16),
                        pltpu.VMEM((nc * N_HEADS, HEAD_DIM, HEAD_DIM), F32)],
        compiler_params=_cparams(("arbitrary",)),
        name="mixer_d_deltanet_bwd" if rev else "mixer_d_deltanet_fwd",
    )(*args)


def _regroup_w_in(w):
    d_qkv_end = 15 * W_GROUP
    small = w[:, d_qkv_end:d_qkv_end + 4 * N_HEADS]
    d_z = w[:, d_qkv_end + 4 * N_HEADS:]
    pad = jnp.zeros((w.shape[0], SMALL_W - 4 * N_HEADS), w.dtype)
    main = jnp.concatenate([w[:, :d_qkv_end], d_z], axis=1).astype(BF16)
    return main, jnp.concatenate([small, pad], axis=1).astype(BF16)


def _rotary_tables(t_max):
    half = HEAD_DIM // 2
    inv = 1.0 / (ROPE_BASE ** (jnp.arange(half, dtype=F32) / half))
    ang = jnp.arange(t_max, dtype=F32)[:, None] * inv[None, :]
    cos, sin = jnp.cos(ang), jnp.sin(ang)
    return jnp.concatenate([cos, cos], axis=-1), jnp.concatenate([-sin, sin], axis=-1)


def _pick_tile(t_prompt, t_sample, want):
    tile = want
    while t_prompt % tile or t_sample % tile:
        tile //= 2
    return tile


def kernel(x_prompt, x_sample, c_prompt, c_sample, ada_w, ada_b, norm_g, w_in, conv_a_w, conv_a_b, ln_a_g, ln_a_b,
           ret_norm_g, hgrn_lb_logits, hgrn_norm_g, dn_conv_w, dn_a_log, dn_dt_bias, dn_norm_g, w_out, final_g):
    depth = w_in.shape[0]
    n_prompt, t_prompt, _ = x_prompt.shape
    n_sample, t_sample, _ = x_sample.shape
    tm_in = _pick_tile(t_prompt, t_sample, 1024)
    tm_out = _pick_tile(t_prompt, t_sample, 512)
    lay = _Layout(t_prompt, n_prompt, t_sample, n_sample, _pick_tile(t_prompt, t_sample, 1024))
    lay_a = _Layout(t_prompt, n_prompt, t_sample, n_sample, _pick_tile(t_prompt, t_sample, 512))
    lay_dp = _Layout(t_prompt, n_prompt, t_sample, n_sample, _pick_tile(t_prompt, t_sample, SUB_BLOCK))

    xs = (x_prompt.reshape(-1, D_MODEL), x_sample.reshape(-1, D_MODEL))
    c_all = jnp.concatenate([c_prompt, c_sample], axis=0)
    n_c = c_all.shape[0]
    c_all = jnp.pad(c_all, ((0, (-n_c) % 8), (0, 0)))
    mod = _modulation(c_all, ada_w, ada_b)

    lb = jnp.cumsum(jax.nn.softmax(hgrn_lb_logits.astype(F32), axis=0), axis=0)
    lb = lb - lb[:1]
    log_lb = jnp.log(lb)
    log1m_lb = jnp.log1p(-lb)
    cos2, sin2 = _rotary_tables(max(t_prompt, t_sample))

    for l in range(depth):
        mod_l = mod[l].reshape(mod.shape[1], 1, 3 * D_MODEL)
        proj, proj_small = _in_proj(xs, mod_l, norm_g[l], *_regroup_w_in(w_in[l]), lay, tm_in)
        m_a = _mixer_a(proj, conv_a_w[l], conv_a_b[l], ln_a_g[l], ln_a_b[l], lay_a)
        ob = _mixer_b(proj, cos2, sin2, None, None, lay, True)
        m_b = _mixer_b(proj, cos2, sin2, ob, ret_norm_g[l], lay, False)
        oc = _mixer_c(proj, log_lb[l], log1m_lb[l], None, None, lay, True)
        m_c = _mixer_c(proj, log_lb[l], log1m_lb[l], oc, hgrn_norm_g[l], lay, False)
        d_pre = _mixer_d_pre(proj, proj_small, dn_conv_w[l], dn_a_log[l], dn_dt_bias[l], lay_dp)
        od = _mixer_d(d_pre, proj, None, None, lay, True)
        m_d = _mixer_d(d_pre, proj, od, dn_norm_g[l], lay, False)
        xs = _out_proj(xs, (m_a, m_b, m_c, m_d), mod_l, w_out[l].astype(BF16), final_g, lay, tm_out,
                       final=(l == depth - 1))

    return (xs[0].reshape(n_prompt, t_prompt, D_MODEL), xs[1].reshape(n_sample, t_sample, D_MODEL))
```

```python
import functools
import math

import numpy as np
import jax
import jax.numpy as jnp
from jax import lax
from jax.experimental import pallas as pl
from jax.experimental.pallas import tpu as pltpu

F32 = jnp.float32
BF16 = jnp.bfloat16

D_MODEL = 1024
W_GROUP = 512
HEAD_DIM = 128
SUBLANES = 8
N_HEADS = 4
CONV_A_WIDTH = 31
DN_CONV_WIDTH = 5
CHUNK = 64
ROPE_BASE = 10000.0
RET_DECAY_OFFSET = 5.0
NORM_EPS = 1e-6
QK_SCALE = HEAD_DIM ** -0.5
NEG_BIG = -1e30

N_COL_BLOCKS = 16
SMALL_W = 128
SUB_BLOCK = 256
DELTA_PIECE = 512
D_PROJ = N_COL_BLOCKS * W_GROUP
PROJ_TN = 2048
HALO_A = 16
HALO_D = 16
VMEM_LIMIT = 56 * 1024 * 1024

CB_A_VAL, CB_A_GLU, CB_A_Z = 0, 1, 2
CB_B_Q, CB_B_K, CB_B_V, CB_B_Z = 3, 4, 5, 6
CB_C_Q, CB_C_FF, CB_C_FB, CB_C_I, CB_C_Z = 7, 8, 9, 10, 11
CB_D_Q, CB_D_K, CB_D_V, CB_D_Z = 12, 13, 14, 15


def _dot(a, b):
    return jnp.dot(a.astype(BF16), b.astype(BF16), preferred_element_type=F32)


def _dot_nt(a, b):
    return lax.dot_general(a.astype(BF16), b.astype(BF16), (((1,), (1,)), ((), ())), preferred_element_type=F32)


def _dot_tn(a, b):
    return lax.dot_general(a.astype(BF16), b.astype(BF16), (((0,), (0,)), ((), ())), preferred_element_type=F32)


def _split2(x):
    hi = x.astype(BF16)
    lo = (x - hi.astype(F32)).astype(BF16)
    return hi, lo


def _split3(x):
    hi = x.astype(BF16)
    r = x - hi.astype(F32)
    mid = r.astype(BF16)
    lo = (r - mid.astype(F32)).astype(BF16)
    return hi, mid, lo


def _dot_exact_lhs(m, x):
    mb = m.astype(BF16)
    acc = None
    for part in _split3(x):
        t = jnp.dot(mb, part, preferred_element_type=F32)
        acc = t if acc is None else acc + t
    return acc


def _sigmoid(x):
    return 0.5 * jnp.tanh(0.5 * x) + 0.5


def _silu(x):
    return x * _sigmoid(x)


def _softplus(x):
    return jnp.maximum(x, 0.0) + jnp.log(1.0 + jnp.exp(-jnp.abs(x)))


def _log_sigmoid(x):
    return jnp.minimum(x, 0.0) - jnp.log(1.0 + jnp.exp(-jnp.abs(x)))


class _Layout:
    def __init__(self, t_prompt, n_prompt, t_sample, n_sample, tb):
        self.tp = t_prompt * n_prompt
        self.t_prompt = t_prompt
        self.ts = t_sample
        self.n = self.tp + t_sample * n_sample
        self.tb = tb
        assert t_prompt % tb == 0 and t_sample % tb == 0
        self.nb = self.n // tb

    def seq_start(self, blk):
        s = blk * self.tb
        return jnp.where(s < self.tp, s % self.t_prompt == 0, (s - self.tp) % self.ts == 0)

    def seq_end(self, blk):
        e = (blk + 1) * self.tb
        return jnp.where(e <= self.tp, e % self.t_prompt == 0, (e - self.tp) % self.ts == 0)

    def pos_block(self, blk):
        s = blk * self.tb
        return jnp.where(s < self.tp, (s % self.t_prompt) // self.tb, ((s - self.tp) % self.ts) // self.tb)

    def batch_index(self, blk, rows):
        s = blk * rows
        return jnp.where(s < self.tp, s // self.t_prompt, self.tp // self.t_prompt + (s - self.tp) // self.ts)


def _cparams(sem):
    return pltpu.CompilerParams(dimension_semantics=sem, vmem_limit_bytes=VMEM_LIMIT)


def _mod_kernel(c_ref, w_ref, b_ref, o_ref):
    c = c_ref[...]
    o_ref[0] = jnp.dot(_silu(c), w_ref[0], preferred_element_type=F32,
                       precision=lax.Precision.HIGHEST) + b_ref[0]


def _modulation(c_all, ada_w, ada_b):
    depth = ada_w.shape[0]
    nb = c_all.shape[0]
    return pl.pallas_call(
        _mod_kernel,
        grid=(depth,),
        in_specs=[pl.BlockSpec((nb, D_MODEL), lambda l: (0, 0)),
                  pl.BlockSpec((1, D_MODEL, 3 * D_MODEL), lambda l: (l, 0, 0)),
                  pl.BlockSpec((1, 1, 3 * D_MODEL), lambda l: (l, 0, 0))],
        out_specs=pl.BlockSpec((1, nb, 3 * D_MODEL), lambda l: (l, 0, 0)),
        out_shape=jax.ShapeDtypeStruct((depth, nb, 3 * D_MODEL), F32),
        compiler_params=_cparams(("arbitrary",)),
        name="adaln_modulation",
    )(c_all, ada_w, ada_b.reshape(depth, 1, 3 * D_MODEL))


def _load_x(x_refs, i, npb):
    if len(x_refs) == 1:
        return x_refs[0][...]
    return jnp.where(i < npb, x_refs[0][...], x_refs[1][...])


def _x_specs(n_x, tm, npb, row_block):
    def spec(fn):
        return pl.BlockSpec((tm, D_MODEL), lambda *ids: (fn(row_block(*ids)), 0))

    if n_x == 1:
        return [spec(lambda i: i)]
    return [spec(lambda i: jnp.minimum(i, npb - 1)), spec(lambda i: jnp.maximum(i - npb, 0))]


def _in_proj_kernel(n_x, npb, *refs):
    x_refs = refs[:n_x]
    mod_ref, g_ref, w_ref, ws_ref, o_ref, os_ref, h_ref = refs[n_x:]

    @pl.when(pl.program_id(1) == 0)
    def _():
        x = _load_x(x_refs, pl.program_id(0), npb)
        y = x * lax.rsqrt(jnp.mean(x * x, axis=-1, keepdims=True) + NORM_EPS) * g_ref[...]
        shift = mod_ref[0, :, 0:D_MODEL]
        scale = mod_ref[0, :, D_MODEL:2 * D_MODEL]
        h = (y * (1.0 + scale) + shift).astype(BF16)
        h_ref[...] = h
        os_ref[...] = jnp.dot(h, ws_ref[...], preferred_element_type=F32)

    o_ref[...] = jnp.dot(h_ref[...], w_ref[...], preferred_element_type=F32).astype(BF16)


def _in_proj(xs, mod, norm_g, w_main, w_small, lay, tm):
    n = lay.n
    npb = lay.tp // tm
    lay_m = _Layout(lay.t_prompt, lay.tp // lay.t_prompt, lay.ts, (lay.n - lay.tp) // lay.ts, tm)
    return pl.pallas_call(
        functools.partial(_in_proj_kernel, len(xs), npb),
        grid=(n // tm, D_PROJ // PROJ_TN),
        in_specs=_x_specs(len(xs), tm, npb, lambda i, j: i) + [
                  pl.BlockSpec((1, 1, 3 * D_MODEL), lambda i, j: (lay_m.batch_index(i, tm), 0, 0)),
                  pl.BlockSpec((1, D_MODEL), lambda i, j: (0, 0)),
                  pl.BlockSpec((D_MODEL, PROJ_TN), lambda i, j: (0, j)),
                  pl.BlockSpec((D_MODEL, SMALL_W), lambda i, j: (0, 0))],
        out_specs=[pl.BlockSpec((tm, PROJ_TN), lambda i, j: (i, j)),
                   pl.BlockSpec((tm, SMALL_W), lambda i, j: (i, 0))],
        out_shape=[jax.ShapeDtypeStruct((n, D_PROJ), BF16), jax.ShapeDtypeStruct((n, SMALL_W), F32)],
        scratch_shapes=[pltpu.VMEM((tm, D_MODEL), BF16)],
        compiler_params=_cparams(("arbitrary", "arbitrary")),
        name="in_proj",
    )(*xs, mod, norm_g.reshape(1, D_MODEL), w_main, w_small)


def _out_proj_kernel(final, n_x, npb, *refs):
    x_refs = refs[:n_x]
    ma_ref, mb_ref, mc_ref, md_ref, mod_ref, w_ref, fg_ref = refs[n_x:n_x + 7]
    o_refs = refs[n_x + 7:]
    i = pl.program_id(0)
    acc = jnp.dot(ma_ref[...], w_ref[0:W_GROUP, :], preferred_element_type=F32)
    acc += jnp.dot(mb_ref[...], w_ref[W_GROUP:2 * W_GROUP, :], preferred_element_type=F32)
    acc += jnp.dot(mc_ref[...], w_ref[2 * W_GROUP:3 * W_GROUP, :], preferred_element_type=F32)
    acc += jnp.dot(md_ref[...], w_ref[3 * W_GROUP:4 * W_GROUP, :], preferred_element_type=F32)
    gate = mod_ref[0, :, 2 * D_MODEL:3 * D_MODEL]
    y = _load_x(x_refs, i, npb) + gate * acc
    if not final:
        o_refs[0][...] = y
        return
    y = y * lax.rsqrt(jnp.mean(y * y, axis=-1, keepdims=True) + NORM_EPS) * fg_ref[...]

    @pl.when(i < npb)
    def _():
        o_refs[0][...] = y

    @pl.when(i >= npb)
    def _():
        o_refs[1][...] = y


def _out_proj(xs, mixed, mod, w_out_bf, final_g, lay, tm, final):
    n = lay.n
    npb = lay.tp // tm
    lay_m = _Layout(lay.t_prompt, lay.tp // lay.t_prompt, lay.ts, (lay.n - lay.tp) // lay.ts, tm)
    mspec = pl.BlockSpec((tm, W_GROUP), lambda i: (i, 0))
    if final:
        out_specs = [pl.BlockSpec((tm, D_MODEL), lambda i: (jnp.minimum(i, npb - 1), 0)),
                     pl.BlockSpec((tm, D_MODEL), lambda i: (jnp.maximum(i - npb, 0), 0))]
        out_shape = [jax.ShapeDtypeStruct((lay.tp, D_MODEL), F32), jax.ShapeDtypeStruct((n - lay.tp, D_MODEL), F32)]
    else:
        out_specs = [pl.BlockSpec((tm, D_MODEL), lambda i: (i, 0))]
        out_shape = [jax.ShapeDtypeStruct((n, D_MODEL), F32)]
    out = pl.pallas_call(
        functools.partial(_out_proj_kernel, final, len(xs), npb),
        grid=(n // tm,),
        in_specs=_x_specs(len(xs), tm, npb, lambda i: i) + [mspec, mspec, mspec, mspec,
                  pl.BlockSpec((1, 1, 3 * D_MODEL), lambda i: (lay_m.batch_index(i, tm), 0, 0)),
                  pl.BlockSpec((4 * W_GROUP, D_MODEL), lambda i: (0, 0)),
                  pl.BlockSpec((1, D_MODEL), lambda i: (0, 0))],
        out_specs=out_specs,
        out_shape=out_shape,
        compiler_params=_cparams(("arbitrary",)),
        name="out_proj",
    )(*xs, *mixed, mod, w_out_bf, final_g.reshape(1, D_MODEL))
    return tuple(out)


A_ROWS = 32


def _mixer_a_kernel(lay, val_ref, glu_ref, z_ref, vp_ref, gp_ref, vn_ref, gn_ref,
                    cw_ref, cb_ref, lg_ref, lb_ref, o_ref, u_ref, ush_ref, acc_ref):
    tb = lay.tb
    blk = pl.program_id(0)
    keep_prev = jnp.where(lay.seq_start(blk), 0.0, 1.0)
    keep_next = jnp.where(lay.seq_end(blk), 0.0, 1.0)
    u_ref[0:HALO_A, :] = vp_ref[...].astype(F32) * _sigmoid(gp_ref[...].astype(F32)) * keep_prev
    u_ref[HALO_A:HALO_A + tb, :] = val_ref[...].astype(F32) * _sigmoid(glu_ref[...].astype(F32))
    u_ref[HALO_A + tb:2 * HALO_A + tb, :] = vn_ref[...].astype(F32) * _sigmoid(gn_ref[...].astype(F32)) * keep_next
    half = CONV_A_WIDTH // 2
    n_sh = tb + 2 * HALO_A - SUBLANES
    for r in range(1, SUBLANES):
        ush_ref[r - 1] = u_ref[r:r + n_sh, :]

    def tile(t, carry):
        r0 = pl.multiple_of(t * A_ROWS, A_ROWS)
        acc = jnp.zeros((A_ROWS, W_GROUP), F32) + cb_ref[...]
        for k in range(CONV_A_WIDTH):
            a, r = divmod(HALO_A - half + k, SUBLANES)
            win = pl.ds(r0 + a * SUBLANES, A_ROWS)
            src = u_ref[win, :] if r == 0 else ush_ref[r - 1, win, :]
            acc = acc + cw_ref[k:k + 1, :] * src
        acc_ref[pl.ds(r0, A_ROWS), :] = acc
        return carry

    lax.fori_loop(0, tb // A_ROWS, tile, 0)
    acc = acc_ref[...]
    mu = jnp.mean(acc, axis=-1, keepdims=True)
    xc = acc - mu
    var = jnp.mean(xc * xc, axis=-1, keepdims=True)
    y = xc * lax.rsqrt(var + NORM_EPS) * lg_ref[...] + lb_ref[...]
    o_ref[...] = (_silu(y) * _silu(z_ref[...].astype(F32))).astype(BF16)


def _mixer_a(proj, conv_w, conv_b, ln_g, ln_b, lay):
    tb, nb = lay.tb, lay.nb
    hb = tb // HALO_A
    n_halo = lay.n // HALO_A

    def cur(cb):
        return pl.BlockSpec((tb, W_GROUP), lambda i: (i, cb))

    def prev(cb):
        return pl.BlockSpec((HALO_A, W_GROUP), lambda i: (jnp.maximum(i * hb - 1, 0), cb))

    def nxt(cb):
        return pl.BlockSpec((HALO_A, W_GROUP), lambda i: (jnp.minimum((i + 1) * hb, n_halo - 1), cb))

    vec = pl.BlockSpec((1, W_GROUP), lambda i: (0, 0))
    return pl.pallas_call(
        functools.partial(_mixer_a_kernel, lay),
        grid=(nb,),
        in_specs=[cur(CB_A_VAL), cur(CB_A_GLU), cur(CB_A_Z), prev(CB_A_VAL), prev(CB_A_GLU),
                  nxt(CB_A_VAL), nxt(CB_A_GLU),
                  pl.BlockSpec((CONV_A_WIDTH, W_GROUP), lambda i: (0, 0)), vec, vec, vec],
        out_specs=pl.BlockSpec((tb, W_GROUP), lambda i: (i, 0)),
        out_shape=jax.ShapeDtypeStruct((lay.n, W_GROUP), BF16),
        scratch_shapes=[pltpu.VMEM((tb + 2 * HALO_A, W_GROUP), F32),
                        pltpu.VMEM((SUBLANES - 1, tb + 2 * HALO_A - SUBLANES, W_GROUP), F32),
                        pltpu.VMEM((tb, W_GROUP), F32)],
        compiler_params=_cparams(("arbitrary",)),
        name="mixer_a_conv",
    )(proj, proj, proj, proj, proj, proj, proj, conv_w, conv_b.reshape(1, W_GROUP),
      ln_g.reshape(1, W_GROUP), ln_b.reshape(1, W_GROUP))


def _blk_index(lay, rev):
    nb = lay.nb
    return (lambda j: nb - 1 - j) if rev else (lambda j: j)


def _reset_state(lay, rev, s_ref):
    blk = _blk_index(lay, rev)(pl.program_id(0))
    boundary = lay.seq_end(blk) if rev else lay.seq_start(blk)

    @pl.when(boundary)
    def _():
        s_ref[...] = jnp.zeros(s_ref.shape, s_ref.dtype)


def _sub_blocks(lay, rev, body, n_rows=SUB_BLOCK):
    nsb = lay.tb // n_rows

    def trip(i, carry):
        piece = (nsb - 1 - i) if rev else i
        body(pl.multiple_of(piece * n_rows, n_rows))
        return carry

    lax.fori_loop(0, nsb, trip, 0)


def _at(base, off, n):
    return pl.ds(pl.multiple_of(base + off, CHUNK), n)


def _finish(rev, base, o_ref, ob_ref, z_ref, gain_ref, out_ref, n_rows=SUB_BLOCK):
    rows = _at(base, 0, n_rows)
    if rev:
        out_ref[rows, :] = o_ref[...]
        return
    for h in range(N_HEADS):
        sl = slice(h * HEAD_DIM, (h + 1) * HEAD_DIM)
        o = o_ref[:, sl] + ob_ref[rows, sl]
        y = o * lax.rsqrt(jnp.mean(o * o, axis=-1, keepdims=True) + NORM_EPS) * gain_ref[:, sl]
        out_ref[rows, sl] = (y * _silu(z_ref[rows, sl].astype(F32))).astype(BF16)


def _causal_masks(rev, c):
    row = lax.broadcasted_iota(jnp.int32, (c, c), 0)
    col = lax.broadcasted_iota(jnp.int32, (c, c), 1)
    incl = (row <= col) if rev else (row >= col)
    strict = (row < col) if rev else (row > col)
    return row, col, incl, strict


def _ret_log_gamma(rev):
    lg = np.log1p(-np.exp2(-RET_DECAY_OFFSET - np.arange(N_HEADS, dtype=np.float64)))
    return lg[::-1].copy() if rev else lg


RET_CHUNK = SUB_BLOCK


def _ret_consts(rev):
    lg = _ret_log_gamma(rev).astype(np.float32)
    i = np.arange(RET_CHUNK, dtype=np.float32)
    idx = (RET_CHUNK - 1 - i) if rev else i
    rel = idx[:, None] - idx[None, :]
    dm = np.where(rel >= 0, np.exp(lg[:, None, None] * np.maximum(rel, 0.0)), 0.0).astype(np.float32)
    qd = np.exp(lg[:, None] * (idx + 1.0)).astype(np.float32)
    kd = np.exp(lg[:, None] * (RET_CHUNK - 1.0 - idx)).astype(np.float32)
    qd = np.broadcast_to(qd[:, :, None], (N_HEADS, RET_CHUNK, HEAD_DIM)).copy()
    kd = np.broadcast_to(kd[:, :, None], (N_HEADS, RET_CHUNK, HEAD_DIM)).copy()
    return jnp.asarray(dm), jnp.asarray(qd), jnp.asarray(kd)


def _ret_kernel(lay, rev, *refs):
    if rev:
        q_ref, k_ref, v_ref, cos_ref, sin_ref, dm_ref, qd_ref, kd_ref, out_ref, s_ref, qt_ref, kt_ref, o_ref = refs
        z_ref = ob_ref = gain_ref = None
    else:
        (q_ref, k_ref, v_ref, cos_ref, sin_ref, dm_ref, qd_ref, kd_ref, z_ref, ob_ref, gain_ref,
         out_ref, s_ref, qt_ref, kt_ref, o_ref) = refs
    nc = SUB_BLOCK // RET_CHUNK
    chunk_decay = [float(np.exp(np.float32(v) * RET_CHUNK)) for v in _ret_log_gamma(rev)]
    heads = [slice(h * HEAD_DIM, (h + 1) * HEAD_DIM) for h in range(N_HEADS)]
    _reset_state(lay, rev, s_ref)

    def piece(base):
        blk = _at(base, 0, SUB_BLOCK)
        cos = cos_ref[blk, :]
        sin = sin_ref[blk, :]
        for sl in heads:
            qh = q_ref[blk, sl].astype(F32)
            kh = k_ref[blk, sl].astype(F32)
            qt_ref[:, sl] = (qh * cos + pltpu.roll(qh, HEAD_DIM // 2, 1) * sin) * QK_SCALE
            kt_ref[:, sl] = kh * cos + pltpu.roll(kh, HEAD_DIM // 2, 1) * sin

        states = [s_ref[h] for h in range(N_HEADS)]
        for ci in range(nc):
            c = (nc - 1 - ci) if rev else ci
            rows = slice(c * RET_CHUNK, (c + 1) * RET_CHUNK)
            qcs = [qt_ref[rows, sl] for sl in heads]
            kcs = [kt_ref[rows, sl] for sl in heads]
            vcs = [v_ref[_at(base, c * RET_CHUNK, RET_CHUNK), sl].astype(BF16) for sl in heads]
            scores = [_dot_nt(q, k) for q, k in zip(qcs, kcs)]
            inters = [_dot(q * qd_ref[h], states[h]) for h, q in enumerate(qcs)]
            upds = [_dot_tn(k * kd_ref[h], v) for h, (k, v) in enumerate(zip(kcs, vcs))]
            intras = [jnp.dot((s * dm_ref[h]).astype(BF16), vcs[h], preferred_element_type=F32)
                      for h, s in enumerate(scores)]
            for h, sl in enumerate(heads):
                o_ref[rows, sl] = intras[h] + inters[h]
                states[h] = chunk_decay[h] * states[h] + upds[h]
        for h in range(N_HEADS):
            s_ref[h] = states[h]
        _finish(rev, base, o_ref, ob_ref, z_ref, gain_ref, out_ref)

    _sub_blocks(lay, rev, piece)


def _mixer_b(proj, cos2, sin2, o_bwd, gain, lay, rev):
    tb, nb = lay.tb, lay.nb
    bi = _blk_index(lay, rev)

    def cur(cb):
        return pl.BlockSpec((tb, W_GROUP), lambda j: (bi(j), cb))

    rot = pl.BlockSpec((tb, HEAD_DIM), lambda j: (lay.pos_block(bi(j)), 0))
    dm, qd, kd = _ret_consts(rev)
    in_specs = [cur(CB_B_Q), cur(CB_B_K), cur(CB_B_V), rot, rot,
                pl.BlockSpec((N_HEADS, RET_CHUNK, RET_CHUNK), lambda j: (0, 0, 0)),
                pl.BlockSpec((N_HEADS, RET_CHUNK, HEAD_DIM), lambda j: (0, 0, 0)),
                pl.BlockSpec((N_HEADS, RET_CHUNK, HEAD_DIM), lambda j: (0, 0, 0))]
    args = [proj, proj, proj, cos2, sin2, dm, qd, kd]
    if not rev:
        in_specs += [cur(CB_B_Z), pl.BlockSpec((tb, W_GROUP), lambda j: (bi(j), 0)),
                     pl.BlockSpec((1, W_GROUP), lambda j: (0, 0))]
        args += [proj, o_bwd, gain.reshape(1, W_GROUP)]
    return pl.pallas_call(
        functools.partial(_ret_kernel, lay, rev),
        grid=(nb,),
        in_specs=in_specs,
        out_specs=pl.BlockSpec((tb, W_GROUP), lambda j: (bi(j), 0)),
        out_shape=jax.ShapeDtypeStruct((lay.n, W_GROUP), F32 if rev else BF16),
        scratch_shapes=[pltpu.VMEM((N_HEADS, HEAD_DIM, HEAD_DIM), F32),
                        pltpu.VMEM((SUB_BLOCK, W_GROUP), F32), pltpu.VMEM((SUB_BLOCK, W_GROUP), F32),
                        pltpu.VMEM((SUB_BLOCK, W_GROUP), F32)],
        compiler_params=_cparams(("arbitrary",)),
        name="mixer_b_retention_bwd" if rev else "mixer_b_retention_fwd",
    )(*args)


HGRN_LEVELS = (32, 16, 8, 4, 2, 1)
LOG2_E = math.log2(math.e)


def _hgrn_consts(rev):
    c = CHUNK
    i = np.arange(c)[:, None]
    t = np.arange(c)[None, :]
    if rev:
        cum = t >= i
        rest = t < i
    else:
        cum = t <= i
        rest = t > i
    blocks = [cum, rest]
    masks = [np.eye(c, dtype=bool)]
    col = np.arange(c)[None, :]
    for s in HGRN_LEVELS:
        start = (i // (2 * s)) * (2 * s)
        later = (i - start) >= s
        if rev:
            ref = start + s
            p = np.where(later, (t >= ref) & (t < i), (t >= i) & (t < ref))
            q_half, k_half = 0, 1
        else:
            ref = start + s - 1
            p = np.where(later, (t > ref) & (t <= i), (t > i) & (t <= ref))
            q_half, k_half = 1, 0
        blocks.append(p)
        same = (i // (2 * s)) == (col // (2 * s))
        masks.append(same & ((i // s) % 2 == q_half) & ((col // s) % 2 == k_half))
    pst = np.concatenate(blocks, axis=0).astype(np.float32)
    pst = np.concatenate([pst, pst], axis=1)
    msk = np.stack(masks, axis=0).astype(np.float32)
    return jnp.asarray(pst, dtype=BF16), jnp.asarray(msk)


def _hgrn_kernel(lay, rev, *refs):
    if rev:
        (q_ref, f_ref, v_ref, llb_ref, l1lb_ref, pst_ref, msk_ref, out_ref,
         st_ref, qs_ref, g_ref, kk_ref, o_ref, e_all_ref, e_last_ref) = refs
        z_ref = ob_ref = gain_ref = None
    else:
        (q_ref, f_ref, v_ref, llb_ref, l1lb_ref, pst_ref, msk_ref, z_ref, ob_ref, gain_ref, out_ref,
         st_ref, qs_ref, g_ref, kk_ref, o_ref, e_all_ref, e_last_ref) = refs
    nc = SUB_BLOCK // CHUNK
    n_lvl = len(HGRN_LEVELS)
    last_row = 0 if rev else CHUNK - 1
    heads = [slice(h * HEAD_DIM, (h + 1) * HEAD_DIM) for h in range(N_HEADS)]
    _reset_state(lay, rev, st_ref)

    def piece(base):
        blk = _at(base, 0, SUB_BLOCK)
        qs_ref[...] = (_silu(q_ref[blk, :].astype(F32)) * QK_SCALE).astype(BF16)
        a = llb_ref[...]
        b = l1lb_ref[...] + _log_sigmoid(f_ref[blk, :].astype(F32))
        g2 = (jnp.maximum(a, b) + jnp.log(1.0 + jnp.exp(-jnp.abs(a - b)))) * LOG2_E
        g_ref[...] = g2
        kk_ref[...] = (1.0 - jnp.exp2(g2)).astype(BF16)

        pst = pst_ref[...]
        for ci in range(nc):
            c = (nc - 1 - ci) if rev else ci
            g_parts = jnp.concatenate(_split2(g_ref[c * CHUNK:(c + 1) * CHUNK, :]), axis=0)
            e_f32 = jnp.exp2(jnp.dot(pst, g_parts, preferred_element_type=F32))
            e_all_ref[ci] = e_f32.astype(BF16)
            e_last_ref[ci] = e_f32[last_row:last_row + 1, :]

        states = [st_ref[h] for h in range(N_HEADS)]
        for ci in range(nc):
            c = (nc - 1 - ci) if rev else ci
            rows = slice(c * CHUNK, (c + 1) * CHUNK)
            e_ref = e_all_ref.at[ci]
            qcs = [qs_ref[rows, sl] for sl in heads]
            kcs = [kk_ref[rows, sl] for sl in heads]
            vcs = [v_ref[_at(base, c * CHUNK, CHUNK), sl].astype(BF16) for sl in heads]
            e_cums = [e_ref[0:CHUNK, sl] for sl in heads]
            scores = [[_dot_nt(q, k) for q, k in zip(qcs, kcs)]]
            for lvl in range(n_lvl):
                e_ls = [e_ref[(2 + lvl) * CHUNK:(3 + lvl) * CHUNK, sl] for sl in heads]
                scores.append([_dot_nt(q * e, k * e) for q, k, e in zip(qcs, kcs, e_ls)])
            inters = [_dot_nt(q * e, s) for q, e, s in zip(qcs, e_cums, states)]
            e_rests = [e_ref[CHUNK:2 * CHUNK, sl] for sl in heads]
            upds = [_dot_tn(v, k * e) for v, k, e in zip(vcs, kcs, e_rests)]
            attns = []
            for h in range(N_HEADS):
                attn = scores[0][h] * msk_ref[0]
                for lvl in range(n_lvl):
                    attn = attn + scores[1 + lvl][h] * msk_ref[1 + lvl]
                attns.append(attn.astype(BF16))
            intras = [jnp.dot(a, v, preferred_element_type=F32) for a, v in zip(attns, vcs)]
            for h, sl in enumerate(heads):
                o_ref[rows, sl] = inters[h] + intras[h]
                states[h] = states[h] * e_last_ref[ci, :, sl] + upds[h]
        for h in range(N_HEADS):
            st_ref[h] = states[h]
        _finish(rev, base, o_ref, ob_ref, z_ref, gain_ref, out_ref)

    _sub_blocks(lay, rev, piece)


def _mixer_c(proj, log_lb, log1m_lb, o_bwd, gain, lay, rev):
    tb, nb = lay.tb, lay.nb
    bi = _blk_index(lay, rev)

    def cur(cb):
        return pl.BlockSpec((tb, W_GROUP), lambda j: (bi(j), cb))

    vec = pl.BlockSpec((1, W_GROUP), lambda j: (0, 0))
    d = 1 if rev else 0
    pst, msk = _hgrn_consts(rev)
    n_stack = pst.shape[0]
    in_specs = [cur(CB_C_Q), cur(CB_C_FB if rev else CB_C_FF), cur(CB_C_I), vec, vec,
                pl.BlockSpec((n_stack, 2 * CHUNK), lambda j: (0, 0)),
                pl.BlockSpec((msk.shape[0], CHUNK, CHUNK), lambda j: (0, 0, 0))]
    args = [proj, proj, proj, log_lb[d].reshape(1, W_GROUP), log1m_lb[d].reshape(1, W_GROUP), pst, msk]
    if not rev:
        in_specs += [cur(CB_C_Z), pl.BlockSpec((tb, W_GROUP), lambda j: (bi(j), 0)), vec]
        args += [proj, o_bwd, gain.reshape(1, W_GROUP)]
    big = pltpu.VMEM((SUB_BLOCK, W_GROUP), F32)
    return pl.pallas_call(
        functools.partial(_hgrn_kernel, lay, rev),
        grid=(nb,),
        in_specs=in_specs,
        out_specs=pl.BlockSpec((tb, W_GROUP), lambda j: (bi(j), 0)),
        out_shape=jax.ShapeDtypeStruct((lay.n, W_GROUP), F32 if rev else BF16),
        scratch_shapes=[pltpu.VMEM((N_HEADS, HEAD_DIM, HEAD_DIM), F32),
                        pltpu.VMEM((SUB_BLOCK, W_GROUP), BF16),
                        big, pltpu.VMEM((SUB_BLOCK, W_GROUP), BF16), big,
                        pltpu.VMEM((SUB_BLOCK // CHUNK, n_stack, W_GROUP), BF16),
                        pltpu.VMEM((SUB_BLOCK // CHUNK, 1, W_GROUP), F32)],
        compiler_params=_cparams(("arbitrary",)),
        name="mixer_c_hgrn2_bwd" if rev else "mixer_c_hgrn2_fwd",
    )(*args)


def _unit_tri_inverses(a_mats, row, col):
    c = a_mats[0].shape[0]
    eye = jnp.where(row == col, 1.0, 0.0)

    def same_block(size):
        return (row // size) == (col // size)

    base = 8
    m_base = same_block(base)
    ns = [jnp.where(m_base, -a, 0.0).astype(BF16) for a in a_mats]
    n2s = [jnp.dot(n, n, preferred_element_type=F32) for n in ns]
    xs = [eye + n.astype(F32) for n in ns]
    xs = [x + _dot(x, n2) for x, n2 in zip(xs, n2s)]
    n4s = [_dot(n2, n2) for n2 in n2s]
    xs = [x + _dot(x, n4) for x, n4 in zip(xs, n4s)]
    size = base
    while size < c:
        m_off = same_block(2 * size) & jnp.logical_not(same_block(size))
        xbs = [x.astype(BF16) for x in xs]
        ts = [jnp.dot(xb, jnp.where(m_off, a, 0.0).astype(BF16), preferred_element_type=F32)
              for xb, a in zip(xbs, a_mats)]
        xs = [x - jnp.dot(t.astype(BF16), xb, preferred_element_type=F32) for x, t, xb in zip(xs, ts, xbs)]
        size *= 2
    return xs


SHIFT_ROWS = 128


def _shift_mats():
    half = DN_CONV_WIDTH // 2
    t = np.arange(SHIFT_ROWS)[:, None]
    s = np.arange(SHIFT_ROWS + 2 * HALO_D)[None, :]
    mats = [(s == HALO_D + t + d) for d in range(-half, half + 1) if d != 0]
    return jnp.asarray(np.concatenate(mats, axis=0).astype(np.float32), dtype=BF16)


def _delta_pre_kernel(lay, q_ref, k_ref, v_ref, qp_ref, kp_ref, vp_ref, qn_ref, kn_ref, vn_ref, sm_ref,
                      cw_ref, alog_ref, dtb_ref, sh_ref, qo_ref, ko_ref, vo_ref, cum_ref, beta_ref, cumt_ref,
                      xq_ref, xk_ref, xv_ref):
    tb = lay.tb
    blk = pl.program_id(0)
    keep_prev = jnp.where(lay.seq_start(blk), 0.0, 1.0)
    keep_next = jnp.where(lay.seq_end(blk), 0.0, 1.0)
    half = DN_CONV_WIDTH // 2
    parts = ((q_ref, qp_ref, qn_ref, qo_ref, xq_ref), (k_ref, kp_ref, kn_ref, ko_ref, xk_ref),
             (v_ref, vp_ref, vn_ref, vo_ref, xv_ref))
    for c_ref, p_ref, n_ref, _, x_ref in parts:
        x_ref[0:HALO_D, :] = (p_ref[...].astype(F32) * keep_prev).astype(BF16)
        x_ref[HALO_D:HALO_D + tb, :] = c_ref[...]
        x_ref[HALO_D + tb:2 * HALO_D + tb, :] = (n_ref[...].astype(F32) * keep_next).astype(BF16)
    tiles = [(part, r0) for part in range(len(parts)) for r0 in range(0, tb, SHIFT_ROWS)]
    shifted = [jnp.dot(sh_ref[...], parts[part][4][r0:r0 + SHIFT_ROWS + 2 * HALO_D, :], preferred_element_type=F32)
               for part, r0 in tiles]
    for (part, r0), sh in zip(tiles, shifted):
        x_ref, dst_ref = parts[part][4], parts[part][3]
        wsl = slice(part * W_GROUP, (part + 1) * W_GROUP)
        acc = cw_ref[half:half + 1, wsl] * x_ref[HALO_D + r0:HALO_D + r0 + SHIFT_ROWS, :].astype(F32)
        for j, k in enumerate([k for k in range(DN_CONV_WIDTH) if k != half]):
            acc = acc + cw_ref[k:k + 1, wsl] * sh[j * SHIFT_ROWS:(j + 1) * SHIFT_ROWS, :]
        y = _silu(acc)
        rows = slice(r0, r0 + SHIFT_ROWS)
        if part < 2:
            for h in range(N_HEADS):
                sl = slice(h * HEAD_DIM, (h + 1) * HEAD_DIM)
                yh = y[:, sl]
                yn = yh * lax.rsqrt(jnp.sum(yh * yh, axis=-1, keepdims=True) + NORM_EPS)
                dst_ref[rows, sl] = (yn * QK_SCALE if part == 0 else yn).astype(BF16)
        else:
            dst_ref[rows, :] = y.astype(BF16)

    sm = sm_ref[...]
    g = -jnp.exp(alog_ref[...]) * _softplus(sm + dtb_ref[...])
    beta_ref[...] = _sigmoid(sm)

    row = lax.broadcasted_iota(jnp.int32, (CHUNK, CHUNK), 0)
    col = lax.broadcasted_iota(jnp.int32, (CHUNK, CHUNK), 1)
    lower = jnp.where(row >= col, 1.0, 0.0)
    upper = jnp.where(row <= col, 1.0, 0.0)
    fwd_lane = lax.broadcasted_iota(jnp.int32, (CHUNK, SMALL_W), 1) < N_HEADS
    eye_rows = jnp.where(lax.broadcasted_iota(jnp.int32, (SUBLANES, SMALL_W), 0)
                         == lax.broadcasted_iota(jnp.int32, (SUBLANES, SMALL_W), 1), 1.0, 0.0).astype(BF16)
    for c in range(tb // CHUNK):
        gc = g[c * CHUNK:(c + 1) * CHUNK, :]
        cum = jnp.where(fwd_lane, _dot_exact_lhs(lower, gc), _dot_exact_lhs(upper, gc))
        cum_ref[c * CHUNK:(c + 1) * CHUNK, :] = cum
        cum_t = None
        for piece in _split3(cum):
            t = lax.dot_general(eye_rows, piece, (((1,), (1,)), ((), ())), preferred_element_type=F32)
            cum_t = t if cum_t is None else cum_t + t
        cumt_ref[c] = cum_t


def _delta_kernel(lay, rev, *refs):
    if rev:
        (qc_ref, kc_ref, vc_ref, cum_ref, beta_ref, cumt_ref, out_ref,
         s_ref, o_ref, qd_ref, kd_ref, qk_ref, rhs_ref, last_ref, gq_ref, bm_ref) = refs
        z_ref = ob_ref = gain_ref = None
    else:
        (qc_ref, kc_ref, vc_ref, cum_ref, beta_ref, cumt_ref, z_ref, ob_ref, gain_ref, out_ref,
         s_ref, o_ref, qd_ref, kd_ref, qk_ref, rhs_ref, last_ref, gq_ref, bm_ref) = refs
    _reset_state(lay, rev, s_ref)
    named = (qc_ref, kc_ref, vc_ref, cum_ref, beta_ref, cumt_ref, z_ref, ob_ref, gain_ref, out_ref,
             s_ref, o_ref, qd_ref, kd_ref, qk_ref, rhs_ref, last_ref, gq_ref, bm_ref)
    _sub_blocks(lay, rev, functools.partial(_delta_piece, rev, named), DELTA_PIECE)


def _delta_piece(rev, named, base):
    (qc_ref, kc_ref, vc_ref, cum_ref, beta_ref, cumt_ref, z_ref, ob_ref, gain_ref, out_ref,
     s_ref, o_ref, qd_ref, kd_ref, qk_ref, rhs_ref, last_ref, gq_ref, bm_ref) = named
    nc = DELTA_PIECE // CHUNK
    row, col, incl, strict = _causal_masks(rev, CHUNK)
    lane0 = N_HEADS if rev else 0

    cums = [cum_ref[_at(base, c * CHUNK, CHUNK), :] for c in range(nc)]
    cum_ts = [cumt_ref[base // CHUNK + c] for c in range(nc)]
    raws = []
    for c in range(nc):
        rows = slice(c * CHUNK, (c + 1) * CHUNK)
        src = _at(base, c * CHUNK, CHUNK)
        beta_all = beta_ref[src, :]
        for h in range(N_HEADS):
            sl = slice(h * HEAD_DIM, (h + 1) * HEAD_DIM)
            qc = qc_ref[src, sl]
            kc = kc_ref[src, sl]
            vc = vc_ref[src, sl]
            ci_b = jnp.broadcast_to(cums[c][:, lane0 + h:lane0 + h + 1], (CHUNK, HEAD_DIM))
            b_lane = 2 * N_HEADS + lane0 + h
            beta_b = jnp.broadcast_to(beta_all[:, b_lane:b_lane + 1], (CHUNK, HEAD_DIM))
            k_beta = kc * beta_b
            raws.append(_dot_nt(jnp.concatenate([k_beta.astype(BF16), qc], axis=0), kc))
            e_ci = jnp.exp(ci_b)
            rhs_ref[c * N_HEADS + h] = jnp.concatenate([vc * beta_b, k_beta * e_ci], axis=1).astype(BF16)
            qd_ref[rows, sl] = qc * e_ci
            c_last = ci_b[0:1, :] if rev else ci_b[CHUNK - 1:CHUNK, :]
            kd_ref[rows, sl] = (kc * jnp.exp(c_last - ci_b)).astype(BF16)
            last_ref[c * N_HEADS + h] = jnp.broadcast_to(jnp.exp(c_last), (8, HEAD_DIM))
    a_mats = []
    for idx, raw in enumerate(raws):
        c, h = divmod(idx, N_HEADS)
        ci_col = cums[c][:, lane0 + h:lane0 + h + 1]
        cj_row = cum_ts[c][lane0 + h:lane0 + h + 1, :]
        decay = jnp.exp(jnp.where(incl, ci_col - cj_row, NEG_BIG))
        a_mats.append(jnp.where(strict, raw[0:CHUNK] * decay, 0.0))
        qk_ref[idx] = (raw[CHUNK:2 * CHUNK] * decay).astype(BF16)

    t_invs = _unit_tri_inverses(a_mats, row, col)
    where = [(slice((idx // N_HEADS) * CHUNK, (idx // N_HEADS + 1) * CHUNK),
              slice((idx % N_HEADS) * HEAD_DIM, (idx % N_HEADS + 1) * HEAD_DIM)) for idx in range(len(t_invs))]
    uws = [jnp.dot(t_inv.astype(BF16), rhs_ref[idx], preferred_element_type=F32).astype(BF16)
           for idx, t_inv in enumerate(t_invs)]
    kuws = [lax.dot_general(kd_ref[rows, sl], uw, (((0,), (0,)), ((), ())), preferred_element_type=F32)
            for (rows, sl), uw in zip(where, uws)]
    quws = [jnp.dot(qk_ref[idx], uw, preferred_element_type=F32) for idx, uw in enumerate(uws)]
    for idx, ((rows, sl), kuw, quw) in enumerate(zip(where, kuws, quws)):
        bm_ref[idx] = kuw[:, 0:HEAD_DIM]
        gq_ref[idx, 0:HEAD_DIM, :] = kuw[:, HEAD_DIM:2 * HEAD_DIM].astype(BF16)
        gq_ref[idx, HEAD_DIM:HEAD_DIM + CHUNK, :] = (qd_ref[rows, sl] - quw[:, HEAD_DIM:2 * HEAD_DIM]).astype(BF16)
        o_ref[rows, sl] = quw[:, 0:HEAD_DIM]

    states = [s_ref[h] for h in range(N_HEADS)]
    for ci in range(nc):
        c = (nc - 1 - ci) if rev else ci
        rows = slice(c * CHUNK, (c + 1) * CHUNK)
        for h in range(N_HEADS):
            idx = c * N_HEADS + h
            sl = slice(h * HEAD_DIM, (h + 1) * HEAD_DIM)
            gs = jnp.dot(gq_ref[idx], states[h].astype(BF16), preferred_element_type=F32)
            o_ref[rows, sl] += gs[HEAD_DIM:HEAD_DIM + CHUNK]
            states[h] = states[h] * last_ref[idx][0:1, :] - gs[0:HEAD_DIM] + bm_ref[idx]
    for h in range(N_HEADS):
        s_ref[h] = states[h]
    _finish(rev, base, o_ref, ob_ref, z_ref, gain_ref, out_ref, DELTA_PIECE)


def _mixer_d_pre(proj, proj_small, conv_w, a_log, dt_bias, lay):
    tb, nb = lay.tb, lay.nb
    hb = tb // HALO_D
    n_halo = lay.n // HALO_D

    def cur(cb):
        return pl.BlockSpec((tb, W_GROUP), lambda i: (i, cb))

    def prev(cb):
        return pl.BlockSpec((HALO_D, W_GROUP), lambda i: (jnp.maximum(i * hb - 1, 0), cb))

    def nxt(cb):
        return pl.BlockSpec((HALO_D, W_GROUP), lambda i: (jnp.minimum((i + 1) * hb, n_halo - 1), cb))

    small = pl.BlockSpec((1, SMALL_W), lambda i: (0, 0))
    pad_lanes = SMALL_W - 2 * N_HEADS
    alog_row = jnp.pad(a_log.reshape(1, 2 * N_HEADS), ((0, 0), (0, pad_lanes)))
    dtb_row = jnp.pad(dt_bias.reshape(1, 2 * N_HEADS), ((0, 0), (0, pad_lanes)))
    wide = pl.BlockSpec((tb, W_GROUP), lambda i: (i, 0))
    narrow = pl.BlockSpec((tb, SMALL_W), lambda i: (i, 0))
    shifts = _shift_mats()
    return pl.pallas_call(
        functools.partial(_delta_pre_kernel, lay),
        grid=(nb,),
        in_specs=[cur(CB_D_Q), cur(CB_D_K), cur(CB_D_V), prev(CB_D_Q), prev(CB_D_K), prev(CB_D_V),
                  nxt(CB_D_Q), nxt(CB_D_K), nxt(CB_D_V),
                  pl.BlockSpec((tb, SMALL_W), lambda i: (i, 0)),
                  pl.BlockSpec((DN_CONV_WIDTH, 3 * W_GROUP), lambda i: (0, 0)), small, small,
                  pl.BlockSpec(shifts.shape, lambda i: (0, 0))],
        out_specs=[wide, wide, wide, narrow, narrow,
                   pl.BlockSpec((tb // CHUNK, SUBLANES, CHUNK), lambda i: (i, 0, 0))],
        out_shape=[jax.ShapeDtypeStruct((lay.n, W_GROUP), BF16)] * 3
        + [jax.ShapeDtypeStruct((lay.n, SMALL_W), F32)] * 2
        + [jax.ShapeDtypeStruct((lay.n // CHUNK, SUBLANES, CHUNK), F32)],
        scratch_shapes=[pltpu.VMEM((tb + 2 * HALO_D, W_GROUP), BF16)] * 3,
        compiler_params=_cparams(("arbitrary",)),
        name="mixer_d_deltanet_pre",
    )(*([proj] * 9), proj_small, conv_w, alog_row, dtb_row, shifts)


def _mixer_d(pre, proj, o_bwd, gain, lay, rev):
    tb, nb = lay.tb, lay.nb
    bi = _blk_index(lay, rev)
    wide = pl.BlockSpec((tb, W_GROUP), lambda j: (bi(j), 0))
    narrow = pl.BlockSpec((tb, SMALL_W), lambda j: (bi(j), 0))
    in_specs = [wide, wide, wide, narrow, narrow,
                pl.BlockSpec((tb // CHUNK, SUBLANES, CHUNK), lambda j: (bi(j), 0, 0))]
    args = list(pre)
    if not rev:
        in_specs += [pl.BlockSpec((tb, W_GROUP), lambda j: (bi(j), CB_D_Z)), wide,
                     pl.BlockSpec((1, W_GROUP), lambda j: (0, 0))]
        args += [proj, o_bwd, gain.reshape(1, W_GROUP)]
    big = pltpu.VMEM((DELTA_PIECE, W_GROUP), F32)
    nc = DELTA_PIECE // CHUNK
    return pl.pallas_call(
        functools.partial(_delta_kernel, lay, rev),
        grid=(nb,),
        in_specs=in_specs,
        out_specs=wide,
        out_shape=jax.ShapeDtypeStruct((lay.n, W_GROUP), F32 if rev else BF16),
        scratch_shapes=[pltpu.VMEM((N_HEADS, HEAD_DIM, HEAD_DIM), F32),
                        big,
                        big,
                        pltpu.VMEM((DELTA_PIECE, W_GROUP), BF16),
                        pltpu.VMEM((nc * N_HEADS, CHUNK, CHUNK), BF16),
                        pltpu.VMEM((nc * N_HEADS, CHUNK, 2 * HEAD_DIM), BF16),
                        pltpu.VMEM((nc * N_HEADS, 8, HEAD_DIM), F32),
                        pltpu.VMEM((nc * N_HEADS, HEAD_DIM + CHUNK, HEAD_DIM), BF16),
                        pltpu.VMEM((nc * N_HEADS, HEAD_DIM, HEAD_DIM), F32)],
        compiler_params=_cparams(("arbitrary",)),
        name="mixer_d_deltanet_bwd" if rev else "mixer_d_deltanet_fwd",
    )(*args)


def _regroup_w_in(w):
    d_qkv_end = 15 * W_GROUP
    small = w[:, d_qkv_end:d_qkv_end + 4 * N_HEADS]
    d_z = w[:, d_qkv_end + 4 * N_HEADS:]
    pad = jnp.zeros((w.shape[0], SMALL_W - 4 * N_HEADS), w.dtype)
    main = jnp.concatenate([w[:, :d_qkv_end], d_z], axis=1).astype(BF16)
    return main, jnp.concatenate([small, pad], axis=1).astype(BF16)


def _rotary_tables(t_max):
    half = HEAD_DIM // 2
    inv = 1.0 / (ROPE_BASE ** (jnp.arange(half, dtype=F32) / half))
    ang = jnp.arange(t_max, dtype=F32)[:, None] * inv[None, :]
    cos, sin = jnp.cos(ang), jnp.sin(ang)
    return jnp.concatenate([cos, cos], axis=-1), jnp.concatenate([-sin, sin], axis=-1)


def _pick_tile(t_prompt, t_sample, want):
    tile = want
    while t_prompt % tile or t_sample % tile:
        tile //= 2
    return tile


def kernel(x_prompt, x_sample, c_prompt, c_sample, ada_w, ada_b, norm_g, w_in, conv_a_w, conv_a_b, ln_a_g, ln_a_b,
           ret_norm_g, hgrn_lb_logits, hgrn_norm_g, dn_conv_w, dn_a_log, dn_dt_bias, dn_norm_g, w_out, final_g):
    depth = w_in.shape[0]
    n_prompt, t_prompt, _ = x_prompt.shape
    n_sample, t_sample, _ = x_sample.shape
    tm_in = _pick_tile(t_prompt, t_sample, 1024)
    tm_out = _pick_tile(t_prompt, t_sample, 512)
    lay = _Layout(t_prompt, n_prompt, t_sample, n_sample, _pick_tile(t_prompt, t_sample, 1024))
    lay_a = _Layout(t_prompt, n_prompt, t_sample, n_sample, _pick_tile(t_prompt, t_sample, 512))
    lay_dp = _Layout(t_prompt, n_prompt, t_sample, n_sample, _pick_tile(t_prompt, t_sample, DELTA_PIECE))

    xs = (x_prompt.reshape(-1, D_MODEL), x_sample.reshape(-1, D_MODEL))
    c_all = jnp.concatenate([c_prompt, c_sample], axis=0)
    n_c = c_all.shape[0]
    c_all = jnp.pad(c_all, ((0, (-n_c) % 8), (0, 0)))
    mod = _modulation(c_all, ada_w, ada_b)

    lb = jnp.cumsum(jax.nn.softmax(hgrn_lb_logits.astype(F32), axis=0), axis=0)
    lb = lb - lb[:1]
    log_lb = jnp.log(lb)
    log1m_lb = jnp.log1p(-lb)
    cos2, sin2 = _rotary_tables(max(t_prompt, t_sample))

    for l in range(depth):
        mod_l = mod[l].reshape(mod.shape[1], 1, 3 * D_MODEL)
        proj, proj_small = _in_proj(xs, mod_l, norm_g[l], *_regroup_w_in(w_in[l]), lay, tm_in)
        m_a = _mixer_a(proj, conv_a_w[l], conv_a_b[l], ln_a_g[l], ln_a_b[l], lay_a)
        ob = _mixer_b(proj, cos2, sin2, None, None, lay, True)
        m_b = _mixer_b(proj, cos2, sin2, ob, ret_norm_g[l], lay, False)
        oc = _mixer_c(proj, log_lb[l], log1m_lb[l], None, None, lay, True)
        m_c = _mixer_c(proj, log_lb[l], log1m_lb[l], oc, hgrn_norm_g[l], lay, False)
        d_pre = _mixer_d_pre(proj, proj_small, dn_conv_w[l], dn_a_log[l], dn_dt_bias[l], lay_dp)
        od = _mixer_d(d_pre, proj, None, None, lay, True)
        m_d = _mixer_d(d_pre, proj, od, dn_norm_g[l], lay, False)
        xs = _out_proj(xs, (m_a, m_b, m_c, m_d), mod_l, w_out[l].astype(BF16), final_g, lay, tm_out,
                       final=(l == depth - 1))

    return (xs[0].reshape(n_prompt, t_prompt, D_MODEL), xs[1].reshape(n_sample, t_sample, D_MODEL))
```

```python
import functools
import math

import numpy as np
import jax
import jax.numpy as jnp
from jax import lax
from jax.experimental import pallas as pl
from jax.experimental.pallas import tpu as pltpu

F32 = jnp.float32
BF16 = jnp.bfloat16

D_MODEL = 1024
W_GROUP = 512
HEAD_DIM = 128
SUBLANES = 8
N_HEADS = 4
CONV_A_WIDTH = 31
DN_CONV_WIDTH = 5
CHUNK = 64
ROPE_BASE = 10000.0
RET_DECAY_OFFSET = 5.0
NORM_EPS = 1e-6
QK_SCALE = HEAD_DIM ** -0.5
NEG_BIG = -1e30

N_COL_BLOCKS = 16
SMALL_W = 128
SUB_BLOCK = 256
DELTA_PIECE = 512
D_PROJ = N_COL_BLOCKS * W_GROUP
PROJ_TN = 2048
HALO_A = 16
HALO_D = 16
VMEM_LIMIT = 56 * 1024 * 1024

CB_A_VAL, CB_A_GLU, CB_A_Z = 0, 1, 2
CB_B_Q, CB_B_K, CB_B_V, CB_B_Z = 3, 4, 5, 6
CB_C_Q, CB_C_FF, CB_C_FB, CB_C_I, CB_C_Z = 7, 8, 9, 10, 11
CB_D_Q, CB_D_K, CB_D_V, CB_D_Z = 12, 13, 14, 15


def _dot(a, b):
    return jnp.dot(a.astype(BF16), b.astype(BF16), preferred_element_type=F32)


def _dot_nt(a, b):
    return lax.dot_general(a.astype(BF16), b.astype(BF16), (((1,), (1,)), ((), ())), preferred_element_type=F32)


def _dot_tn(a, b):
    return lax.dot_general(a.astype(BF16), b.astype(BF16), (((0,), (0,)), ((), ())), preferred_element_type=F32)


def _split2(x):
    hi = x.astype(BF16)
    lo = (x - hi.astype(F32)).astype(BF16)
    return hi, lo


def _split3(x):
    hi = x.astype(BF16)
    r = x - hi.astype(F32)
    mid = r.astype(BF16)
    lo = (r - mid.astype(F32)).astype(BF16)
    return hi, mid, lo


def _dot_exact_lhs(m, x):
    mb = m.astype(BF16)
    acc = None
    for part in _split3(x):
        t = jnp.dot(mb, part, preferred_element_type=F32)
        acc = t if acc is None else acc + t
    return acc


def _sigmoid(x):
    return 0.5 * jnp.tanh(0.5 * x) + 0.5


def _silu(x):
    return x * _sigmoid(x)


def _softplus(x):
    return jnp.maximum(x, 0.0) + jnp.log(1.0 + jnp.exp(-jnp.abs(x)))


def _log_sigmoid(x):
    return jnp.minimum(x, 0.0) - jnp.log(1.0 + jnp.exp(-jnp.abs(x)))


class _Layout:
    def __init__(self, t_prompt, n_prompt, t_sample, n_sample, tb):
        self.tp = t_prompt * n_prompt
        self.t_prompt = t_prompt
        self.ts = t_sample
        self.n = self.tp + t_sample * n_sample
        self.tb = tb
        assert t_prompt % tb == 0 and t_sample % tb == 0
        self.nb = self.n // tb

    def seq_start(self, blk):
        s = blk * self.tb
        return jnp.where(s < self.tp, s % self.t_prompt == 0, (s - self.tp) % self.ts == 0)

    def seq_end(self, blk):
        e = (blk + 1) * self.tb
        return jnp.where(e <= self.tp, e % self.t_prompt == 0, (e - self.tp) % self.ts == 0)

    def pos_block(self, blk):
        s = blk * self.tb
        return jnp.where(s < self.tp, (s % self.t_prompt) // self.tb, ((s - self.tp) % self.ts) // self.tb)

    def batch_index(self, blk, rows):
        s = blk * rows
        return jnp.where(s < self.tp, s // self.t_prompt, self.tp // self.t_prompt + (s - self.tp) // self.ts)


def _cparams(sem):
    return pltpu.CompilerParams(dimension_semantics=sem, vmem_limit_bytes=VMEM_LIMIT)


def _mod_kernel(c_ref, w_ref, b_ref, o_ref):
    c = c_ref[...]
    o_ref[0] = jnp.dot(_silu(c), w_ref[0], preferred_element_type=F32,
                       precision=lax.Precision.HIGHEST) + b_ref[0]


def _modulation(c_all, ada_w, ada_b):
    depth = ada_w.shape[0]
    nb = c_all.shape[0]
    return pl.pallas_call(
        _mod_kernel,
        grid=(depth,),
        in_specs=[pl.BlockSpec((nb, D_MODEL), lambda l: (0, 0)),
                  pl.BlockSpec((1, D_MODEL, 3 * D_MODEL), lambda l: (l, 0, 0)),
                  pl.BlockSpec((1, 1, 3 * D_MODEL), lambda l: (l, 0, 0))],
        out_specs=pl.BlockSpec((1, nb, 3 * D_MODEL), lambda l: (l, 0, 0)),
        out_shape=jax.ShapeDtypeStruct((depth, nb, 3 * D_MODEL), F32),
        compiler_params=_cparams(("arbitrary",)),
        name="adaln_modulation",
    )(c_all, ada_w, ada_b.reshape(depth, 1, 3 * D_MODEL))


def _load_x(x_refs, i, npb):
    if len(x_refs) == 1:
        return x_refs[0][...]
    return jnp.where(i < npb, x_refs[0][...], x_refs[1][...])


def _x_specs(n_x, tm, npb, row_block):
    def spec(fn):
        return pl.BlockSpec((tm, D_MODEL), lambda *ids: (fn(row_block(*ids)), 0))

    if n_x == 1:
        return [spec(lambda i: i)]
    return [spec(lambda i: jnp.minimum(i, npb - 1)), spec(lambda i: jnp.maximum(i - npb, 0))]


def _in_proj_kernel(n_x, npb, *refs):
    x_refs = refs[:n_x]
    mod_ref, g_ref, w_ref, ws_ref, o_ref, os_ref, h_ref = refs[n_x:]

    @pl.when(pl.program_id(1) == 0)
    def _():
        x = _load_x(x_refs, pl.program_id(0), npb)
        y = x * lax.rsqrt(jnp.mean(x * x, axis=-1, keepdims=True) + NORM_EPS) * g_ref[...]
        shift = mod_ref[0, :, 0:D_MODEL]
        scale = mod_ref[0, :, D_MODEL:2 * D_MODEL]
        h = (y * (1.0 + scale) + shift).astype(BF16)
        h_ref[...] = h
        os_ref[...] = jnp.dot(h, ws_ref[...], preferred_element_type=F32)

    o_ref[...] = jnp.dot(h_ref[...], w_ref[...], preferred_element_type=F32).astype(BF16)


def _in_proj(xs, mod, norm_g, w_main, w_small, lay, tm):
    n = lay.n
    npb = lay.tp // tm
    lay_m = _Layout(lay.t_prompt, lay.tp // lay.t_prompt, lay.ts, (lay.n - lay.tp) // lay.ts, tm)
    return pl.pallas_call(
        functools.partial(_in_proj_kernel, len(xs), npb),
        grid=(n // tm, D_PROJ // PROJ_TN),
        in_specs=_x_specs(len(xs), tm, npb, lambda i, j: i) + [
                  pl.BlockSpec((1, 1, 3 * D_MODEL), lambda i, j: (lay_m.batch_index(i, tm), 0, 0)),
                  pl.BlockSpec((1, D_MODEL), lambda i, j: (0, 0)),
                  pl.BlockSpec((D_MODEL, PROJ_TN), lambda i, j: (0, j)),
                  pl.BlockSpec((D_MODEL, SMALL_W), lambda i, j: (0, 0))],
        out_specs=[pl.BlockSpec((tm, PROJ_TN), lambda i, j: (i, j)),
                   pl.BlockSpec((tm, SMALL_W), lambda i, j: (i, 0))],
        out_shape=[jax.ShapeDtypeStruct((n, D_PROJ), BF16), jax.ShapeDtypeStruct((n, SMALL_W), F32)],
        scratch_shapes=[pltpu.VMEM((tm, D_MODEL), BF16)],
        compiler_params=_cparams(("arbitrary", "arbitrary")),
        name="in_proj",
    )(*xs, mod, norm_g.reshape(1, D_MODEL), w_main, w_small)


def _out_proj_kernel(final, n_x, npb, *refs):
    x_refs = refs[:n_x]
    ma_ref, mb_ref, mc_ref, md_ref, mod_ref, w_ref, fg_ref = refs[n_x:n_x + 7]
    o_refs = refs[n_x + 7:]
    i = pl.program_id(0)
    acc = jnp.dot(ma_ref[...], w_ref[0:W_GROUP, :], preferred_element_type=F32)
    acc += jnp.dot(mb_ref[...], w_ref[W_GROUP:2 * W_GROUP, :], preferred_element_type=F32)
    acc += jnp.dot(mc_ref[...], w_ref[2 * W_GROUP:3 * W_GROUP, :], preferred_element_type=F32)
    acc += jnp.dot(md_ref[...], w_ref[3 * W_GROUP:4 * W_GROUP, :], preferred_element_type=F32)
    gate = mod_ref[0, :, 2 * D_MODEL:3 * D_MODEL]
    y = _load_x(x_refs, i, npb) + gate * acc
    if not final:
        o_refs[0][...] = y
        return
    y = y * lax.rsqrt(jnp.mean(y * y, axis=-1, keepdims=True) + NORM_EPS) * fg_ref[...]

    @pl.when(i < npb)
    def _():
        o_refs[0][...] = y

    @pl.when(i >= npb)
    def _():
        o_refs[1][...] = y


def _out_proj(xs, mixed, mod, w_out_bf, final_g, lay, tm, final):
    n = lay.n
    npb = lay.tp // tm
    lay_m = _Layout(lay.t_prompt, lay.tp // lay.t_prompt, lay.ts, (lay.n - lay.tp) // lay.ts, tm)
    mspec = pl.BlockSpec((tm, W_GROUP), lambda i: (i, 0))
    if final:
        out_specs = [pl.BlockSpec((tm, D_MODEL), lambda i: (jnp.minimum(i, npb - 1), 0)),
                     pl.BlockSpec((tm, D_MODEL), lambda i: (jnp.maximum(i - npb, 0), 0))]
        out_shape = [jax.ShapeDtypeStruct((lay.tp, D_MODEL), F32), jax.ShapeDtypeStruct((n - lay.tp, D_MODEL), F32)]
    else:
        out_specs = [pl.BlockSpec((tm, D_MODEL), lambda i: (i, 0))]
        out_shape = [jax.ShapeDtypeStruct((n, D_MODEL), F32)]
    out = pl.pallas_call(
        functools.partial(_out_proj_kernel, final, len(xs), npb),
        grid=(n // tm,),
        in_specs=_x_specs(len(xs), tm, npb, lambda i: i) + [mspec, mspec, mspec, mspec,
                  pl.BlockSpec((1, 1, 3 * D_MODEL), lambda i: (lay_m.batch_index(i, tm), 0, 0)),
                  pl.BlockSpec((4 * W_GROUP, D_MODEL), lambda i: (0, 0)),
                  pl.BlockSpec((1, D_MODEL), lambda i: (0, 0))],
        out_specs=out_specs,
        out_shape=out_shape,
        compiler_params=_cparams(("arbitrary",)),
        name="out_proj",
    )(*xs, *mixed, mod, w_out_bf, final_g.reshape(1, D_MODEL))
    return tuple(out)


A_ROWS = 32


def _mixer_a_kernel(lay, val_ref, glu_ref, z_ref, vp_ref, gp_ref, vn_ref, gn_ref,
                    cw_ref, cb_ref, lg_ref, lb_ref, o_ref, u_ref, ush_ref, acc_ref):
    tb = lay.tb
    blk = pl.program_id(0)
    keep_prev = jnp.where(lay.seq_start(blk), 0.0, 1.0)
    keep_next = jnp.where(lay.seq_end(blk), 0.0, 1.0)
    u_ref[0:HALO_A, :] = vp_ref[...].astype(F32) * _sigmoid(gp_ref[...].astype(F32)) * keep_prev
    u_ref[HALO_A:HALO_A + tb, :] = val_ref[...].astype(F32) * _sigmoid(glu_ref[...].astype(F32))
    u_ref[HALO_A + tb:2 * HALO_A + tb, :] = vn_ref[...].astype(F32) * _sigmoid(gn_ref[...].astype(F32)) * keep_next
    half = CONV_A_WIDTH // 2
    n_sh = tb + 2 * HALO_A - SUBLANES
    for r in range(1, SUBLANES):
        ush_ref[r - 1] = u_ref[r:r + n_sh, :]

    def tile(t, carry):
        r0 = pl.multiple_of(t * A_ROWS, A_ROWS)
        acc = jnp.zeros((A_ROWS // SUBLANES, SUBLANES, W_GROUP), F32) + cb_ref[...]
        for k in range(CONV_A_WIDTH):
            a, r = divmod(HALO_A - half + k, SUBLANES)
            win = pl.ds(r0 + a * SUBLANES, A_ROWS)
            src = u_ref[win, :] if r == 0 else ush_ref[r - 1, win, :]
            acc = acc + cw_ref[k] * src.reshape(A_ROWS // SUBLANES, SUBLANES, W_GROUP)
        acc_ref[pl.ds(r0, A_ROWS), :] = acc.reshape(A_ROWS, W_GROUP)
        return carry

    lax.fori_loop(0, tb // A_ROWS, tile, 0)
    acc = acc_ref[...]
    mu = jnp.mean(acc, axis=-1, keepdims=True)
    xc = acc - mu
    var = jnp.mean(xc * xc, axis=-1, keepdims=True)
    y = xc * lax.rsqrt(var + NORM_EPS) * lg_ref[...] + lb_ref[...]
    o_ref[...] = (_silu(y) * _silu(z_ref[...].astype(F32))).astype(BF16)


def _mixer_a(proj, conv_w, conv_b, ln_g, ln_b, lay):
    tb, nb = lay.tb, lay.nb
    hb = tb // HALO_A
    n_halo = lay.n // HALO_A

    def cur(cb):
        return pl.BlockSpec((tb, W_GROUP), lambda i: (i, cb))

    def prev(cb):
        return pl.BlockSpec((HALO_A, W_GROUP), lambda i: (jnp.maximum(i * hb - 1, 0), cb))

    def nxt(cb):
        return pl.BlockSpec((HALO_A, W_GROUP), lambda i: (jnp.minimum((i + 1) * hb, n_halo - 1), cb))

    vec = pl.BlockSpec((1, W_GROUP), lambda i: (0, 0))
    return pl.pallas_call(
        functools.partial(_mixer_a_kernel, lay),
        grid=(nb,),
        in_specs=[cur(CB_A_VAL), cur(CB_A_GLU), cur(CB_A_Z), prev(CB_A_VAL), prev(CB_A_GLU),
                  nxt(CB_A_VAL), nxt(CB_A_GLU),
                  pl.BlockSpec((CONV_A_WIDTH, SUBLANES, W_GROUP), lambda i: (0, 0, 0)), vec, vec, vec],
        out_specs=pl.BlockSpec((tb, W_GROUP), lambda i: (i, 0)),
        out_shape=jax.ShapeDtypeStruct((lay.n, W_GROUP), BF16),
        scratch_shapes=[pltpu.VMEM((tb + 2 * HALO_A, W_GROUP), F32),
                        pltpu.VMEM((SUBLANES - 1, tb + 2 * HALO_A - SUBLANES, W_GROUP), F32),
                        pltpu.VMEM((tb, W_GROUP), F32)],
        compiler_params=_cparams(("arbitrary",)),
        name="mixer_a_conv",
    )(proj, proj, proj, proj, proj, proj, proj,
      jnp.broadcast_to(conv_w[:, None, :], (CONV_A_WIDTH, SUBLANES, W_GROUP)), conv_b.reshape(1, W_GROUP),
      ln_g.reshape(1, W_GROUP), ln_b.reshape(1, W_GROUP))


def _blk_index(lay, rev):
    nb = lay.nb
    return (lambda j: nb - 1 - j) if rev else (lambda j: j)


def _reset_state(lay, rev, s_ref):
    blk = _blk_index(lay, rev)(pl.program_id(0))
    boundary = lay.seq_end(blk) if rev else lay.seq_start(blk)

    @pl.when(boundary)
    def _():
        s_ref[...] = jnp.zeros(s_ref.shape, s_ref.dtype)


def _sub_blocks(lay, rev, body, n_rows=SUB_BLOCK):
    nsb = lay.tb // n_rows

    def trip(i, carry):
        piece = (nsb - 1 - i) if rev else i
        body(pl.multiple_of(piece * n_rows, n_rows))
        return carry

    lax.fori_loop(0, nsb, trip, 0)


def _at(base, off, n):
    return pl.ds(pl.multiple_of(base + off, CHUNK), n)


def _finish(rev, base, o_ref, ob_ref, z_ref, gain_ref, out_ref, n_rows=SUB_BLOCK):
    rows = _at(base, 0, n_rows)
    if rev:
        out_ref[rows, :] = o_ref[...]
        return
    for h in range(N_HEADS):
        sl = slice(h * HEAD_DIM, (h + 1) * HEAD_DIM)
        o = o_ref[:, sl] + ob_ref[rows, sl]
        y = o * lax.rsqrt(jnp.mean(o * o, axis=-1, keepdims=True) + NORM_EPS) * gain_ref[:, sl]
        out_ref[rows, sl] = (y * _silu(z_ref[rows, sl].astype(F32))).astype(BF16)


def _causal_masks(rev, c):
    row = lax.broadcasted_iota(jnp.int32, (c, c), 0)
    col = lax.broadcasted_iota(jnp.int32, (c, c), 1)
    incl = (row <= col) if rev else (row >= col)
    strict = (row < col) if rev else (row > col)
    return row, col, incl, strict


def _ret_log_gamma(rev):
    lg = np.log1p(-np.exp2(-RET_DECAY_OFFSET - np.arange(N_HEADS, dtype=np.float64)))
    return lg[::-1].copy() if rev else lg


RET_CHUNK = SUB_BLOCK


def _ret_consts(rev):
    lg = _ret_log_gamma(rev).astype(np.float32)
    i = np.arange(RET_CHUNK, dtype=np.float32)
    idx = (RET_CHUNK - 1 - i) if rev else i
    rel = idx[:, None] - idx[None, :]
    dm = np.where(rel >= 0, np.exp(lg[:, None, None] * np.maximum(rel, 0.0)), 0.0).astype(np.float32)
    qd = np.exp(lg[:, None] * (idx + 1.0)).astype(np.float32)
    kd = np.exp(lg[:, None] * (RET_CHUNK - 1.0 - idx)).astype(np.float32)
    qd = np.broadcast_to(qd[:, :, None], (N_HEADS, RET_CHUNK, HEAD_DIM)).copy()
    kd = np.broadcast_to(kd[:, :, None], (N_HEADS, RET_CHUNK, HEAD_DIM)).copy()
    return jnp.asarray(dm), jnp.asarray(qd), jnp.asarray(kd)


def _ret_kernel(lay, rev, *refs):
    if rev:
        q_ref, k_ref, v_ref, cos_ref, sin_ref, dm_ref, qd_ref, kd_ref, out_ref, s_ref, qt_ref, kt_ref, o_ref = refs
        z_ref = ob_ref = gain_ref = None
    else:
        (q_ref, k_ref, v_ref, cos_ref, sin_ref, dm_ref, qd_ref, kd_ref, z_ref, ob_ref, gain_ref,
         out_ref, s_ref, qt_ref, kt_ref, o_ref) = refs
    nc = SUB_BLOCK // RET_CHUNK
    chunk_decay = [float(np.exp(np.float32(v) * RET_CHUNK)) for v in _ret_log_gamma(rev)]
    heads = [slice(h * HEAD_DIM, (h + 1) * HEAD_DIM) for h in range(N_HEADS)]
    _reset_state(lay, rev, s_ref)

    def piece(base):
        blk = _at(base, 0, SUB_BLOCK)
        cos = cos_ref[blk, :]
        sin = sin_ref[blk, :]
        for sl in heads:
            qh = q_ref[blk, sl].astype(F32)
            kh = k_ref[blk, sl].astype(F32)
            qt_ref[:, sl] = (qh * cos + pltpu.roll(qh, HEAD_DIM // 2, 1) * sin) * QK_SCALE
            kt_ref[:, sl] = kh * cos + pltpu.roll(kh, HEAD_DIM // 2, 1) * sin

        states = [s_ref[h] for h in range(N_HEADS)]
        for ci in range(nc):
            c = (nc - 1 - ci) if rev else ci
            rows = slice(c * RET_CHUNK, (c + 1) * RET_CHUNK)
            qcs = [qt_ref[rows, sl] for sl in heads]
            kcs = [kt_ref[rows, sl] for sl in heads]
            vcs = [v_ref[_at(base, c * RET_CHUNK, RET_CHUNK), sl].astype(BF16) for sl in heads]
            scores = [_dot_nt(q, k) for q, k in zip(qcs, kcs)]
            inters = [_dot(q * qd_ref[h], states[h]) for h, q in enumerate(qcs)]
            upds = [_dot_tn(k * kd_ref[h], v) for h, (k, v) in enumerate(zip(kcs, vcs))]
            intras = [jnp.dot((s * dm_ref[h]).astype(BF16), vcs[h], preferred_element_type=F32)
                      for h, s in enumerate(scores)]
            for h, sl in enumerate(heads):
                o_ref[rows, sl] = intras[h] + inters[h]
                states[h] = chunk_decay[h] * states[h] + upds[h]
        for h in range(N_HEADS):
            s_ref[h] = states[h]
        _finish(rev, base, o_ref, ob_ref, z_ref, gain_ref, out_ref)

    _sub_blocks(lay, rev, piece)


def _mixer_b(proj, cos2, sin2, o_bwd, gain, lay, rev):
    tb, nb = lay.tb, lay.nb
    bi = _blk_index(lay, rev)

    def cur(cb):
        return pl.BlockSpec((tb, W_GROUP), lambda j: (bi(j), cb))

    rot = pl.BlockSpec((tb, HEAD_DIM), lambda j: (lay.pos_block(bi(j)), 0))
    dm, qd, kd = _ret_consts(rev)
    in_specs = [cur(CB_B_Q), cur(CB_B_K), cur(CB_B_V), rot, rot,
                pl.BlockSpec((N_HEADS, RET_CHUNK, RET_CHUNK), lambda j: (0, 0, 0)),
                pl.BlockSpec((N_HEADS, RET_CHUNK, HEAD_DIM), lambda j: (0, 0, 0)),
                pl.BlockSpec((N_HEADS, RET_CHUNK, HEAD_DIM), lambda j: (0, 0, 0))]
    args = [proj, proj, proj, cos2, sin2, dm, qd, kd]
    if not rev:
        in_specs += [cur(CB_B_Z), pl.BlockSpec((tb, W_GROUP), lambda j: (bi(j), 0)),
                     pl.BlockSpec((1, W_GROUP), lambda j: (0, 0))]
        args += [proj, o_bwd, gain.reshape(1, W_GROUP)]
    return pl.pallas_call(
        functools.partial(_ret_kernel, lay, rev),
        grid=(nb,),
        in_specs=in_specs,
        out_specs=pl.BlockSpec((tb, W_GROUP), lambda j: (bi(j), 0)),
        out_shape=jax.ShapeDtypeStruct((lay.n, W_GROUP), F32 if rev else BF16),
        scratch_shapes=[pltpu.VMEM((N_HEADS, HEAD_DIM, HEAD_DIM), F32),
                        pltpu.VMEM((SUB_BLOCK, W_GROUP), F32), pltpu.VMEM((SUB_BLOCK, W_GROUP), F32),
                        pltpu.VMEM((SUB_BLOCK, W_GROUP), F32)],
        compiler_params=_cparams(("arbitrary",)),
        name="mixer_b_retention_bwd" if rev else "mixer_b_retention_fwd",
    )(*args)


HGRN_LEVELS = (32, 16, 8, 4, 2, 1)
LOG2_E = math.log2(math.e)
HGRN_GROUP = N_HEADS


def _hgrn_consts(rev):
    c = CHUNK
    i = np.arange(c)[:, None]
    t = np.arange(c)[None, :]
    if rev:
        cum = t >= i
        rest = t < i
    else:
        cum = t <= i
        rest = t > i
    blocks = [cum, rest]
    masks = [np.eye(c, dtype=bool)]
    col = np.arange(c)[None, :]
    for s in HGRN_LEVELS:
        start = (i // (2 * s)) * (2 * s)
        later = (i - start) >= s
        if rev:
            ref = start + s
            p = np.where(later, (t >= ref) & (t < i), (t >= i) & (t < ref))
            q_half, k_half = 0, 1
        else:
            ref = start + s - 1
            p = np.where(later, (t > ref) & (t <= i), (t > i) & (t <= ref))
            q_half, k_half = 1, 0
        blocks.append(p)
        same = (i // (2 * s)) == (col // (2 * s))
        masks.append(same & ((i // s) % 2 == q_half) & ((col // s) % 2 == k_half))
    pst = np.concatenate(blocks, axis=0).astype(np.float32)
    pst = np.concatenate([pst, pst], axis=1)
    msk = np.stack(masks, axis=0).astype(np.float32)
    return jnp.asarray(pst, dtype=BF16), jnp.asarray(msk)


def _hgrn_kernel(lay, rev, *refs):
    if rev:
        (q_ref, f_ref, v_ref, llb_ref, l1lb_ref, pst_ref, msk_ref, out_ref,
         st_ref, qs_ref, g_ref, kk_ref, o_ref, e_all_ref, e_last_ref) = refs
        z_ref = ob_ref = gain_ref = None
    else:
        (q_ref, f_ref, v_ref, llb_ref, l1lb_ref, pst_ref, msk_ref, z_ref, ob_ref, gain_ref, out_ref,
         st_ref, qs_ref, g_ref, kk_ref, o_ref, e_all_ref, e_last_ref) = refs
    nc = SUB_BLOCK // CHUNK
    n_lvl = len(HGRN_LEVELS)
    last_row = 0 if rev else CHUNK - 1
    heads = [slice(h * HEAD_DIM, (h + 1) * HEAD_DIM) for h in range(N_HEADS)]
    _reset_state(lay, rev, st_ref)

    def piece(base):
        blk = _at(base, 0, SUB_BLOCK)
        qs_ref[...] = (_silu(q_ref[blk, :].astype(F32)) * QK_SCALE).astype(BF16)
        a = llb_ref[...]
        b = l1lb_ref[...] + _log_sigmoid(f_ref[blk, :].astype(F32))
        g2 = (jnp.maximum(a, b) + jnp.log(1.0 + jnp.exp(-jnp.abs(a - b)))) * LOG2_E
        g_ref[...] = g2
        kk_ref[...] = (1.0 - jnp.exp2(g2)).astype(BF16)

        pst = pst_ref[...]
        for ci in range(nc):
            c = (nc - 1 - ci) if rev else ci
            g_parts = jnp.concatenate(_split2(g_ref[c * CHUNK:(c + 1) * CHUNK, :]), axis=0)
            e_f32 = jnp.exp2(jnp.dot(pst, g_parts, preferred_element_type=F32))
            e_all_ref[ci] = e_f32.astype(BF16)
            e_last_ref[ci] = e_f32[last_row:last_row + 1, :]

        chains = [(ci, (nc - 1 - ci) if rev else ci, sl) for ci in range(nc) for sl in heads]

        def operands(ci, c, sl):
            rows = slice(c * CHUNK, (c + 1) * CHUNK)
            return (qs_ref[rows, sl], kk_ref[rows, sl], v_ref[_at(base, c * CHUNK, CHUNK), sl].astype(BF16),
                    e_all_ref.at[ci])

        attns, upds = [], []
        for first in range(0, len(chains), HGRN_GROUP):
            ops = [operands(*chain) for chain in chains[first:first + HGRN_GROUP]]
            parts = [[_dot_nt(q, k) for q, k, _, _ in ops]]
            for lvl in range(n_lvl):
                es = [e_ref[(2 + lvl) * CHUNK:(3 + lvl) * CHUNK, sl]
                      for (_, _, _, e_ref), (_, _, sl) in zip(ops, chains[first:])]
                parts.append([_dot_nt(q * e, k * e) for (q, k, _, _), e in zip(ops, es)])
            upds += [_dot_tn(v, k * e_ref[CHUNK:2 * CHUNK, sl])
                     for (_, k, v, e_ref), (_, _, sl) in zip(ops, chains[first:])]
            for h in range(HGRN_GROUP):
                attn = parts[0][h] * msk_ref[0]
                for lvl in range(n_lvl):
                    attn = attn + parts[1 + lvl][h] * msk_ref[1 + lvl]
                attns.append(attn.astype(BF16))
        states = [st_ref[h] for h in range(N_HEADS)]
        entering = []
        for idx, (ci, c, sl) in enumerate(chains):
            h = idx % N_HEADS
            entering.append(states[h])
            states[h] = states[h] * e_last_ref[ci, :, sl] + upds[idx]
        for h in range(N_HEADS):
            st_ref[h] = states[h]
        ops = [operands(*chain) for chain in chains]
        inters = [_dot_nt(q * e_ref[0:CHUNK, sl], s)
                  for (q, _, _, e_ref), (_, _, sl), s in zip(ops, chains, entering)]
        intras = [jnp.dot(a, v, preferred_element_type=F32) for a, (_, _, v, _) in zip(attns, ops)]
        for (ci, c, sl), inter, intra in zip(chains, inters, intras):
            o_ref[c * CHUNK:(c + 1) * CHUNK, sl] = inter + intra
        _finish(rev, base, o_ref, ob_ref, z_ref, gain_ref, out_ref)

    _sub_blocks(lay, rev, piece)


def _mixer_c(proj, log_lb, log1m_lb, o_bwd, gain, lay, rev):
    tb, nb = lay.tb, lay.nb
    bi = _blk_index(lay, rev)

    def cur(cb):
        return pl.BlockSpec((tb, W_GROUP), lambda j: (bi(j), cb))

    vec = pl.BlockSpec((1, W_GROUP), lambda j: (0, 0))
    d = 1 if rev else 0
    pst, msk = _hgrn_consts(rev)
    n_stack = pst.shape[0]
    in_specs = [cur(CB_C_Q), cur(CB_C_FB if rev else CB_C_FF), cur(CB_C_I), vec, vec,
                pl.BlockSpec((n_stack, 2 * CHUNK), lambda j: (0, 0)),
                pl.BlockSpec((msk.shape[0], CHUNK, CHUNK), lambda j: (0, 0, 0))]
    args = [proj, proj, proj, log_lb[d].reshape(1, W_GROUP), log1m_lb[d].reshape(1, W_GROUP), pst, msk]
    if not rev:
        in_specs += [cur(CB_C_Z), pl.BlockSpec((tb, W_GROUP), lambda j: (bi(j), 0)), vec]
        args += [proj, o_bwd, gain.reshape(1, W_GROUP)]
    big = pltpu.VMEM((SUB_BLOCK, W_GROUP), F32)
    return pl.pallas_call(
        functools.partial(_hgrn_kernel, lay, rev),
        grid=(nb,),
        in_specs=in_specs,
        out_specs=pl.BlockSpec((tb, W_GROUP), lambda j: (bi(j), 0)),
        out_shape=jax.ShapeDtypeStruct((lay.n, W_GROUP), F32 if rev else BF16),
        scratch_shapes=[pltpu.VMEM((N_HEADS, HEAD_DIM, HEAD_DIM), F32),
                        pltpu.VMEM((SUB_BLOCK, W_GROUP), BF16),
                        big, pltpu.VMEM((SUB_BLOCK, W_GROUP), BF16), big,
                        pltpu.VMEM((SUB_BLOCK // CHUNK, n_stack, W_GROUP), BF16),
                        pltpu.VMEM((SUB_BLOCK // CHUNK, 1, W_GROUP), F32)],
        compiler_params=_cparams(("arbitrary",)),
        name="mixer_c_hgrn2_bwd" if rev else "mixer_c_hgrn2_fwd",
    )(*args)


def _unit_tri_inverses(a_mats, row, col):
    c = a_mats[0].shape[0]
    eye = jnp.where(row == col, 1.0, 0.0)

    def same_block(size):
        return (row // size) == (col // size)

    base = 8
    m_base = same_block(base)
    ns = [jnp.where(m_base, -a, 0.0).astype(BF16) for a in a_mats]
    n2s = [jnp.dot(n, n, preferred_element_type=F32) for n in ns]
    xs = [eye + n.astype(F32) for n in ns]
    xs = [x + _dot(x, n2) for x, n2 in zip(xs, n2s)]
    n4s = [_dot(n2, n2) for n2 in n2s]
    xs = [x + _dot(x, n4) for x, n4 in zip(xs, n4s)]
    size = base
    while size < c:
        m_off = same_block(2 * size) & jnp.logical_not(same_block(size))
        xbs = [x.astype(BF16) for x in xs]
        ts = [jnp.dot(xb, jnp.where(m_off, a, 0.0).astype(BF16), preferred_element_type=F32)
              for xb, a in zip(xbs, a_mats)]
        xs = [x - jnp.dot(t.astype(BF16), xb, preferred_element_type=F32) for x, t, xb in zip(xs, ts, xbs)]
        size *= 2
    return xs


SHIFT_ROWS = 128


def _shift_mats():
    half = DN_CONV_WIDTH // 2
    t = np.arange(SHIFT_ROWS)[:, None]
    s = np.arange(SHIFT_ROWS + 2 * HALO_D)[None, :]
    mats = [(s == HALO_D + t + d) for d in range(-half, half + 1) if d != 0]
    return jnp.asarray(np.concatenate(mats, axis=0).astype(np.float32), dtype=BF16)


def _delta_pre_kernel(lay, q_ref, k_ref, v_ref, qp_ref, kp_ref, vp_ref, qn_ref, kn_ref, vn_ref, sm_ref,
                      cw_ref, alog_ref, dtb_ref, sh_ref, qo_ref, ko_ref, vo_ref, cum_ref, beta_ref, cumt_ref,
                      xq_ref, xk_ref, xv_ref):
    tb = lay.tb
    blk = pl.program_id(0)
    keep_prev = jnp.where(lay.seq_start(blk), 0.0, 1.0)
    keep_next = jnp.where(lay.seq_end(blk), 0.0, 1.0)
    half = DN_CONV_WIDTH // 2
    parts = ((q_ref, qp_ref, qn_ref, qo_ref, xq_ref), (k_ref, kp_ref, kn_ref, ko_ref, xk_ref),
             (v_ref, vp_ref, vn_ref, vo_ref, xv_ref))
    for c_ref, p_ref, n_ref, _, x_ref in parts:
        x_ref[0:HALO_D, :] = (p_ref[...].astype(F32) * keep_prev).astype(BF16)
        x_ref[HALO_D:HALO_D + tb, :] = c_ref[...]
        x_ref[HALO_D + tb:2 * HALO_D + tb, :] = (n_ref[...].astype(F32) * keep_next).astype(BF16)
    tiles = [(part, r0) for part in range(len(parts)) for r0 in range(0, tb, SHIFT_ROWS)]
    shifted = [jnp.dot(sh_ref[...], parts[part][4][r0:r0 + SHIFT_ROWS + 2 * HALO_D, :], preferred_element_type=F32)
               for part, r0 in tiles]
    for (part, r0), sh in zip(tiles, shifted):
        x_ref, dst_ref = parts[part][4], parts[part][3]
        wsl = slice(part * W_GROUP, (part + 1) * W_GROUP)
        acc = cw_ref[half:half + 1, wsl] * x_ref[HALO_D + r0:HALO_D + r0 + SHIFT_ROWS, :].astype(F32)
        for j, k in enumerate([k for k in range(DN_CONV_WIDTH) if k != half]):
            acc = acc + cw_ref[k:k + 1, wsl] * sh[j * SHIFT_ROWS:(j + 1) * SHIFT_ROWS, :]
        y = _silu(acc)
        rows = slice(r0, r0 + SHIFT_ROWS)
        if part < 2:
            for h in range(N_HEADS):
                sl = slice(h * HEAD_DIM, (h + 1) * HEAD_DIM)
                yh = y[:, sl]
                yn = yh * lax.rsqrt(jnp.sum(yh * yh, axis=-1, keepdims=True) + NORM_EPS)
                dst_ref[rows, sl] = (yn * QK_SCALE if part == 0 else yn).astype(BF16)
        else:
            dst_ref[rows, :] = y.astype(BF16)

    sm = sm_ref[...]
    g = -jnp.exp(alog_ref[...]) * _softplus(sm + dtb_ref[...])
    beta_ref[...] = _sigmoid(sm)

    row = lax.broadcasted_iota(jnp.int32, (CHUNK, CHUNK), 0)
    col = lax.broadcasted_iota(jnp.int32, (CHUNK, CHUNK), 1)
    lower = jnp.where(row >= col, 1.0, 0.0)
    upper = jnp.where(row <= col, 1.0, 0.0)
    fwd_lane = lax.broadcasted_iota(jnp.int32, (CHUNK, SMALL_W), 1) < N_HEADS
    eye_rows = jnp.where(lax.broadcasted_iota(jnp.int32, (SUBLANES, SMALL_W), 0)
                         == lax.broadcasted_iota(jnp.int32, (SUBLANES, SMALL_W), 1), 1.0, 0.0).astype(BF16)
    for c in range(tb // CHUNK):
        gc = g[c * CHUNK:(c + 1) * CHUNK, :]
        cum = jnp.where(fwd_lane, _dot_exact_lhs(lower, gc), _dot_exact_lhs(upper, gc))
        cum_ref[c * CHUNK:(c + 1) * CHUNK, :] = cum
        cum_t = None
        for piece in _split3(cum):
            t = lax.dot_general(eye_rows, piece, (((1,), (1,)), ((), ())), preferred_element_type=F32)
            cum_t = t if cum_t is None else cum_t + t
        cumt_ref[c] = cum_t


def _delta_kernel(lay, rev, *refs):
    if rev:
        (qc_ref, kc_ref, vc_ref, cum_ref, beta_ref, cumt_ref, out_ref,
         s_ref, o_ref, qd_ref, kd_ref, qk_ref, rhs_ref, last_ref, gq_ref, bm_ref) = refs
        z_ref = ob_ref = gain_ref = None
    else:
        (qc_ref, kc_ref, vc_ref, cum_ref, beta_ref, cumt_ref, z_ref, ob_ref, gain_ref, out_ref,
         s_ref, o_ref, qd_ref, kd_ref, qk_ref, rhs_ref, last_ref, gq_ref, bm_ref) = refs
    _reset_state(lay, rev, s_ref)
    named = (qc_ref, kc_ref, vc_ref, cum_ref, beta_ref, cumt_ref, z_ref, ob_ref, gain_ref, out_ref,
             s_ref, o_ref, qd_ref, kd_ref, qk_ref, rhs_ref, last_ref, gq_ref, bm_ref)
    _sub_blocks(lay, rev, functools.partial(_delta_piece, rev, named), DELTA_PIECE)


def _delta_piece(rev, named, base):
    (qc_ref, kc_ref, vc_ref, cum_ref, beta_ref, cumt_ref, z_ref, ob_ref, gain_ref, out_ref,
     s_ref, o_ref, qd_ref, kd_ref, qk_ref, rhs_ref, last_ref, gq_ref, bm_ref) = named
    nc = DELTA_PIECE // CHUNK
    row, col, incl, strict = _causal_masks(rev, CHUNK)
    lane0 = N_HEADS if rev else 0

    cums = [cum_ref[_at(base, c * CHUNK, CHUNK), :] for c in range(nc)]
    cum_ts = [cumt_ref[base // CHUNK + c] for c in range(nc)]
    raws = []
    for c in range(nc):
        rows = slice(c * CHUNK, (c + 1) * CHUNK)
        src = _at(base, c * CHUNK, CHUNK)
        beta_all = beta_ref[src, :]
        for h in range(N_HEADS):
            sl = slice(h * HEAD_DIM, (h + 1) * HEAD_DIM)
            qc = qc_ref[src, sl]
            kc = kc_ref[src, sl]
            vc = vc_ref[src, sl]
            ci_b = jnp.broadcast_to(cums[c][:, lane0 + h:lane0 + h + 1], (CHUNK, HEAD_DIM))
            b_lane = 2 * N_HEADS + lane0 + h
            beta_b = jnp.broadcast_to(beta_all[:, b_lane:b_lane + 1], (CHUNK, HEAD_DIM))
            k_beta = kc * beta_b
            raws.append(_dot_nt(jnp.concatenate([k_beta.astype(BF16), qc], axis=0), kc))
            e_ci = jnp.exp(ci_b)
            rhs_ref[c * N_HEADS + h] = jnp.concatenate([vc * beta_b, k_beta * e_ci], axis=1).astype(BF16)
            qd_ref[rows, sl] = qc * e_ci
            c_last = ci_b[0:1, :] if rev else ci_b[CHUNK - 1:CHUNK, :]
            kd_ref[rows, sl] = (kc * jnp.exp(c_last - ci_b)).astype(BF16)
            last_ref[c * N_HEADS + h] = jnp.broadcast_to(jnp.exp(c_last), (8, HEAD_DIM))
    a_mats = []
    for idx, raw in enumerate(raws):
        c, h = divmod(idx, N_HEADS)
        ci_col = cums[c][:, lane0 + h:lane0 + h + 1]
        cj_row = cum_ts[c][lane0 + h:lane0 + h + 1, :]
        decay = jnp.exp(jnp.where(incl, ci_col - cj_row, NEG_BIG))
        a_mats.append(jnp.where(strict, raw[0:CHUNK] * decay, 0.0))
        qk_ref[idx] = (raw[CHUNK:2 * CHUNK] * decay).astype(BF16)

    t_invs = _unit_tri_inverses(a_mats, row, col)
    where = [(slice((idx // N_HEADS) * CHUNK, (idx // N_HEADS + 1) * CHUNK),
              slice((idx % N_HEADS) * HEAD_DIM, (idx % N_HEADS + 1) * HEAD_DIM)) for idx in range(len(t_invs))]
    uws = [jnp.dot(t_inv.astype(BF16), rhs_ref[idx], preferred_element_type=F32).astype(BF16)
           for idx, t_inv in enumerate(t_invs)]
    kuws = [lax.dot_general(kd_ref[rows, sl], uw, (((0,), (0,)), ((), ())), preferred_element_type=F32)
            for (rows, sl), uw in zip(where, uws)]
    quws = [jnp.dot(qk_ref[idx], uw, preferred_element_type=F32) for idx, uw in enumerate(uws)]
    for idx, ((rows, sl), kuw, quw) in enumerate(zip(where, kuws, quws)):
        bm_ref[idx] = kuw[:, 0:HEAD_DIM]
        gq_ref[idx, 0:HEAD_DIM, :] = kuw[:, HEAD_DIM:2 * HEAD_DIM].astype(BF16)
        gq_ref[idx, HEAD_DIM:HEAD_DIM + CHUNK, :] = (qd_ref[rows, sl] - quw[:, HEAD_DIM:2 * HEAD_DIM]).astype(BF16)
        o_ref[rows, sl] = quw[:, 0:HEAD_DIM]

    states = [s_ref[h] for h in range(N_HEADS)]
    for ci in range(nc):
        c = (nc - 1 - ci) if rev else ci
        rows = slice(c * CHUNK, (c + 1) * CHUNK)
        for h in range(N_HEADS):
            idx = c * N_HEADS + h
            sl = slice(h * HEAD_DIM, (h + 1) * HEAD_DIM)
            gs = jnp.dot(gq_ref[idx], states[h].astype(BF16), preferred_element_type=F32)
            o_ref[rows, sl] += gs[HEAD_DIM:HEAD_DIM + CHUNK]
            states[h] = states[h] * last_ref[idx][0:1, :] - gs[0:HEAD_DIM] + bm_ref[idx]
    for h in range(N_HEADS):
        s_ref[h] = states[h]
    _finish(rev, base, o_ref, ob_ref, z_ref, gain_ref, out_ref, DELTA_PIECE)


def _mixer_d_pre(proj, proj_small, conv_w, a_log, dt_bias, lay):
    tb, nb = lay.tb, lay.nb
    hb = tb // HALO_D
    n_halo = lay.n // HALO_D

    def cur(cb):
        return pl.BlockSpec((tb, W_GROUP), lambda i: (i, cb))

    def prev(cb):
        return pl.BlockSpec((HALO_D, W_GROUP), lambda i: (jnp.maximum(i * hb - 1, 0), cb))

    def nxt(cb):
        return pl.BlockSpec((HALO_D, W_GROUP), lambda i: (jnp.minimum((i + 1) * hb, n_halo - 1), cb))

    small = pl.BlockSpec((1, SMALL_W), lambda i: (0, 0))
    pad_lanes = SMALL_W - 2 * N_HEADS
    alog_row = jnp.pad(a_log.reshape(1, 2 * N_HEADS), ((0, 0), (0, pad_lanes)))
    dtb_row = jnp.pad(dt_bias.reshape(1, 2 * N_HEADS), ((0, 0), (0, pad_lanes)))
    wide = pl.BlockSpec((tb, W_GROUP), lambda i: (i, 0))
    narrow = pl.BlockSpec((tb, SMALL_W), lambda i: (i, 0))
    shifts = _shift_mats()
    return pl.pallas_call(
        functools.partial(_delta_pre_kernel, lay),
        grid=(nb,),
        in_specs=[cur(CB_D_Q), cur(CB_D_K), cur(CB_D_V), prev(CB_D_Q), prev(CB_D_K), prev(CB_D_V),
                  nxt(CB_D_Q), nxt(CB_D_K), nxt(CB_D_V),
                  pl.BlockSpec((tb, SMALL_W), lambda i: (i, 0)),
                  pl.BlockSpec((DN_CONV_WIDTH, 3 * W_GROUP), lambda i: (0, 0)), small, small,
                  pl.BlockSpec(shifts.shape, lambda i: (0, 0))],
        out_specs=[wide, wide, wide, narrow, narrow,
                   pl.BlockSpec((tb // CHUNK, SUBLANES, CHUNK), lambda i: (i, 0, 0))],
        out_shape=[jax.ShapeDtypeStruct((lay.n, W_GROUP), BF16)] * 3
        + [jax.ShapeDtypeStruct((lay.n, SMALL_W), F32)] * 2
        + [jax.ShapeDtypeStruct((lay.n // CHUNK, SUBLANES, CHUNK), F32)],
        scratch_shapes=[pltpu.VMEM((tb + 2 * HALO_D, W_GROUP), BF16)] * 3,
        compiler_params=_cparams(("arbitrary",)),
        name="mixer_d_deltanet_pre",
    )(*([proj] * 9), proj_small, conv_w, alog_row, dtb_row, shifts)


def _mixer_d(pre, proj, o_bwd, gain, lay, rev):
    tb, nb = lay.tb, lay.nb
    bi = _blk_index(lay, rev)
    wide = pl.BlockSpec((tb, W_GROUP), lambda j: (bi(j), 0))
    narrow = pl.BlockSpec((tb, SMALL_W), lambda j: (bi(j), 0))
    in_specs = [wide, wide, wide, narrow, narrow,
                pl.BlockSpec((tb // CHUNK, SUBLANES, CHUNK), lambda j: (bi(j), 0, 0))]
    args = list(pre)
    if not rev:
        in_specs += [pl.BlockSpec((tb, W_GROUP), lambda j: (bi(j), CB_D_Z)), wide,
                     pl.BlockSpec((1, W_GROUP), lambda j: (0, 0))]
        args += [proj, o_bwd, gain.reshape(1, W_GROUP)]
    big = pltpu.VMEM((DELTA_PIECE, W_GROUP), F32)
    nc = DELTA_PIECE // CHUNK
    return pl.pallas_call(
        functools.partial(_delta_kernel, lay, rev),
        grid=(nb,),
        in_specs=in_specs,
        out_specs=wide,
        out_shape=jax.ShapeDtypeStruct((lay.n, W_GROUP), F32 if rev else BF16),
        scratch_shapes=[pltpu.VMEM((N_HEADS, HEAD_DIM, HEAD_DIM), F32),
                        big,
                        big,
                        pltpu.VMEM((DELTA_PIECE, W_GROUP), BF16),
                        pltpu.VMEM((nc * N_HEADS, CHUNK, CHUNK), BF16),
                        pltpu.VMEM((nc * N_HEADS, CHUNK, 2 * HEAD_DIM), BF16),
                        pltpu.VMEM((nc * N_HEADS, 8, HEAD_DIM), F32),
                        pltpu.VMEM((nc * N_HEADS, HEAD_DIM + CHUNK, HEAD_DIM), BF16),
                        pltpu.VMEM((nc * N_HEADS, HEAD_DIM, HEAD_DIM), F32)],
        compiler_params=_cparams(("arbitrary",)),
        name="mixer_d_deltanet_bwd" if rev else "mixer_d_deltanet_fwd",
    )(*args)


def _regroup_w_in(w):
    d_qkv_end = 15 * W_GROUP
    small = w[:, d_qkv_end:d_qkv_end + 4 * N_HEADS]
    d_z = w[:, d_qkv_end + 4 * N_HEADS:]
    pad = jnp.zeros((w.shape[0], SMALL_W - 4 * N_HEADS), w.dtype)
    main = jnp.concatenate([w[:, :d_qkv_end], d_z], axis=1).astype(BF16)
    return main, jnp.concatenate([small, pad], axis=1).astype(BF16)


def _rotary_tables(t_max):
    half = HEAD_DIM // 2
    inv = 1.0 / (ROPE_BASE ** (jnp.arange(half, dtype=F32) / half))
    ang = jnp.arange(t_max, dtype=F32)[:, None] * inv[None, :]
    cos, sin = jnp.cos(ang), jnp.sin(ang)
    return jnp.concatenate([cos, cos], axis=-1), jnp.concatenate([-sin, sin], axis=-1)


def _pick_tile(t_prompt, t_sample, want):
    tile = want
    while t_prompt % tile or t_sample % tile:
        tile //= 2
    return tile


def kernel(x_prompt, x_sample, c_prompt, c_sample, ada_w, ada_b, norm_g, w_in, conv_a_w, conv_a_b, ln_a_g, ln_a_b,
           ret_norm_g, hgrn_lb_logits, hgrn_norm_g, dn_conv_w, dn_a_log, dn_dt_bias, dn_norm_g, w_out, final_g):
    depth = w_in.shape[0]
    n_prompt, t_prompt, _ = x_prompt.shape
    n_sample, t_sample, _ = x_sample.shape
    tm_in = _pick_tile(t_prompt, t_sample, 1024)
    tm_out = _pick_tile(t_prompt, t_sample, 512)
    lay = _Layout(t_prompt, n_prompt, t_sample, n_sample, _pick_tile(t_prompt, t_sample, 1024))
    lay_a = _Layout(t_prompt, n_prompt, t_sample, n_sample, _pick_tile(t_prompt, t_sample, 512))
    lay_dp = _Layout(t_prompt, n_prompt, t_sample, n_sample, _pick_tile(t_prompt, t_sample, DELTA_PIECE))

    xs = (x_prompt.reshape(-1, D_MODEL), x_sample.reshape(-1, D_MODEL))
    c_all = jnp.concatenate([c_prompt, c_sample], axis=0)
    n_c = c_all.shape[0]
    c_all = jnp.pad(c_all, ((0, (-n_c) % 8), (0, 0)))
    mod = _modulation(c_all, ada_w, ada_b)

    lb = jnp.cumsum(jax.nn.softmax(hgrn_lb_logits.astype(F32), axis=0), axis=0)
    lb = lb - lb[:1]
    log_lb = jnp.log(lb)
    log1m_lb = jnp.log1p(-lb)
    cos2, sin2 = _rotary_tables(max(t_prompt, t_sample))

    for l in range(depth):
        mod_l = mod[l].reshape(mod.shape[1], 1, 3 * D_MODEL)
        proj, proj_small = _in_proj(xs, mod_l, norm_g[l], *_regroup_w_in(w_in[l]), lay, tm_in)
        m_a = _mixer_a(proj, conv_a_w[l], conv_a_b[l], ln_a_g[l], ln_a_b[l], lay_a)
        ob = _mixer_b(proj, cos2, sin2, None, None, lay, True)
        m_b = _mixer_b(proj, cos2, sin2, ob, ret_norm_g[l], lay, False)
        oc = _mixer_c(proj, log_lb[l], log1m_lb[l], None, None, lay, True)
        m_c = _mixer_c(proj, log_lb[l], log1m_lb[l], oc, hgrn_norm_g[l], lay, False)
        d_pre = _mixer_d_pre(proj, proj_small, dn_conv_w[l], dn_a_log[l], dn_dt_bias[l], lay_dp)
        od = _mixer_d(d_pre, proj, None, None, lay, True)
        m_d = _mixer_d(d_pre, proj, od, dn_norm_g[l], lay, False)
        xs = _out_proj(xs, (m_a, m_b, m_c, m_d), mod_l, w_out[l].astype(BF16), final_g, lay, tm_out,
                       final=(l == depth - 1))

    return (xs[0].reshape(n_prompt, t_prompt, D_MODEL), xs[1].reshape(n_sample, t_sample, D_MODEL))
```

```python
import functools
import math

import numpy as np
import jax
import jax.numpy as jnp
from jax import lax
from jax.experimental import pallas as pl
from jax.experimental.pallas import tpu as pltpu

F32 = jnp.float32
BF16 = jnp.bfloat16

D_MODEL = 1024
W_GROUP = 512
HEAD_DIM = 128
SUBLANES = 8
N_HEADS = 4
CONV_A_WIDTH = 31
DN_CONV_WIDTH = 5
CHUNK = 64
ROPE_BASE = 10000.0
RET_DECAY_OFFSET = 5.0
NORM_EPS = 1e-6
QK_SCALE = HEAD_DIM ** -0.5
NEG_BIG = -1e30

N_COL_BLOCKS = 16
SMALL_W = 128
SUB_BLOCK = 256
DELTA_PIECE = 512
D_PROJ = N_COL_BLOCKS * W_GROUP
PROJ_TN = 2048
HALO_A = 16
HALO_D = 16
VMEM_LIMIT = 56 * 1024 * 1024

CB_A_VAL, CB_A_GLU, CB_A_Z = 0, 1, 2
CB_B_Q, CB_B_K, CB_B_V, CB_B_Z = 3, 4, 5, 6
CB_C_Q, CB_C_FF, CB_C_FB, CB_C_I, CB_C_Z = 7, 8, 9, 10, 11
CB_D_Q, CB_D_K, CB_D_V, CB_D_Z = 12, 13, 14, 15


def _dot(a, b):
    return jnp.dot(a.astype(BF16), b.astype(BF16), preferred_element_type=F32)


def _dot_nt(a, b):
    return lax.dot_general(a.astype(BF16), b.astype(BF16), (((1,), (1,)), ((), ())), preferred_element_type=F32)


def _dot_tn(a, b):
    return lax.dot_general(a.astype(BF16), b.astype(BF16), (((0,), (0,)), ((), ())), preferred_element_type=F32)


def _split2(x):
    hi = x.astype(BF16)
    lo = (x - hi.astype(F32)).astype(BF16)
    return hi, lo


def _split3(x):
    hi = x.astype(BF16)
    r = x - hi.astype(F32)
    mid = r.astype(BF16)
    lo = (r - mid.astype(F32)).astype(BF16)
    return hi, mid, lo


def _dot_exact_lhs(m, x):
    mb = m.astype(BF16)
    acc = None
    for part in _split3(x):
        t = jnp.dot(mb, part, preferred_element_type=F32)
        acc = t if acc is None else acc + t
    return acc


def _sigmoid(x):
    return 0.5 * jnp.tanh(0.5 * x) + 0.5


def _silu(x):
    return x * _sigmoid(x)


def _softplus(x):
    return jnp.maximum(x, 0.0) + jnp.log(1.0 + jnp.exp(-jnp.abs(x)))


def _log_sigmoid(x):
    return jnp.minimum(x, 0.0) - jnp.log(1.0 + jnp.exp(-jnp.abs(x)))


class _Layout:
    def __init__(self, t_prompt, n_prompt, t_sample, n_sample, tb):
        self.tp = t_prompt * n_prompt
        self.t_prompt = t_prompt
        self.ts = t_sample
        self.n = self.tp + t_sample * n_sample
        self.tb = tb
        assert t_prompt % tb == 0 and t_sample % tb == 0
        self.nb = self.n // tb

    def seq_start(self, blk):
        s = blk * self.tb
        return jnp.where(s < self.tp, s % self.t_prompt == 0, (s - self.tp) % self.ts == 0)

    def seq_end(self, blk):
        e = (blk + 1) * self.tb
        return jnp.where(e <= self.tp, e % self.t_prompt == 0, (e - self.tp) % self.ts == 0)

    def pos_block(self, blk):
        s = blk * self.tb
        return jnp.where(s < self.tp, (s % self.t_prompt) // self.tb, ((s - self.tp) % self.ts) // self.tb)

    def batch_index(self, blk, rows):
        s = blk * rows
        return jnp.where(s < self.tp, s // self.t_prompt, self.tp // self.t_prompt + (s - self.tp) // self.ts)


def _cparams(sem):
    return pltpu.CompilerParams(dimension_semantics=sem, vmem_limit_bytes=VMEM_LIMIT)


def _mod_kernel(c_ref, w_ref, b_ref, o_ref):
    c = c_ref[...]
    o_ref[0] = jnp.dot(_silu(c), w_ref[0], preferred_element_type=F32,
                       precision=lax.Precision.HIGHEST) + b_ref[0]


def _modulation(c_all, ada_w, ada_b):
    depth = ada_w.shape[0]
    nb = c_all.shape[0]
    return pl.pallas_call(
        _mod_kernel,
        grid=(depth,),
        in_specs=[pl.BlockSpec((nb, D_MODEL), lambda l: (0, 0)),
                  pl.BlockSpec((1, D_MODEL, 3 * D_MODEL), lambda l: (l, 0, 0)),
                  pl.BlockSpec((1, 1, 3 * D_MODEL), lambda l: (l, 0, 0))],
        out_specs=pl.BlockSpec((1, nb, 3 * D_MODEL), lambda l: (l, 0, 0)),
        out_shape=jax.ShapeDtypeStruct((depth, nb, 3 * D_MODEL), F32),
        compiler_params=_cparams(("arbitrary",)),
        name="adaln_modulation",
    )(c_all, ada_w, ada_b.reshape(depth, 1, 3 * D_MODEL))


def _load_x(x_refs, i, npb):
    if len(x_refs) == 1:
        return x_refs[0][...]
    return jnp.where(i < npb, x_refs[0][...], x_refs[1][...])


def _x_specs(n_x, tm, npb, row_block):
    def spec(fn):
        return pl.BlockSpec((tm, D_MODEL), lambda *ids: (fn(row_block(*ids)), 0))

    if n_x == 1:
        return [spec(lambda i: i)]
    return [spec(lambda i: jnp.minimum(i, npb - 1)), spec(lambda i: jnp.maximum(i - npb, 0))]


def _in_proj_kernel(n_x, npb, *refs):
    x_refs = refs[:n_x]
    mod_ref, g_ref, w_ref, ws_ref, o_ref, os_ref, h_ref = refs[n_x:]

    @pl.when(pl.program_id(1) == 0)
    def _():
        x = _load_x(x_refs, pl.program_id(0), npb)
        y = x * lax.rsqrt(jnp.mean(x * x, axis=-1, keepdims=True) + NORM_EPS) * g_ref[...]
        shift = mod_ref[0, :, 0:D_MODEL]
        scale = mod_ref[0, :, D_MODEL:2 * D_MODEL]
        h = (y * (1.0 + scale) + shift).astype(BF16)
        h_ref[...] = h
        os_ref[...] = jnp.dot(h, ws_ref[...], preferred_element_type=F32)

    o_ref[...] = jnp.dot(h_ref[...], w_ref[...], preferred_element_type=F32).astype(BF16)


def _in_proj(xs, mod, norm_g, w_main, w_small, lay, tm):
    n = lay.n
    npb = lay.tp // tm
    lay_m = _Layout(lay.t_prompt, lay.tp // lay.t_prompt, lay.ts, (lay.n - lay.tp) // lay.ts, tm)
    return pl.pallas_call(
        functools.partial(_in_proj_kernel, len(xs), npb),
        grid=(n // tm, D_PROJ // PROJ_TN),
        in_specs=_x_specs(len(xs), tm, npb, lambda i, j: i) + [
                  pl.BlockSpec((1, 1, 3 * D_MODEL), lambda i, j: (lay_m.batch_index(i, tm), 0, 0)),
                  pl.BlockSpec((1, D_MODEL), lambda i, j: (0, 0)),
                  pl.BlockSpec((D_MODEL, PROJ_TN), lambda i, j: (0, j)),
                  pl.BlockSpec((D_MODEL, SMALL_W), lambda i, j: (0, 0))],
        out_specs=[pl.BlockSpec((tm, PROJ_TN), lambda i, j: (i, j)),
                   pl.BlockSpec((tm, SMALL_W), lambda i, j: (i, 0))],
        out_shape=[jax.ShapeDtypeStruct((n, D_PROJ), BF16), jax.ShapeDtypeStruct((n, SMALL_W), F32)],
        scratch_shapes=[pltpu.VMEM((tm, D_MODEL), BF16)],
        compiler_params=_cparams(("arbitrary", "arbitrary")),
        name="in_proj",
    )(*xs, mod, norm_g.reshape(1, D_MODEL), w_main, w_small)


def _out_proj_kernel(final, n_x, npb, *refs):
    x_refs = refs[:n_x]
    ma_ref, mb_ref, mc_ref, md_ref, mod_ref, w_ref, fg_ref = refs[n_x:n_x + 7]
    o_refs = refs[n_x + 7:]
    i = pl.program_id(0)
    acc = jnp.dot(ma_ref[...], w_ref[0:W_GROUP, :], preferred_element_type=F32)
    acc += jnp.dot(mb_ref[...], w_ref[W_GROUP:2 * W_GROUP, :], preferred_element_type=F32)
    acc += jnp.dot(mc_ref[...], w_ref[2 * W_GROUP:3 * W_GROUP, :], preferred_element_type=F32)
    acc += jnp.dot(md_ref[...], w_ref[3 * W_GROUP:4 * W_GROUP, :], preferred_element_type=F32)
    gate = mod_ref[0, :, 2 * D_MODEL:3 * D_MODEL]
    y = _load_x(x_refs, i, npb) + gate * acc
    if not final:
        o_refs[0][...] = y
        return
    y = y * lax.rsqrt(jnp.mean(y * y, axis=-1, keepdims=True) + NORM_EPS) * fg_ref[...]

    @pl.when(i < npb)
    def _():
        o_refs[0][...] = y

    @pl.when(i >= npb)
    def _():
        o_refs[1][...] = y


def _out_proj(xs, mixed, mod, w_out_bf, final_g, lay, tm, final):
    n = lay.n
    npb = lay.tp // tm
    lay_m = _Layout(lay.t_prompt, lay.tp // lay.t_prompt, lay.ts, (lay.n - lay.tp) // lay.ts, tm)
    mspec = pl.BlockSpec((tm, W_GROUP), lambda i: (i, 0))
    if final:
        out_specs = [pl.BlockSpec((tm, D_MODEL), lambda i: (jnp.minimum(i, npb - 1), 0)),
                     pl.BlockSpec((tm, D_MODEL), lambda i: (jnp.maximum(i - npb, 0), 0))]
        out_shape = [jax.ShapeDtypeStruct((lay.tp, D_MODEL), F32), jax.ShapeDtypeStruct((n - lay.tp, D_MODEL), F32)]
    else:
        out_specs = [pl.BlockSpec((tm, D_MODEL), lambda i: (i, 0))]
        out_shape = [jax.ShapeDtypeStruct((n, D_MODEL), F32)]
    out = pl.pallas_call(
        functools.partial(_out_proj_kernel, final, len(xs), npb),
        grid=(n // tm,),
        in_specs=_x_specs(len(xs), tm, npb, lambda i: i) + [mspec, mspec, mspec, mspec,
                  pl.BlockSpec((1, 1, 3 * D_MODEL), lambda i: (lay_m.batch_index(i, tm), 0, 0)),
                  pl.BlockSpec((4 * W_GROUP, D_MODEL), lambda i: (0, 0)),
                  pl.BlockSpec((1, D_MODEL), lambda i: (0, 0))],
        out_specs=out_specs,
        out_shape=out_shape,
        compiler_params=_cparams(("arbitrary",)),
        name="out_proj",
    )(*xs, *mixed, mod, w_out_bf, final_g.reshape(1, D_MODEL))
    return tuple(out)


A_ROWS = 32


def _mixer_a_kernel(lay, val_ref, glu_ref, z_ref, vp_ref, gp_ref, vn_ref, gn_ref,
                    cw_ref, cb_ref, lg_ref, lb_ref, o_ref, u_ref, ush_ref, acc_ref):
    tb = lay.tb
    blk = pl.program_id(0)
    keep_prev = jnp.where(lay.seq_start(blk), 0.0, 1.0)
    keep_next = jnp.where(lay.seq_end(blk), 0.0, 1.0)
    u_ref[0:HALO_A, :] = vp_ref[...].astype(F32) * _sigmoid(gp_ref[...].astype(F32)) * keep_prev
    u_ref[HALO_A:HALO_A + tb, :] = val_ref[...].astype(F32) * _sigmoid(glu_ref[...].astype(F32))
    u_ref[HALO_A + tb:2 * HALO_A + tb, :] = vn_ref[...].astype(F32) * _sigmoid(gn_ref[...].astype(F32)) * keep_next
    half = CONV_A_WIDTH // 2
    n_sh = tb + 2 * HALO_A - SUBLANES
    for r in range(1, SUBLANES):
        ush_ref[r - 1] = u_ref[r:r + n_sh, :]

    def tile(t, carry):
        r0 = pl.multiple_of(t * A_ROWS, A_ROWS)
        acc = jnp.zeros((A_ROWS // SUBLANES, SUBLANES, W_GROUP), F32) + cb_ref[...]
        for k in range(CONV_A_WIDTH):
            a, r = divmod(HALO_A - half + k, SUBLANES)
            win = pl.ds(r0 + a * SUBLANES, A_ROWS)
            src = u_ref[win, :] if r == 0 else ush_ref[r - 1, win, :]
            acc = acc + cw_ref[k] * src.reshape(A_ROWS // SUBLANES, SUBLANES, W_GROUP)
        acc_ref[pl.ds(r0, A_ROWS), :] = acc.reshape(A_ROWS, W_GROUP)
        return carry

    lax.fori_loop(0, tb // A_ROWS, tile, 0)
    acc = acc_ref[...]
    mu = jnp.mean(acc, axis=-1, keepdims=True)
    xc = acc - mu
    var = jnp.mean(xc * xc, axis=-1, keepdims=True)
    y = xc * lax.rsqrt(var + NORM_EPS) * lg_ref[...] + lb_ref[...]
    o_ref[...] = (_silu(y) * _silu(z_ref[...].astype(F32))).astype(BF16)


def _mixer_a(proj, conv_w, conv_b, ln_g, ln_b, lay):
    tb, nb = lay.tb, lay.nb
    hb = tb // HALO_A
    n_halo = lay.n // HALO_A

    def cur(cb):
        return pl.BlockSpec((tb, W_GROUP), lambda i: (i, cb))

    def prev(cb):
        return pl.BlockSpec((HALO_A, W_GROUP), lambda i: (jnp.maximum(i * hb - 1, 0), cb))

    def nxt(cb):
        return pl.BlockSpec((HALO_A, W_GROUP), lambda i: (jnp.minimum((i + 1) * hb, n_halo - 1), cb))

    vec = pl.BlockSpec((1, W_GROUP), lambda i: (0, 0))
    return pl.pallas_call(
        functools.partial(_mixer_a_kernel, lay),
        grid=(nb,),
        in_specs=[cur(CB_A_VAL), cur(CB_A_GLU), cur(CB_A_Z), prev(CB_A_VAL), prev(CB_A_GLU),
                  nxt(CB_A_VAL), nxt(CB_A_GLU),
                  pl.BlockSpec((CONV_A_WIDTH, SUBLANES, W_GROUP), lambda i: (0, 0, 0)), vec, vec, vec],
        out_specs=pl.BlockSpec((tb, W_GROUP), lambda i: (i, 0)),
        out_shape=jax.ShapeDtypeStruct((lay.n, W_GROUP), BF16),
        scratch_shapes=[pltpu.VMEM((tb + 2 * HALO_A, W_GROUP), F32),
                        pltpu.VMEM((SUBLANES - 1, tb + 2 * HALO_A - SUBLANES, W_GROUP), F32),
                        pltpu.VMEM((tb, W_GROUP), F32)],
        compiler_params=_cparams(("arbitrary",)),
        name="mixer_a_conv",
    )(proj, proj, proj, proj, proj, proj, proj,
      jnp.broadcast_to(conv_w[:, None, :], (CONV_A_WIDTH, SUBLANES, W_GROUP)), conv_b.reshape(1, W_GROUP),
      ln_g.reshape(1, W_GROUP), ln_b.reshape(1, W_GROUP))


def _blk_index(lay, rev):
    nb = lay.nb
    return (lambda j: nb - 1 - j) if rev else (lambda j: j)


def _reset_state(lay, rev, s_ref):
    blk = _blk_index(lay, rev)(pl.program_id(0))
    boundary = lay.seq_end(blk) if rev else lay.seq_start(blk)

    @pl.when(boundary)
    def _():
        s_ref[...] = jnp.zeros(s_ref.shape, s_ref.dtype)


def _sub_blocks(lay, rev, body, n_rows=SUB_BLOCK):
    nsb = lay.tb // n_rows

    def trip(i, carry):
        piece = (nsb - 1 - i) if rev else i
        body(pl.multiple_of(piece * n_rows, n_rows))
        return carry

    lax.fori_loop(0, nsb, trip, 0)


def _at(base, off, n):
    return pl.ds(pl.multiple_of(base + off, CHUNK), n)


def _finish(rev, base, o_ref, ob_ref, z_ref, gain_ref, out_ref, n_rows=SUB_BLOCK):
    rows = _at(base, 0, n_rows)
    if rev:
        out_ref[rows, :] = o_ref[...]
        return
    for h in range(N_HEADS):
        sl = slice(h * HEAD_DIM, (h + 1) * HEAD_DIM)
        o = o_ref[:, sl] + ob_ref[rows, sl]
        y = o * lax.rsqrt(jnp.mean(o * o, axis=-1, keepdims=True) + NORM_EPS) * gain_ref[:, sl]
        out_ref[rows, sl] = (y * _silu(z_ref[rows, sl].astype(F32))).astype(BF16)


def _causal_masks(rev, c):
    row = lax.broadcasted_iota(jnp.int32, (c, c), 0)
    col = lax.broadcasted_iota(jnp.int32, (c, c), 1)
    incl = (row <= col) if rev else (row >= col)
    strict = (row < col) if rev else (row > col)
    return row, col, incl, strict


def _ret_log_gamma(rev):
    lg = np.log1p(-np.exp2(-RET_DECAY_OFFSET - np.arange(N_HEADS, dtype=np.float64)))
    return lg[::-1].copy() if rev else lg


RET_CHUNK = 256
RET_PIECE = 512


def _ret_consts(rev):
    lg = _ret_log_gamma(rev).astype(np.float32)
    i = np.arange(RET_CHUNK, dtype=np.float32)
    idx = (RET_CHUNK - 1 - i) if rev else i
    rel = idx[:, None] - idx[None, :]
    dm = np.where(rel >= 0, np.exp(lg[:, None, None] * np.maximum(rel, 0.0)), 0.0).astype(np.float32)
    qd = np.exp(lg[:, None] * (idx + 1.0)).astype(np.float32)
    kd = np.exp(lg[:, None] * (RET_CHUNK - 1.0 - idx)).astype(np.float32)
    qd = np.broadcast_to(qd[:, :, None], (N_HEADS, RET_CHUNK, HEAD_DIM)).copy()
    kd = np.broadcast_to(kd[:, :, None], (N_HEADS, RET_CHUNK, HEAD_DIM)).copy()
    return jnp.asarray(dm), jnp.asarray(qd), jnp.asarray(kd)


def _ret_kernel(lay, rev, *refs):
    if rev:
        q_ref, k_ref, v_ref, cos_ref, sin_ref, dm_ref, qd_ref, kd_ref, out_ref, s_ref, qt_ref, kt_ref, o_ref = refs
        z_ref = ob_ref = gain_ref = None
    else:
        (q_ref, k_ref, v_ref, cos_ref, sin_ref, dm_ref, qd_ref, kd_ref, z_ref, ob_ref, gain_ref,
         out_ref, s_ref, qt_ref, kt_ref, o_ref) = refs
    nc = RET_PIECE // RET_CHUNK
    chunk_decay = [float(np.exp(np.float32(v) * RET_CHUNK)) for v in _ret_log_gamma(rev)]
    heads = [slice(h * HEAD_DIM, (h + 1) * HEAD_DIM) for h in range(N_HEADS)]
    _reset_state(lay, rev, s_ref)

    def piece(base):
        blk = _at(base, 0, RET_PIECE)
        cos = cos_ref[blk, :]
        sin = sin_ref[blk, :]
        for sl in heads:
            qh = q_ref[blk, sl].astype(F32)
            kh = k_ref[blk, sl].astype(F32)
            qt_ref[:, sl] = (qh * cos + pltpu.roll(qh, HEAD_DIM // 2, 1) * sin) * QK_SCALE
            kt_ref[:, sl] = kh * cos + pltpu.roll(kh, HEAD_DIM // 2, 1) * sin

        states = [s_ref[h] for h in range(N_HEADS)]
        for ci in range(nc):
            c = (nc - 1 - ci) if rev else ci
            rows = slice(c * RET_CHUNK, (c + 1) * RET_CHUNK)
            qcs = [qt_ref[rows, sl] for sl in heads]
            kcs = [kt_ref[rows, sl] for sl in heads]
            vcs = [v_ref[_at(base, c * RET_CHUNK, RET_CHUNK), sl].astype(BF16) for sl in heads]
            scores = [_dot_nt(q, k) for q, k in zip(qcs, kcs)]
            inters = [_dot(q * qd_ref[h], states[h]) for h, q in enumerate(qcs)]
            upds = [_dot_tn(k * kd_ref[h], v) for h, (k, v) in enumerate(zip(kcs, vcs))]
            intras = [jnp.dot((s * dm_ref[h]).astype(BF16), vcs[h], preferred_element_type=F32)
                      for h, s in enumerate(scores)]
            for h, sl in enumerate(heads):
                o_ref[rows, sl] = intras[h] + inters[h]
                states[h] = chunk_decay[h] * states[h] + upds[h]
        for h in range(N_HEADS):
            s_ref[h] = states[h]
        _finish(rev, base, o_ref, ob_ref, z_ref, gain_ref, out_ref, RET_PIECE)

    _sub_blocks(lay, rev, piece, RET_PIECE)


def _mixer_b(proj, cos2, sin2, o_bwd, gain, lay, rev):
    tb, nb = lay.tb, lay.nb
    bi = _blk_index(lay, rev)

    def cur(cb):
        return pl.BlockSpec((tb, W_GROUP), lambda j: (bi(j), cb))

    rot = pl.BlockSpec((tb, HEAD_DIM), lambda j: (lay.pos_block(bi(j)), 0))
    dm, qd, kd = _ret_consts(rev)
    in_specs = [cur(CB_B_Q), cur(CB_B_K), cur(CB_B_V), rot, rot,
                pl.BlockSpec((N_HEADS, RET_CHUNK, RET_CHUNK), lambda j: (0, 0, 0)),
                pl.BlockSpec((N_HEADS, RET_CHUNK, HEAD_DIM), lambda j: (0, 0, 0)),
                pl.BlockSpec((N_HEADS, RET_CHUNK, HEAD_DIM), lambda j: (0, 0, 0))]
    args = [proj, proj, proj, cos2, sin2, dm, qd, kd]
    if not rev:
        in_specs += [cur(CB_B_Z), pl.BlockSpec((tb, W_GROUP), lambda j: (bi(j), 0)),
                     pl.BlockSpec((1, W_GROUP), lambda j: (0, 0))]
        args += [proj, o_bwd, gain.reshape(1, W_GROUP)]
    return pl.pallas_call(
        functools.partial(_ret_kernel, lay, rev),
        grid=(nb,),
        in_specs=in_specs,
        out_specs=pl.BlockSpec((tb, W_GROUP), lambda j: (bi(j), 0)),
        out_shape=jax.ShapeDtypeStruct((lay.n, W_GROUP), F32 if rev else BF16),
        scratch_shapes=[pltpu.VMEM((N_HEADS, HEAD_DIM, HEAD_DIM), F32),
                        pltpu.VMEM((RET_PIECE, W_GROUP), F32), pltpu.VMEM((RET_PIECE, W_GROUP), F32),
                        pltpu.VMEM((RET_PIECE, W_GROUP), F32)],
        compiler_params=_cparams(("arbitrary",)),
        name="mixer_b_retention_bwd" if rev else "mixer_b_retention_fwd",
    )(*args)


HGRN_LEVELS = (32, 16, 8, 4, 2, 1)
LOG2_E = math.log2(math.e)
HGRN_GROUP = N_HEADS


def _hgrn_consts(rev):
    c = CHUNK
    i = np.arange(c)[:, None]
    t = np.arange(c)[None, :]
    if rev:
        cum = t >= i
        rest = t < i
    else:
        cum = t <= i
        rest = t > i
    blocks = [cum, rest]
    masks = [np.eye(c, dtype=bool)]
    col = np.arange(c)[None, :]
    for s in HGRN_LEVELS:
        start = (i // (2 * s)) * (2 * s)
        later = (i - start) >= s
        if rev:
            ref = start + s
            p = np.where(later, (t >= ref) & (t < i), (t >= i) & (t < ref))
            q_half, k_half = 0, 1
        else:
            ref = start + s - 1
            p = np.where(later, (t > ref) & (t <= i), (t > i) & (t <= ref))
            q_half, k_half = 1, 0
        blocks.append(p)
        same = (i // (2 * s)) == (col // (2 * s))
        masks.append(same & ((i // s) % 2 == q_half) & ((col // s) % 2 == k_half))
    pst = np.concatenate(blocks, axis=0).astype(np.float32)
    pst = np.concatenate([pst, pst], axis=1)
    msk = np.stack(masks, axis=0).astype(np.float32)
    return jnp.asarray(pst, dtype=BF16), jnp.asarray(msk)


def _hgrn_kernel(lay, rev, *refs):
    if rev:
        (q_ref, f_ref, v_ref, llb_ref, l1lb_ref, pst_ref, msk_ref, out_ref,
         st_ref, qs_ref, g_ref, kk_ref, o_ref, e_all_ref, e_last_ref) = refs
        z_ref = ob_ref = gain_ref = None
    else:
        (q_ref, f_ref, v_ref, llb_ref, l1lb_ref, pst_ref, msk_ref, z_ref, ob_ref, gain_ref, out_ref,
         st_ref, qs_ref, g_ref, kk_ref, o_ref, e_all_ref, e_last_ref) = refs
    nc = SUB_BLOCK // CHUNK
    n_lvl = len(HGRN_LEVELS)
    last_row = 0 if rev else CHUNK - 1
    heads = [slice(h * HEAD_DIM, (h + 1) * HEAD_DIM) for h in range(N_HEADS)]
    _reset_state(lay, rev, st_ref)

    def piece(base):
        blk = _at(base, 0, SUB_BLOCK)
        qs_ref[...] = (_silu(q_ref[blk, :].astype(F32)) * QK_SCALE).astype(BF16)
        a = llb_ref[...]
        b = l1lb_ref[...] + _log_sigmoid(f_ref[blk, :].astype(F32))
        g2 = (jnp.maximum(a, b) + jnp.log(1.0 + jnp.exp(-jnp.abs(a - b)))) * LOG2_E
        g_ref[...] = g2
        kk_ref[...] = (1.0 - jnp.exp2(g2)).astype(BF16)

        pst = pst_ref[...]
        for ci in range(nc):
            c = (nc - 1 - ci) if rev else ci
            g_parts = jnp.concatenate(_split2(g_ref[c * CHUNK:(c + 1) * CHUNK, :]), axis=0)
            e_f32 = jnp.exp2(jnp.dot(pst, g_parts, preferred_element_type=F32))
            e_all_ref[ci] = e_f32.astype(BF16)
            e_last_ref[ci] = e_f32[last_row:last_row + 1, :]

        chains = [(ci, (nc - 1 - ci) if rev else ci, sl) for ci in range(nc) for sl in heads]

        def operands(ci, c, sl):
            rows = slice(c * CHUNK, (c + 1) * CHUNK)
            return (qs_ref[rows, sl], kk_ref[rows, sl], v_ref[_at(base, c * CHUNK, CHUNK), sl].astype(BF16),
                    e_all_ref.at[ci])

        attns, upds = [], []
        for first in range(0, len(chains), HGRN_GROUP):
            ops = [operands(*chain) for chain in chains[first:first + HGRN_GROUP]]
            parts = [[_dot_nt(q, k) for q, k, _, _ in ops]]
            for lvl in range(n_lvl):
                es = [e_ref[(2 + lvl) * CHUNK:(3 + lvl) * CHUNK, sl]
                      for (_, _, _, e_ref), (_, _, sl) in zip(ops, chains[first:])]
                parts.append([_dot_nt(q * e, k * e) for (q, k, _, _), e in zip(ops, es)])
            upds += [_dot_tn(v, k * e_ref[CHUNK:2 * CHUNK, sl])
                     for (_, k, v, e_ref), (_, _, sl) in zip(ops, chains[first:])]
            for h in range(HGRN_GROUP):
                attn = parts[0][h] * msk_ref[0]
                for lvl in range(n_lvl):
                    attn = attn + parts[1 + lvl][h] * msk_ref[1 + lvl]
                attns.append(attn.astype(BF16))
        states = [st_ref[h] for h in range(N_HEADS)]
        entering = []
        for idx, (ci, c, sl) in enumerate(chains):
            h = idx % N_HEADS
            entering.append(states[h])
            states[h] = states[h] * e_last_ref[ci, :, sl] + upds[idx]
        for h in range(N_HEADS):
            st_ref[h] = states[h]
        ops = [operands(*chain) for chain in chains]
        inters = [_dot_nt(q * e_ref[0:CHUNK, sl], s)
                  for (q, _, _, e_ref), (_, _, sl), s in zip(ops, chains, entering)]
        intras = [jnp.dot(a, v, preferred_element_type=F32) for a, (_, _, v, _) in zip(attns, ops)]
        for (ci, c, sl), inter, intra in zip(chains, inters, intras):
            o_ref[c * CHUNK:(c + 1) * CHUNK, sl] = inter + intra
        _finish(rev, base, o_ref, ob_ref, z_ref, gain_ref, out_ref)

    _sub_blocks(lay, rev, piece)


def _mixer_c(proj, log_lb, log1m_lb, o_bwd, gain, lay, rev):
    tb, nb = lay.tb, lay.nb
    bi = _blk_index(lay, rev)

    def cur(cb):
        return pl.BlockSpec((tb, W_GROUP), lambda j: (bi(j), cb))

    vec = pl.BlockSpec((1, W_GROUP), lambda j: (0, 0))
    d = 1 if rev else 0
    pst, msk = _hgrn_consts(rev)
    n_stack = pst.shape[0]
    in_specs = [cur(CB_C_Q), cur(CB_C_FB if rev else CB_C_FF), cur(CB_C_I), vec, vec,
                pl.BlockSpec((n_stack, 2 * CHUNK), lambda j: (0, 0)),
                pl.BlockSpec((msk.shape[0], CHUNK, CHUNK), lambda j: (0, 0, 0))]
    args = [proj, proj, proj, log_lb[d].reshape(1, W_GROUP), log1m_lb[d].reshape(1, W_GROUP), pst, msk]
    if not rev:
        in_specs += [cur(CB_C_Z), pl.BlockSpec((tb, W_GROUP), lambda j: (bi(j), 0)), vec]
        args += [proj, o_bwd, gain.reshape(1, W_GROUP)]
    big = pltpu.VMEM((SUB_BLOCK, W_GROUP), F32)
    return pl.pallas_call(
        functools.partial(_hgrn_kernel, lay, rev),
        grid=(nb,),
        in_specs=in_specs,
        out_specs=pl.BlockSpec((tb, W_GROUP), lambda j: (bi(j), 0)),
        out_shape=jax.ShapeDtypeStruct((lay.n, W_GROUP), F32 if rev else BF16),
        scratch_shapes=[pltpu.VMEM((N_HEADS, HEAD_DIM, HEAD_DIM), F32),
                        pltpu.VMEM((SUB_BLOCK, W_GROUP), BF16),
                        big, pltpu.VMEM((SUB_BLOCK, W_GROUP), BF16), big,
                        pltpu.VMEM((SUB_BLOCK // CHUNK, n_stack, W_GROUP), BF16),
                        pltpu.VMEM((SUB_BLOCK // CHUNK, 1, W_GROUP), F32)],
        compiler_params=_cparams(("arbitrary",)),
        name="mixer_c_hgrn2_bwd" if rev else "mixer_c_hgrn2_fwd",
    )(*args)


def _unit_tri_inverses(a_mats, row, col):
    c = a_mats[0].shape[0]
    eye = jnp.where(row == col, 1.0, 0.0)

    def same_block(size):
        return (row // size) == (col // size)

    base = 8
    m_base = same_block(base)
    ns = [jnp.where(m_base, -a, 0.0).astype(BF16) for a in a_mats]
    n2s = [jnp.dot(n, n, preferred_element_type=F32) for n in ns]
    xs = [eye + n.astype(F32) for n in ns]
    xs = [x + _dot(x, n2) for x, n2 in zip(xs, n2s)]
    n4s = [_dot(n2, n2) for n2 in n2s]
    xs = [x + _dot(x, n4) for x, n4 in zip(xs, n4s)]
    size = base
    while size < c:
        m_off = same_block(2 * size) & jnp.logical_not(same_block(size))
        xbs = [x.astype(BF16) for x in xs]
        ts = [jnp.dot(xb, jnp.where(m_off, a, 0.0).astype(BF16), preferred_element_type=F32)
              for xb, a in zip(xbs, a_mats)]
        xs = [x - jnp.dot(t.astype(BF16), xb, preferred_element_type=F32) for x, t, xb in zip(xs, ts, xbs)]
        size *= 2
    return xs


SHIFT_ROWS = 128


def _shift_mats():
    half = DN_CONV_WIDTH // 2
    t = np.arange(SHIFT_ROWS)[:, None]
    s = np.arange(SHIFT_ROWS + 2 * HALO_D)[None, :]
    mats = [(s == HALO_D + t + d) for d in range(-half, half + 1) if d != 0]
    return jnp.asarray(np.concatenate(mats, axis=0).astype(np.float32), dtype=BF16)


def _delta_pre_kernel(lay, q_ref, k_ref, v_ref, qp_ref, kp_ref, vp_ref, qn_ref, kn_ref, vn_ref, sm_ref,
                      cw_ref, alog_ref, dtb_ref, sh_ref, qo_ref, ko_ref, vo_ref, cum_ref, beta_ref, cumt_ref,
                      xq_ref, xk_ref, xv_ref):
    tb = lay.tb
    blk = pl.program_id(0)
    keep_prev = jnp.where(lay.seq_start(blk), 0.0, 1.0)
    keep_next = jnp.where(lay.seq_end(blk), 0.0, 1.0)
    half = DN_CONV_WIDTH // 2
    parts = ((q_ref, qp_ref, qn_ref, qo_ref, xq_ref), (k_ref, kp_ref, kn_ref, ko_ref, xk_ref),
             (v_ref, vp_ref, vn_ref, vo_ref, xv_ref))
    for c_ref, p_ref, n_ref, _, x_ref in parts:
        x_ref[0:HALO_D, :] = (p_ref[...].astype(F32) * keep_prev).astype(BF16)
        x_ref[HALO_D:HALO_D + tb, :] = c_ref[...]
        x_ref[HALO_D + tb:2 * HALO_D + tb, :] = (n_ref[...].astype(F32) * keep_next).astype(BF16)
    tiles = [(part, r0) for part in range(len(parts)) for r0 in range(0, tb, SHIFT_ROWS)]
    shifted = [jnp.dot(sh_ref[...], parts[part][4][r0:r0 + SHIFT_ROWS + 2 * HALO_D, :], preferred_element_type=F32)
               for part, r0 in tiles]
    for (part, r0), sh in zip(tiles, shifted):
        x_ref, dst_ref = parts[part][4], parts[part][3]
        wsl = slice(part * W_GROUP, (part + 1) * W_GROUP)
        acc = cw_ref[half:half + 1, wsl] * x_ref[HALO_D + r0:HALO_D + r0 + SHIFT_ROWS, :].astype(F32)
        for j, k in enumerate([k for k in range(DN_CONV_WIDTH) if k != half]):
            acc = acc + cw_ref[k:k + 1, wsl] * sh[j * SHIFT_ROWS:(j + 1) * SHIFT_ROWS, :]
        y = _silu(acc)
        rows = slice(r0, r0 + SHIFT_ROWS)
        if part < 2:
            for h in range(N_HEADS):
                sl = slice(h * HEAD_DIM, (h + 1) * HEAD_DIM)
                yh = y[:, sl]
                yn = yh * lax.rsqrt(jnp.sum(yh * yh, axis=-1, keepdims=True) + NORM_EPS)
                dst_ref[rows, sl] = (yn * QK_SCALE if part == 0 else yn).astype(BF16)
        else:
            dst_ref[rows, :] = y.astype(BF16)

    sm = sm_ref[...]
    g = -jnp.exp(alog_ref[...]) * _softplus(sm + dtb_ref[...])
    beta_ref[...] = _sigmoid(sm)

    row = lax.broadcasted_iota(jnp.int32, (CHUNK, CHUNK), 0)
    col = lax.broadcasted_iota(jnp.int32, (CHUNK, CHUNK), 1)
    lower = jnp.where(row >= col, 1.0, 0.0)
    upper = jnp.where(row <= col, 1.0, 0.0)
    fwd_lane = lax.broadcasted_iota(jnp.int32, (CHUNK, SMALL_W), 1) < N_HEADS
    eye_rows = jnp.where(lax.broadcasted_iota(jnp.int32, (SUBLANES, SMALL_W), 0)
                         == lax.broadcasted_iota(jnp.int32, (SUBLANES, SMALL_W), 1), 1.0, 0.0).astype(BF16)
    for c in range(tb // CHUNK):
        gc = g[c * CHUNK:(c + 1) * CHUNK, :]
        cum = jnp.where(fwd_lane, _dot_exact_lhs(lower, gc), _dot_exact_lhs(upper, gc))
        cum_ref[c * CHUNK:(c + 1) * CHUNK, :] = cum
        cum_t = None
        for piece in _split3(cum):
            t = lax.dot_general(eye_rows, piece, (((1,), (1,)), ((), ())), preferred_element_type=F32)
            cum_t = t if cum_t is None else cum_t + t
        cumt_ref[c] = cum_t


def _delta_kernel(lay, rev, *refs):
    if rev:
        (qc_ref, kc_ref, vc_ref, cum_ref, beta_ref, cumt_ref, out_ref,
         s_ref, o_ref, qd_ref, kd_ref, qk_ref, rhs_ref, last_ref, gq_ref, bm_ref) = refs
        z_ref = ob_ref = gain_ref = None
    else:
        (qc_ref, kc_ref, vc_ref, cum_ref, beta_ref, cumt_ref, z_ref, ob_ref, gain_ref, out_ref,
         s_ref, o_ref, qd_ref, kd_ref, qk_ref, rhs_ref, last_ref, gq_ref, bm_ref) = refs
    _reset_state(lay, rev, s_ref)
    named = (qc_ref, kc_ref, vc_ref, cum_ref, beta_ref, cumt_ref, z_ref, ob_ref, gain_ref, out_ref,
             s_ref, o_ref, qd_ref, kd_ref, qk_ref, rhs_ref, last_ref, gq_ref, bm_ref)
    _sub_blocks(lay, rev, functools.partial(_delta_piece, rev, named), DELTA_PIECE)


def _delta_piece(rev, named, base):
    (qc_ref, kc_ref, vc_ref, cum_ref, beta_ref, cumt_ref, z_ref, ob_ref, gain_ref, out_ref,
     s_ref, o_ref, qd_ref, kd_ref, qk_ref, rhs_ref, last_ref, gq_ref, bm_ref) = named
    nc = DELTA_PIECE // CHUNK
    row, col, incl, strict = _causal_masks(rev, CHUNK)
    lane0 = N_HEADS if rev else 0

    cums = [cum_ref[_at(base, c * CHUNK, CHUNK), :] for c in range(nc)]
    cum_ts = [cumt_ref[base // CHUNK + c] for c in range(nc)]
    raws = []
    for c in range(nc):
        rows = slice(c * CHUNK, (c + 1) * CHUNK)
        src = _at(base, c * CHUNK, CHUNK)
        beta_all = beta_ref[src, :]
        for h in range(N_HEADS):
            sl = slice(h * HEAD_DIM, (h + 1) * HEAD_DIM)
            qc = qc_ref[src, sl]
            kc = kc_ref[src, sl]
            vc = vc_ref[src, sl]
            ci_b = jnp.broadcast_to(cums[c][:, lane0 + h:lane0 + h + 1], (CHUNK, HEAD_DIM))
            b_lane = 2 * N_HEADS + lane0 + h
            beta_b = jnp.broadcast_to(beta_all[:, b_lane:b_lane + 1], (CHUNK, HEAD_DIM))
            k_beta = kc * beta_b
            raws.append(_dot_nt(jnp.concatenate([k_beta.astype(BF16), qc], axis=0), kc))
            e_ci = jnp.exp(ci_b)
            rhs_ref[c * N_HEADS + h] = jnp.concatenate([vc * beta_b, k_beta * e_ci], axis=1).astype(BF16)
            qd_ref[rows, sl] = qc * e_ci
            c_last = ci_b[0:1, :] if rev else ci_b[CHUNK - 1:CHUNK, :]
            kd_ref[rows, sl] = (kc * jnp.exp(c_last - ci_b)).astype(BF16)
            last_ref[c * N_HEADS + h] = jnp.broadcast_to(jnp.exp(c_last), (8, HEAD_DIM))
    a_mats = []
    for idx, raw in enumerate(raws):
        c, h = divmod(idx, N_HEADS)
        ci_col = cums[c][:, lane0 + h:lane0 + h + 1]
        cj_row = cum_ts[c][lane0 + h:lane0 + h + 1, :]
        decay = jnp.exp(jnp.where(incl, ci_col - cj_row, NEG_BIG))
        a_mats.append(jnp.where(strict, raw[0:CHUNK] * decay, 0.0))
        qk_ref[idx] = (raw[CHUNK:2 * CHUNK] * decay).astype(BF16)

    t_invs = _unit_tri_inverses(a_mats, row, col)
    where = [(slice((idx // N_HEADS) * CHUNK, (idx // N_HEADS + 1) * CHUNK),
              slice((idx % N_HEADS) * HEAD_DIM, (idx % N_HEADS + 1) * HEAD_DIM)) for idx in range(len(t_invs))]
    uws = [jnp.dot(t_inv.astype(BF16), rhs_ref[idx], preferred_element_type=F32).astype(BF16)
           for idx, t_inv in enumerate(t_invs)]
    kuws = [lax.dot_general(kd_ref[rows, sl], uw, (((0,), (0,)), ((), ())), preferred_element_type=F32)
            for (rows, sl), uw in zip(where, uws)]
    quws = [jnp.dot(qk_ref[idx], uw, preferred_element_type=F32) for idx, uw in enumerate(uws)]
    for idx, ((rows, sl), kuw, quw) in enumerate(zip(where, kuws, quws)):
        bm_ref[idx] = kuw[:, 0:HEAD_DIM]
        gq_ref[idx, 0:HEAD_DIM, :] = kuw[:, HEAD_DIM:2 * HEAD_DIM].astype(BF16)
        gq_ref[idx, HEAD_DIM:HEAD_DIM + CHUNK, :] = (qd_ref[rows, sl] - quw[:, HEAD_DIM:2 * HEAD_DIM]).astype(BF16)
        o_ref[rows, sl] = quw[:, 0:HEAD_DIM]

    states = [s_ref[h] for h in range(N_HEADS)]
    for ci in range(nc):
        c = (nc - 1 - ci) if rev else ci
        rows = slice(c * CHUNK, (c + 1) * CHUNK)
        for h in range(N_HEADS):
            idx = c * N_HEADS + h
            sl = slice(h * HEAD_DIM, (h + 1) * HEAD_DIM)
            gs = jnp.dot(gq_ref[idx], states[h].astype(BF16), preferred_element_type=F32)
            o_ref[rows, sl] += gs[HEAD_DIM:HEAD_DIM + CHUNK]
            states[h] = states[h] * last_ref[idx][0:1, :] - gs[0:HEAD_DIM] + bm_ref[idx]
    for h in range(N_HEADS):
        s_ref[h] = states[h]
    _finish(rev, base, o_ref, ob_ref, z_ref, gain_ref, out_ref, DELTA_PIECE)


def _mixer_d_pre(proj, proj_small, conv_w, a_log, dt_bias, lay):
    tb, nb = lay.tb, lay.nb
    hb = tb // HALO_D
    n_halo = lay.n // HALO_D

    def cur(cb):
        return pl.BlockSpec((tb, W_GROUP), lambda i: (i, cb))

    def prev(cb):
        return pl.BlockSpec((HALO_D, W_GROUP), lambda i: (jnp.maximum(i * hb - 1, 0), cb))

    def nxt(cb):
        return pl.BlockSpec((HALO_D, W_GROUP), lambda i: (jnp.minimum((i + 1) * hb, n_halo - 1), cb))

    small = pl.BlockSpec((1, SMALL_W), lambda i: (0, 0))
    pad_lanes = SMALL_W - 2 * N_HEADS
    alog_row = jnp.pad(a_log.reshape(1, 2 * N_HEADS), ((0, 0), (0, pad_lanes)))
    dtb_row = jnp.pad(dt_bias.reshape(1, 2 * N_HEADS), ((0, 0), (0, pad_lanes)))
    wide = pl.BlockSpec((tb, W_GROUP), lambda i: (i, 0))
    narrow = pl.BlockSpec((tb, SMALL_W), lambda i: (i, 0))
    shifts = _shift_mats()
    return pl.pallas_call(
        functools.partial(_delta_pre_kernel, lay),
        grid=(nb,),
        in_specs=[cur(CB_D_Q), cur(CB_D_K), cur(CB_D_V), prev(CB_D_Q), prev(CB_D_K), prev(CB_D_V),
                  nxt(CB_D_Q), nxt(CB_D_K), nxt(CB_D_V),
                  pl.BlockSpec((tb, SMALL_W), lambda i: (i, 0)),
                  pl.BlockSpec((DN_CONV_WIDTH, 3 * W_GROUP), lambda i: (0, 0)), small, small,
                  pl.BlockSpec(shifts.shape, lambda i: (0, 0))],
        out_specs=[wide, wide, wide, narrow, narrow,
                   pl.BlockSpec((tb // CHUNK, SUBLANES, CHUNK), lambda i: (i, 0, 0))],
        out_shape=[jax.ShapeDtypeStruct((lay.n, W_GROUP), BF16)] * 3
        + [jax.ShapeDtypeStruct((lay.n, SMALL_W), F32)] * 2
        + [jax.ShapeDtypeStruct((lay.n // CHUNK, SUBLANES, CHUNK), F32)],
        scratch_shapes=[pltpu.VMEM((tb + 2 * HALO_D, W_GROUP), BF16)] * 3,
        compiler_params=_cparams(("arbitrary",)),
        name="mixer_d_deltanet_pre",
    )(*([proj] * 9), proj_small, conv_w, alog_row, dtb_row, shifts)


def _mixer_d(pre, proj, o_bwd, gain, lay, rev):
    tb, nb = lay.tb, lay.nb
    bi = _blk_index(lay, rev)
    wide = pl.BlockSpec((tb, W_GROUP), lambda j: (bi(j), 0))
    narrow = pl.BlockSpec((tb, SMALL_W), lambda j: (bi(j), 0))
    in_specs = [wide, wide, wide, narrow, narrow,
                pl.BlockSpec((tb // CHUNK, SUBLANES, CHUNK), lambda j: (bi(j), 0, 0))]
    args = list(pre)
    if not rev:
        in_specs += [pl.BlockSpec((tb, W_GROUP), lambda j: (bi(j), CB_D_Z)), wide,
                     pl.BlockSpec((1, W_GROUP), lambda j: (0, 0))]
        args += [proj, o_bwd, gain.reshape(1, W_GROUP)]
    big = pltpu.VMEM((DELTA_PIECE, W_GROUP), F32)
    nc = DELTA_PIECE // CHUNK
    return pl.pallas_call(
        functools.partial(_delta_kernel, lay, rev),
        grid=(nb,),
        in_specs=in_specs,
        out_specs=wide,
        out_shape=jax.ShapeDtypeStruct((lay.n, W_GROUP), F32 if rev else BF16),
        scratch_shapes=[pltpu.VMEM((N_HEADS, HEAD_DIM, HEAD_DIM), F32),
                        big,
                        big,
                        pltpu.VMEM((DELTA_PIECE, W_GROUP), BF16),
                        pltpu.VMEM((nc * N_HEADS, CHUNK, CHUNK), BF16),
                        pltpu.VMEM((nc * N_HEADS, CHUNK, 2 * HEAD_DIM), BF16),
                        pltpu.VMEM((nc * N_HEADS, 8, HEAD_DIM), F32),
                        pltpu.VMEM((nc * N_HEADS, HEAD_DIM + CHUNK, HEAD_DIM), BF16),
                        pltpu.VMEM((nc * N_HEADS, HEAD_DIM, HEAD_DIM), F32)],
        compiler_params=_cparams(("arbitrary",)),
        name="mixer_d_deltanet_bwd" if rev else "mixer_d_deltanet_fwd",
    )(*args)


def _regroup_w_in(w):
    d_qkv_end = 15 * W_GROUP
    small = w[:, d_qkv_end:d_qkv_end + 4 * N_HEADS]
    d_z = w[:, d_qkv_end + 4 * N_HEADS:]
    pad = jnp.zeros((w.shape[0], SMALL_W - 4 * N_HEADS), w.dtype)
    main = jnp.concatenate([w[:, :d_qkv_end], d_z], axis=1).astype(BF16)
    return main, jnp.concatenate([small, pad], axis=1).astype(BF16)


def _rotary_tables(t_max):
    half = HEAD_DIM // 2
    inv = 1.0 / (ROPE_BASE ** (jnp.arange(half, dtype=F32) / half))
    ang = jnp.arange(t_max, dtype=F32)[:, None] * inv[None, :]
    cos, sin = jnp.cos(ang), jnp.sin(ang)
    return jnp.concatenate([cos, cos], axis=-1), jnp.concatenate([-sin, sin], axis=-1)


def _pick_tile(t_prompt, t_sample, want):
    tile = want
    while t_prompt % tile or t_sample % tile:
        tile //= 2
    return tile


def kernel(x_prompt, x_sample, c_prompt, c_sample, ada_w, ada_b, norm_g, w_in, conv_a_w, conv_a_b, ln_a_g, ln_a_b,
           ret_norm_g, hgrn_lb_logits, hgrn_norm_g, dn_conv_w, dn_a_log, dn_dt_bias, dn_norm_g, w_out, final_g):
    depth = w_in.shape[0]
    n_prompt, t_prompt, _ = x_prompt.shape
    n_sample, t_sample, _ = x_sample.shape
    tm_in = _pick_tile(t_prompt, t_sample, 1024)
    tm_out = _pick_tile(t_prompt, t_sample, 512)
    lay = _Layout(t_prompt, n_prompt, t_sample, n_sample, _pick_tile(t_prompt, t_sample, 1024))
    lay_a = _Layout(t_prompt, n_prompt, t_sample, n_sample, _pick_tile(t_prompt, t_sample, 512))
    lay_dp = _Layout(t_prompt, n_prompt, t_sample, n_sample, _pick_tile(t_prompt, t_sample, DELTA_PIECE))

    xs = (x_prompt.reshape(-1, D_MODEL), x_sample.reshape(-1, D_MODEL))
    c_all = jnp.concatenate([c_prompt, c_sample], axis=0)
    n_c = c_all.shape[0]
    c_all = jnp.pad(c_all, ((0, (-n_c) % 8), (0, 0)))
    mod = _modulation(c_all, ada_w, ada_b)

    lb = jnp.cumsum(jax.nn.softmax(hgrn_lb_logits.astype(F32), axis=0), axis=0)
    lb = lb - lb[:1]
    log_lb = jnp.log(lb)
    log1m_lb = jnp.log1p(-lb)
    cos2, sin2 = _rotary_tables(max(t_prompt, t_sample))

    for l in range(depth):
        mod_l = mod[l].reshape(mod.shape[1], 1, 3 * D_MODEL)
        proj, proj_small = _in_proj(xs, mod_l, norm_g[l], *_regroup_w_in(w_in[l]), lay, tm_in)
        m_a = _mixer_a(proj, conv_a_w[l], conv_a_b[l], ln_a_g[l], ln_a_b[l], lay_a)
        ob = _mixer_b(proj, cos2, sin2, None, None, lay, True)
        m_b = _mixer_b(proj, cos2, sin2, ob, ret_norm_g[l], lay, False)
        oc = _mixer_c(proj, log_lb[l], log1m_lb[l], None, None, lay, True)
        m_c = _mixer_c(proj, log_lb[l], log1m_lb[l], oc, hgrn_norm_g[l], lay, False)
        d_pre = _mixer_d_pre(proj, proj_small, dn_conv_w[l], dn_a_log[l], dn_dt_bias[l], lay_dp)
        od = _mixer_d(d_pre, proj, None, None, lay, True)
        m_d = _mixer_d(d_pre, proj, od, dn_norm_g[l], lay, False)
        xs = _out_proj(xs, (m_a, m_b, m_c, m_d), mod_l, w_out[l].astype(BF16), final_g, lay, tm_out,
                       final=(l == depth - 1))

    return (xs[0].reshape(n_prompt, t_prompt, D_MODEL), xs[1].reshape(n_sample, t_sample, D_MODEL))
```

```python
import functools
import math

import numpy as np
import jax
import jax.numpy as jnp
from jax import lax
from jax.experimental import pallas as pl
from jax.experimental.pallas import tpu as pltpu

F32 = jnp.float32
BF16 = jnp.bfloat16

D_MODEL = 1024
W_GROUP = 512
HEAD_DIM = 128
SUBLANES = 8
N_HEADS = 4
CONV_A_WIDTH = 31
DN_CONV_WIDTH = 5
CHUNK = 64
ROPE_BASE = 10000.0
RET_DECAY_OFFSET = 5.0
NORM_EPS = 1e-6
QK_SCALE = HEAD_DIM ** -0.5
NEG_BIG = -1e30

N_COL_BLOCKS = 16
SMALL_W = 128
SUB_BLOCK = 256
DELTA_PIECE = 512
D_PROJ = N_COL_BLOCKS * W_GROUP
PROJ_TN = 2048
HALO_A = 16
HALO_D = 16
VMEM_LIMIT = 56 * 1024 * 1024

CB_A_VAL, CB_A_GLU, CB_A_Z = 0, 1, 2
CB_B_Q, CB_B_K, CB_B_V, CB_B_Z = 3, 4, 5, 6
CB_C_Q, CB_C_FF, CB_C_FB, CB_C_I, CB_C_Z = 7, 8, 9, 10, 11
CB_D_Q, CB_D_K, CB_D_V, CB_D_Z = 12, 13, 14, 15


def _dot(a, b):
    return jnp.dot(a.astype(BF16), b.astype(BF16), preferred_element_type=F32)


def _dot_nt(a, b):
    return lax.dot_general(a.astype(BF16), b.astype(BF16), (((1,), (1,)), ((), ())), preferred_element_type=F32)


def _dot_tn(a, b):
    return lax.dot_general(a.astype(BF16), b.astype(BF16), (((0,), (0,)), ((), ())), preferred_element_type=F32)


def _split2(x):
    hi = x.astype(BF16)
    lo = (x - hi.astype(F32)).astype(BF16)
    return hi, lo


def _split3(x):
    hi = x.astype(BF16)
    r = x - hi.astype(F32)
    mid = r.astype(BF16)
    lo = (r - mid.astype(F32)).astype(BF16)
    return hi, mid, lo


def _dot_exact_lhs(m, x):
    mb = m.astype(BF16)
    acc = None
    for part in _split3(x):
        t = jnp.dot(mb, part, preferred_element_type=F32)
        acc = t if acc is None else acc + t
    return acc


def _sigmoid(x):
    return 0.5 * jnp.tanh(0.5 * x) + 0.5


def _silu(x):
    return x * _sigmoid(x)


def _softplus(x):
    return jnp.maximum(x, 0.0) + jnp.log(1.0 + jnp.exp(-jnp.abs(x)))


def _log_sigmoid(x):
    return jnp.minimum(x, 0.0) - jnp.log(1.0 + jnp.exp(-jnp.abs(x)))


class _Layout:
    def __init__(self, t_prompt, n_prompt, t_sample, n_sample, tb):
        self.tp = t_prompt * n_prompt
        self.t_prompt = t_prompt
        self.ts = t_sample
        self.n = self.tp + t_sample * n_sample
        self.tb = tb
        assert t_prompt % tb == 0 and t_sample % tb == 0
        self.nb = self.n // tb

    def seq_start(self, blk):
        s = blk * self.tb
        return jnp.where(s < self.tp, s % self.t_prompt == 0, (s - self.tp) % self.ts == 0)

    def seq_end(self, blk):
        e = (blk + 1) * self.tb
        return jnp.where(e <= self.tp, e % self.t_prompt == 0, (e - self.tp) % self.ts == 0)

    def pos_block(self, blk):
        s = blk * self.tb
        return jnp.where(s < self.tp, (s % self.t_prompt) // self.tb, ((s - self.tp) % self.ts) // self.tb)

    def batch_index(self, blk, rows):
        s = blk * rows
        return jnp.where(s < self.tp, s // self.t_prompt, self.tp // self.t_prompt + (s - self.tp) // self.ts)


def _cparams(sem):
    return pltpu.CompilerParams(dimension_semantics=sem, vmem_limit_bytes=VMEM_LIMIT)


def _mod_kernel(c_ref, w_ref, b_ref, o_ref):
    c = c_ref[...]
    o_ref[0] = jnp.dot(_silu(c), w_ref[0], preferred_element_type=F32,
                       precision=lax.Precision.HIGHEST) + b_ref[0]


def _modulation(c_all, ada_w, ada_b):
    depth = ada_w.shape[0]
    nb = c_all.shape[0]
    return pl.pallas_call(
        _mod_kernel,
        grid=(depth,),
        in_specs=[pl.BlockSpec((nb, D_MODEL), lambda l: (0, 0)),
                  pl.BlockSpec((1, D_MODEL, 3 * D_MODEL), lambda l: (l, 0, 0)),
                  pl.BlockSpec((1, 1, 3 * D_MODEL), lambda l: (l, 0, 0))],
        out_specs=pl.BlockSpec((1, nb, 3 * D_MODEL), lambda l: (l, 0, 0)),
        out_shape=jax.ShapeDtypeStruct((depth, nb, 3 * D_MODEL), F32),
        compiler_params=_cparams(("arbitrary",)),
        name="adaln_modulation",
    )(c_all, ada_w, ada_b.reshape(depth, 1, 3 * D_MODEL))


def _load_x(x_refs, i, npb):
    if len(x_refs) == 1:
        return x_refs[0][...]
    return jnp.where(i < npb, x_refs[0][...], x_refs[1][...])


def _x_specs(n_x, tm, npb, row_block):
    def spec(fn):
        return pl.BlockSpec((tm, D_MODEL), lambda *ids: (fn(row_block(*ids)), 0))

    if n_x == 1:
        return [spec(lambda i: i)]
    return [spec(lambda i: jnp.minimum(i, npb - 1)), spec(lambda i: jnp.maximum(i - npb, 0))]


def _in_proj_kernel(n_x, npb, *refs):
    x_refs = refs[:n_x]
    mod_ref, g_ref, w_ref, ws_ref, o_ref, os_ref, h_ref = refs[n_x:]

    @pl.when(pl.program_id(1) == 0)
    def _():
        x = _load_x(x_refs, pl.program_id(0), npb)
        y = x * lax.rsqrt(jnp.mean(x * x, axis=-1, keepdims=True) + NORM_EPS) * g_ref[...]
        shift = mod_ref[0, :, 0:D_MODEL]
        scale = mod_ref[0, :, D_MODEL:2 * D_MODEL]
        h = (y * (1.0 + scale) + shift).astype(BF16)
        h_ref[...] = h
        os_ref[...] = jnp.dot(h, ws_ref[...], preferred_element_type=F32)

    o_ref[...] = jnp.dot(h_ref[...], w_ref[...], preferred_element_type=F32).astype(BF16)


def _in_proj(xs, mod, norm_g, w_main, w_small, lay, tm):
    n = lay.n
    npb = lay.tp // tm
    lay_m = _Layout(lay.t_prompt, lay.tp // lay.t_prompt, lay.ts, (lay.n - lay.tp) // lay.ts, tm)
    return pl.pallas_call(
        functools.partial(_in_proj_kernel, len(xs), npb),
        grid=(n // tm, D_PROJ // PROJ_TN),
        in_specs=_x_specs(len(xs), tm, npb, lambda i, j: i) + [
                  pl.BlockSpec((1, 1, 3 * D_MODEL), lambda i, j: (lay_m.batch_index(i, tm), 0, 0)),
                  pl.BlockSpec((1, D_MODEL), lambda i, j: (0, 0)),
                  pl.BlockSpec((D_MODEL, PROJ_TN), lambda i, j: (0, j)),
                  pl.BlockSpec((D_MODEL, SMALL_W), lambda i, j: (0, 0))],
        out_specs=[pl.BlockSpec((tm, PROJ_TN), lambda i, j: (i, j)),
                   pl.BlockSpec((tm, SMALL_W), lambda i, j: (i, 0))],
        out_shape=[jax.ShapeDtypeStruct((n, D_PROJ), BF16), jax.ShapeDtypeStruct((n, SMALL_W), F32)],
        scratch_shapes=[pltpu.VMEM((tm, D_MODEL), BF16)],
        compiler_params=_cparams(("arbitrary", "arbitrary")),
        name="in_proj",
    )(*xs, mod, norm_g.reshape(1, D_MODEL), w_main, w_small)


def _out_proj_kernel(final, n_x, npb, *refs):
    x_refs = refs[:n_x]
    ma_ref, mb_ref, mc_ref, md_ref, mod_ref, w_ref, fg_ref = refs[n_x:n_x + 7]
    o_refs = refs[n_x + 7:]
    i = pl.program_id(0)
    acc = jnp.dot(ma_ref[...], w_ref[0:W_GROUP, :], preferred_element_type=F32)
    acc += jnp.dot(mb_ref[...], w_ref[W_GROUP:2 * W_GROUP, :], preferred_element_type=F32)
    acc += jnp.dot(mc_ref[...], w_ref[2 * W_GROUP:3 * W_GROUP, :], preferred_element_type=F32)
    acc += jnp.dot(md_ref[...], w_ref[3 * W_GROUP:4 * W_GROUP, :], preferred_element_type=F32)
    gate = mod_ref[0, :, 2 * D_MODEL:3 * D_MODEL]
    y = _load_x(x_refs, i, npb) + gate * acc
    if not final:
        o_refs[0][...] = y
        return
    y = y * lax.rsqrt(jnp.mean(y * y, axis=-1, keepdims=True) + NORM_EPS) * fg_ref[...]

    @pl.when(i < npb)
    def _():
        o_refs[0][...] = y

    @pl.when(i >= npb)
    def _():
        o_refs[1][...] = y


def _out_proj(xs, mixed, mod, w_out_bf, final_g, lay, tm, final):
    n = lay.n
    npb = lay.tp // tm
    lay_m = _Layout(lay.t_prompt, lay.tp // lay.t_prompt, lay.ts, (lay.n - lay.tp) // lay.ts, tm)
    mspec = pl.BlockSpec((tm, W_GROUP), lambda i: (i, 0))
    if final:
        out_specs = [pl.BlockSpec((tm, D_MODEL), lambda i: (jnp.minimum(i, npb - 1), 0)),
                     pl.BlockSpec((tm, D_MODEL), lambda i: (jnp.maximum(i - npb, 0), 0))]
        out_shape = [jax.ShapeDtypeStruct((lay.tp, D_MODEL), F32), jax.ShapeDtypeStruct((n - lay.tp, D_MODEL), F32)]
    else:
        out_specs = [pl.BlockSpec((tm, D_MODEL), lambda i: (i, 0))]
        out_shape = [jax.ShapeDtypeStruct((n, D_MODEL), F32)]
    out = pl.pallas_call(
        functools.partial(_out_proj_kernel, final, len(xs), npb),
        grid=(n // tm,),
        in_specs=_x_specs(len(xs), tm, npb, lambda i: i) + [mspec, mspec, mspec, mspec,
                  pl.BlockSpec((1, 1, 3 * D_MODEL), lambda i: (lay_m.batch_index(i, tm), 0, 0)),
                  pl.BlockSpec((4 * W_GROUP, D_MODEL), lambda i: (0, 0)),
                  pl.BlockSpec((1, D_MODEL), lambda i: (0, 0))],
        out_specs=out_specs,
        out_shape=out_shape,
        compiler_params=_cparams(("arbitrary",)),
        name="out_proj",
    )(*xs, *mixed, mod, w_out_bf, final_g.reshape(1, D_MODEL))
    return tuple(out)


A_ROWS = 32


def _mixer_a_kernel(lay, val_ref, glu_ref, z_ref, vp_ref, gp_ref, vn_ref, gn_ref,
                    cw_ref, cb_ref, lg_ref, lb_ref, o_ref, u_ref, ush_ref, acc_ref):
    tb = lay.tb
    blk = pl.program_id(0)
    keep_prev = jnp.where(lay.seq_start(blk), 0.0, 1.0)
    keep_next = jnp.where(lay.seq_end(blk), 0.0, 1.0)
    u_ref[0:HALO_A, :] = vp_ref[...].astype(F32) * _sigmoid(gp_ref[...].astype(F32)) * keep_prev
    u_ref[HALO_A:HALO_A + tb, :] = val_ref[...].astype(F32) * _sigmoid(glu_ref[...].astype(F32))
    u_ref[HALO_A + tb:2 * HALO_A + tb, :] = vn_ref[...].astype(F32) * _sigmoid(gn_ref[...].astype(F32)) * keep_next
    half = CONV_A_WIDTH // 2
    n_sh = tb + 2 * HALO_A - SUBLANES
    for r in range(1, SUBLANES):
        ush_ref[r - 1] = u_ref[r:r + n_sh, :]

    def tile(t, carry):
        r0 = pl.multiple_of(t * A_ROWS, A_ROWS)
        acc = jnp.zeros((A_ROWS // SUBLANES, SUBLANES, W_GROUP), F32) + cb_ref[...]
        for k in range(CONV_A_WIDTH):
            a, r = divmod(HALO_A - half + k, SUBLANES)
            win = pl.ds(r0 + a * SUBLANES, A_ROWS)
            src = u_ref[win, :] if r == 0 else ush_ref[r - 1, win, :]
            acc = acc + cw_ref[k] * src.reshape(A_ROWS // SUBLANES, SUBLANES, W_GROUP)
        acc_ref[pl.ds(r0, A_ROWS), :] = acc.reshape(A_ROWS, W_GROUP)
        return carry

    lax.fori_loop(0, tb // A_ROWS, tile, 0)
    acc = acc_ref[...]
    mu = jnp.mean(acc, axis=-1, keepdims=True)
    xc = acc - mu
    var = jnp.mean(xc * xc, axis=-1, keepdims=True)
    y = xc * lax.rsqrt(var + NORM_EPS) * lg_ref[...] + lb_ref[...]
    o_ref[...] = (_silu(y) * _silu(z_ref[...].astype(F32))).astype(BF16)


def _mixer_a(proj, conv_w, conv_b, ln_g, ln_b, lay):
    tb, nb = lay.tb, lay.nb
    hb = tb // HALO_A
    n_halo = lay.n // HALO_A

    def cur(cb):
        return pl.BlockSpec((tb, W_GROUP), lambda i: (i, cb))

    def prev(cb):
        return pl.BlockSpec((HALO_A, W_GROUP), lambda i: (jnp.maximum(i * hb - 1, 0), cb))

    def nxt(cb):
        return pl.BlockSpec((HALO_A, W_GROUP), lambda i: (jnp.minimum((i + 1) * hb, n_halo - 1), cb))

    vec = pl.BlockSpec((1, W_GROUP), lambda i: (0, 0))
    return pl.pallas_call(
        functools.partial(_mixer_a_kernel, lay),
        grid=(nb,),
        in_specs=[cur(CB_A_VAL), cur(CB_A_GLU), cur(CB_A_Z), prev(CB_A_VAL), prev(CB_A_GLU),
                  nxt(CB_A_VAL), nxt(CB_A_GLU),
                  pl.BlockSpec((CONV_A_WIDTH, SUBLANES, W_GROUP), lambda i: (0, 0, 0)), vec, vec, vec],
        out_specs=pl.BlockSpec((tb, W_GROUP), lambda i: (i, 0)),
        out_shape=jax.ShapeDtypeStruct((lay.n, W_GROUP), BF16),
        scratch_shapes=[pltpu.VMEM((tb + 2 * HALO_A, W_GROUP), F32),
                        pltpu.VMEM((SUBLANES - 1, tb + 2 * HALO_A - SUBLANES, W_GROUP), F32),
                        pltpu.VMEM((tb, W_GROUP), F32)],
        compiler_params=_cparams(("arbitrary",)),
        name="mixer_a_conv",
    )(proj, proj, proj, proj, proj, proj, proj,
      jnp.broadcast_to(conv_w[:, None, :], (CONV_A_WIDTH, SUBLANES, W_GROUP)), conv_b.reshape(1, W_GROUP),
      ln_g.reshape(1, W_GROUP), ln_b.reshape(1, W_GROUP))


def _blk_index(lay, rev):
    nb = lay.nb
    return (lambda j: nb - 1 - j) if rev else (lambda j: j)


def _reset_state(lay, rev, s_ref):
    blk = _blk_index(lay, rev)(pl.program_id(0))
    boundary = lay.seq_end(blk) if rev else lay.seq_start(blk)

    @pl.when(boundary)
    def _():
        s_ref[...] = jnp.zeros(s_ref.shape, s_ref.dtype)


def _sub_blocks(lay, rev, body, n_rows=SUB_BLOCK):
    nsb = lay.tb // n_rows

    def trip(i, carry):
        piece = (nsb - 1 - i) if rev else i
        body(pl.multiple_of(piece * n_rows, n_rows))
        return carry

    lax.fori_loop(0, nsb, trip, 0)


def _at(base, off, n):
    return pl.ds(pl.multiple_of(base + off, CHUNK), n)


def _finish(rev, base, o_ref, ob_ref, z_ref, gain_ref, out_ref, n_rows=SUB_BLOCK):
    rows = _at(base, 0, n_rows)
    if rev:
        out_ref[rows, :] = o_ref[...]
        return
    for h in range(N_HEADS):
        sl = slice(h * HEAD_DIM, (h + 1) * HEAD_DIM)
        o = o_ref[:, sl] + ob_ref[rows, sl]
        y = o * lax.rsqrt(jnp.mean(o * o, axis=-1, keepdims=True) + NORM_EPS) * gain_ref[:, sl]
        out_ref[rows, sl] = (y * _silu(z_ref[rows, sl].astype(F32))).astype(BF16)


def _causal_masks(rev, c):
    row = lax.broadcasted_iota(jnp.int32, (c, c), 0)
    col = lax.broadcasted_iota(jnp.int32, (c, c), 1)
    incl = (row <= col) if rev else (row >= col)
    strict = (row < col) if rev else (row > col)
    return row, col, incl, strict


def _ret_log_gamma(rev):
    lg = np.log1p(-np.exp2(-RET_DECAY_OFFSET - np.arange(N_HEADS, dtype=np.float64)))
    return lg[::-1].copy() if rev else lg


RET_CHUNK = 256
RET_PIECE = 512


def _ret_consts(rev):
    lg = _ret_log_gamma(rev).astype(np.float32)
    i = np.arange(RET_CHUNK, dtype=np.float32)
    idx = (RET_CHUNK - 1 - i) if rev else i
    rel = idx[:, None] - idx[None, :]
    dm = np.where(rel >= 0, np.exp(lg[:, None, None] * np.maximum(rel, 0.0)), 0.0).astype(np.float32)
    qd = np.exp(lg[:, None] * (idx + 1.0)).astype(np.float32)
    kd = np.exp(lg[:, None] * (RET_CHUNK - 1.0 - idx)).astype(np.float32)
    qd = np.broadcast_to(qd[:, :, None], (N_HEADS, RET_CHUNK, HEAD_DIM)).copy()
    kd = np.broadcast_to(kd[:, :, None], (N_HEADS, RET_CHUNK, HEAD_DIM)).copy()
    return jnp.asarray(dm), jnp.asarray(qd), jnp.asarray(kd)


def _ret_kernel(lay, rev, *refs):
    if rev:
        q_ref, k_ref, v_ref, cos_ref, sin_ref, dm_ref, qd_ref, kd_ref, out_ref, s_ref, qt_ref, kt_ref, o_ref = refs
        z_ref = ob_ref = gain_ref = None
    else:
        (q_ref, k_ref, v_ref, cos_ref, sin_ref, dm_ref, qd_ref, kd_ref, z_ref, ob_ref, gain_ref,
         out_ref, s_ref, qt_ref, kt_ref, o_ref) = refs
    nc = RET_PIECE // RET_CHUNK
    chunk_decay = [float(np.exp(np.float32(v) * RET_CHUNK)) for v in _ret_log_gamma(rev)]
    heads = [slice(h * HEAD_DIM, (h + 1) * HEAD_DIM) for h in range(N_HEADS)]
    _reset_state(lay, rev, s_ref)

    def piece(base):
        blk = _at(base, 0, RET_PIECE)
        cos = cos_ref[blk, :]
        sin = sin_ref[blk, :]
        for sl in heads:
            qh = q_ref[blk, sl].astype(F32)
            kh = k_ref[blk, sl].astype(F32)
            qt_ref[:, sl] = (qh * cos + pltpu.roll(qh, HEAD_DIM // 2, 1) * sin) * QK_SCALE
            kt_ref[:, sl] = kh * cos + pltpu.roll(kh, HEAD_DIM // 2, 1) * sin

        states = [s_ref[h] for h in range(N_HEADS)]
        for ci in range(nc):
            c = (nc - 1 - ci) if rev else ci
            rows = slice(c * RET_CHUNK, (c + 1) * RET_CHUNK)
            qcs = [qt_ref[rows, sl] for sl in heads]
            kcs = [kt_ref[rows, sl] for sl in heads]
            vcs = [v_ref[_at(base, c * RET_CHUNK, RET_CHUNK), sl].astype(BF16) for sl in heads]
            scores = [_dot_nt(q, k) for q, k in zip(qcs, kcs)]
            inters = [_dot(q * qd_ref[h], states[h]) for h, q in enumerate(qcs)]
            upds = [_dot_tn(k * kd_ref[h], v) for h, (k, v) in enumerate(zip(kcs, vcs))]
            intras = [jnp.dot((s * dm_ref[h]).astype(BF16), vcs[h], preferred_element_type=F32)
                      for h, s in enumerate(scores)]
            for h, sl in enumerate(heads):
                o_ref[rows, sl] = intras[h] + inters[h]
                states[h] = chunk_decay[h] * states[h] + upds[h]
        for h in range(N_HEADS):
            s_ref[h] = states[h]
        _finish(rev, base, o_ref, ob_ref, z_ref, gain_ref, out_ref, RET_PIECE)

    _sub_blocks(lay, rev, piece, RET_PIECE)


def _mixer_b(proj, cos2, sin2, o_bwd, gain, lay, rev):
    tb, nb = lay.tb, lay.nb
    bi = _blk_index(lay, rev)

    def cur(cb):
        return pl.BlockSpec((tb, W_GROUP), lambda j: (bi(j), cb))

    rot = pl.BlockSpec((tb, HEAD_DIM), lambda j: (lay.pos_block(bi(j)), 0))
    dm, qd, kd = _ret_consts(rev)
    in_specs = [cur(CB_B_Q), cur(CB_B_K), cur(CB_B_V), rot, rot,
                pl.BlockSpec((N_HEADS, RET_CHUNK, RET_CHUNK), lambda j: (0, 0, 0)),
                pl.BlockSpec((N_HEADS, RET_CHUNK, HEAD_DIM), lambda j: (0, 0, 0)),
                pl.BlockSpec((N_HEADS, RET_CHUNK, HEAD_DIM), lambda j: (0, 0, 0))]
    args = [proj, proj, proj, cos2, sin2, dm, qd, kd]
    if not rev:
        in_specs += [cur(CB_B_Z), pl.BlockSpec((tb, W_GROUP), lambda j: (bi(j), 0)),
                     pl.BlockSpec((1, W_GROUP), lambda j: (0, 0))]
        args += [proj, o_bwd, gain.reshape(1, W_GROUP)]
    return pl.pallas_call(
        functools.partial(_ret_kernel, lay, rev),
        grid=(nb,),
        in_specs=in_specs,
        out_specs=pl.BlockSpec((tb, W_GROUP), lambda j: (bi(j), 0)),
        out_shape=jax.ShapeDtypeStruct((lay.n, W_GROUP), F32 if rev else BF16),
        scratch_shapes=[pltpu.VMEM((N_HEADS, HEAD_DIM, HEAD_DIM), F32),
                        pltpu.VMEM((RET_PIECE, W_GROUP), F32), pltpu.VMEM((RET_PIECE, W_GROUP), F32),
                        pltpu.VMEM((RET_PIECE, W_GROUP), F32)],
        compiler_params=_cparams(("arbitrary",)),
        name="mixer_b_retention_bwd" if rev else "mixer_b_retention_fwd",
    )(*args)


HG_CHUNK = CHUNK
HGRN_LEVELS = tuple(HG_CHUNK >> n for n in range(1, HG_CHUNK.bit_length()))
LOG2_E = math.log2(math.e)
HGRN_GROUP = N_HEADS


def _hgrn_consts(rev):
    c = HG_CHUNK
    i = np.arange(c)[:, None]
    t = np.arange(c)[None, :]
    if rev:
        cum = t >= i
        rest = t < i
    else:
        cum = t <= i
        rest = t > i
    blocks = [cum, rest]
    masks = [np.eye(c, dtype=bool)]
    col = np.arange(c)[None, :]
    for s in HGRN_LEVELS:
        start = (i // (2 * s)) * (2 * s)
        later = (i - start) >= s
        if rev:
            ref = start + s
            p = np.where(later, (t >= ref) & (t < i), (t >= i) & (t < ref))
            q_half, k_half = 0, 1
        else:
            ref = start + s - 1
            p = np.where(later, (t > ref) & (t <= i), (t > i) & (t <= ref))
            q_half, k_half = 1, 0
        blocks.append(p)
        same = (i // (2 * s)) == (col // (2 * s))
        masks.append(same & ((i // s) % 2 == q_half) & ((col // s) % 2 == k_half))
    pst = np.concatenate(blocks, axis=0).astype(np.float32)
    pst = np.concatenate([pst, pst], axis=1)
    msk = np.stack(masks, axis=0).astype(np.float32)
    return jnp.asarray(pst, dtype=BF16), jnp.asarray(msk)


def _hgrn_kernel(lay, rev, *refs):
    if rev:
        (q_ref, f_ref, v_ref, llb_ref, l1lb_ref, pst_ref, msk_ref, out_ref,
         st_ref, qs_ref, g_ref, kk_ref, o_ref, e_all_ref, e_last_ref) = refs
        z_ref = ob_ref = gain_ref = None
    else:
        (q_ref, f_ref, v_ref, llb_ref, l1lb_ref, pst_ref, msk_ref, z_ref, ob_ref, gain_ref, out_ref,
         st_ref, qs_ref, g_ref, kk_ref, o_ref, e_all_ref, e_last_ref) = refs
    nc = SUB_BLOCK // HG_CHUNK
    n_lvl = len(HGRN_LEVELS)
    last_row = 0 if rev else HG_CHUNK - 1
    heads = [slice(h * HEAD_DIM, (h + 1) * HEAD_DIM) for h in range(N_HEADS)]
    _reset_state(lay, rev, st_ref)

    def piece(base):
        blk = _at(base, 0, SUB_BLOCK)
        qs_ref[...] = (_silu(q_ref[blk, :].astype(F32)) * QK_SCALE).astype(BF16)
        a = llb_ref[...]
        b = l1lb_ref[...] + _log_sigmoid(f_ref[blk, :].astype(F32))
        g2 = (jnp.maximum(a, b) + jnp.log(1.0 + jnp.exp(-jnp.abs(a - b)))) * LOG2_E
        g_ref[...] = g2
        kk_ref[...] = (1.0 - jnp.exp2(g2)).astype(BF16)

        pst = pst_ref[...]
        for ci in range(nc):
            c = (nc - 1 - ci) if rev else ci
            g_parts = jnp.concatenate(_split2(g_ref[c * HG_CHUNK:(c + 1) * HG_CHUNK, :]), axis=0)
            e_f32 = jnp.exp2(jnp.dot(pst, g_parts, preferred_element_type=F32))
            e_all_ref[ci] = e_f32.astype(BF16)
            e_last_ref[ci] = e_f32[last_row:last_row + 1, :]

        chains = [(ci, (nc - 1 - ci) if rev else ci, sl) for ci in range(nc) for sl in heads]

        def operands(ci, c, sl):
            rows = slice(c * HG_CHUNK, (c + 1) * HG_CHUNK)
            return (qs_ref[rows, sl], kk_ref[rows, sl], v_ref[_at(base, c * HG_CHUNK, HG_CHUNK), sl].astype(BF16),
                    e_all_ref.at[ci])

        attns, upds = [], []
        for first in range(0, len(chains), HGRN_GROUP):
            ops = [operands(*chain) for chain in chains[first:first + HGRN_GROUP]]
            parts = [[_dot_nt(q, k) for q, k, _, _ in ops]]
            for lvl in range(n_lvl):
                es = [e_ref[(2 + lvl) * HG_CHUNK:(3 + lvl) * HG_CHUNK, sl]
                      for (_, _, _, e_ref), (_, _, sl) in zip(ops, chains[first:])]
                parts.append([_dot_nt(q * e, k * e) for (q, k, _, _), e in zip(ops, es)])
            upds += [_dot_tn(v, k * e_ref[HG_CHUNK:2 * HG_CHUNK, sl])
                     for (_, k, v, e_ref), (_, _, sl) in zip(ops, chains[first:])]
            for h in range(HGRN_GROUP):
                attn = parts[0][h] * msk_ref[0]
                for lvl in range(n_lvl):
                    attn = attn + parts[1 + lvl][h] * msk_ref[1 + lvl]
                attns.append(attn.astype(BF16))
        states = [st_ref[h] for h in range(N_HEADS)]
        entering = []
        for idx, (ci, c, sl) in enumerate(chains):
            h = idx % N_HEADS
            entering.append(states[h])
            states[h] = states[h] * e_last_ref[ci, :, sl] + upds[idx]
        for h in range(N_HEADS):
            st_ref[h] = states[h]
        ops = [operands(*chain) for chain in chains]
        inters = [_dot_nt(q * e_ref[0:HG_CHUNK, sl], s)
                  for (q, _, _, e_ref), (_, _, sl), s in zip(ops, chains, entering)]
        intras = [jnp.dot(a, v, preferred_element_type=F32) for a, (_, _, v, _) in zip(attns, ops)]
        for (ci, c, sl), inter, intra in zip(chains, inters, intras):
            o_ref[c * HG_CHUNK:(c + 1) * HG_CHUNK, sl] = inter + intra
        _finish(rev, base, o_ref, ob_ref, z_ref, gain_ref, out_ref)

    _sub_blocks(lay, rev, piece)


def _mixer_c(proj, log_lb, log1m_lb, o_bwd, gain, lay, rev):
    tb, nb = lay.tb, lay.nb
    bi = _blk_index(lay, rev)

    def cur(cb):
        return pl.BlockSpec((tb, W_GROUP), lambda j: (bi(j), cb))

    vec = pl.BlockSpec((1, W_GROUP), lambda j: (0, 0))
    d = 1 if rev else 0
    pst, msk = _hgrn_consts(rev)
    n_stack = pst.shape[0]
    in_specs = [cur(CB_C_Q), cur(CB_C_FB if rev else CB_C_FF), cur(CB_C_I), vec, vec,
                pl.BlockSpec((n_stack, 2 * HG_CHUNK), lambda j: (0, 0)),
                pl.BlockSpec((msk.shape[0], HG_CHUNK, HG_CHUNK), lambda j: (0, 0, 0))]
    args = [proj, proj, proj, log_lb[d].reshape(1, W_GROUP), log1m_lb[d].reshape(1, W_GROUP), pst, msk]
    if not rev:
        in_specs += [cur(CB_C_Z), pl.BlockSpec((tb, W_GROUP), lambda j: (bi(j), 0)), vec]
        args += [proj, o_bwd, gain.reshape(1, W_GROUP)]
    big = pltpu.VMEM((SUB_BLOCK, W_GROUP), F32)
    return pl.pallas_call(
        functools.partial(_hgrn_kernel, lay, rev),
        grid=(nb,),
        in_specs=in_specs,
        out_specs=pl.BlockSpec((tb, W_GROUP), lambda j: (bi(j), 0)),
        out_shape=jax.ShapeDtypeStruct((lay.n, W_GROUP), F32 if rev else BF16),
        scratch_shapes=[pltpu.VMEM((N_HEADS, HEAD_DIM, HEAD_DIM), F32),
                        pltpu.VMEM((SUB_BLOCK, W_GROUP), BF16),
                        big, pltpu.VMEM((SUB_BLOCK, W_GROUP), BF16), big,
                        pltpu.VMEM((SUB_BLOCK // HG_CHUNK, n_stack, W_GROUP), BF16),
                        pltpu.VMEM((SUB_BLOCK // HG_CHUNK, 1, W_GROUP), F32)],
        compiler_params=_cparams(("arbitrary",)),
        name="mixer_c_hgrn2_bwd" if rev else "mixer_c_hgrn2_fwd",
    )(*args)


def _unit_tri_inverses(a_mats, row, col):
    c = a_mats[0].shape[0]
    eye = jnp.where(row == col, 1.0, 0.0)

    def same_block(size):
        return (row // size) == (col // size)

    base = 8
    m_base = same_block(base)
    ns = [jnp.where(m_base, -a, 0.0).astype(BF16) for a in a_mats]
    n2s = [jnp.dot(n, n, preferred_element_type=F32) for n in ns]
    xs = [eye + n.astype(F32) for n in ns]
    xs = [x + _dot(x, n2) for x, n2 in zip(xs, n2s)]
    n4s = [_dot(n2, n2) for n2 in n2s]
    xs = [x + _dot(x, n4) for x, n4 in zip(xs, n4s)]
    size = base
    while size < c:
        m_off = same_block(2 * size) & jnp.logical_not(same_block(size))
        xbs = [x.astype(BF16) for x in xs]
        ts = [jnp.dot(xb, jnp.where(m_off, a, 0.0).astype(BF16), preferred_element_type=F32)
              for xb, a in zip(xbs, a_mats)]
        xs = [x - jnp.dot(t.astype(BF16), xb, preferred_element_type=F32) for x, t, xb in zip(xs, ts, xbs)]
        size *= 2
    return xs


SHIFT_ROWS = 128


def _shift_mats():
    half = DN_CONV_WIDTH // 2
    t = np.arange(SHIFT_ROWS)[:, None]
    s = np.arange(SHIFT_ROWS + 2 * HALO_D)[None, :]
    mats = [(s == HALO_D + t + d) for d in range(-half, half + 1) if d != 0]
    return jnp.asarray(np.concatenate(mats, axis=0).astype(np.float32), dtype=BF16)


def _delta_pre_kernel(lay, q_ref, k_ref, v_ref, qp_ref, kp_ref, vp_ref, qn_ref, kn_ref, vn_ref, sm_ref,
                      cw_ref, alog_ref, dtb_ref, sh_ref, qo_ref, ko_ref, vo_ref, cum_ref, beta_ref, cumt_ref,
                      xq_ref, xk_ref, xv_ref):
    tb = lay.tb
    blk = pl.program_id(0)
    keep_prev = jnp.where(lay.seq_start(blk), 0.0, 1.0)
    keep_next = jnp.where(lay.seq_end(blk), 0.0, 1.0)
    half = DN_CONV_WIDTH // 2
    parts = ((q_ref, qp_ref, qn_ref, qo_ref, xq_ref), (k_ref, kp_ref, kn_ref, ko_ref, xk_ref),
             (v_ref, vp_ref, vn_ref, vo_ref, xv_ref))
    for c_ref, p_ref, n_ref, _, x_ref in parts:
        x_ref[0:HALO_D, :] = (p_ref[...].astype(F32) * keep_prev).astype(BF16)
        x_ref[HALO_D:HALO_D + tb, :] = c_ref[...]
        x_ref[HALO_D + tb:2 * HALO_D + tb, :] = (n_ref[...].astype(F32) * keep_next).astype(BF16)
    tiles = [(part, r0) for part in range(len(parts)) for r0 in range(0, tb, SHIFT_ROWS)]
    shifted = [jnp.dot(sh_ref[...], parts[part][4][r0:r0 + SHIFT_ROWS + 2 * HALO_D, :], preferred_element_type=F32)
               for part, r0 in tiles]
    for (part, r0), sh in zip(tiles, shifted):
        x_ref, dst_ref = parts[part][4], parts[part][3]
        wsl = slice(part * W_GROUP, (part + 1) * W_GROUP)
        acc = cw_ref[half:half + 1, wsl] * x_ref[HALO_D + r0:HALO_D + r0 + SHIFT_ROWS, :].astype(F32)
        for j, k in enumerate([k for k in range(DN_CONV_WIDTH) if k != half]):
            acc = acc + cw_ref[k:k + 1, wsl] * sh[j * SHIFT_ROWS:(j + 1) * SHIFT_ROWS, :]
        y = _silu(acc)
        rows = slice(r0, r0 + SHIFT_ROWS)
        if part < 2:
            for h in range(N_HEADS):
                sl = slice(h * HEAD_DIM, (h + 1) * HEAD_DIM)
                yh = y[:, sl]
                yn = yh * lax.rsqrt(jnp.sum(yh * yh, axis=-1, keepdims=True) + NORM_EPS)
                dst_ref[rows, sl] = (yn * QK_SCALE if part == 0 else yn).astype(BF16)
        else:
            dst_ref[rows, :] = y.astype(BF16)

    sm = sm_ref[...]
    g = -jnp.exp(alog_ref[...]) * _softplus(sm + dtb_ref[...])
    beta_ref[...] = _sigmoid(sm)

    row = lax.broadcasted_iota(jnp.int32, (CHUNK, CHUNK), 0)
    col = lax.broadcasted_iota(jnp.int32, (CHUNK, CHUNK), 1)
    lower = jnp.where(row >= col, 1.0, 0.0)
    upper = jnp.where(row <= col, 1.0, 0.0)
    fwd_lane = lax.broadcasted_iota(jnp.int32, (CHUNK, SMALL_W), 1) < N_HEADS
    eye_rows = jnp.where(lax.broadcasted_iota(jnp.int32, (SUBLANES, SMALL_W), 0)
                         == lax.broadcasted_iota(jnp.int32, (SUBLANES, SMALL_W), 1), 1.0, 0.0).astype(BF16)
    for c in range(tb // CHUNK):
        gc = g[c * CHUNK:(c + 1) * CHUNK, :]
        cum = jnp.where(fwd_lane, _dot_exact_lhs(lower, gc), _dot_exact_lhs(upper, gc))
        cum_ref[c * CHUNK:(c + 1) * CHUNK, :] = cum
        cum_t = None
        for piece in _split3(cum):
            t = lax.dot_general(eye_rows, piece, (((1,), (1,)), ((), ())), preferred_element_type=F32)
            cum_t = t if cum_t is None else cum_t + t
        cumt_ref[c] = cum_t


def _delta_kernel(lay, rev, *refs):
    if rev:
        (qc_ref, kc_ref, vc_ref, cum_ref, beta_ref, cumt_ref, out_ref,
         s_ref, o_ref, qd_ref, kd_ref, qk_ref, rhs_ref, last_ref, gq_ref, bm_ref) = refs
        z_ref = ob_ref = gain_ref = None
    else:
        (qc_ref, kc_ref, vc_ref, cum_ref, beta_ref, cumt_ref, z_ref, ob_ref, gain_ref, out_ref,
         s_ref, o_ref, qd_ref, kd_ref, qk_ref, rhs_ref, last_ref, gq_ref, bm_ref) = refs
    _reset_state(lay, rev, s_ref)
    named = (qc_ref, kc_ref, vc_ref, cum_ref, beta_ref, cumt_ref, z_ref, ob_ref, gain_ref, out_ref,
             s_ref, o_ref, qd_ref, kd_ref, qk_ref, rhs_ref, last_ref, gq_ref, bm_ref)
    _sub_blocks(lay, rev, functools.partial(_delta_piece, rev, named), DELTA_PIECE)


def _delta_piece(rev, named, base):
    (qc_ref, kc_ref, vc_ref, cum_ref, beta_ref, cumt_ref, z_ref, ob_ref, gain_ref, out_ref,
     s_ref, o_ref, qd_ref, kd_ref, qk_ref, rhs_ref, last_ref, gq_ref, bm_ref) = named
    nc = DELTA_PIECE // CHUNK
    row, col, incl, strict = _causal_masks(rev, CHUNK)
    lane0 = N_HEADS if rev else 0

    cums = [cum_ref[_at(base, c * CHUNK, CHUNK), :] for c in range(nc)]
    cum_ts = [cumt_ref[base // CHUNK + c] for c in range(nc)]
    raws = []
    for c in range(nc):
        rows = slice(c * CHUNK, (c + 1) * CHUNK)
        src = _at(base, c * CHUNK, CHUNK)
        beta_all = beta_ref[src, :]
        for h in range(N_HEADS):
            sl = slice(h * HEAD_DIM, (h + 1) * HEAD_DIM)
            qc = qc_ref[src, sl]
            kc = kc_ref[src, sl]
            vc = vc_ref[src, sl]
            ci_b = jnp.broadcast_to(cums[c][:, lane0 + h:lane0 + h + 1], (CHUNK, HEAD_DIM))
            b_lane = 2 * N_HEADS + lane0 + h
            beta_b = jnp.broadcast_to(beta_all[:, b_lane:b_lane + 1], (CHUNK, HEAD_DIM))
            k_beta = kc * beta_b
            raws.append(_dot_nt(jnp.concatenate([k_beta.astype(BF16), qc], axis=0), kc))
            e_ci = jnp.exp(ci_b)
            rhs_ref[c * N_HEADS + h] = jnp.concatenate([vc * beta_b, k_beta * e_ci], axis=1).astype(BF16)
            qd_ref[rows, sl] = qc * e_ci
            c_last = ci_b[0:1, :] if rev else ci_b[CHUNK - 1:CHUNK, :]
            kd_ref[rows, sl] = (kc * jnp.exp(c_last - ci_b)).astype(BF16)
            last_ref[c * N_HEADS + h] = jnp.broadcast_to(jnp.exp(c_last), (SUBLANES, HEAD_DIM))
    a_mats = []
    for idx, raw in enumerate(raws):
        c, h = divmod(idx, N_HEADS)
        ci_col = cums[c][:, lane0 + h:lane0 + h + 1]
        cj_row = cum_ts[c][lane0 + h:lane0 + h + 1, :]
        decay = jnp.exp(jnp.where(incl, ci_col - cj_row, NEG_BIG))
        a_mats.append(jnp.where(strict, raw[0:CHUNK] * decay, 0.0))
        qk_ref[idx] = (raw[CHUNK:2 * CHUNK] * decay).astype(BF16)

    t_invs = _unit_tri_inverses(a_mats, row, col)
    where = [(slice((idx // N_HEADS) * CHUNK, (idx // N_HEADS + 1) * CHUNK),
              slice((idx % N_HEADS) * HEAD_DIM, (idx % N_HEADS + 1) * HEAD_DIM)) for idx in range(len(t_invs))]
    uws = [jnp.dot(t_inv.astype(BF16), rhs_ref[idx], preferred_element_type=F32).astype(BF16)
           for idx, t_inv in enumerate(t_invs)]
    kuws = [lax.dot_general(kd_ref[rows, sl], uw, (((0,), (0,)), ((), ())), preferred_element_type=F32)
            for (rows, sl), uw in zip(where, uws)]
    quws = [jnp.dot(qk_ref[idx], uw, preferred_element_type=F32) for idx, uw in enumerate(uws)]
    for idx, ((rows, sl), kuw, quw) in enumerate(zip(where, kuws, quws)):
        bm_ref[idx] = kuw[:, 0:HEAD_DIM]
        gq_ref[idx, 0:HEAD_DIM, :] = kuw[:, HEAD_DIM:2 * HEAD_DIM].astype(BF16)
        gq_ref[idx, HEAD_DIM:HEAD_DIM + CHUNK, :] = (qd_ref[rows, sl] - quw[:, HEAD_DIM:2 * HEAD_DIM]).astype(BF16)
        o_ref[rows, sl] = quw[:, 0:HEAD_DIM]

    states = [s_ref[h] for h in range(N_HEADS)]
    for ci in range(nc):
        c = (nc - 1 - ci) if rev else ci
        rows = slice(c * CHUNK, (c + 1) * CHUNK)
        for h in range(N_HEADS):
            idx = c * N_HEADS + h
            sl = slice(h * HEAD_DIM, (h + 1) * HEAD_DIM)
            gs = jnp.dot(gq_ref[idx], states[h].astype(BF16), preferred_element_type=F32)
            o_ref[rows, sl] += gs[HEAD_DIM:HEAD_DIM + CHUNK]
            states[h] = states[h] * last_ref[idx][0:1, :] - gs[0:HEAD_DIM] + bm_ref[idx]
    for h in range(N_HEADS):
        s_ref[h] = states[h]
    _finish(rev, base, o_ref, ob_ref, z_ref, gain_ref, out_ref, DELTA_PIECE)


def _mixer_d_pre(proj, proj_small, conv_w, a_log, dt_bias, lay):
    tb, nb = lay.tb, lay.nb
    hb = tb // HALO_D
    n_halo = lay.n // HALO_D

    def cur(cb):
        return pl.BlockSpec((tb, W_GROUP), lambda i: (i, cb))

    def prev(cb):
        return pl.BlockSpec((HALO_D, W_GROUP), lambda i: (jnp.maximum(i * hb - 1, 0), cb))

    def nxt(cb):
        return pl.BlockSpec((HALO_D, W_GROUP), lambda i: (jnp.minimum((i + 1) * hb, n_halo - 1), cb))

    small = pl.BlockSpec((1, SMALL_W), lambda i: (0, 0))
    pad_lanes = SMALL_W - 2 * N_HEADS
    alog_row = jnp.pad(a_log.reshape(1, 2 * N_HEADS), ((0, 0), (0, pad_lanes)))
    dtb_row = jnp.pad(dt_bias.reshape(1, 2 * N_HEADS), ((0, 0), (0, pad_lanes)))
    wide = pl.BlockSpec((tb, W_GROUP), lambda i: (i, 0))
    narrow = pl.BlockSpec((tb, SMALL_W), lambda i: (i, 0))
    shifts = _shift_mats()
    return pl.pallas_call(
        functools.partial(_delta_pre_kernel, lay),
        grid=(nb,),
        in_specs=[cur(CB_D_Q), cur(CB_D_K), cur(CB_D_V), prev(CB_D_Q), prev(CB_D_K), prev(CB_D_V),
                  nxt(CB_D_Q), nxt(CB_D_K), nxt(CB_D_V),
                  pl.BlockSpec((tb, SMALL_W), lambda i: (i, 0)),
                  pl.BlockSpec((DN_CONV_WIDTH, 3 * W_GROUP), lambda i: (0, 0)), small, small,
                  pl.BlockSpec(shifts.shape, lambda i: (0, 0))],
        out_specs=[wide, wide, wide, narrow, narrow,
                   pl.BlockSpec((tb // CHUNK, SUBLANES, CHUNK), lambda i: (i, 0, 0))],
        out_shape=[jax.ShapeDtypeStruct((lay.n, W_GROUP), BF16)] * 3
        + [jax.ShapeDtypeStruct((lay.n, SMALL_W), F32)] * 2
        + [jax.ShapeDtypeStruct((lay.n // CHUNK, SUBLANES, CHUNK), F32)],
        scratch_shapes=[pltpu.VMEM((tb + 2 * HALO_D, W_GROUP), BF16)] * 3,
        compiler_params=_cparams(("arbitrary",)),
        name="mixer_d_deltanet_pre",
    )(*([proj] * 9), proj_small, conv_w, alog_row, dtb_row, shifts)


def _mixer_d(pre, proj, o_bwd, gain, lay, rev):
    tb, nb = lay.tb, lay.nb
    bi = _blk_index(lay, rev)
    wide = pl.BlockSpec((tb, W_GROUP), lambda j: (bi(j), 0))
    narrow = pl.BlockSpec((tb, SMALL_W), lambda j: (bi(j), 0))
    in_specs = [wide, wide, wide, narrow, narrow,
                pl.BlockSpec((tb // CHUNK, SUBLANES, CHUNK), lambda j: (bi(j), 0, 0))]
    args = list(pre)
    if not rev:
        in_specs += [pl.BlockSpec((tb, W_GROUP), lambda j: (bi(j), CB_D_Z)), wide,
                     pl.BlockSpec((1, W_GROUP), lambda j: (0, 0))]
        args += [proj, o_bwd, gain.reshape(1, W_GROUP)]
    big = pltpu.VMEM((DELTA_PIECE, W_GROUP), F32)
    nc = DELTA_PIECE // CHUNK
    return pl.pallas_call(
        functools.partial(_delta_kernel, lay, rev),
        grid=(nb,),
        in_specs=in_specs,
        out_specs=wide,
        out_shape=jax.ShapeDtypeStruct((lay.n, W_GROUP), F32 if rev else BF16),
        scratch_shapes=[pltpu.VMEM((N_HEADS, HEAD_DIM, HEAD_DIM), F32),
                        big,
                        big,
                        pltpu.VMEM((DELTA_PIECE, W_GROUP), BF16),
                        pltpu.VMEM((nc * N_HEADS, CHUNK, CHUNK), BF16),
                        pltpu.VMEM((nc * N_HEADS, CHUNK, 2 * HEAD_DIM), BF16),
                        pltpu.VMEM((nc * N_HEADS, SUBLANES, HEAD_DIM), F32),
                        pltpu.VMEM((nc * N_HEADS, HEAD_DIM + CHUNK, HEAD_DIM), BF16),
                        pltpu.VMEM((nc * N_HEADS, HEAD_DIM, HEAD_DIM), F32)],
        compiler_params=_cparams(("arbitrary",)),
        name="mixer_d_deltanet_bwd" if rev else "mixer_d_deltanet_fwd",
    )(*args)


def _regroup_w_in(w):
    d_qkv_end = 15 * W_GROUP
    small = w[:, d_qkv_end:d_qkv_end + 4 * N_HEADS]
    d_z = w[:, d_qkv_end + 4 * N_HEADS:]
    pad = jnp.zeros((w.shape[0], SMALL_W - 4 * N_HEADS), w.dtype)
    main = jnp.concatenate([w[:, :d_qkv_end], d_z], axis=1).astype(BF16)
    return main, jnp.concatenate([small, pad], axis=1).astype(BF16)


def _rotary_tables(t_max):
    half = HEAD_DIM // 2
    inv = 1.0 / (ROPE_BASE ** (jnp.arange(half, dtype=F32) / half))
    ang = jnp.arange(t_max, dtype=F32)[:, None] * inv[None, :]
    cos, sin = jnp.cos(ang), jnp.sin(ang)
    return jnp.concatenate([cos, cos], axis=-1), jnp.concatenate([-sin, sin], axis=-1)


def _pick_tile(t_prompt, t_sample, want):
    tile = want
    while t_prompt % tile or t_sample % tile:
        tile //= 2
    return tile


def kernel(x_prompt, x_sample, c_prompt, c_sample, ada_w, ada_b, norm_g, w_in, conv_a_w, conv_a_b, ln_a_g, ln_a_b,
           ret_norm_g, hgrn_lb_logits, hgrn_norm_g, dn_conv_w, dn_a_log, dn_dt_bias, dn_norm_g, w_out, final_g):
    depth = w_in.shape[0]
    n_prompt, t_prompt, _ = x_prompt.shape
    n_sample, t_sample, _ = x_sample.shape
    tm_in = _pick_tile(t_prompt, t_sample, 1024)
    tm_out = _pick_tile(t_prompt, t_sample, 512)
    lay = _Layout(t_prompt, n_prompt, t_sample, n_sample, _pick_tile(t_prompt, t_sample, 1024))
    lay_a = _Layout(t_prompt, n_prompt, t_sample, n_sample, _pick_tile(t_prompt, t_sample, 512))
    lay_dp = _Layout(t_prompt, n_prompt, t_sample, n_sample, _pick_tile(t_prompt, t_sample, 1024))

    xs = (x_prompt.reshape(-1, D_MODEL), x_sample.reshape(-1, D_MODEL))
    c_all = jnp.concatenate([c_prompt, c_sample], axis=0)
    n_c = c_all.shape[0]
    c_all = jnp.pad(c_all, ((0, (-n_c) % SUBLANES), (0, 0)))
    mod = _modulation(c_all, ada_w, ada_b)

    lb = jnp.cumsum(jax.nn.softmax(hgrn_lb_logits.astype(F32), axis=0), axis=0)
    lb = lb - lb[:1]
    log_lb = jnp.log(lb)
    log1m_lb = jnp.log1p(-lb)
    cos2, sin2 = _rotary_tables(max(t_prompt, t_sample))

    for l in range(depth):
        mod_l = mod[l].reshape(mod.shape[1], 1, 3 * D_MODEL)
        proj, proj_small = _in_proj(xs, mod_l, norm_g[l], *_regroup_w_in(w_in[l]), lay, tm_in)
        m_a = _mixer_a(proj, conv_a_w[l], conv_a_b[l], ln_a_g[l], ln_a_b[l], lay_a)
        ob = _mixer_b(proj, cos2, sin2, None, None, lay, True)
        m_b = _mixer_b(proj, cos2, sin2, ob, ret_norm_g[l], lay, False)
        oc = _mixer_c(proj, log_lb[l], log1m_lb[l], None, None, lay, True)
        m_c = _mixer_c(proj, log_lb[l], log1m_lb[l], oc, hgrn_norm_g[l], lay, False)
        d_pre = _mixer_d_pre(proj, proj_small, dn_conv_w[l], dn_a_log[l], dn_dt_bias[l], lay_dp)
        od = _mixer_d(d_pre, proj, None, None, lay, True)
        m_d = _mixer_d(d_pre, proj, od, dn_norm_g[l], lay, False)
        xs = _out_proj(xs, (m_a, m_b, m_c, m_d), mod_l, w_out[l].astype(BF16), final_g, lay, tm_out,
                       final=(l == depth - 1))

    return (xs[0].reshape(n_prompt, t_prompt, D_MODEL), xs[1].reshape(n_sample, t_sample, D_MODEL))
```

```python
import functools
import math

import numpy as np
import jax
import jax.numpy as jnp
from jax import lax
from jax.experimental import pallas as pl
from jax.experimental.pallas import tpu as pltpu

F32 = jnp.float32
BF16 = jnp.bfloat16

D_MODEL = 1024
W_GROUP = 512
HEAD_DIM = 128
SUBLANES = 8
N_HEADS = 4
CONV_A_WIDTH = 31
DN_CONV_WIDTH = 5
CHUNK = 64
ROPE_BASE = 10000.0
RET_DECAY_OFFSET = 5.0
NORM_EPS = 1e-6
QK_SCALE = HEAD_DIM ** -0.5
NEG_BIG = -1e30

N_COL_BLOCKS = 16
SMALL_W = 128
SUB_BLOCK = 256
DELTA_PIECE = 512
D_PROJ = N_COL_BLOCKS * W_GROUP
PROJ_TN = 2048
HALO_A = 16
HALO_D = 16
VMEM_LIMIT = 56 * 1024 * 1024

CB_A_VAL, CB_A_GLU, CB_A_Z = 0, 1, 2
CB_B_Q, CB_B_K, CB_B_V, CB_B_Z = 3, 4, 5, 6
CB_C_Q, CB_C_FF, CB_C_FB, CB_C_I, CB_C_Z = 7, 8, 9, 10, 11
CB_D_Q, CB_D_K, CB_D_V, CB_D_Z = 12, 13, 14, 15


def _dot(a, b):
    return jnp.dot(a.astype(BF16), b.astype(BF16), preferred_element_type=F32)


def _dot_nt(a, b):
    return lax.dot_general(a.astype(BF16), b.astype(BF16), (((1,), (1,)), ((), ())), preferred_element_type=F32)


def _dot_tn(a, b):
    return lax.dot_general(a.astype(BF16), b.astype(BF16), (((0,), (0,)), ((), ())), preferred_element_type=F32)


def _split2(x):
    hi = x.astype(BF16)
    lo = (x - hi.astype(F32)).astype(BF16)
    return hi, lo


def _split3(x):
    hi = x.astype(BF16)
    r = x - hi.astype(F32)
    mid = r.astype(BF16)
    lo = (r - mid.astype(F32)).astype(BF16)
    return hi, mid, lo


def _dot_exact_lhs(m, x):
    mb = m.astype(BF16)
    acc = None
    for part in _split3(x):
        t = jnp.dot(mb, part, preferred_element_type=F32)
        acc = t if acc is None else acc + t
    return acc


def _sigmoid(x):
    return 0.5 * jnp.tanh(0.5 * x) + 0.5


def _silu(x):
    return x * _sigmoid(x)


def _softplus(x):
    return jnp.maximum(x, 0.0) + jnp.log(1.0 + jnp.exp(-jnp.abs(x)))


def _log_sigmoid(x):
    return jnp.minimum(x, 0.0) - jnp.log(1.0 + jnp.exp(-jnp.abs(x)))


class _Layout:
    def __init__(self, t_prompt, n_prompt, t_sample, n_sample, tb):
        self.tp = t_prompt * n_prompt
        self.t_prompt = t_prompt
        self.ts = t_sample
        self.n = self.tp + t_sample * n_sample
        self.tb = tb
        assert t_prompt % tb == 0 and t_sample % tb == 0
        self.nb = self.n // tb

    def seq_start(self, blk):
        s = blk * self.tb
        return jnp.where(s < self.tp, s % self.t_prompt == 0, (s - self.tp) % self.ts == 0)

    def seq_end(self, blk):
        e = (blk + 1) * self.tb
        return jnp.where(e <= self.tp, e % self.t_prompt == 0, (e - self.tp) % self.ts == 0)

    def pos_block(self, blk):
        s = blk * self.tb
        return jnp.where(s < self.tp, (s % self.t_prompt) // self.tb, ((s - self.tp) % self.ts) // self.tb)

    def batch_index(self, blk, rows):
        s = blk * rows
        return jnp.where(s < self.tp, s // self.t_prompt, self.tp // self.t_prompt + (s - self.tp) // self.ts)


def _cparams(sem):
    return pltpu.CompilerParams(dimension_semantics=sem, vmem_limit_bytes=VMEM_LIMIT)


def _mod_kernel(c_ref, w_ref, b_ref, o_ref):
    c = c_ref[...]
    o_ref[0] = jnp.dot(_silu(c), w_ref[0], preferred_element_type=F32,
                       precision=lax.Precision.HIGHEST) + b_ref[0]


def _modulation(c_all, ada_w, ada_b):
    depth = ada_w.shape[0]
    nb = c_all.shape[0]
    return pl.pallas_call(
        _mod_kernel,
        grid=(depth,),
        in_specs=[pl.BlockSpec((nb, D_MODEL), lambda l: (0, 0)),
                  pl.BlockSpec((1, D_MODEL, 3 * D_MODEL), lambda l: (l, 0, 0)),
                  pl.BlockSpec((1, 1, 3 * D_MODEL), lambda l: (l, 0, 0))],
        out_specs=pl.BlockSpec((1, nb, 3 * D_MODEL), lambda l: (l, 0, 0)),
        out_shape=jax.ShapeDtypeStruct((depth, nb, 3 * D_MODEL), F32),
        compiler_params=_cparams(("arbitrary",)),
        name="adaln_modulation",
    )(c_all, ada_w, ada_b.reshape(depth, 1, 3 * D_MODEL))


def _load_x(x_refs, i, npb):
    if len(x_refs) == 1:
        return x_refs[0][...]
    return jnp.where(i < npb, x_refs[0][...], x_refs[1][...])


def _x_specs(n_x, tm, npb, row_block):
    def spec(fn):
        return pl.BlockSpec((tm, D_MODEL), lambda *ids: (fn(row_block(*ids)), 0))

    if n_x == 1:
        return [spec(lambda i: i)]
    return [spec(lambda i: jnp.minimum(i, npb - 1)), spec(lambda i: jnp.maximum(i - npb, 0))]


def _in_proj_kernel(n_x, npb, *refs):
    x_refs = refs[:n_x]
    mod_ref, g_ref, w_ref, ws_ref, o_ref, os_ref, h_ref = refs[n_x:]

    @pl.when(pl.program_id(1) == 0)
    def _():
        x = _load_x(x_refs, pl.program_id(0), npb)
        y = x * lax.rsqrt(jnp.mean(x * x, axis=-1, keepdims=True) + NORM_EPS) * g_ref[...]
        shift = mod_ref[0, :, 0:D_MODEL]
        scale = mod_ref[0, :, D_MODEL:2 * D_MODEL]
        h = (y * (1.0 + scale) + shift).astype(BF16)
        h_ref[...] = h
        os_ref[...] = jnp.dot(h, ws_ref[...], preferred_element_type=F32)

    o_ref[...] = jnp.dot(h_ref[...], w_ref[...], preferred_element_type=F32).astype(BF16)


def _in_proj(xs, mod, norm_g, w_main, w_small, lay, tm):
    n = lay.n
    npb = lay.tp // tm
    lay_m = _Layout(lay.t_prompt, lay.tp // lay.t_prompt, lay.ts, (lay.n - lay.tp) // lay.ts, tm)
    return pl.pallas_call(
        functools.partial(_in_proj_kernel, len(xs), npb),
        grid=(n // tm, D_PROJ // PROJ_TN),
        in_specs=_x_specs(len(xs), tm, npb, lambda i, j: i) + [
                  pl.BlockSpec((1, 1, 3 * D_MODEL), lambda i, j: (lay_m.batch_index(i, tm), 0, 0)),
                  pl.BlockSpec((1, D_MODEL), lambda i, j: (0, 0)),
                  pl.BlockSpec((D_MODEL, PROJ_TN), lambda i, j: (0, j)),
                  pl.BlockSpec((D_MODEL, SMALL_W), lambda i, j: (0, 0))],
        out_specs=[pl.BlockSpec((tm, PROJ_TN), lambda i, j: (i, j)),
                   pl.BlockSpec((tm, SMALL_W), lambda i, j: (i, 0))],
        out_shape=[jax.ShapeDtypeStruct((n, D_PROJ), BF16), jax.ShapeDtypeStruct((n, SMALL_W), F32)],
        scratch_shapes=[pltpu.VMEM((tm, D_MODEL), BF16)],
        compiler_params=_cparams(("arbitrary", "arbitrary")),
        name="in_proj",
    )(*xs, mod, norm_g.reshape(1, D_MODEL), w_main, w_small)


def _out_proj_kernel(final, n_x, npb, *refs):
    x_refs = refs[:n_x]
    ca_ref, za_ref, mb_ref, mc_ref, md_ref, mod_ref, w_ref, fg_ref, lg_ref, lb_ref = refs[n_x:n_x + 10]
    o_refs = refs[n_x + 10:]
    i = pl.program_id(0)
    conv = ca_ref[...]
    mu = jnp.mean(conv, axis=-1, keepdims=True)
    xc = conv - mu
    var = jnp.mean(xc * xc, axis=-1, keepdims=True)
    y = xc * lax.rsqrt(var + NORM_EPS) * lg_ref[...] + lb_ref[...]
    m_a = (_silu(y) * _silu(za_ref[...].astype(F32))).astype(BF16)
    acc = jnp.dot(m_a, w_ref[0:W_GROUP, :], preferred_element_type=F32)
    acc += jnp.dot(mb_ref[...], w_ref[W_GROUP:2 * W_GROUP, :], preferred_element_type=F32)
    acc += jnp.dot(mc_ref[...], w_ref[2 * W_GROUP:3 * W_GROUP, :], preferred_element_type=F32)
    acc += jnp.dot(md_ref[...], w_ref[3 * W_GROUP:4 * W_GROUP, :], preferred_element_type=F32)
    gate = mod_ref[0, :, 2 * D_MODEL:3 * D_MODEL]
    y = _load_x(x_refs, i, npb) + gate * acc
    if not final:
        o_refs[0][...] = y
        return
    y = y * lax.rsqrt(jnp.mean(y * y, axis=-1, keepdims=True) + NORM_EPS) * fg_ref[...]

    @pl.when(i < npb)
    def _():
        o_refs[0][...] = y

    @pl.when(i >= npb)
    def _():
        o_refs[1][...] = y


def _out_proj(xs, mixed, mod, w_out_bf, final_g, ln_g, ln_b, lay, tm, final):
    n = lay.n
    npb = lay.tp // tm
    lay_m = _Layout(lay.t_prompt, lay.tp // lay.t_prompt, lay.ts, (lay.n - lay.tp) // lay.ts, tm)
    mspec = pl.BlockSpec((tm, W_GROUP), lambda i: (i, 0))
    if final:
        out_specs = [pl.BlockSpec((tm, D_MODEL), lambda i: (jnp.minimum(i, npb - 1), 0)),
                     pl.BlockSpec((tm, D_MODEL), lambda i: (jnp.maximum(i - npb, 0), 0))]
        out_shape = [jax.ShapeDtypeStruct((lay.tp, D_MODEL), F32), jax.ShapeDtypeStruct((n - lay.tp, D_MODEL), F32)]
    else:
        out_specs = [pl.BlockSpec((tm, D_MODEL), lambda i: (i, 0))]
        out_shape = [jax.ShapeDtypeStruct((n, D_MODEL), F32)]
    out = pl.pallas_call(
        functools.partial(_out_proj_kernel, final, len(xs), npb),
        grid=(n // tm,),
        in_specs=_x_specs(len(xs), tm, npb, lambda i: i) + [
                  mspec, pl.BlockSpec((tm, W_GROUP), lambda i: (i, CB_A_Z)), mspec, mspec, mspec,
                  pl.BlockSpec((1, 1, 3 * D_MODEL), lambda i: (lay_m.batch_index(i, tm), 0, 0)),
                  pl.BlockSpec((4 * W_GROUP, D_MODEL), lambda i: (0, 0)),
                  pl.BlockSpec((1, D_MODEL), lambda i: (0, 0)),
                  pl.BlockSpec((1, W_GROUP), lambda i: (0, 0)), pl.BlockSpec((1, W_GROUP), lambda i: (0, 0))],
        out_specs=out_specs,
        out_shape=out_shape,
        compiler_params=_cparams(("arbitrary",)),
        name="out_proj",
    )(*xs, *mixed, mod, w_out_bf, final_g.reshape(1, D_MODEL), ln_g.reshape(1, W_GROUP), ln_b.reshape(1, W_GROUP))
    return tuple(out)


A_ROWS = 32


def _mixer_a_kernel(lay, val_ref, glu_ref, vp_ref, gp_ref, vn_ref, gn_ref, cw_ref, cb_ref, o_ref, u_ref, ush_ref):
    tb = lay.tb
    blk = pl.program_id(0)
    keep_prev = jnp.where(lay.seq_start(blk), 0.0, 1.0)
    keep_next = jnp.where(lay.seq_end(blk), 0.0, 1.0)
    u_ref[0:HALO_A, :] = vp_ref[...].astype(F32) * _sigmoid(gp_ref[...].astype(F32)) * keep_prev
    u_ref[HALO_A:HALO_A + tb, :] = val_ref[...].astype(F32) * _sigmoid(glu_ref[...].astype(F32))
    u_ref[HALO_A + tb:2 * HALO_A + tb, :] = vn_ref[...].astype(F32) * _sigmoid(gn_ref[...].astype(F32)) * keep_next
    half = CONV_A_WIDTH // 2
    n_sh = tb + 2 * HALO_A - SUBLANES
    for r in range(1, SUBLANES):
        ush_ref[r - 1] = u_ref[r:r + n_sh, :]

    def tile(t, carry):
        r0 = pl.multiple_of(t * A_ROWS, A_ROWS)
        acc = jnp.zeros((A_ROWS // SUBLANES, SUBLANES, W_GROUP), F32) + cb_ref[...]
        for k in range(CONV_A_WIDTH):
            a, r = divmod(HALO_A - half + k, SUBLANES)
            win = pl.ds(r0 + a * SUBLANES, A_ROWS)
            src = u_ref[win, :] if r == 0 else ush_ref[r - 1, win, :]
            acc = acc + cw_ref[k] * src.reshape(A_ROWS // SUBLANES, SUBLANES, W_GROUP)
        o_ref[pl.ds(r0, A_ROWS), :] = acc.reshape(A_ROWS, W_GROUP)
        return carry

    lax.fori_loop(0, tb // A_ROWS, tile, 0)


def _mixer_a(proj, conv_w, conv_b, lay):
    tb, nb = lay.tb, lay.nb
    hb = tb // HALO_A
    n_halo = lay.n // HALO_A

    def cur(cb):
        return pl.BlockSpec((tb, W_GROUP), lambda i: (i, cb))

    def prev(cb):
        return pl.BlockSpec((HALO_A, W_GROUP), lambda i: (jnp.maximum(i * hb - 1, 0), cb))

    def nxt(cb):
        return pl.BlockSpec((HALO_A, W_GROUP), lambda i: (jnp.minimum((i + 1) * hb, n_halo - 1), cb))

    vec = pl.BlockSpec((1, W_GROUP), lambda i: (0, 0))
    return pl.pallas_call(
        functools.partial(_mixer_a_kernel, lay),
        grid=(nb,),
        in_specs=[cur(CB_A_VAL), cur(CB_A_GLU), prev(CB_A_VAL), prev(CB_A_GLU), nxt(CB_A_VAL), nxt(CB_A_GLU),
                  pl.BlockSpec((CONV_A_WIDTH, SUBLANES, W_GROUP), lambda i: (0, 0, 0)), vec],
        out_specs=pl.BlockSpec((tb, W_GROUP), lambda i: (i, 0)),
        out_shape=jax.ShapeDtypeStruct((lay.n, W_GROUP), F32),
        scratch_shapes=[pltpu.VMEM((tb + 2 * HALO_A, W_GROUP), F32),
                        pltpu.VMEM((SUBLANES - 1, tb + 2 * HALO_A - SUBLANES, W_GROUP), F32)],
        compiler_params=_cparams(("arbitrary",)),
        name="mixer_a_conv",
    )(proj, proj, proj, proj, proj, proj,
      jnp.broadcast_to(conv_w[:, None, :], (CONV_A_WIDTH, SUBLANES, W_GROUP)), conv_b.reshape(1, W_GROUP))


def _blk_index(lay, rev):
    nb = lay.nb
    return (lambda j: nb - 1 - j) if rev else (lambda j: j)


def _reset_state(lay, rev, s_ref):
    blk = _blk_index(lay, rev)(pl.program_id(0))
    boundary = lay.seq_end(blk) if rev else lay.seq_start(blk)

    @pl.when(boundary)
    def _():
        s_ref[...] = jnp.zeros(s_ref.shape, s_ref.dtype)


def _sub_blocks(lay, rev, body, n_rows=SUB_BLOCK):
    nsb = lay.tb // n_rows

    def trip(i, carry):
        piece = (nsb - 1 - i) if rev else i
        body(pl.multiple_of(piece * n_rows, n_rows))
        return carry

    lax.fori_loop(0, nsb, trip, 0)


def _at(base, off, n):
    return pl.ds(pl.multiple_of(base + off, CHUNK), n)


def _finish(rev, base, o_ref, ob_ref, z_ref, gain_ref, out_ref, n_rows=SUB_BLOCK):
    rows = _at(base, 0, n_rows)
    if rev:
        out_ref[rows, :] = o_ref[...]
        return
    for h in range(N_HEADS):
        sl = slice(h * HEAD_DIM, (h + 1) * HEAD_DIM)
        o = o_ref[:, sl] + ob_ref[rows, sl]
        y = o * lax.rsqrt(jnp.mean(o * o, axis=-1, keepdims=True) + NORM_EPS) * gain_ref[:, sl]
        out_ref[rows, sl] = (y * _silu(z_ref[rows, sl].astype(F32))).astype(BF16)


def _causal_masks(rev, c):
    row = lax.broadcasted_iota(jnp.int32, (c, c), 0)
    col = lax.broadcasted_iota(jnp.int32, (c, c), 1)
    incl = (row <= col) if rev else (row >= col)
    strict = (row < col) if rev else (row > col)
    return row, col, incl, strict


def _ret_log_gamma(rev):
    lg = np.log1p(-np.exp2(-RET_DECAY_OFFSET - np.arange(N_HEADS, dtype=np.float64)))
    return lg[::-1].copy() if rev else lg


RET_CHUNK = 256
RET_PIECE = 512


def _ret_consts(rev):
    lg = _ret_log_gamma(rev).astype(np.float32)
    i = np.arange(RET_CHUNK, dtype=np.float32)
    idx = (RET_CHUNK - 1 - i) if rev else i
    rel = idx[:, None] - idx[None, :]
    dm = np.where(rel >= 0, np.exp(lg[:, None, None] * np.maximum(rel, 0.0)), 0.0).astype(np.float32)
    qd = np.exp(lg[:, None] * (idx + 1.0)).astype(np.float32)
    kd = np.exp(lg[:, None] * (RET_CHUNK - 1.0 - idx)).astype(np.float32)
    qd = np.broadcast_to(qd[:, :, None], (N_HEADS, RET_CHUNK, HEAD_DIM)).copy()
    kd = np.broadcast_to(kd[:, :, None], (N_HEADS, RET_CHUNK, HEAD_DIM)).copy()
    return jnp.asarray(dm), jnp.asarray(qd), jnp.asarray(kd)


def _ret_kernel(lay, rev, *refs):
    if rev:
        q_ref, k_ref, v_ref, cos_ref, sin_ref, dm_ref, qd_ref, kd_ref, out_ref, s_ref, qt_ref, kt_ref, o_ref = refs
        z_ref = ob_ref = gain_ref = None
    else:
        (q_ref, k_ref, v_ref, cos_ref, sin_ref, dm_ref, qd_ref, kd_ref, z_ref, ob_ref, gain_ref,
         out_ref, s_ref, qt_ref, kt_ref, o_ref) = refs
    nc = RET_PIECE // RET_CHUNK
    chunk_decay = [float(np.exp(np.float32(v) * RET_CHUNK)) for v in _ret_log_gamma(rev)]
    heads = [slice(h * HEAD_DIM, (h + 1) * HEAD_DIM) for h in range(N_HEADS)]
    _reset_state(lay, rev, s_ref)

    def piece(base):
        blk = _at(base, 0, RET_PIECE)
        cos = cos_ref[blk, :]
        sin = sin_ref[blk, :]
        for sl in heads:
            qh = q_ref[blk, sl].astype(F32)
            kh = k_ref[blk, sl].astype(F32)
            qt_ref[:, sl] = (qh * cos + pltpu.roll(qh, HEAD_DIM // 2, 1) * sin) * QK_SCALE
            kt_ref[:, sl] = kh * cos + pltpu.roll(kh, HEAD_DIM // 2, 1) * sin

        states = [s_ref[h] for h in range(N_HEADS)]
        for ci in range(nc):
            c = (nc - 1 - ci) if rev else ci
            rows = slice(c * RET_CHUNK, (c + 1) * RET_CHUNK)
            qcs = [qt_ref[rows, sl] for sl in heads]
            kcs = [kt_ref[rows, sl] for sl in heads]
            vcs = [v_ref[_at(base, c * RET_CHUNK, RET_CHUNK), sl].astype(BF16) for sl in heads]
            scores = [_dot_nt(q, k) for q, k in zip(qcs, kcs)]
            inters = [_dot(q * qd_ref[h], states[h]) for h, q in enumerate(qcs)]
            upds = [_dot_tn(k * kd_ref[h], v) for h, (k, v) in enumerate(zip(kcs, vcs))]
            intras = [jnp.dot((s * dm_ref[h]).astype(BF16), vcs[h], preferred_element_type=F32)
                      for h, s in enumerate(scores)]
            for h, sl in enumerate(heads):
                o_ref[rows, sl] = intras[h] + inters[h]
                states[h] = chunk_decay[h] * states[h] + upds[h]
        for h in range(N_HEADS):
            s_ref[h] = states[h]
        _finish(rev, base, o_ref, ob_ref, z_ref, gain_ref, out_ref, RET_PIECE)

    _sub_blocks(lay, rev, piece, RET_PIECE)


def _mixer_b(proj, cos2, sin2, o_bwd, gain, lay, rev):
    tb, nb = lay.tb, lay.nb
    bi = _blk_index(lay, rev)

    def cur(cb):
        return pl.BlockSpec((tb, W_GROUP), lambda j: (bi(j), cb))

    rot = pl.BlockSpec((tb, HEAD_DIM), lambda j: (lay.pos_block(bi(j)), 0))
    dm, qd, kd = _ret_consts(rev)
    in_specs = [cur(CB_B_Q), cur(CB_B_K), cur(CB_B_V), rot, rot,
                pl.BlockSpec((N_HEADS, RET_CHUNK, RET_CHUNK), lambda j: (0, 0, 0)),
                pl.BlockSpec((N_HEADS, RET_CHUNK, HEAD_DIM), lambda j: (0, 0, 0)),
                pl.BlockSpec((N_HEADS, RET_CHUNK, HEAD_DIM), lambda j: (0, 0, 0))]
    args = [proj, proj, proj, cos2, sin2, dm, qd, kd]
    if not rev:
        in_specs += [cur(CB_B_Z), pl.BlockSpec((tb, W_GROUP), lambda j: (bi(j), 0)),
                     pl.BlockSpec((1, W_GROUP), lambda j: (0, 0))]
        args += [proj, o_bwd, gain.reshape(1, W_GROUP)]
    return pl.pallas_call(
        functools.partial(_ret_kernel, lay, rev),
        grid=(nb,),
        in_specs=in_specs,
        out_specs=pl.BlockSpec((tb, W_GROUP), lambda j: (bi(j), 0)),
        out_shape=jax.ShapeDtypeStruct((lay.n, W_GROUP), F32 if rev else BF16),
        scratch_shapes=[pltpu.VMEM((N_HEADS, HEAD_DIM, HEAD_DIM), F32),
                        pltpu.VMEM((RET_PIECE, W_GROUP), F32), pltpu.VMEM((RET_PIECE, W_GROUP), F32),
                        pltpu.VMEM((RET_PIECE, W_GROUP), F32)],
        compiler_params=_cparams(("arbitrary",)),
        name="mixer_b_retention_bwd" if rev else "mixer_b_retention_fwd",
    )(*args)


HG_CHUNK = CHUNK
HGRN_LEVELS = tuple(HG_CHUNK >> n for n in range(1, HG_CHUNK.bit_length()))
LOG2_E = math.log2(math.e)
HGRN_GROUP = N_HEADS


def _hgrn_consts(rev):
    c = HG_CHUNK
    i = np.arange(c)[:, None]
    t = np.arange(c)[None, :]
    if rev:
        cum = t >= i
        rest = t < i
    else:
        cum = t <= i
        rest = t > i
    blocks = [cum, rest]
    masks = [np.eye(c, dtype=bool)]
    col = np.arange(c)[None, :]
    for s in HGRN_LEVELS:
        start = (i // (2 * s)) * (2 * s)
        later = (i - start) >= s
        if rev:
            ref = start + s
            p = np.where(later, (t >= ref) & (t < i), (t >= i) & (t < ref))
            q_half, k_half = 0, 1
        else:
            ref = start + s - 1
            p = np.where(later, (t > ref) & (t <= i), (t > i) & (t <= ref))
            q_half, k_half = 1, 0
        blocks.append(p)
        same = (i // (2 * s)) == (col // (2 * s))
        masks.append(same & ((i // s) % 2 == q_half) & ((col // s) % 2 == k_half))
    pst = np.concatenate(blocks, axis=0).astype(np.float32)
    pst = np.concatenate([pst, pst], axis=1)
    msk = np.stack(masks, axis=0).astype(np.float32)
    return jnp.asarray(pst, dtype=BF16), jnp.asarray(msk)


def _hgrn_kernel(lay, rev, *refs):
    if rev:
        (q_ref, f_ref, v_ref, llb_ref, l1lb_ref, pst_ref, msk_ref, out_ref,
         st_ref, qs_ref, g_ref, kk_ref, o_ref, e_all_ref, e_last_ref) = refs
        z_ref = ob_ref = gain_ref = None
    else:
        (q_ref, f_ref, v_ref, llb_ref, l1lb_ref, pst_ref, msk_ref, z_ref, ob_ref, gain_ref, out_ref,
         st_ref, qs_ref, g_ref, kk_ref, o_ref, e_all_ref, e_last_ref) = refs
    nc = SUB_BLOCK // HG_CHUNK
    n_lvl = len(HGRN_LEVELS)
    last_row = 0 if rev else HG_CHUNK - 1
    heads = [slice(h * HEAD_DIM, (h + 1) * HEAD_DIM) for h in range(N_HEADS)]
    _reset_state(lay, rev, st_ref)

    def piece(base):
        blk = _at(base, 0, SUB_BLOCK)
        qs_ref[...] = (_silu(q_ref[blk, :].astype(F32)) * QK_SCALE).astype(BF16)
        a = llb_ref[...]
        b = l1lb_ref[...] + _log_sigmoid(f_ref[blk, :].astype(F32))
        g2 = (jnp.maximum(a, b) + jnp.log(1.0 + jnp.exp(-jnp.abs(a - b)))) * LOG2_E
        g_ref[...] = g2
        kk_ref[...] = (1.0 - jnp.exp2(g2)).astype(BF16)

        pst = pst_ref[...]
        for ci in range(nc):
            c = (nc - 1 - ci) if rev else ci
            g_parts = jnp.concatenate(_split2(g_ref[c * HG_CHUNK:(c + 1) * HG_CHUNK, :]), axis=0)
            e_f32 = jnp.exp2(jnp.dot(pst, g_parts, preferred_element_type=F32))
            e_all_ref[ci] = e_f32.astype(BF16)
            e_last_ref[ci] = e_f32[last_row:last_row + 1, :]

        chains = [(ci, (nc - 1 - ci) if rev else ci, sl) for ci in range(nc) for sl in heads]

        def operands(ci, c, sl):
            rows = slice(c * HG_CHUNK, (c + 1) * HG_CHUNK)
            return (qs_ref[rows, sl], kk_ref[rows, sl], v_ref[_at(base, c * HG_CHUNK, HG_CHUNK), sl].astype(BF16),
                    e_all_ref.at[ci])

        attns, upds = [], []
        for first in range(0, len(chains), HGRN_GROUP):
            ops = [operands(*chain) for chain in chains[first:first + HGRN_GROUP]]
            parts = [[_dot_nt(q, k) for q, k, _, _ in ops]]
            for lvl in range(n_lvl):
                es = [e_ref[(2 + lvl) * HG_CHUNK:(3 + lvl) * HG_CHUNK, sl]
                      for (_, _, _, e_ref), (_, _, sl) in zip(ops, chains[first:])]
                parts.append([_dot_nt(q * e, k * e) for (q, k, _, _), e in zip(ops, es)])
            upds += [_dot_tn(v, k * e_ref[HG_CHUNK:2 * HG_CHUNK, sl])
                     for (_, k, v, e_ref), (_, _, sl) in zip(ops, chains[first:])]
            for h in range(HGRN_GROUP):
                attn = parts[0][h] * msk_ref[0]
                for lvl in range(n_lvl):
                    attn = attn + parts[1 + lvl][h] * msk_ref[1 + lvl]
                attns.append(attn.astype(BF16))
        states = [st_ref[h] for h in range(N_HEADS)]
        entering = []
        for idx, (ci, c, sl) in enumerate(chains):
            h = idx % N_HEADS
            entering.append(states[h])
            states[h] = states[h] * e_last_ref[ci, :, sl] + upds[idx]
        for h in range(N_HEADS):
            st_ref[h] = states[h]
        ops = [operands(*chain) for chain in chains]
        inters = [_dot_nt(q * e_ref[0:HG_CHUNK, sl], s)
                  for (q, _, _, e_ref), (_, _, sl), s in zip(ops, chains, entering)]
        intras = [jnp.dot(a, v, preferred_element_type=F32) for a, (_, _, v, _) in zip(attns, ops)]
        for (ci, c, sl), inter, intra in zip(chains, inters, intras):
            o_ref[c * HG_CHUNK:(c + 1) * HG_CHUNK, sl] = inter + intra
        _finish(rev, base, o_ref, ob_ref, z_ref, gain_ref, out_ref)

    _sub_blocks(lay, rev, piece)


def _mixer_c(proj, log_lb, log1m_lb, o_bwd, gain, lay, rev):
    tb, nb = lay.tb, lay.nb
    bi = _blk_index(lay, rev)

    def cur(cb):
        return pl.BlockSpec((tb, W_GROUP), lambda j: (bi(j), cb))

    vec = pl.BlockSpec((1, W_GROUP), lambda j: (0, 0))
    d = 1 if rev else 0
    pst, msk = _hgrn_consts(rev)
    n_stack = pst.shape[0]
    in_specs = [cur(CB_C_Q), cur(CB_C_FB if rev else CB_C_FF), cur(CB_C_I), vec, vec,
                pl.BlockSpec((n_stack, 2 * HG_CHUNK), lambda j: (0, 0)),
                pl.BlockSpec((msk.shape[0], HG_CHUNK, HG_CHUNK), lambda j: (0, 0, 0))]
    args = [proj, proj, proj, log_lb[d].reshape(1, W_GROUP), log1m_lb[d].reshape(1, W_GROUP), pst, msk]
    if not rev:
        in_specs += [cur(CB_C_Z), pl.BlockSpec((tb, W_GROUP), lambda j: (bi(j), 0)), vec]
        args += [proj, o_bwd, gain.reshape(1, W_GROUP)]
    big = pltpu.VMEM((SUB_BLOCK, W_GROUP), F32)
    return pl.pallas_call(
        functools.partial(_hgrn_kernel, lay, rev),
        grid=(nb,),
        in_specs=in_specs,
        out_specs=pl.BlockSpec((tb, W_GROUP), lambda j: (bi(j), 0)),
        out_shape=jax.ShapeDtypeStruct((lay.n, W_GROUP), F32 if rev else BF16),
        scratch_shapes=[pltpu.VMEM((N_HEADS, HEAD_DIM, HEAD_DIM), F32),
                        pltpu.VMEM((SUB_BLOCK, W_GROUP), BF16),
                        big, pltpu.VMEM((SUB_BLOCK, W_GROUP), BF16), big,
                        pltpu.VMEM((SUB_BLOCK // HG_CHUNK, n_stack, W_GROUP), BF16),
                        pltpu.VMEM((SUB_BLOCK // HG_CHUNK, 1, W_GROUP), F32)],
        compiler_params=_cparams(("arbitrary",)),
        name="mixer_c_hgrn2_bwd" if rev else "mixer_c_hgrn2_fwd",
    )(*args)


def _unit_tri_inverses(a_mats, row, col):
    c = a_mats[0].shape[0]
    eye = jnp.where(row == col, 1.0, 0.0)

    def same_block(size):
        return (row // size) == (col // size)

    base = 8
    m_base = same_block(base)
    ns = [jnp.where(m_base, -a, 0.0).astype(BF16) for a in a_mats]
    n2s = [jnp.dot(n, n, preferred_element_type=F32) for n in ns]
    xs = [eye + n.astype(F32) for n in ns]
    xs = [x + _dot(x, n2) for x, n2 in zip(xs, n2s)]
    n4s = [_dot(n2, n2) for n2 in n2s]
    xs = [x + _dot(x, n4) for x, n4 in zip(xs, n4s)]
    size = base
    while size < c:
        m_off = same_block(2 * size) & jnp.logical_not(same_block(size))
        xbs = [x.astype(BF16) for x in xs]
        ts = [jnp.dot(xb, jnp.where(m_off, a, 0.0).astype(BF16), preferred_element_type=F32)
              for xb, a in zip(xbs, a_mats)]
        xs = [x - jnp.dot(t.astype(BF16), xb, preferred_element_type=F32) for x, t, xb in zip(xs, ts, xbs)]
        size *= 2
    return xs


SHIFT_ROWS = 128


def _shift_mats():
    half = DN_CONV_WIDTH // 2
    t = np.arange(SHIFT_ROWS)[:, None]
    s = np.arange(SHIFT_ROWS + 2 * HALO_D)[None, :]
    mats = [(s == HALO_D + t + d) for d in range(-half, half + 1) if d != 0]
    return jnp.asarray(np.concatenate(mats, axis=0).astype(np.float32), dtype=BF16)


def _delta_pre_kernel(lay, q_ref, k_ref, v_ref, qp_ref, kp_ref, vp_ref, qn_ref, kn_ref, vn_ref, sm_ref,
                      cw_ref, alog_ref, dtb_ref, sh_ref, qo_ref, ko_ref, vo_ref, cum_ref, beta_ref, cumt_ref,
                      xq_ref, xk_ref, xv_ref):
    tb = lay.tb
    blk = pl.program_id(0)
    keep_prev = jnp.where(lay.seq_start(blk), 0.0, 1.0)
    keep_next = jnp.where(lay.seq_end(blk), 0.0, 1.0)
    half = DN_CONV_WIDTH // 2
    parts = ((q_ref, qp_ref, qn_ref, qo_ref, xq_ref), (k_ref, kp_ref, kn_ref, ko_ref, xk_ref),
             (v_ref, vp_ref, vn_ref, vo_ref, xv_ref))
    for c_ref, p_ref, n_ref, _, x_ref in parts:
        x_ref[0:HALO_D, :] = (p_ref[...].astype(F32) * keep_prev).astype(BF16)
        x_ref[HALO_D:HALO_D + tb, :] = c_ref[...]
        x_ref[HALO_D + tb:2 * HALO_D + tb, :] = (n_ref[...].astype(F32) * keep_next).astype(BF16)
    tiles = [(part, r0) for part in range(len(parts)) for r0 in range(0, tb, SHIFT_ROWS)]
    shifted = [jnp.dot(sh_ref[...], parts[part][4][r0:r0 + SHIFT_ROWS + 2 * HALO_D, :], preferred_element_type=F32)
               for part, r0 in tiles]
    for (part, r0), sh in zip(tiles, shifted):
        x_ref, dst_ref = parts[part][4], parts[part][3]
        wsl = slice(part * W_GROUP, (part + 1) * W_GROUP)
        acc = cw_ref[half:half + 1, wsl] * x_ref[HALO_D + r0:HALO_D + r0 + SHIFT_ROWS, :].astype(F32)
        for j, k in enumerate([k for k in range(DN_CONV_WIDTH) if k != half]):
            acc = acc + cw_ref[k:k + 1, wsl] * sh[j * SHIFT_ROWS:(j + 1) * SHIFT_ROWS, :]
        y = _silu(acc)
        rows = slice(r0, r0 + SHIFT_ROWS)
        if part < 2:
            for h in range(N_HEADS):
                sl = slice(h * HEAD_DIM, (h + 1) * HEAD_DIM)
                yh = y[:, sl]
                yn = yh * lax.rsqrt(jnp.sum(yh * yh, axis=-1, keepdims=True) + NORM_EPS)
                dst_ref[rows, sl] = (yn * QK_SCALE if part == 0 else yn).astype(BF16)
        else:
            dst_ref[rows, :] = y.astype(BF16)

    sm = sm_ref[...]
    g = -jnp.exp(alog_ref[...]) * _softplus(sm + dtb_ref[...])
    beta_ref[...] = _sigmoid(sm)

    row = lax.broadcasted_iota(jnp.int32, (CHUNK, CHUNK), 0)
    col = lax.broadcasted_iota(jnp.int32, (CHUNK, CHUNK), 1)
    lower = jnp.where(row >= col, 1.0, 0.0)
    upper = jnp.where(row <= col, 1.0, 0.0)
    fwd_lane = lax.broadcasted_iota(jnp.int32, (CHUNK, SMALL_W), 1) < N_HEADS
    eye_rows = jnp.where(lax.broadcasted_iota(jnp.int32, (SUBLANES, SMALL_W), 0)
                         == lax.broadcasted_iota(jnp.int32, (SUBLANES, SMALL_W), 1), 1.0, 0.0).astype(BF16)
    for c in range(tb // CHUNK):
        gc = g[c * CHUNK:(c + 1) * CHUNK, :]
        cum = jnp.where(fwd_lane, _dot_exact_lhs(lower, gc), _dot_exact_lhs(upper, gc))
        cum_ref[c * CHUNK:(c + 1) * CHUNK, :] = cum
        cum_t = None
        for piece in _split3(cum):
            t = lax.dot_general(eye_rows, piece, (((1,), (1,)), ((), ())), preferred_element_type=F32)
            cum_t = t if cum_t is None else cum_t + t
        cumt_ref[c] = cum_t


def _delta_kernel(lay, rev, *refs):
    if rev:
        (qc_ref, kc_ref, vc_ref, cum_ref, beta_ref, cumt_ref, out_ref,
         s_ref, o_ref, qd_ref, kd_ref, qk_ref, rhs_ref, last_ref, gq_ref, bm_ref) = refs
        z_ref = ob_ref = gain_ref = None
    else:
        (qc_ref, kc_ref, vc_ref, cum_ref, beta_ref, cumt_ref, z_ref, ob_ref, gain_ref, out_ref,
         s_ref, o_ref, qd_ref, kd_ref, qk_ref, rhs_ref, last_ref, gq_ref, bm_ref) = refs
    _reset_state(lay, rev, s_ref)
    named = (qc_ref, kc_ref, vc_ref, cum_ref, beta_ref, cumt_ref, z_ref, ob_ref, gain_ref, out_ref,
             s_ref, o_ref, qd_ref, kd_ref, qk_ref, rhs_ref, last_ref, gq_ref, bm_ref)
    _sub_blocks(lay, rev, functools.partial(_delta_piece, rev, named), DELTA_PIECE)


def _delta_piece(rev, named, base):
    (qc_ref, kc_ref, vc_ref, cum_ref, beta_ref, cumt_ref, z_ref, ob_ref, gain_ref, out_ref,
     s_ref, o_ref, qd_ref, kd_ref, qk_ref, rhs_ref, last_ref, gq_ref, bm_ref) = named
    nc = DELTA_PIECE // CHUNK
    row, col, incl, strict = _causal_masks(rev, CHUNK)
    lane0 = N_HEADS if rev else 0

    cums = [cum_ref[_at(base, c * CHUNK, CHUNK), :] for c in range(nc)]
    cum_ts = [cumt_ref[base // CHUNK + c] for c in range(nc)]
    raws = []
    for c in range(nc):
        rows = slice(c * CHUNK, (c + 1) * CHUNK)
        src = _at(base, c * CHUNK, CHUNK)
        beta_all = beta_ref[src, :]
        for h in range(N_HEADS):
            sl = slice(h * HEAD_DIM, (h + 1) * HEAD_DIM)
            qc = qc_ref[src, sl]
            kc = kc_ref[src, sl]
            vc = vc_ref[src, sl]
            ci_b = jnp.broadcast_to(cums[c][:, lane0 + h:lane0 + h + 1], (CHUNK, HEAD_DIM))
            b_lane = 2 * N_HEADS + lane0 + h
            beta_b = jnp.broadcast_to(beta_all[:, b_lane:b_lane + 1], (CHUNK, HEAD_DIM))
            k_beta = kc * beta_b
            raws.append(_dot_nt(jnp.concatenate([k_beta.astype(BF16), qc], axis=0), kc))
            e_ci = jnp.exp(ci_b)
            rhs_ref[c * N_HEADS + h] = jnp.concatenate([vc * beta_b, k_beta * e_ci], axis=1).astype(BF16)
            qd_ref[rows, sl] = qc * e_ci
            c_last = ci_b[0:1, :] if rev else ci_b[CHUNK - 1:CHUNK, :]
            kd_ref[rows, sl] = (kc * jnp.exp(c_last - ci_b)).astype(BF16)
            last_ref[c * N_HEADS + h] = jnp.broadcast_to(jnp.exp(c_last), (SUBLANES, HEAD_DIM))
    a_mats = []
    for idx, raw in enumerate(raws):
        c, h = divmod(idx, N_HEADS)
        ci_col = cums[c][:, lane0 + h:lane0 + h + 1]
        cj_row = cum_ts[c][lane0 + h:lane0 + h + 1, :]
        decay = jnp.exp(jnp.where(incl, ci_col - cj_row, NEG_BIG))
        a_mats.append(jnp.where(strict, raw[0:CHUNK] * decay, 0.0))
        qk_ref[idx] = (raw[CHUNK:2 * CHUNK] * decay).astype(BF16)

    t_invs = _unit_tri_inverses(a_mats, row, col)
    where = [(slice((idx // N_HEADS) * CHUNK, (idx // N_HEADS + 1) * CHUNK),
              slice((idx % N_HEADS) * HEAD_DIM, (idx % N_HEADS + 1) * HEAD_DIM)) for idx in range(len(t_invs))]
    uws = [jnp.dot(t_inv.astype(BF16), rhs_ref[idx], preferred_element_type=F32).astype(BF16)
           for idx, t_inv in enumerate(t_invs)]
    kuws = [lax.dot_general(kd_ref[rows, sl], uw, (((0,), (0,)), ((), ())), preferred_element_type=F32)
            for (rows, sl), uw in zip(where, uws)]
    quws = [jnp.dot(qk_ref[idx], uw, preferred_element_type=F32) for idx, uw in enumerate(uws)]
    for idx, ((rows, sl), kuw, quw) in enumerate(zip(where, kuws, quws)):
        bm_ref[idx] = kuw[:, 0:HEAD_DIM]
        gq_ref[idx, 0:HEAD_DIM, :] = kuw[:, HEAD_DIM:2 * HEAD_DIM].astype(BF16)
        gq_ref[idx, HEAD_DIM:HEAD_DIM + CHUNK, :] = (qd_ref[rows, sl] - quw[:, HEAD_DIM:2 * HEAD_DIM]).astype(BF16)
        o_ref[rows, sl] = quw[:, 0:HEAD_DIM]

    states = [s_ref[h] for h in range(N_HEADS)]
    for ci in range(nc):
        c = (nc - 1 - ci) if rev else ci
        rows = slice(c * CHUNK, (c + 1) * CHUNK)
        for h in range(N_HEADS):
            idx = c * N_HEADS + h
            sl = slice(h * HEAD_DIM, (h + 1) * HEAD_DIM)
            gs = jnp.dot(gq_ref[idx], states[h].astype(BF16), preferred_element_type=F32)
            o_ref[rows, sl] += gs[HEAD_DIM:HEAD_DIM + CHUNK]
            states[h] = states[h] * last_ref[idx][0:1, :] - gs[0:HEAD_DIM] + bm_ref[idx]
    for h in range(N_HEADS):
        s_ref[h] = states[h]
    _finish(rev, base, o_ref, ob_ref, z_ref, gain_ref, out_ref, DELTA_PIECE)


def _mixer_d_pre(proj, proj_small, conv_w, a_log, dt_bias, lay):
    tb, nb = lay.tb, lay.nb
    hb = tb // HALO_D
    n_halo = lay.n // HALO_D

    def cur(cb):
        return pl.BlockSpec((tb, W_GROUP), lambda i: (i, cb))

    def prev(cb):
        return pl.BlockSpec((HALO_D, W_GROUP), lambda i: (jnp.maximum(i * hb - 1, 0), cb))

    def nxt(cb):
        return pl.BlockSpec((HALO_D, W_GROUP), lambda i: (jnp.minimum((i + 1) * hb, n_halo - 1), cb))

    small = pl.BlockSpec((1, SMALL_W), lambda i: (0, 0))
    pad_lanes = SMALL_W - 2 * N_HEADS
    alog_row = jnp.pad(a_log.reshape(1, 2 * N_HEADS), ((0, 0), (0, pad_lanes)))
    dtb_row = jnp.pad(dt_bias.reshape(1, 2 * N_HEADS), ((0, 0), (0, pad_lanes)))
    wide = pl.BlockSpec((tb, W_GROUP), lambda i: (i, 0))
    narrow = pl.BlockSpec((tb, SMALL_W), lambda i: (i, 0))
    shifts = _shift_mats()
    return pl.pallas_call(
        functools.partial(_delta_pre_kernel, lay),
        grid=(nb,),
        in_specs=[cur(CB_D_Q), cur(CB_D_K), cur(CB_D_V), prev(CB_D_Q), prev(CB_D_K), prev(CB_D_V),
                  nxt(CB_D_Q), nxt(CB_D_K), nxt(CB_D_V),
                  pl.BlockSpec((tb, SMALL_W), lambda i: (i, 0)),
                  pl.BlockSpec((DN_CONV_WIDTH, 3 * W_GROUP), lambda i: (0, 0)), small, small,
                  pl.BlockSpec(shifts.shape, lambda i: (0, 0))],
        out_specs=[wide, wide, wide, narrow, narrow,
                   pl.BlockSpec((tb // CHUNK, SUBLANES, CHUNK), lambda i: (i, 0, 0))],
        out_shape=[jax.ShapeDtypeStruct((lay.n, W_GROUP), BF16)] * 3
        + [jax.ShapeDtypeStruct((lay.n, SMALL_W), F32)] * 2
        + [jax.ShapeDtypeStruct((lay.n // CHUNK, SUBLANES, CHUNK), F32)],
        scratch_shapes=[pltpu.VMEM((tb + 2 * HALO_D, W_GROUP), BF16)] * 3,
        compiler_params=_cparams(("arbitrary",)),
        name="mixer_d_deltanet_pre",
    )(*([proj] * 9), proj_small, conv_w, alog_row, dtb_row, shifts)


def _mixer_d(pre, proj, o_bwd, gain, lay, rev):
    tb, nb = lay.tb, lay.nb
    bi = _blk_index(lay, rev)
    wide = pl.BlockSpec((tb, W_GROUP), lambda j: (bi(j), 0))
    narrow = pl.BlockSpec((tb, SMALL_W), lambda j: (bi(j), 0))
    in_specs = [wide, wide, wide, narrow, narrow,
                pl.BlockSpec((tb // CHUNK, SUBLANES, CHUNK), lambda j: (bi(j), 0, 0))]
    args = list(pre)
    if not rev:
        in_specs += [pl.BlockSpec((tb, W_GROUP), lambda j: (bi(j), CB_D_Z)), wide,
                     pl.BlockSpec((1, W_GROUP), lambda j: (0, 0))]
        args += [proj, o_bwd, gain.reshape(1, W_GROUP)]
    big = pltpu.VMEM((DELTA_PIECE, W_GROUP), F32)
    nc = DELTA_PIECE // CHUNK
    return pl.pallas_call(
        functools.partial(_delta_kernel, lay, rev),
        grid=(nb,),
        in_specs=in_specs,
        out_specs=wide,
        out_shape=jax.ShapeDtypeStruct((lay.n, W_GROUP), F32 if rev else BF16),
        scratch_shapes=[pltpu.VMEM((N_HEADS, HEAD_DIM, HEAD_DIM), F32),
                        big,
                        big,
                        pltpu.VMEM((DELTA_PIECE, W_GROUP), BF16),
                        pltpu.VMEM((nc * N_HEADS, CHUNK, CHUNK), BF16),
                        pltpu.VMEM((nc * N_HEADS, CHUNK, 2 * HEAD_DIM), BF16),
                        pltpu.VMEM((nc * N_HEADS, SUBLANES, HEAD_DIM), F32),
                        pltpu.VMEM((nc * N_HEADS, HEAD_DIM + CHUNK, HEAD_DIM), BF16),
                        pltpu.VMEM((nc * N_HEADS, HEAD_DIM, HEAD_DIM), F32)],
        compiler_params=_cparams(("arbitrary",)),
        name="mixer_d_deltanet_bwd" if rev else "mixer_d_deltanet_fwd",
    )(*args)


def _regroup_w_in(w):
    d_qkv_end = 15 * W_GROUP
    small = w[:, d_qkv_end:d_qkv_end + 4 * N_HEADS]
    d_z = w[:, d_qkv_end + 4 * N_HEADS:]
    pad = jnp.zeros((w.shape[0], SMALL_W - 4 * N_HEADS), w.dtype)
    main = jnp.concatenate([w[:, :d_qkv_end], d_z], axis=1).astype(BF16)
    return main, jnp.concatenate([small, pad], axis=1).astype(BF16)


def _rotary_tables(t_max):
    half = HEAD_DIM // 2
    inv = 1.0 / (ROPE_BASE ** (jnp.arange(half, dtype=F32) / half))
    ang = jnp.arange(t_max, dtype=F32)[:, None] * inv[None, :]
    cos, sin = jnp.cos(ang), jnp.sin(ang)
    return jnp.concatenate([cos, cos], axis=-1), jnp.concatenate([-sin, sin], axis=-1)


def _pick_tile(t_prompt, t_sample, want):
    tile = want
    while t_prompt % tile or t_sample % tile:
        tile //= 2
    return tile


def kernel(x_prompt, x_sample, c_prompt, c_sample, ada_w, ada_b, norm_g, w_in, conv_a_w, conv_a_b, ln_a_g, ln_a_b,
           ret_norm_g, hgrn_lb_logits, hgrn_norm_g, dn_conv_w, dn_a_log, dn_dt_bias, dn_norm_g, w_out, final_g):
    depth = w_in.shape[0]
    n_prompt, t_prompt, _ = x_prompt.shape
    n_sample, t_sample, _ = x_sample.shape
    tm_in = _pick_tile(t_prompt, t_sample, 1024)
    tm_out = _pick_tile(t_prompt, t_sample, 512)
    lay = _Layout(t_prompt, n_prompt, t_sample, n_sample, _pick_tile(t_prompt, t_sample, 1024))
    lay_a = _Layout(t_prompt, n_prompt, t_sample, n_sample, _pick_tile(t_prompt, t_sample, 512))
    lay_dp = _Layout(t_prompt, n_prompt, t_sample, n_sample, _pick_tile(t_prompt, t_sample, 1024))

    xs = (x_prompt.reshape(-1, D_MODEL), x_sample.reshape(-1, D_MODEL))
    c_all = jnp.concatenate([c_prompt, c_sample], axis=0)
    n_c = c_all.shape[0]
    c_all = jnp.pad(c_all, ((0, (-n_c) % SUBLANES), (0, 0)))
    mod = _modulation(c_all, ada_w, ada_b)

    lb = jnp.cumsum(jax.nn.softmax(hgrn_lb_logits.astype(F32), axis=0), axis=0)
    lb = lb - lb[:1]
    log_lb = jnp.log(lb)
    log1m_lb = jnp.log1p(-lb)
    cos2, sin2 = _rotary_tables(max(t_prompt, t_sample))

    for l in range(depth):
        mod_l = mod[l].reshape(mod.shape[1], 1, 3 * D_MODEL)
        proj, proj_small = _in_proj(xs, mod_l, norm_g[l], *_regroup_w_in(w_in[l]), lay, tm_in)
        conv_a = _mixer_a(proj, conv_a_w[l], conv_a_b[l], lay_a)
        ob = _mixer_b(proj, cos2, sin2, None, None, lay, True)
        m_b = _mixer_b(proj, cos2, sin2, ob, ret_norm_g[l], lay, False)
        oc = _mixer_c(proj, log_lb[l], log1m_lb[l], None, None, lay, True)
        m_c = _mixer_c(proj, log_lb[l], log1m_lb[l], oc, hgrn_norm_g[l], lay, False)
        d_pre = _mixer_d_pre(proj, proj_small, dn_conv_w[l], dn_a_log[l], dn_dt_bias[l], lay_dp)
        od = _mixer_d(d_pre, proj, None, None, lay, True)
        m_d = _mixer_d(d_pre, proj, od, dn_norm_g[l], lay, False)
        xs = _out_proj(xs, (conv_a, proj, m_b, m_c, m_d), mod_l, w_out[l].astype(BF16), final_g,
                       ln_a_g[l], ln_a_b[l], lay, tm_out, final=(l == depth - 1))

    return (xs[0].reshape(n_prompt, t_prompt, D_MODEL), xs[1].reshape(n_sample, t_sample, D_MODEL))
```

```python
import functools
import math

import numpy as np
import jax
import jax.numpy as jnp
from jax import lax
from jax.experimental import pallas as pl
from jax.experimental.pallas import tpu as pltpu

F32 = jnp.float32
BF16 = jnp.bfloat16

D_MODEL = 1024
W_GROUP = 512
HEAD_DIM = 128
SUBLANES = 8
N_HEADS = 4
CONV_A_WIDTH = 31
DN_CONV_WIDTH = 5
CHUNK = 64
ROPE_BASE = 10000.0
RET_DECAY_OFFSET = 5.0
NORM_EPS = 1e-6
QK_SCALE = HEAD_DIM ** -0.5
NEG_BIG = -1e30

N_COL_BLOCKS = 16
SMALL_W = 128
SUB_BLOCK = 256
DELTA_PIECE = 512
D_PROJ = N_COL_BLOCKS * W_GROUP
PROJ_TN = 2048
HALO_A = 16
HALO_D = 16
VMEM_LIMIT = 56 * 1024 * 1024

CB_A_VAL, CB_A_GLU, CB_A_Z = 0, 1, 2
CB_B_Q, CB_B_K, CB_B_V, CB_B_Z = 3, 4, 5, 6
CB_C_Q, CB_C_FF, CB_C_FB, CB_C_I, CB_C_Z = 7, 8, 9, 10, 11
CB_D_Q, CB_D_K, CB_D_V, CB_D_Z = 12, 13, 14, 15


def _dot(a, b):
    return jnp.dot(a.astype(BF16), b.astype(BF16), preferred_element_type=F32)


def _dot_nt(a, b):
    return lax.dot_general(a.astype(BF16), b.astype(BF16), (((1,), (1,)), ((), ())), preferred_element_type=F32)


def _dot_tn(a, b):
    return lax.dot_general(a.astype(BF16), b.astype(BF16), (((0,), (0,)), ((), ())), preferred_element_type=F32)


def _split2(x):
    hi = x.astype(BF16)
    lo = (x - hi.astype(F32)).astype(BF16)
    return hi, lo


def _split3(x):
    hi = x.astype(BF16)
    r = x - hi.astype(F32)
    mid = r.astype(BF16)
    lo = (r - mid.astype(F32)).astype(BF16)
    return hi, mid, lo


def _dot_exact_lhs(m, x):
    mb = m.astype(BF16)
    acc = None
    for part in _split3(x):
        t = jnp.dot(mb, part, preferred_element_type=F32)
        acc = t if acc is None else acc + t
    return acc


def _sigmoid(x):
    return 0.5 * jnp.tanh(0.5 * x) + 0.5


def _silu(x):
    return x * _sigmoid(x)


def _softplus(x):
    return jnp.maximum(x, 0.0) + jnp.log(1.0 + jnp.exp(-jnp.abs(x)))


def _log_sigmoid(x):
    return jnp.minimum(x, 0.0) - jnp.log(1.0 + jnp.exp(-jnp.abs(x)))


class _Layout:
    def __init__(self, t_prompt, n_prompt, t_sample, n_sample, tb):
        self.tp = t_prompt * n_prompt
        self.t_prompt = t_prompt
        self.ts = t_sample
        self.n = self.tp + t_sample * n_sample
        self.tb = tb
        assert t_prompt % tb == 0 and t_sample % tb == 0
        self.nb = self.n // tb

    def seq_start(self, blk):
        s = blk * self.tb
        return jnp.where(s < self.tp, s % self.t_prompt == 0, (s - self.tp) % self.ts == 0)

    def seq_end(self, blk):
        e = (blk + 1) * self.tb
        return jnp.where(e <= self.tp, e % self.t_prompt == 0, (e - self.tp) % self.ts == 0)

    def pos_block(self, blk):
        s = blk * self.tb
        return jnp.where(s < self.tp, (s % self.t_prompt) // self.tb, ((s - self.tp) % self.ts) // self.tb)

    def batch_index(self, blk, rows):
        s = blk * rows
        return jnp.where(s < self.tp, s // self.t_prompt, self.tp // self.t_prompt + (s - self.tp) // self.ts)


def _cparams(sem):
    return pltpu.CompilerParams(dimension_semantics=sem, vmem_limit_bytes=VMEM_LIMIT)


def _mod_kernel(c_ref, w_ref, b_ref, o_ref):
    c = c_ref[...]
    o_ref[0] = jnp.dot(_silu(c), w_ref[0], preferred_element_type=F32,
                       precision=lax.Precision.HIGHEST) + b_ref[0]


def _modulation(c_all, ada_w, ada_b):
    depth = ada_w.shape[0]
    nb = c_all.shape[0]
    return pl.pallas_call(
        _mod_kernel,
        grid=(depth,),
        in_specs=[pl.BlockSpec((nb, D_MODEL), lambda l: (0, 0)),
                  pl.BlockSpec((1, D_MODEL, 3 * D_MODEL), lambda l: (l, 0, 0)),
                  pl.BlockSpec((1, 1, 3 * D_MODEL), lambda l: (l, 0, 0))],
        out_specs=pl.BlockSpec((1, nb, 3 * D_MODEL), lambda l: (l, 0, 0)),
        out_shape=jax.ShapeDtypeStruct((depth, nb, 3 * D_MODEL), F32),
        compiler_params=_cparams(("arbitrary",)),
        name="adaln_modulation",
    )(c_all, ada_w, ada_b.reshape(depth, 1, 3 * D_MODEL))


def _load_x(x_refs, i, npb):
    if len(x_refs) == 1:
        return x_refs[0][...]
    return jnp.where(i < npb, x_refs[0][...], x_refs[1][...])


def _x_specs(n_x, tm, npb, row_block):
    def spec(fn):
        return pl.BlockSpec((tm, D_MODEL), lambda *ids: (fn(row_block(*ids)), 0))

    if n_x == 1:
        return [spec(lambda i: i)]
    return [spec(lambda i: jnp.minimum(i, npb - 1)), spec(lambda i: jnp.maximum(i - npb, 0))]


def _in_proj_kernel(n_x, npb, *refs):
    x_refs = refs[:n_x]
    mod_ref, g_ref, w_ref, ws_ref, o_ref, os_ref, h_ref = refs[n_x:]

    @pl.when(pl.program_id(1) == 0)
    def _():
        x = _load_x(x_refs, pl.program_id(0), npb)
        y = x * lax.rsqrt(jnp.mean(x * x, axis=-1, keepdims=True) + NORM_EPS) * g_ref[...]
        shift = mod_ref[0, :, 0:D_MODEL]
        scale = mod_ref[0, :, D_MODEL:2 * D_MODEL]
        h = (y * (1.0 + scale) + shift).astype(BF16)
        h_ref[...] = h
        os_ref[...] = jnp.dot(h, ws_ref[...], preferred_element_type=F32)

    res = jnp.dot(h_ref[...], w_ref[...], preferred_element_type=F32)
    tile = pl.program_id(1)
    glu_tile, glu_col = divmod(CB_A_VAL * W_GROUP, PROJ_TN)
    q_tile, q_col = divmod(CB_C_Q * W_GROUP, PROJ_TN)

    @pl.when(tile == glu_tile)
    def _():
        val = res[:, glu_col:glu_col + W_GROUP]
        glu = res[:, glu_col + W_GROUP:glu_col + 2 * W_GROUP]
        o_ref[...] = res.astype(BF16)
        o_ref[:, glu_col:glu_col + W_GROUP] = (val * _sigmoid(glu)).astype(BF16)

    @pl.when(tile == q_tile)
    def _():
        o_ref[...] = res.astype(BF16)
        o_ref[:, q_col:q_col + W_GROUP] = (_silu(res[:, q_col:q_col + W_GROUP]) * QK_SCALE).astype(BF16)

    @pl.when((tile != glu_tile) & (tile != q_tile))
    def _():
        o_ref[...] = res.astype(BF16)


def _in_proj(xs, mod, norm_g, w_main, w_small, lay, tm):
    n = lay.n
    npb = lay.tp // tm
    lay_m = _Layout(lay.t_prompt, lay.tp // lay.t_prompt, lay.ts, (lay.n - lay.tp) // lay.ts, tm)
    return pl.pallas_call(
        functools.partial(_in_proj_kernel, len(xs), npb),
        grid=(n // tm, D_PROJ // PROJ_TN),
        in_specs=_x_specs(len(xs), tm, npb, lambda i, j: i) + [
                  pl.BlockSpec((1, 1, 3 * D_MODEL), lambda i, j: (lay_m.batch_index(i, tm), 0, 0)),
                  pl.BlockSpec((1, D_MODEL), lambda i, j: (0, 0)),
                  pl.BlockSpec((D_MODEL, PROJ_TN), lambda i, j: (0, j)),
                  pl.BlockSpec((D_MODEL, SMALL_W), lambda i, j: (0, 0))],
        out_specs=[pl.BlockSpec((tm, PROJ_TN), lambda i, j: (i, j)),
                   pl.BlockSpec((tm, SMALL_W), lambda i, j: (i, 0))],
        out_shape=[jax.ShapeDtypeStruct((n, D_PROJ), BF16), jax.ShapeDtypeStruct((n, SMALL_W), F32)],
        scratch_shapes=[pltpu.VMEM((tm, D_MODEL), BF16)],
        compiler_params=_cparams(("arbitrary", "arbitrary")),
        name="in_proj",
    )(*xs, mod, norm_g.reshape(1, D_MODEL), w_main, w_small)


def _out_proj_kernel(final, n_x, npb, *refs):
    x_refs = refs[:n_x]
    ca_ref, za_ref, mb_ref, mc_ref, md_ref, mod_ref, w_ref, fg_ref, lg_ref, lb_ref = refs[n_x:n_x + 10]
    o_refs = refs[n_x + 10:]
    i = pl.program_id(0)
    conv = ca_ref[...]
    mu = jnp.mean(conv, axis=-1, keepdims=True)
    xc = conv - mu
    var = jnp.mean(xc * xc, axis=-1, keepdims=True)
    y = xc * lax.rsqrt(var + NORM_EPS) * lg_ref[...] + lb_ref[...]
    m_a = (_silu(y) * _silu(za_ref[...].astype(F32))).astype(BF16)
    acc = jnp.dot(m_a, w_ref[0:W_GROUP, :], preferred_element_type=F32)
    acc += jnp.dot(mb_ref[...], w_ref[W_GROUP:2 * W_GROUP, :], preferred_element_type=F32)
    acc += jnp.dot(mc_ref[...], w_ref[2 * W_GROUP:3 * W_GROUP, :], preferred_element_type=F32)
    acc += jnp.dot(md_ref[...], w_ref[3 * W_GROUP:4 * W_GROUP, :], preferred_element_type=F32)
    gate = mod_ref[0, :, 2 * D_MODEL:3 * D_MODEL]
    y = _load_x(x_refs, i, npb) + gate * acc
    if not final:
        o_refs[0][...] = y
        return
    y = y * lax.rsqrt(jnp.mean(y * y, axis=-1, keepdims=True) + NORM_EPS) * fg_ref[...]

    @pl.when(i < npb)
    def _():
        o_refs[0][...] = y

    @pl.when(i >= npb)
    def _():
        o_refs[1][...] = y


def _out_proj(xs, mixed, mod, w_out_bf, final_g, ln_g, ln_b, lay, tm, final):
    n = lay.n
    npb = lay.tp // tm
    lay_m = _Layout(lay.t_prompt, lay.tp // lay.t_prompt, lay.ts, (lay.n - lay.tp) // lay.ts, tm)
    mspec = pl.BlockSpec((tm, W_GROUP), lambda i: (i, 0))
    if final:
        out_specs = [pl.BlockSpec((tm, D_MODEL), lambda i: (jnp.minimum(i, npb - 1), 0)),
                     pl.BlockSpec((tm, D_MODEL), lambda i: (jnp.maximum(i - npb, 0), 0))]
        out_shape = [jax.ShapeDtypeStruct((lay.tp, D_MODEL), F32), jax.ShapeDtypeStruct((n - lay.tp, D_MODEL), F32)]
    else:
        out_specs = [pl.BlockSpec((tm, D_MODEL), lambda i: (i, 0))]
        out_shape = [jax.ShapeDtypeStruct((n, D_MODEL), F32)]
    out = pl.pallas_call(
        functools.partial(_out_proj_kernel, final, len(xs), npb),
        grid=(n // tm,),
        in_specs=_x_specs(len(xs), tm, npb, lambda i: i) + [
                  mspec, pl.BlockSpec((tm, W_GROUP), lambda i: (i, CB_A_Z)), mspec, mspec, mspec,
                  pl.BlockSpec((1, 1, 3 * D_MODEL), lambda i: (lay_m.batch_index(i, tm), 0, 0)),
                  pl.BlockSpec((4 * W_GROUP, D_MODEL), lambda i: (0, 0)),
                  pl.BlockSpec((1, D_MODEL), lambda i: (0, 0)),
                  pl.BlockSpec((1, W_GROUP), lambda i: (0, 0)), pl.BlockSpec((1, W_GROUP), lambda i: (0, 0))],
        out_specs=out_specs,
        out_shape=out_shape,
        compiler_params=_cparams(("arbitrary",)),
        name="out_proj",
    )(*xs, *mixed, mod, w_out_bf, final_g.reshape(1, D_MODEL), ln_g.reshape(1, W_GROUP), ln_b.reshape(1, W_GROUP))
    return tuple(out)


A_ROWS = 32


def _mixer_a_kernel(lay, u_in_ref, up_ref, un_ref, cw_ref, cb_ref, o_ref, u_ref, ush_ref):
    tb = lay.tb
    blk = pl.program_id(0)
    keep_prev = jnp.where(lay.seq_start(blk), 0.0, 1.0)
    keep_next = jnp.where(lay.seq_end(blk), 0.0, 1.0)
    u_ref[0:HALO_A, :] = up_ref[...].astype(F32) * keep_prev
    u_ref[HALO_A:HALO_A + tb, :] = u_in_ref[...].astype(F32)
    u_ref[HALO_A + tb:2 * HALO_A + tb, :] = un_ref[...].astype(F32) * keep_next
    half = CONV_A_WIDTH // 2
    n_sh = tb + 2 * HALO_A - SUBLANES
    for r in range(1, SUBLANES):
        ush_ref[r - 1] = u_ref[r:r + n_sh, :]

    def tile(t, carry):
        r0 = pl.multiple_of(t * A_ROWS, A_ROWS)
        acc = jnp.zeros((A_ROWS // SUBLANES, SUBLANES, W_GROUP), F32) + cb_ref[...]
        for k in range(CONV_A_WIDTH):
            a, r = divmod(HALO_A - half + k, SUBLANES)
            win = pl.ds(r0 + a * SUBLANES, A_ROWS)
            src = u_ref[win, :] if r == 0 else ush_ref[r - 1, win, :]
            acc = acc + cw_ref[k] * src.reshape(A_ROWS // SUBLANES, SUBLANES, W_GROUP)
        o_ref[pl.ds(r0, A_ROWS), :] = acc.reshape(A_ROWS, W_GROUP)
        return carry

    lax.fori_loop(0, tb // A_ROWS, tile, 0)


def _mixer_a(proj, conv_w, conv_b, lay):
    tb, nb = lay.tb, lay.nb
    hb = tb // HALO_A
    n_halo = lay.n // HALO_A

    def cur(cb):
        return pl.BlockSpec((tb, W_GROUP), lambda i: (i, cb))

    def prev(cb):
        return pl.BlockSpec((HALO_A, W_GROUP), lambda i: (jnp.maximum(i * hb - 1, 0), cb))

    def nxt(cb):
        return pl.BlockSpec((HALO_A, W_GROUP), lambda i: (jnp.minimum((i + 1) * hb, n_halo - 1), cb))

    vec = pl.BlockSpec((1, W_GROUP), lambda i: (0, 0))
    return pl.pallas_call(
        functools.partial(_mixer_a_kernel, lay),
        grid=(nb,),
        in_specs=[cur(CB_A_VAL), prev(CB_A_VAL), nxt(CB_A_VAL),
                  pl.BlockSpec((CONV_A_WIDTH, SUBLANES, W_GROUP), lambda i: (0, 0, 0)), vec],
        out_specs=pl.BlockSpec((tb, W_GROUP), lambda i: (i, 0)),
        out_shape=jax.ShapeDtypeStruct((lay.n, W_GROUP), F32),
        scratch_shapes=[pltpu.VMEM((tb + 2 * HALO_A, W_GROUP), F32),
                        pltpu.VMEM((SUBLANES - 1, tb + 2 * HALO_A - SUBLANES, W_GROUP), F32)],
        compiler_params=_cparams(("arbitrary",)),
        name="mixer_a_conv",
    )(proj, proj, proj,
      jnp.broadcast_to(conv_w[:, None, :], (CONV_A_WIDTH, SUBLANES, W_GROUP)), conv_b.reshape(1, W_GROUP))


def _blk_index(lay, rev):
    nb = lay.nb
    return (lambda j: nb - 1 - j) if rev else (lambda j: j)


def _reset_state(lay, rev, s_ref):
    blk = _blk_index(lay, rev)(pl.program_id(0))
    boundary = lay.seq_end(blk) if rev else lay.seq_start(blk)

    @pl.when(boundary)
    def _():
        s_ref[...] = jnp.zeros(s_ref.shape, s_ref.dtype)


def _sub_blocks(lay, rev, body, n_rows=SUB_BLOCK):
    nsb = lay.tb // n_rows

    def trip(i, carry):
        piece = (nsb - 1 - i) if rev else i
        body(pl.multiple_of(piece * n_rows, n_rows))
        return carry

    lax.fori_loop(0, nsb, trip, 0)


def _at(base, off, n):
    return pl.ds(pl.multiple_of(base + off, CHUNK), n)


def _finish(rev, base, o_ref, ob_ref, z_ref, gain_ref, out_ref, n_rows=SUB_BLOCK):
    rows = _at(base, 0, n_rows)
    if rev:
        out_ref[rows, :] = o_ref[...]
        return
    for h in range(N_HEADS):
        sl = slice(h * HEAD_DIM, (h + 1) * HEAD_DIM)
        o = o_ref[:, sl] + ob_ref[rows, sl]
        y = o * lax.rsqrt(jnp.mean(o * o, axis=-1, keepdims=True) + NORM_EPS) * gain_ref[:, sl]
        out_ref[rows, sl] = (y * _silu(z_ref[rows, sl].astype(F32))).astype(BF16)


def _causal_masks(rev, c):
    row = lax.broadcasted_iota(jnp.int32, (c, c), 0)
    col = lax.broadcasted_iota(jnp.int32, (c, c), 1)
    incl = (row <= col) if rev else (row >= col)
    strict = (row < col) if rev else (row > col)
    return row, col, incl, strict


def _ret_log_gamma(rev):
    lg = np.log1p(-np.exp2(-RET_DECAY_OFFSET - np.arange(N_HEADS, dtype=np.float64)))
    return lg[::-1].copy() if rev else lg


RET_CHUNK = 256
RET_PIECE = 512


def _ret_consts(rev):
    lg = _ret_log_gamma(rev).astype(np.float32)
    i = np.arange(RET_CHUNK, dtype=np.float32)
    idx = (RET_CHUNK - 1 - i) if rev else i
    rel = idx[:, None] - idx[None, :]
    dm = np.where(rel >= 0, np.exp(lg[:, None, None] * np.maximum(rel, 0.0)), 0.0).astype(np.float32)
    qd = np.exp(lg[:, None] * (idx + 1.0)).astype(np.float32)
    kd = np.exp(lg[:, None] * (RET_CHUNK - 1.0 - idx)).astype(np.float32)
    qd = np.broadcast_to(qd[:, :, None], (N_HEADS, RET_CHUNK, HEAD_DIM)).copy()
    kd = np.broadcast_to(kd[:, :, None], (N_HEADS, RET_CHUNK, HEAD_DIM)).copy()
    return jnp.asarray(dm), jnp.asarray(qd), jnp.asarray(kd)


def _ret_kernel(lay, rev, *refs):
    if rev:
        q_ref, k_ref, v_ref, cos_ref, sin_ref, dm_ref, qd_ref, kd_ref, out_ref, s_ref, qt_ref, kt_ref, o_ref = refs
        z_ref = ob_ref = gain_ref = None
    else:
        (q_ref, k_ref, v_ref, cos_ref, sin_ref, dm_ref, qd_ref, kd_ref, z_ref, ob_ref, gain_ref,
         out_ref, s_ref, qt_ref, kt_ref, o_ref) = refs
    nc = RET_PIECE // RET_CHUNK
    chunk_decay = [float(np.exp(np.float32(v) * RET_CHUNK)) for v in _ret_log_gamma(rev)]
    heads = [slice(h * HEAD_DIM, (h + 1) * HEAD_DIM) for h in range(N_HEADS)]
    _reset_state(lay, rev, s_ref)

    def piece(base):
        blk = _at(base, 0, RET_PIECE)
        cos = cos_ref[blk, :]
        sin = sin_ref[blk, :]
        for sl in heads:
            qh = q_ref[blk, sl].astype(F32)
            kh = k_ref[blk, sl].astype(F32)
            qt_ref[:, sl] = (qh * cos + pltpu.roll(qh, HEAD_DIM // 2, 1) * sin) * QK_SCALE
            kt_ref[:, sl] = kh * cos + pltpu.roll(kh, HEAD_DIM // 2, 1) * sin

        states = [s_ref[h] for h in range(N_HEADS)]
        for ci in range(nc):
            c = (nc - 1 - ci) if rev else ci
            rows = slice(c * RET_CHUNK, (c + 1) * RET_CHUNK)
            qcs = [qt_ref[rows, sl] for sl in heads]
            kcs = [kt_ref[rows, sl] for sl in heads]
            vcs = [v_ref[_at(base, c * RET_CHUNK, RET_CHUNK), sl].astype(BF16) for sl in heads]
            scores = [_dot_nt(q, k) for q, k in zip(qcs, kcs)]
            inters = [_dot(q * qd_ref[h], states[h]) for h, q in enumerate(qcs)]
            upds = [_dot_tn(k * kd_ref[h], v) for h, (k, v) in enumerate(zip(kcs, vcs))]
            intras = [jnp.dot((s * dm_ref[h]).astype(BF16), vcs[h], preferred_element_type=F32)
                      for h, s in enumerate(scores)]
            for h, sl in enumerate(heads):
                o_ref[rows, sl] = intras[h] + inters[h]
                states[h] = chunk_decay[h] * states[h] + upds[h]
        for h in range(N_HEADS):
            s_ref[h] = states[h]
        _finish(rev, base, o_ref, ob_ref, z_ref, gain_ref, out_ref, RET_PIECE)

    _sub_blocks(lay, rev, piece, RET_PIECE)


def _mixer_b(proj, cos2, sin2, o_bwd, gain, lay, rev):
    tb, nb = lay.tb, lay.nb
    bi = _blk_index(lay, rev)

    def cur(cb):
        return pl.BlockSpec((tb, W_GROUP), lambda j: (bi(j), cb))

    rot = pl.BlockSpec((tb, HEAD_DIM), lambda j: (lay.pos_block(bi(j)), 0))
    dm, qd, kd = _ret_consts(rev)
    in_specs = [cur(CB_B_Q), cur(CB_B_K), cur(CB_B_V), rot, rot,
                pl.BlockSpec((N_HEADS, RET_CHUNK, RET_CHUNK), lambda j: (0, 0, 0)),
                pl.BlockSpec((N_HEADS, RET_CHUNK, HEAD_DIM), lambda j: (0, 0, 0)),
                pl.BlockSpec((N_HEADS, RET_CHUNK, HEAD_DIM), lambda j: (0, 0, 0))]
    args = [proj, proj, proj, cos2, sin2, dm, qd, kd]
    if not rev:
        in_specs += [cur(CB_B_Z), pl.BlockSpec((tb, W_GROUP), lambda j: (bi(j), 0)),
                     pl.BlockSpec((1, W_GROUP), lambda j: (0, 0))]
        args += [proj, o_bwd, gain.reshape(1, W_GROUP)]
    return pl.pallas_call(
        functools.partial(_ret_kernel, lay, rev),
        grid=(nb,),
        in_specs=in_specs,
        out_specs=pl.BlockSpec((tb, W_GROUP), lambda j: (bi(j), 0)),
        out_shape=jax.ShapeDtypeStruct((lay.n, W_GROUP), F32 if rev else BF16),
        scratch_shapes=[pltpu.VMEM((N_HEADS, HEAD_DIM, HEAD_DIM), F32),
                        pltpu.VMEM((RET_PIECE, W_GROUP), F32), pltpu.VMEM((RET_PIECE, W_GROUP), F32),
                        pltpu.VMEM((RET_PIECE, W_GROUP), F32)],
        compiler_params=_cparams(("arbitrary",)),
        name="mixer_b_retention_bwd" if rev else "mixer_b_retention_fwd",
    )(*args)


HG_CHUNK = CHUNK
HGRN_LEVELS = tuple(HG_CHUNK >> n for n in range(1, HG_CHUNK.bit_length()))
LOG2_E = math.log2(math.e)
HGRN_GROUP = N_HEADS


def _hgrn_consts(rev):
    c = HG_CHUNK
    i = np.arange(c)[:, None]
    t = np.arange(c)[None, :]
    if rev:
        cum = t >= i
        rest = t < i
    else:
        cum = t <= i
        rest = t > i
    blocks = [cum, rest]
    masks = [np.eye(c, dtype=bool)]
    col = np.arange(c)[None, :]
    for s in HGRN_LEVELS:
        start = (i // (2 * s)) * (2 * s)
        later = (i - start) >= s
        if rev:
            ref = start + s
            p = np.where(later, (t >= ref) & (t < i), (t >= i) & (t < ref))
            q_half, k_half = 0, 1
        else:
            ref = start + s - 1
            p = np.where(later, (t > ref) & (t <= i), (t > i) & (t <= ref))
            q_half, k_half = 1, 0
        blocks.append(p)
        same = (i // (2 * s)) == (col // (2 * s))
        masks.append(same & ((i // s) % 2 == q_half) & ((col // s) % 2 == k_half))
    pst = np.concatenate(blocks, axis=0).astype(np.float32)
    pst = np.concatenate([pst, pst], axis=1)
    msk = np.stack(masks, axis=0).astype(np.float32)
    return jnp.asarray(pst, dtype=BF16), jnp.asarray(msk)


def _hgrn_kernel(lay, rev, *refs):
    if rev:
        (q_ref, f_ref, v_ref, llb_ref, l1lb_ref, pst_ref, msk_ref, out_ref,
         st_ref, g_ref, kk_ref, o_ref, e_all_ref, e_last_ref) = refs
        z_ref = ob_ref = gain_ref = None
    else:
        (q_ref, f_ref, v_ref, llb_ref, l1lb_ref, pst_ref, msk_ref, z_ref, ob_ref, gain_ref, out_ref,
         st_ref, g_ref, kk_ref, o_ref, e_all_ref, e_last_ref) = refs
    nc = SUB_BLOCK // HG_CHUNK
    n_lvl = len(HGRN_LEVELS)
    last_row = 0 if rev else HG_CHUNK - 1
    heads = [slice(h * HEAD_DIM, (h + 1) * HEAD_DIM) for h in range(N_HEADS)]
    _reset_state(lay, rev, st_ref)

    def piece(base):
        blk = _at(base, 0, SUB_BLOCK)
        a = llb_ref[...]
        b = l1lb_ref[...] + _log_sigmoid(f_ref[blk, :].astype(F32))
        g2 = (jnp.maximum(a, b) + jnp.log(1.0 + jnp.exp(-jnp.abs(a - b)))) * LOG2_E
        g_ref[...] = g2
        kk_ref[...] = (1.0 - jnp.exp2(g2)).astype(BF16)

        pst = pst_ref[...]
        for ci in range(nc):
            c = (nc - 1 - ci) if rev else ci
            g_parts = jnp.concatenate(_split2(g_ref[c * HG_CHUNK:(c + 1) * HG_CHUNK, :]), axis=0)
            e_f32 = jnp.exp2(jnp.dot(pst, g_parts, preferred_element_type=F32))
            e_all_ref[ci] = e_f32.astype(BF16)
            e_last_ref[ci] = e_f32[last_row:last_row + 1, :]

        chains = [(ci, (nc - 1 - ci) if rev else ci, sl) for ci in range(nc) for sl in heads]

        def operands(ci, c, sl):
            rows = slice(c * HG_CHUNK, (c + 1) * HG_CHUNK)
            src = _at(base, c * HG_CHUNK, HG_CHUNK)
            return (q_ref[src, sl], kk_ref[rows, sl], v_ref[src, sl].astype(BF16),
                    e_all_ref.at[ci])

        attns, upds = [], []
        for first in range(0, len(chains), HGRN_GROUP):
            ops = [operands(*chain) for chain in chains[first:first + HGRN_GROUP]]
            parts = [[_dot_nt(q, k) for q, k, _, _ in ops]]
            for lvl in range(n_lvl):
                es = [e_ref[(2 + lvl) * HG_CHUNK:(3 + lvl) * HG_CHUNK, sl]
                      for (_, _, _, e_ref), (_, _, sl) in zip(ops, chains[first:])]
                parts.append([_dot_nt(q * e, k * e) for (q, k, _, _), e in zip(ops, es)])
            upds += [_dot_tn(v, k * e_ref[HG_CHUNK:2 * HG_CHUNK, sl])
                     for (_, k, v, e_ref), (_, _, sl) in zip(ops, chains[first:])]
            for h in range(HGRN_GROUP):
                attn = parts[0][h] * msk_ref[0]
                for lvl in range(n_lvl):
                    attn = attn + parts[1 + lvl][h] * msk_ref[1 + lvl]
                attns.append(attn.astype(BF16))
        states = [st_ref[h] for h in range(N_HEADS)]
        entering = []
        for idx, (ci, c, sl) in enumerate(chains):
            h = idx % N_HEADS
            entering.append(states[h])
            states[h] = states[h] * e_last_ref[ci, :, sl] + upds[idx]
        for h in range(N_HEADS):
            st_ref[h] = states[h]
        ops = [operands(*chain) for chain in chains]
        inters = [_dot_nt(q * e_ref[0:HG_CHUNK, sl], s)
                  for (q, _, _, e_ref), (_, _, sl), s in zip(ops, chains, entering)]
        intras = [jnp.dot(a, v, preferred_element_type=F32) for a, (_, _, v, _) in zip(attns, ops)]
        for (ci, c, sl), inter, intra in zip(chains, inters, intras):
            o_ref[c * HG_CHUNK:(c + 1) * HG_CHUNK, sl] = inter + intra
        _finish(rev, base, o_ref, ob_ref, z_ref, gain_ref, out_ref)

    _sub_blocks(lay, rev, piece)


def _mixer_c(proj, log_lb, log1m_lb, o_bwd, gain, lay, rev):
    tb, nb = lay.tb, lay.nb
    bi = _blk_index(lay, rev)

    def cur(cb):
        return pl.BlockSpec((tb, W_GROUP), lambda j: (bi(j), cb))

    vec = pl.BlockSpec((1, W_GROUP), lambda j: (0, 0))
    d = 1 if rev else 0
    pst, msk = _hgrn_consts(rev)
    n_stack = pst.shape[0]
    in_specs = [cur(CB_C_Q), cur(CB_C_FB if rev else CB_C_FF), cur(CB_C_I), vec, vec,
                pl.BlockSpec((n_stack, 2 * HG_CHUNK), lambda j: (0, 0)),
                pl.BlockSpec((msk.shape[0], HG_CHUNK, HG_CHUNK), lambda j: (0, 0, 0))]
    args = [proj, proj, proj, log_lb[d].reshape(1, W_GROUP), log1m_lb[d].reshape(1, W_GROUP), pst, msk]
    if not rev:
        in_specs += [cur(CB_C_Z), pl.BlockSpec((tb, W_GROUP), lambda j: (bi(j), 0)), vec]
        args += [proj, o_bwd, gain.reshape(1, W_GROUP)]
    big = pltpu.VMEM((SUB_BLOCK, W_GROUP), F32)
    return pl.pallas_call(
        functools.partial(_hgrn_kernel, lay, rev),
        grid=(nb,),
        in_specs=in_specs,
        out_specs=pl.BlockSpec((tb, W_GROUP), lambda j: (bi(j), 0)),
        out_shape=jax.ShapeDtypeStruct((lay.n, W_GROUP), F32 if rev else BF16),
        scratch_shapes=[pltpu.VMEM((N_HEADS, HEAD_DIM, HEAD_DIM), F32),
                        big, pltpu.VMEM((SUB_BLOCK, W_GROUP), BF16), big,
                        pltpu.VMEM((SUB_BLOCK // HG_CHUNK, n_stack, W_GROUP), BF16),
                        pltpu.VMEM((SUB_BLOCK // HG_CHUNK, 1, W_GROUP), F32)],
        compiler_params=_cparams(("arbitrary",)),
        name="mixer_c_hgrn2_bwd" if rev else "mixer_c_hgrn2_fwd",
    )(*args)


def _unit_tri_inverses(a_mats, row, col):
    c = a_mats[0].shape[0]
    eye = jnp.where(row == col, 1.0, 0.0)

    def same_block(size):
        return (row // size) == (col // size)

    base = 8
    m_base = same_block(base)
    ns = [jnp.where(m_base, -a, 0.0).astype(BF16) for a in a_mats]
    n2s = [jnp.dot(n, n, preferred_element_type=F32) for n in ns]
    xs = [eye + n.astype(F32) for n in ns]
    xs = [x + _dot(x, n2) for x, n2 in zip(xs, n2s)]
    n4s = [_dot(n2, n2) for n2 in n2s]
    xs = [x + _dot(x, n4) for x, n4 in zip(xs, n4s)]
    size = base
    while size < c:
        m_off = same_block(2 * size) & jnp.logical_not(same_block(size))
        xbs = [x.astype(BF16) for x in xs]
        ts = [jnp.dot(xb, jnp.where(m_off, a, 0.0).astype(BF16), preferred_element_type=F32)
              for xb, a in zip(xbs, a_mats)]
        xs = [x - jnp.dot(t.astype(BF16), xb, preferred_element_type=F32) for x, t, xb in zip(xs, ts, xbs)]
        size *= 2
    return xs


SHIFT_ROWS = 128


def _shift_mats():
    half = DN_CONV_WIDTH // 2
    t = np.arange(SHIFT_ROWS)[:, None]
    s = np.arange(SHIFT_ROWS + 2 * HALO_D)[None, :]
    mats = [(s == HALO_D + t + d) for d in range(-half, half + 1) if d != 0]
    return jnp.asarray(np.concatenate(mats, axis=0).astype(np.float32), dtype=BF16)


def _delta_pre_kernel(lay, q_ref, k_ref, v_ref, qp_ref, kp_ref, vp_ref, qn_ref, kn_ref, vn_ref, sm_ref,
                      cw_ref, alog_ref, dtb_ref, sh_ref, qo_ref, ko_ref, vo_ref, cum_ref, beta_ref, cumt_ref,
                      xq_ref, xk_ref, xv_ref):
    tb = lay.tb
    blk = pl.program_id(0)
    keep_prev = jnp.where(lay.seq_start(blk), 0.0, 1.0)
    keep_next = jnp.where(lay.seq_end(blk), 0.0, 1.0)
    half = DN_CONV_WIDTH // 2
    parts = ((q_ref, qp_ref, qn_ref, qo_ref, xq_ref), (k_ref, kp_ref, kn_ref, ko_ref, xk_ref),
             (v_ref, vp_ref, vn_ref, vo_ref, xv_ref))
    for c_ref, p_ref, n_ref, _, x_ref in parts:
        x_ref[0:HALO_D, :] = (p_ref[...].astype(F32) * keep_prev).astype(BF16)
        x_ref[HALO_D:HALO_D + tb, :] = c_ref[...]
        x_ref[HALO_D + tb:2 * HALO_D + tb, :] = (n_ref[...].astype(F32) * keep_next).astype(BF16)
    tiles = [(part, r0) for part in range(len(parts)) for r0 in range(0, tb, SHIFT_ROWS)]
    shifted = [jnp.dot(sh_ref[...], parts[part][4][r0:r0 + SHIFT_ROWS + 2 * HALO_D, :], preferred_element_type=F32)
               for part, r0 in tiles]
    for (part, r0), sh in zip(tiles, shifted):
        x_ref, dst_ref = parts[part][4], parts[part][3]
        wsl = slice(part * W_GROUP, (part + 1) * W_GROUP)
        acc = cw_ref[half:half + 1, wsl] * x_ref[HALO_D + r0:HALO_D + r0 + SHIFT_ROWS, :].astype(F32)
        for j, k in enumerate([k for k in range(DN_CONV_WIDTH) if k != half]):
            acc = acc + cw_ref[k:k + 1, wsl] * sh[j * SHIFT_ROWS:(j + 1) * SHIFT_ROWS, :]
        y = _silu(acc)
        rows = slice(r0, r0 + SHIFT_ROWS)
        if part < 2:
            for h in range(N_HEADS):
                sl = slice(h * HEAD_DIM, (h + 1) * HEAD_DIM)
                yh = y[:, sl]
                yn = yh * lax.rsqrt(jnp.sum(yh * yh, axis=-1, keepdims=True) + NORM_EPS)
                dst_ref[rows, sl] = (yn * QK_SCALE if part == 0 else yn).astype(BF16)
        else:
            dst_ref[rows, :] = y.astype(BF16)

    sm = sm_ref[...]
    g = -jnp.exp(alog_ref[...]) * _softplus(sm + dtb_ref[...])
    beta_ref[...] = _sigmoid(sm)

    row = lax.broadcasted_iota(jnp.int32, (CHUNK, CHUNK), 0)
    col = lax.broadcasted_iota(jnp.int32, (CHUNK, CHUNK), 1)
    lower = jnp.where(row >= col, 1.0, 0.0)
    upper = jnp.where(row <= col, 1.0, 0.0)
    fwd_lane = lax.broadcasted_iota(jnp.int32, (CHUNK, SMALL_W), 1) < N_HEADS
    eye_rows = jnp.where(lax.broadcasted_iota(jnp.int32, (SUBLANES, SMALL_W), 0)
                         == lax.broadcasted_iota(jnp.int32, (SUBLANES, SMALL_W), 1), 1.0, 0.0).astype(BF16)
    for c in range(tb // CHUNK):
        gc = g[c * CHUNK:(c + 1) * CHUNK, :]
        cum = jnp.where(fwd_lane, _dot_exact_lhs(lower, gc), _dot_exact_lhs(upper, gc))
        cum_ref[c * CHUNK:(c + 1) * CHUNK, :] = cum
        cum_t = None
        for piece in _split3(cum):
            t = lax.dot_general(eye_rows, piece, (((1,), (1,)), ((), ())), preferred_element_type=F32)
            cum_t = t if cum_t is None else cum_t + t
        cumt_ref[c] = cum_t


def _delta_kernel(lay, rev, *refs):
    if rev:
        (qc_ref, kc_ref, vc_ref, cum_ref, beta_ref, cumt_ref, out_ref,
         s_ref, o_ref, qd_ref, kd_ref, qk_ref, rhs_ref, last_ref, gq_ref, bm_ref) = refs
        z_ref = ob_ref = gain_ref = None
    else:
        (qc_ref, kc_ref, vc_ref, cum_ref, beta_ref, cumt_ref, z_ref, ob_ref, gain_ref, out_ref,
         s_ref, o_ref, qd_ref, kd_ref, qk_ref, rhs_ref, last_ref, gq_ref, bm_ref) = refs
    _reset_state(lay, rev, s_ref)
    named = (qc_ref, kc_ref, vc_ref, cum_ref, beta_ref, cumt_ref, z_ref, ob_ref, gain_ref, out_ref,
             s_ref, o_ref, qd_ref, kd_ref, qk_ref, rhs_ref, last_ref, gq_ref, bm_ref)
    _sub_blocks(lay, rev, functools.partial(_delta_piece, rev, named), DELTA_PIECE)


def _delta_piece(rev, named, base):
    (qc_ref, kc_ref, vc_ref, cum_ref, beta_ref, cumt_ref, z_ref, ob_ref, gain_ref, out_ref,
     s_ref, o_ref, qd_ref, kd_ref, qk_ref, rhs_ref, last_ref, gq_ref, bm_ref) = named
    nc = DELTA_PIECE // CHUNK
    row, col, incl, strict = _causal_masks(rev, CHUNK)
    lane0 = N_HEADS if rev else 0

    cums = [cum_ref[_at(base, c * CHUNK, CHUNK), :] for c in range(nc)]
    cum_ts = [cumt_ref[base // CHUNK + c] for c in range(nc)]
    raws = []
    for c in range(nc):
        rows = slice(c * CHUNK, (c + 1) * CHUNK)
        src = _at(base, c * CHUNK, CHUNK)
        beta_all = beta_ref[src, :]
        for h in range(N_HEADS):
            sl = slice(h * HEAD_DIM, (h + 1) * HEAD_DIM)
            qc = qc_ref[src, sl]
            kc = kc_ref[src, sl]
            vc = vc_ref[src, sl]
            ci_b = jnp.broadcast_to(cums[c][:, lane0 + h:lane0 + h + 1], (CHUNK, HEAD_DIM))
            b_lane = 2 * N_HEADS + lane0 + h
            beta_b = jnp.broadcast_to(beta_all[:, b_lane:b_lane + 1], (CHUNK, HEAD_DIM))
            k_beta = kc * beta_b
            raws.append(_dot_nt(jnp.concatenate([k_beta.astype(BF16), qc], axis=0), kc))
            e_ci = jnp.exp(ci_b)
            rhs_ref[c * N_HEADS + h] = jnp.concatenate([vc * beta_b, k_beta * e_ci], axis=1).astype(BF16)
            qd_ref[rows, sl] = qc * e_ci
            c_last = ci_b[0:1, :] if rev else ci_b[CHUNK - 1:CHUNK, :]
            kd_ref[rows, sl] = (kc * jnp.exp(c_last - ci_b)).astype(BF16)
            last_ref[c * N_HEADS + h] = jnp.broadcast_to(jnp.exp(c_last), (SUBLANES, HEAD_DIM))
    a_mats = []
    for idx, raw in enumerate(raws):
        c, h = divmod(idx, N_HEADS)
        ci_col = cums[c][:, lane0 + h:lane0 + h + 1]
        cj_row = cum_ts[c][lane0 + h:lane0 + h + 1, :]
        decay = jnp.exp(jnp.where(incl, ci_col - cj_row, NEG_BIG))
        a_mats.append(jnp.where(strict, raw[0:CHUNK] * decay, 0.0))
        qk_ref[idx] = (raw[CHUNK:2 * CHUNK] * decay).astype(BF16)

    t_invs = _unit_tri_inverses(a_mats, row, col)
    where = [(slice((idx // N_HEADS) * CHUNK, (idx // N_HEADS + 1) * CHUNK),
              slice((idx % N_HEADS) * HEAD_DIM, (idx % N_HEADS + 1) * HEAD_DIM)) for idx in range(len(t_invs))]
    uws = [jnp.dot(t_inv.astype(BF16), rhs_ref[idx], preferred_element_type=F32).astype(BF16)
           for idx, t_inv in enumerate(t_invs)]
    kuws = [lax.dot_general(kd_ref[rows, sl], uw, (((0,), (0,)), ((), ())), preferred_element_type=F32)
            for (rows, sl), uw in zip(where, uws)]
    quws = [jnp.dot(qk_ref[idx], uw, preferred_element_type=F32) for idx, uw in enumerate(uws)]
    for idx, ((rows, sl), kuw, quw) in enumerate(zip(where, kuws, quws)):
        bm_ref[idx] = kuw[:, 0:HEAD_DIM]
        gq_ref[idx, 0:HEAD_DIM, :] = kuw[:, HEAD_DIM:2 * HEAD_DIM].astype(BF16)
        gq_ref[idx, HEAD_DIM:HEAD_DIM + CHUNK, :] = (qd_ref[rows, sl] - quw[:, HEAD_DIM:2 * HEAD_DIM]).astype(BF16)
        o_ref[rows, sl] = quw[:, 0:HEAD_DIM]

    states = [s_ref[h] for h in range(N_HEADS)]
    for ci in range(nc):
        c = (nc - 1 - ci) if rev else ci
        rows = slice(c * CHUNK, (c + 1) * CHUNK)
        for h in range(N_HEADS):
            idx = c * N_HEADS + h
            sl = slice(h * HEAD_DIM, (h + 1) * HEAD_DIM)
            gs = jnp.dot(gq_ref[idx], states[h].astype(BF16), preferred_element_type=F32)
            o_ref[rows, sl] += gs[HEAD_DIM:HEAD_DIM + CHUNK]
            states[h] = states[h] * last_ref[idx][0:1, :] - gs[0:HEAD_DIM] + bm_ref[idx]
    for h in range(N_HEADS):
        s_ref[h] = states[h]
    _finish(rev, base, o_ref, ob_ref, z_ref, gain_ref, out_ref, DELTA_PIECE)


def _mixer_d_pre(proj, proj_small, conv_w, a_log, dt_bias, lay):
    tb, nb = lay.tb, lay.nb
    hb = tb // HALO_D
    n_halo = lay.n // HALO_D

    def cur(cb):
        return pl.BlockSpec((tb, W_GROUP), lambda i: (i, cb))

    def prev(cb):
        return pl.BlockSpec((HALO_D, W_GROUP), lambda i: (jnp.maximum(i * hb - 1, 0), cb))

    def nxt(cb):
        return pl.BlockSpec((HALO_D, W_GROUP), lambda i: (jnp.minimum((i + 1) * hb, n_halo - 1), cb))

    small = pl.BlockSpec((1, SMALL_W), lambda i: (0, 0))
    pad_lanes = SMALL_W - 2 * N_HEADS
    alog_row = jnp.pad(a_log.reshape(1, 2 * N_HEADS), ((0, 0), (0, pad_lanes)))
    dtb_row = jnp.pad(dt_bias.reshape(1, 2 * N_HEADS), ((0, 0), (0, pad_lanes)))
    wide = pl.BlockSpec((tb, W_GROUP), lambda i: (i, 0))
    narrow = pl.BlockSpec((tb, SMALL_W), lambda i: (i, 0))
    shifts = _shift_mats()
    return pl.pallas_call(
        functools.partial(_delta_pre_kernel, lay),
        grid=(nb,),
        in_specs=[cur(CB_D_Q), cur(CB_D_K), cur(CB_D_V), prev(CB_D_Q), prev(CB_D_K), prev(CB_D_V),
                  nxt(CB_D_Q), nxt(CB_D_K), nxt(CB_D_V),
                  pl.BlockSpec((tb, SMALL_W), lambda i: (i, 0)),
                  pl.BlockSpec((DN_CONV_WIDTH, 3 * W_GROUP), lambda i: (0, 0)), small, small,
                  pl.BlockSpec(shifts.shape, lambda i: (0, 0))],
        out_specs=[wide, wide, wide, narrow, narrow,
                   pl.BlockSpec((tb // CHUNK, SUBLANES, CHUNK), lambda i: (i, 0, 0))],
        out_shape=[jax.ShapeDtypeStruct((lay.n, W_GROUP), BF16)] * 3
        + [jax.ShapeDtypeStruct((lay.n, SMALL_W), F32)] * 2
        + [jax.ShapeDtypeStruct((lay.n // CHUNK, SUBLANES, CHUNK), F32)],
        scratch_shapes=[pltpu.VMEM((tb + 2 * HALO_D, W_GROUP), BF16)] * 3,
        compiler_params=_cparams(("arbitrary",)),
        name="mixer_d_deltanet_pre",
    )(*([proj] * 9), proj_small, conv_w, alog_row, dtb_row, shifts)


def _mixer_d(pre, proj, o_bwd, gain, lay, rev):
    tb, nb = lay.tb, lay.nb
    bi = _blk_index(lay, rev)
    wide = pl.BlockSpec((tb, W_GROUP), lambda j: (bi(j), 0))
    narrow = pl.BlockSpec((tb, SMALL_W), lambda j: (bi(j), 0))
    in_specs = [wide, wide, wide, narrow, narrow,
                pl.BlockSpec((tb // CHUNK, SUBLANES, CHUNK), lambda j: (bi(j), 0, 0))]
    args = list(pre)
    if not rev:
        in_specs += [pl.BlockSpec((tb, W_GROUP), lambda j: (bi(j), CB_D_Z)), wide,
                     pl.BlockSpec((1, W_GROUP), lambda j: (0, 0))]
        args += [proj, o_bwd, gain.reshape(1, W_GROUP)]
    big = pltpu.VMEM((DELTA_PIECE, W_GROUP), F32)
    nc = DELTA_PIECE // CHUNK
    return pl.pallas_call(
        functools.partial(_delta_kernel, lay, rev),
        grid=(nb,),
        in_specs=in_specs,
        out_specs=wide,
        out_shape=jax.ShapeDtypeStruct((lay.n, W_GROUP), F32 if rev else BF16),
        scratch_shapes=[pltpu.VMEM((N_HEADS, HEAD_DIM, HEAD_DIM), F32),
                        big,
                        big,
                        pltpu.VMEM((DELTA_PIECE, W_GROUP), BF16),
                        pltpu.VMEM((nc * N_HEADS, CHUNK, CHUNK), BF16),
                        pltpu.VMEM((nc * N_HEADS, CHUNK, 2 * HEAD_DIM), BF16),
                        pltpu.VMEM((nc * N_HEADS, SUBLANES, HEAD_DIM), F32),
                        pltpu.VMEM((nc * N_HEADS, HEAD_DIM + CHUNK, HEAD_DIM), BF16),
                        pltpu.VMEM((nc * N_HEADS, HEAD_DIM, HEAD_DIM), F32)],
        compiler_params=_cparams(("arbitrary",)),
        name="mixer_d_deltanet_bwd" if rev else "mixer_d_deltanet_fwd",
    )(*args)


def _regroup_w_in(w):
    d_qkv_end = 15 * W_GROUP
    small = w[:, d_qkv_end:d_qkv_end + 4 * N_HEADS]
    d_z = w[:, d_qkv_end + 4 * N_HEADS:]
    pad = jnp.zeros((w.shape[0], SMALL_W - 4 * N_HEADS), w.dtype)
    main = jnp.concatenate([w[:, :d_qkv_end], d_z], axis=1).astype(BF16)
    return main, jnp.concatenate([small, pad], axis=1).astype(BF16)


def _rotary_tables(t_max):
    half = HEAD_DIM // 2
    inv = 1.0 / (ROPE_BASE ** (jnp.arange(half, dtype=F32) / half))
    ang = jnp.arange(t_max, dtype=F32)[:, None] * inv[None, :]
    cos, sin = jnp.cos(ang), jnp.sin(ang)
    return jnp.concatenate([cos, cos], axis=-1), jnp.concatenate([-sin, sin], axis=-1)


def _pick_tile(t_prompt, t_sample, want):
    tile = want
    while t_prompt % tile or t_sample % tile:
        tile //= 2
    return tile


def kernel(x_prompt, x_sample, c_prompt, c_sample, ada_w, ada_b, norm_g, w_in, conv_a_w, conv_a_b, ln_a_g, ln_a_b,
           ret_norm_g, hgrn_lb_logits, hgrn_norm_g, dn_conv_w, dn_a_log, dn_dt_bias, dn_norm_g, w_out, final_g):
    depth = w_in.shape[0]
    n_prompt, t_prompt, _ = x_prompt.shape
    n_sample, t_sample, _ = x_sample.shape
    tm_in = _pick_tile(t_prompt, t_sample, 1024)
    tm_out = _pick_tile(t_prompt, t_sample, 512)
    lay = _Layout(t_prompt, n_prompt, t_sample, n_sample, _pick_tile(t_prompt, t_sample, 1024))
    lay_a = _Layout(t_prompt, n_prompt, t_sample, n_sample, _pick_tile(t_prompt, t_sample, 512))
    lay_dp = _Layout(t_prompt, n_prompt, t_sample, n_sample, _pick_tile(t_prompt, t_sample, 1024))

    xs = (x_prompt.reshape(-1, D_MODEL), x_sample.reshape(-1, D_MODEL))
    c_all = jnp.concatenate([c_prompt, c_sample], axis=0)
    n_c = c_all.shape[0]
    c_all = jnp.pad(c_all, ((0, (-n_c) % SUBLANES), (0, 0)))
    mod = _modulation(c_all, ada_w, ada_b)

    lb = jnp.cumsum(jax.nn.softmax(hgrn_lb_logits.astype(F32), axis=0), axis=0)
    lb = lb - lb[:1]
    log_lb = jnp.log(lb)
    log1m_lb = jnp.log1p(-lb)
    cos2, sin2 = _rotary_tables(max(t_prompt, t_sample))

    for l in range(depth):
        mod_l = mod[l].reshape(mod.shape[1], 1, 3 * D_MODEL)
        proj, proj_small = _in_proj(xs, mod_l, norm_g[l], *_regroup_w_in(w_in[l]), lay, tm_in)
        conv_a = _mixer_a(proj, conv_a_w[l], conv_a_b[l], lay_a)
        ob = _mixer_b(proj, cos2, sin2, None, None, lay, True)
        m_b = _mixer_b(proj, cos2, sin2, ob, ret_norm_g[l], lay, False)
        oc = _mixer_c(proj, log_lb[l], log1m_lb[l], None, None, lay, True)
        m_c = _mixer_c(proj, log_lb[l], log1m_lb[l], oc, hgrn_norm_g[l], lay, False)
        d_pre = _mixer_d_pre(proj, proj_small, dn_conv_w[l], dn_a_log[l], dn_dt_bias[l], lay_dp)
        od = _mixer_d(d_pre, proj, None, None, lay, True)
        m_d = _mixer_d(d_pre, proj, od, dn_norm_g[l], lay, False)
        xs = _out_proj(xs, (conv_a, proj, m_b, m_c, m_d), mod_l, w_out[l].astype(BF16), final_g,
                       ln_a_g[l], ln_a_b[l], lay, tm_out, final=(l == depth - 1))

    return (xs[0].reshape(n_prompt, t_prompt, D_MODEL), xs[1].reshape(n_sample, t_sample, D_MODEL))
```
